```python
import math
import jax, jax.numpy as jnp
from jax import lax
import numpy as np

D_MODEL = 1024
BATCH = 8
SEQ = 2048
DEPTH = 2
DEC_BATCH = 32
DEC_SEQ = 8
PAST_LEN = 8192
PAGE_SIZE = 128

N_A_LAYERS = DEPTH // 2
N_B_LAYERS = DEPTH - N_A_LAYERS
POOL_WINDOWS = (2, 4, 8, 16)
N_POOL_GROUPS = len(POOL_WINDOWS)
POOL_GROUP_DIM = D_MODEL // N_POOL_GROUPS
POOL_STATE = max(POOL_WINDOWS) - 1
N_HEADS = 16
HEAD_DIM = D_MODEL // N_HEADS
N_KV_HEADS = 4
HEADS_PER_KV = N_HEADS // N_KV_HEADS
CMP_BLOCK = 32
CMP_STRIDE = 16
CMP_HIDDEN = 2 * HEAD_DIM
SEL_BLOCK = 64
N_SEL = 16
WINDOW = 512
Q_BLOCK = 128
SEL_Q_BLOCK = 64
N_PAGED_SLOTS = 4
ROPE_THETA = 10000.0
SCALE = HEAD_DIM ** -0.5
N_GROUPS = 4
EXPERTS_PER_GROUP = 4
N_EXPERTS = N_GROUPS * EXPERTS_PER_GROUP
D_EXPERT = 512
TOP_K_INNER = 2

EPS = 1e-6
NEG = -1e30
TINY = 1e-30
BIG = 1e4

kernel_name = 'yoco_pool_nsa_hmoe_step'


def rms_norm(x, g):
    xf = x.astype(jnp.float32)
    y = xf * lax.rsqrt(jnp.mean(xf * xf, axis=-1, keepdims=True) + EPS) * g.astype(jnp.float32)
    return y.astype(x.dtype)


def rope(x, pos):
    half = HEAD_DIM // 2
    inv = 1.0 / (ROPE_THETA ** (jnp.arange(half, dtype=jnp.float32) * (2.0 / HEAD_DIM)))
    ang = pos.astype(jnp.float32)[:, None] * inv[None, :]
    c = jnp.cos(ang)[None, :, None, :]
    s = jnp.sin(ang)[None, :, None, :]
    xf = x.astype(jnp.float32)
    x1, x2 = xf[..., :half], xf[..., half:]
    return jnp.concatenate([x1 * c - x2 * s, x2 * c + x1 * s], axis=-1).astype(x.dtype)


def masked_softmax(s, mask):
    s = jnp.where(mask, s.astype(jnp.float32), NEG)
    m = jnp.max(s, axis=-1, keepdims=True)
    p = jnp.where(mask, jnp.exp(s - m), 0.0)
    return p / jnp.maximum(jnp.sum(p, axis=-1, keepdims=True), TINY)


def pool_mix(ext, n_prev, w_pool, scale):
    T = ext.shape[1] - n_prev
    cs = jnp.pad(jnp.cumsum(ext.astype(jnp.float32), axis=1), ((0, 0), (1, 0), (0, 0)))
    end = jnp.arange(T) + (n_prev + 1)
    x_new = ext[:, n_prev:].astype(jnp.float32)
    outs = []
    for g, w in enumerate(POOL_WINDOWS):
        lo, hi = g * POOL_GROUP_DIM, (g + 1) * POOL_GROUP_DIM
        start = jnp.maximum(end - w, 0)
        cnt = (end - start).astype(jnp.float32)[None, :, None]
        mean = (cs[:, end, lo:hi] - cs[:, start, lo:hi]) / cnt
        outs.append(jnp.einsum('btc,cd->btd', (mean - x_new[..., lo:hi]).astype(ext.dtype), w_pool[g]))
    return (jnp.concatenate(outs, axis=-1) * scale).astype(ext.dtype)


def hmoe(xn, r_group, r_expert, w_gu, w_down):
    B, T, D = xn.shape
    x2 = xn.reshape(B * T, D)
    n = x2.shape[0]
    pg = jax.nn.softmax((x2 @ r_group).astype(jnp.float32), axis=-1)
    g_val, g_idx = lax.top_k(pg, 1)
    le = (x2 @ r_expert).astype(jnp.float32).reshape(n, N_GROUPS, EXPERTS_PER_GROUP)
    le_g = le[jnp.arange(n), g_idx[:, 0]]
    pe = jax.nn.softmax(le_g, axis=-1)
    e_val, e_idx = lax.top_k(pe, TOP_K_INNER)
    wts = g_val * (e_val / jnp.sum(e_val, axis=-1, keepdims=True))
    eid = g_idx * EXPERTS_PER_GROUP + e_idx
    comb = jnp.sum(jax.nn.one_hot(eid, N_EXPERTS, dtype=jnp.float32) * wts[..., None], axis=1)
    y = jnp.zeros((n, D), jnp.float32)
    for e in range(N_EXPERTS):
        gu = x2 @ w_gu[e]
        hdn = jax.nn.silu(gu[:, :D_EXPERT]) * gu[:, D_EXPERT:]
        y = y + comb[:, e:e + 1] * (hdn @ w_down[e]).astype(jnp.float32)
    return y.astype(xn.dtype).reshape(B, T, D)


def compress(raw, pe, w1, b1, w2):
    B, L = raw.shape[:2]
    n_chunk = L // CMP_STRIDE
    ratio = CMP_BLOCK // CMP_STRIDE
    nc = n_chunk - ratio + 1
    ch = raw[:, :n_chunk * CMP_STRIDE].reshape(B, n_chunk, CMP_STRIDE, N_KV_HEADS, HEAD_DIM)
    blocks = jnp.concatenate([ch[:, r:r + nc] for r in range(ratio)], axis=2)
    blocks = blocks + pe[None, None, :, None, :]
    flat = blocks.transpose(0, 1, 3, 2, 4).reshape(B, nc, N_KV_HEADS, CMP_BLOCK * HEAD_DIM)
    out = jax.nn.gelu(flat @ w1 + b1) @ w2
    return out.transpose(0, 2, 1, 3)


def sel_block_view(k_full):
    B, L = k_full.shape[:2]
    n_sb = -(-L // SEL_BLOCK)
    kp = jnp.pad(k_full, ((0, 0), (0, n_sb * SEL_BLOCK - L), (0, 0), (0, 0)))
    return kp.reshape(B, n_sb, SEL_BLOCK, N_KV_HEADS, HEAD_DIM).transpose(0, 3, 1, 2, 4).reshape(
        B, N_KV_HEADS, n_sb, SEL_BLOCK * HEAD_DIM)


def build_shared_kv(h, pos, kv_past, win_prev, kv_norm, w_kv, k_norm, cmp_pe, cmp_w1, cmp_b1, cmp_w2):
    B, T, _ = h.shape
    kv = (rms_norm(h, kv_norm) @ w_kv).reshape(B, T, 6, N_KV_HEADS, HEAD_DIM)
    k_sel = rope(rms_norm(kv[:, :, 2], k_norm[1]), pos)
    k_win = rope(rms_norm(kv[:, :, 4], k_norm[2]), pos)
    kv_new = jnp.stack([kv[:, :, 0], kv[:, :, 1], k_sel, kv[:, :, 3]], axis=2)
    win_rows = jnp.stack([k_win, kv[:, :, 5]], axis=2)
    if kv_past is None:
        kv_full = kv_new
    else:
        kv_full = jnp.concatenate([kv_past.astype(kv_new.dtype), kv_new], axis=1)
    ck = rms_norm(compress(kv_full[:, :, 0], cmp_pe[0], cmp_w1[0], cmp_b1[0], cmp_w2[0]), k_norm[0])
    cv = compress(kv_full[:, :, 1], cmp_pe[1], cmp_w1[1], cmp_b1[1], cmp_w2[1])
    cend = jnp.arange(ck.shape[2]) * CMP_STRIDE + (CMP_BLOCK - 1)
    ksb = sel_block_view(kv_full[:, :, 2])
    vsb = sel_block_view(kv_full[:, :, 3])
    if win_prev is None:
        win_new = win_rows[:, -min(WINDOW, T):]
        wk, wv, wpos = win_rows[:, :, 0], win_rows[:, :, 1], None
    else:
        nbuf = win_prev.shape[1]
        wfull = jnp.concatenate([win_prev.astype(win_rows.dtype), win_rows], axis=1)
        win_new = wfull[:, -nbuf:]
        wk, wv = wfull[:, :, 0], wfull[:, :, 1]
        wpos = jnp.concatenate([pos[0] - nbuf + jnp.arange(nbuf), pos])
    return {'ck': ck, 'cv': cv, 'cend': cend, 'ksb': ksb, 'vsb': vsb, 'wk': wk, 'wv': wv,
            'wpos': wpos, 'kv_new': kv_new, 'win_new': win_new}


def select_blocks(p_cmp, qpos, n_sb):
    pg = jnp.sum(p_cmp, axis=2)
    nc = pg.shape[-1]
    offs = jnp.arange(1 - CMP_BLOCK // CMP_STRIDE, SEL_BLOCK // CMP_STRIDE)
    ci = (SEL_BLOCK // CMP_STRIDE) * jnp.arange(n_sb)[:, None] + offs[None, :]
    ok = (ci >= 0) & (ci < nc)
    gathered = jnp.take(pg, jnp.clip(ci, 0, nc - 1), axis=-1)
    score = jnp.sum(jnp.where(ok, gathered, 0.0), axis=-1)
    j = jnp.arange(n_sb)[None, :]
    cur = (qpos // SEL_BLOCK)[:, None]
    selectable = j * SEL_BLOCK <= qpos[:, None]
    forced = (j == 0) | (j == cur) | (j == cur - 1)
    score = jnp.where(selectable, score + jnp.where(forced, BIG, 0.0), -BIG)
    _, idx = lax.top_k(score, min(N_SEL, n_sb))
    return idx


def sel_attend_chunk(q, idx, qpos, ksb, vsb):
    B, G, HP, Tq, _ = q.shape
    k = idx.shape[-1]
    flat = idx.reshape(B, G, Tq * k)
    bi = jnp.arange(B)[:, None, None]
    gi = jnp.arange(G)[None, :, None]
    kg = ksb[bi, gi, flat].reshape(B, G, Tq, k, SEL_BLOCK, HEAD_DIM)
    vg = vsb[bi, gi, flat].reshape(B, G, Tq, k, SEL_BLOCK, HEAD_DIM)
    s = jnp.einsum('bghtd,bgtksd->bghtks', q, kg).astype(jnp.float32) * SCALE
    kpos = idx[..., None] * SEL_BLOCK + jnp.arange(SEL_BLOCK)
    mask = (kpos <= qpos[:, None, None])[:, :, None]
    p = masked_softmax(s.reshape(B, G, HP, Tq, k * SEL_BLOCK), mask.reshape(B, G, 1, Tq, k * SEL_BLOCK))
    p = p.reshape(B, G, HP, Tq, k, SEL_BLOCK).astype(vg.dtype)
    return jnp.einsum('bghtks,bgtksd->bghtd', p, vg)


def sel_attend(q, idx, qpos, ksb, vsb):
    B, G, HP, T, DH = q.shape
    if T <= SEL_Q_BLOCK or T % SEL_Q_BLOCK:
        return sel_attend_chunk(q, idx, qpos, ksb, vsb)
    nq = T // SEL_Q_BLOCK
    qc = q.reshape(B, G, HP, nq, SEL_Q_BLOCK, DH).transpose(3, 0, 1, 2, 4, 5)
    ic = idx.reshape(B, G, nq, SEL_Q_BLOCK, idx.shape[-1]).transpose(2, 0, 1, 3, 4)
    pc = qpos.reshape(nq, SEL_Q_BLOCK)
    out = lax.map(lambda a: sel_attend_chunk(a[0], a[1], a[2], ksb, vsb), (qc, ic, pc))
    return out.transpose(1, 2, 3, 0, 4, 5).reshape(B, G, HP, T, DH)


def window_banded(q, k, v):
    B, G, HP, T, DH = q.shape
    nb = T // Q_BLOCK
    wb = WINDOW // Q_BLOCK

    def bands(x):
        xt = x.transpose(0, 2, 1, 3)
        xp = jnp.pad(xt, ((0, 0), (0, 0), (WINDOW, 0), (0, 0))).reshape(B, G, nb + wb, Q_BLOCK, DH)
        return jnp.concatenate([xp[:, :, w:w + nb] for w in range(wb + 1)], axis=3)

    kb, vb = bands(k), bands(v)
    qb = q.reshape(B, G, HP, nb, Q_BLOCK, DH)
    s = jnp.einsum('bghnqd,bgnkd->bghnqk', qb, kb).astype(jnp.float32) * SCALE
    qpos = jnp.arange(nb)[:, None] * Q_BLOCK + jnp.arange(Q_BLOCK)[None, :]
    kpos = jnp.arange(nb)[:, None] * Q_BLOCK - WINDOW + jnp.arange((wb + 1) * Q_BLOCK)[None, :]
    dq = qpos[:, :, None] - kpos[:, None, :]
    mask = (kpos[:, None, :] >= 0) & (dq >= 0) & (dq < WINDOW)
    p = masked_softmax(s, mask).astype(vb.dtype)
    return jnp.einsum('bghnqk,bgnkd->bghnqd', p, vb).reshape(B, G, HP, T, DH)


def window_buffer(q, qpos, k, v, kpos):
    s = jnp.einsum('bghtd,bkgd->bghtk', q, k).astype(jnp.float32) * SCALE
    dq = qpos[:, None] - kpos[None, :]
    p = masked_softmax(s, (dq >= 0) & (dq < WINDOW)).astype(v.dtype)
    return jnp.einsum('bghtk,bkgd->bghtd', p, v)


def to_groups(a):
    B, T = a.shape[:2]
    return a.reshape(B, T, N_KV_HEADS, HEADS_PER_KV, HEAD_DIM).transpose(0, 2, 3, 1, 4)


def from_groups(a):
    B, T = a.shape[0], a.shape[3]
    return a.transpose(0, 3, 1, 2, 4).reshape(B, T, N_HEADS, HEAD_DIM)


def nsa_mix(xn, qpos, ctx, w_qg, q_norm, w_o):
    B, T, _ = xn.shape
    qg = xn @ w_qg
    q = rms_norm(qg[..., :N_HEADS * HEAD_DIM].reshape(B, T, N_HEADS, HEAD_DIM), q_norm)
    gates = jax.nn.sigmoid(qg[..., N_HEADS * HEAD_DIM:].astype(jnp.float32)).reshape(B, T, 3, N_HEADS)
    qn_g = to_groups(q)
    qr_g = to_groups(rope(q, qpos))
    s = jnp.einsum('bghtd,bgnd->bghtn', qn_g, ctx['ck']).astype(jnp.float32) * SCALE
    p_cmp = masked_softmax(s, ctx['cend'][None, :] <= qpos[:, None])
    o_cmp = jnp.einsum('bghtn,bgnd->bghtd', p_cmp.astype(ctx['cv'].dtype), ctx['cv'])
    idx = select_blocks(p_cmp, qpos, ctx['ksb'].shape[2])
    o_sel = sel_attend(qr_g, idx, qpos, ctx['ksb'], ctx['vsb'])
    if ctx['wpos'] is None:
        o_win = window_banded(qr_g, ctx['wk'], ctx['wv'])
    else:
        o_win = window_buffer(qr_g, qpos, ctx['wk'], ctx['wv'], ctx['wpos'])
    o = (gates[:, :, 0, :, None] * from_groups(o_cmp).astype(jnp.float32)
         + gates[:, :, 1, :, None] * from_groups(o_sel).astype(jnp.float32)
         + gates[:, :, 2, :, None] * from_groups(o_win).astype(jnp.float32))
    return o.astype(xn.dtype).reshape(B, T, N_HEADS * HEAD_DIM) @ w_o


def trunk(x, pos, pool_prev, kv_past, win_prev, params):
    (norm_mix, norm_ffn, pool_w, pool_scale, kv_norm, w_kv, k_norm, cmp_pe, cmp_w1, cmp_b1, cmp_w2,
     w_qg, q_norm, w_o, router_group, router_expert, w_gate_up, w_down) = params
    h = x
    new_pool = []
    ctx = None
    for layer in range(DEPTH):
        if layer < N_A_LAYERS:
            xn = rms_norm(h, norm_mix[layer])
            if pool_prev is None:
                ext, n_prev = xn, 0
            else:
                ext, n_prev = jnp.concatenate([pool_prev[layer].astype(xn.dtype), xn], axis=1), POOL_STATE
            h = h + pool_mix(ext, n_prev, pool_w[layer], pool_scale[layer])
            new_pool.append(ext[:, -POOL_STATE:])
        else:
            if ctx is None:
                ctx = build_shared_kv(h, pos, kv_past, win_prev, kv_norm, w_kv, k_norm,
                                      cmp_pe, cmp_w1, cmp_b1, cmp_w2)
            bi = layer - N_A_LAYERS
            xn = rms_norm(h, norm_mix[layer])
            h = h + nsa_mix(xn, pos, ctx, w_qg[bi], q_norm[bi], w_o[bi])
        h = h + hmoe(rms_norm(h, norm_ffn[layer]), router_group[layer], router_expert[layer],
                     w_gate_up[layer], w_down[layer])
    return h, jnp.stack(new_pool), ctx['kv_new'], ctx['win_new']


def setup_inputs(seed: int = 0) -> dict:
    key = jax.random.key(seed)
    ks = jax.random.split(key, 24)
    f32 = jnp.float32

    def nrm(k, shape, scale):
        return jax.random.normal(k, shape, f32) * scale

    def gain(k, shape):
        return 1.0 + 0.02 * jax.random.normal(k, shape, f32)

    n_pages = PAST_LEN // PAGE_SIZE
    n_pool_pages = (DEC_BATCH * n_pages * 5) // 4
    page_table = jax.random.permutation(ks[4], n_pool_pages)[:DEC_BATCH * n_pages].reshape(
        DEC_BATCH, n_pages).astype(jnp.int32)
    qg_width = N_HEADS * HEAD_DIM + 3 * N_HEADS
    return {
        'x_prompt': nrm(ks[0], (BATCH, SEQ, D_MODEL), 1.0),
        'x_sample': nrm(ks[1], (DEC_BATCH, DEC_SEQ, D_MODEL), 1.0),
        'state_pool': nrm(ks[2], (N_A_LAYERS, DEC_BATCH, POOL_STATE, D_MODEL), 1.0),
        'cache_kv': nrm(ks[3], (n_pool_pages, PAGE_SIZE, N_PAGED_SLOTS, N_KV_HEADS, HEAD_DIM), 1.0),
        'page_table': page_table,
        'state_win': nrm(ks[5], (DEC_BATCH, min(WINDOW, PAST_LEN), 2, N_KV_HEADS, HEAD_DIM), 1.0),
        'norm_mix': gain(ks[6], (DEPTH, D_MODEL)),
        'norm_ffn': gain(ks[7], (DEPTH, D_MODEL)),
        'pool_w': nrm(ks[8], (N_A_LAYERS, N_POOL_GROUPS, POOL_GROUP_DIM, POOL_GROUP_DIM), POOL_GROUP_DIM ** -0.5),
        'pool_scale': gain(ks[9], (N_A_LAYERS, D_MODEL)),
        'kv_norm': gain(ks[10], (D_MODEL,)),
        'w_kv': nrm(ks[11], (D_MODEL, 6 * N_KV_HEADS * HEAD_DIM), D_MODEL ** -0.5),
        'k_norm': gain(ks[12], (3, HEAD_DIM)),
        'cmp_pe': nrm(ks[13], (2, CMP_BLOCK, HEAD_DIM), 0.1),
        'cmp_w1': nrm(ks[14], (2, CMP_BLOCK * HEAD_DIM, CMP_HIDDEN), (CMP_BLOCK * HEAD_DIM) ** -0.5),
        'cmp_b1': nrm(ks[15], (2, CMP_HIDDEN), 0.02),
        'cmp_w2': nrm(ks[16], (2, CMP_HIDDEN, HEAD_DIM), CMP_HIDDEN ** -0.5),
        'w_qg': nrm(ks[17], (N_B_LAYERS, D_MODEL, qg_width), D_MODEL ** -0.5),
        'q_norm': gain(ks[18], (N_B_LAYERS, HEAD_DIM)),
        'w_o': nrm(ks[19], (N_B_LAYERS, N_HEADS * HEAD_DIM, D_MODEL), (N_HEADS * HEAD_DIM) ** -0.5),
        'router_group': nrm(ks[20], (DEPTH, D_MODEL, N_GROUPS), D_MODEL ** -0.5),
        'router_expert': nrm(ks[21], (DEPTH, D_MODEL, N_EXPERTS), D_MODEL ** -0.5),
        'w_gate_up': nrm(ks[22], (DEPTH, N_EXPERTS, D_MODEL, 2 * D_EXPERT), D_MODEL ** -0.5),
        'w_down': nrm(ks[23], (DEPTH, N_EXPERTS, D_EXPERT, D_MODEL), D_EXPERT ** -0.5),
    }


def reference(x_prompt, x_sample, state_pool, cache_kv, page_table, state_win, norm_mix, norm_ffn,
              pool_w, pool_scale, kv_norm, w_kv, k_norm, cmp_pe, cmp_w1, cmp_b1, cmp_w2, w_qg, q_norm,
              w_o, router_group, router_expert, w_gate_up, w_down):
    params = (norm_mix, norm_ffn, pool_w, pool_scale, kv_norm, w_kv, k_norm, cmp_pe, cmp_w1, cmp_b1,
              cmp_w2, w_qg, q_norm, w_o, router_group, router_expert, w_gate_up, w_down)
    pos_p = jnp.arange(x_prompt.shape[1])
    y_prompt, new_pool_prompt, new_kv_prompt, new_win_prompt = trunk(
        x_prompt, pos_p, None, None, None, params)
    n_seq, n_pages = page_table.shape
    kv_past = cache_kv[page_table].reshape(n_seq, n_pages * PAGE_SIZE, N_PAGED_SLOTS, N_KV_HEADS, HEAD_DIM)
    pos_s = PAST_LEN + jnp.arange(x_sample.shape[1])
    y_sample, new_pool_sample, new_kv_sample, new_win_sample = trunk(
        x_sample, pos_s, state_pool, kv_past, state_win, params)
    return (y_prompt, y_sample, new_pool_prompt, new_pool_sample, new_kv_prompt, new_kv_sample,
            new_win_prompt, new_win_sample)
```

```python
import functools

import jax
import jax.numpy as jnp
from jax import lax
from jax.experimental import pallas as pl
from jax.experimental.pallas import tpu as pltpu

F32 = jnp.float32
BF16 = jnp.bfloat16
I32 = jnp.int32
HIGHEST = lax.Precision.HIGHEST

D_MODEL = 1024
POOL_WINDOWS = (2, 4, 8, 16)
POOL_GROUP_DIM = D_MODEL // len(POOL_WINDOWS)
POOL_STATE = max(POOL_WINDOWS) - 1
POOL_HALO = 16
N_HEADS = 16
HEAD_DIM = 64
HALF_DIM = HEAD_DIM // 2
N_KV_HEADS = 4
HEADS_PER_KV = N_HEADS // N_KV_HEADS
KV_WIDTH = N_KV_HEADS * HEAD_DIM
CMP_BLOCK = 32
CMP_STRIDE = 16
CMP_HIDDEN = 2 * HEAD_DIM
SEL_BLOCK = 64
N_SEL = 16
WINDOW = 512
PAGE_SIZE = 128
ROPE_THETA = 10000.0
SCALE = HEAD_DIM ** -0.5
N_GROUPS = 4
EXPERTS_PER_GROUP = 4
N_EXPERTS = N_GROUPS * EXPERTS_PER_GROUP
D_EXPERT = 512
EPS = 1e-6
NEG = -1e30
TINY = 1e-30
BIG = 1e4

LANES = 128
GATE_ROWS = 16
VMEM_LIMIT = 56 * 1024 * 1024


def _params(sem, vmem=None):
    return pltpu.CompilerParams(dimension_semantics=sem, vmem_limit_bytes=vmem)


def _rms(x, g):
    return x * lax.rsqrt(jnp.mean(x * x, axis=-1, keepdims=True) + EPS) * g


def _nt_dot(a, b, precision=None):
    return lax.dot_general(a, b, (((1,), (1,)), ((), ())), precision=precision,
                           preferred_element_type=F32)


def _dot(a, b, precision=None):
    return jnp.dot(a, b, precision=precision, preferred_element_type=F32)


def _sigmoid(x):
    return 1.0 / (1.0 + jnp.exp(-x))


def _pool_kernel(h_ref, prev_ref, g_ref, w_ref, sc_ref, o_ref, np_ref, ext_ref, *, tt, clip):
    t = pl.program_id(1)
    x = h_ref[0]
    xn = _rms(x, g_ref[...])

    @pl.when(t == 0)
    def _():
        ext_ref[0:POOL_HALO, :] = prev_ref[0]

    @pl.when(t > 0)
    def _():
        ext_ref[0:POOL_HALO, :] = ext_ref[tt:tt + POOL_HALO, :]

    ext_ref[POOL_HALO:POOL_HALO + tt, :] = xn
    if clip:
        tpos = t * tt + lax.broadcasted_iota(I32, (tt, 1), 0)
    outs = []
    for gi, w in enumerate(POOL_WINDOWS):
        lo = gi * POOL_GROUP_DIM
        hi = lo + POOL_GROUP_DIM
        acc = ext_ref[POOL_HALO:POOL_HALO + tt, lo:hi]
        for j in range(1, w):
            acc = acc + ext_ref[POOL_HALO - j:POOL_HALO - j + tt, lo:hi]
        if clip:
            mean = acc / jnp.minimum(tpos + 1, w).astype(F32)
        else:
            mean = acc * (1.0 / w)
        d = (mean - xn[:, lo:hi]).astype(BF16)
        outs.append(_dot(d, w_ref[gi].astype(BF16)))
    o_ref[0] = x + jnp.concatenate(outs, axis=1) * sc_ref[...]

    @pl.when(t == pl.num_programs(1) - 1)
    def _():
        np_ref[0] = ext_ref[tt + POOL_HALO - POOL_STATE:tt + POOL_HALO, :]


def _pool_layer(h, prev16, gain, w_pool, scale, *, tt, clip):
    b, t, d = h.shape
    return pl.pallas_call(
        functools.partial(_pool_kernel, tt=tt, clip=clip),
        grid=(b, t // tt),
        in_specs=[
            pl.BlockSpec((1, tt, d), lambda i, j: (i, j, 0)),
            pl.BlockSpec((1, POOL_HALO, d), lambda i, j: (i, 0, 0)),
            pl.BlockSpec((1, d), lambda i, j: (0, 0)),
            pl.BlockSpec(w_pool.shape, lambda i, j: (0, 0, 0)),
            pl.BlockSpec((1, d), lambda i, j: (0, 0)),
        ],
        out_specs=[
            pl.BlockSpec((1, tt, d), lambda i, j: (i, j, 0)),
            pl.BlockSpec((1, POOL_STATE, d), lambda i, j: (i, 0, 0)),
        ],
        out_shape=[jax.ShapeDtypeStruct((b, t, d), F32),
                   jax.ShapeDtypeStruct((b, POOL_STATE, d), F32)],
        scratch_shapes=[pltpu.VMEM((tt + POOL_HALO, d), F32)],
        compiler_params=_params(("arbitrary", "arbitrary")),
        name="pool_layer",
    )(h, prev16, gain.reshape(1, d), w_pool, scale.reshape(1, d))


def _softmax_rows(rows):
    m = functools.reduce(jnp.maximum, rows)
    es = [jnp.exp(r - m) for r in rows]
    s = functools.reduce(lambda a, b: a + b, es)
    return [e / s for e in es]


def _router_kernel(h_ref, g_ref, rt_ref, xn_ref, comb_ref):
    xn = _rms(h_ref[...], g_ref[...])
    xn_ref[...] = xn.astype(BF16)
    lt = _nt_dot(rt_ref[...], xn, precision=HIGHEST)
    pg = _softmax_rows([lt[i:i + 1, :] for i in range(N_GROUPS)])
    g_val = functools.reduce(jnp.maximum, pg)
    g_idx = jnp.full(g_val.shape, N_GROUPS - 1, I32)
    for i in range(N_GROUPS - 2, -1, -1):
        g_idx = jnp.where(pg[i] == g_val, i, g_idx)
    le = []
    for j in range(EXPERTS_PER_GROUP):
        v = lt[N_GROUPS + (N_GROUPS - 1) * EXPERTS_PER_GROUP + j:N_GROUPS + (N_GROUPS - 1) * EXPERTS_PER_GROUP + j + 1, :]
        for gi in range(N_GROUPS - 2, -1, -1):
            r = N_GROUPS + gi * EXPERTS_PER_GROUP + j
            v = jnp.where(g_idx == gi, lt[r:r + 1, :], v)
        le.append(v)
    pe = _softmax_rows(le)
    ranks = []
    for j in range(EXPERTS_PER_GROUP):
        r = jnp.zeros(g_val.shape, I32)
        for i in range(EXPERTS_PER_GROUP):
            if i == j:
                continue
            beats = (pe[i] > pe[j]) | (pe[i] == pe[j]) if i < j else (pe[i] > pe[j])
            r = r + beats.astype(I32)
        ranks.append(r)
    vals, idxs = [], []
    for k in range(2):
        v = jnp.zeros(g_val.shape, F32)
        ix = jnp.zeros(g_val.shape, I32)
        for j in range(EXPERTS_PER_GROUP):
            hit = ranks[j] == k
            v = jnp.where(hit, pe[j], v)
            ix = jnp.where(hit, j, ix)
        vals.append(v)
        idxs.append(ix)
    tot = vals[0] + vals[1]
    erow = lax.broadcasted_iota(I32, (LANES, g_val.shape[1]), 0)
    comb_t = jnp.zeros(erow.shape, F32)
    for k in range(2):
        wk = g_val * (vals[k] / tot)
        comb_t = comb_t + jnp.where(erow == g_idx * EXPERTS_PER_GROUP + idxs[k], wk, 0.0)
    comb_ref[...] = comb_t.T


def _router(h2, gain, rt, *, tm):
    n, d = h2.shape
    return pl.pallas_call(
        _router_kernel,
        grid=(n // tm,),
        in_specs=[
            pl.BlockSpec((tm, d), lambda i: (i, 0)),
            pl.BlockSpec((1, d), lambda i: (0, 0)),
            pl.BlockSpec(rt.shape, lambda i: (0, 0)),
        ],
        out_specs=[
            pl.BlockSpec((tm, d), lambda i: (i, 0)),
            pl.BlockSpec((tm, LANES), lambda i: (i, 0)),
        ],
        out_shape=[jax.ShapeDtypeStruct((n, d), BF16), jax.ShapeDtypeStruct((n, LANES), F32)],
        compiler_params=_params(("arbitrary",)),
        name="moe_router",
    )(h2, gain.reshape(1, d), rt)


def _moe_kernel(x_ref, comb_ref, h_ref, wgu_ref, wd_ref, o_ref, wg_s, wd_s, *, sub):
    e = pl.program_id(1)
    wg_s[...] = wgu_ref[0, 0].astype(BF16)
    wd_s[...] = wd_ref[0, 0].astype(BF16)

    @pl.when(e == 0)
    def _():
        o_ref[...] = h_ref[...]

    lane = lax.broadcasted_iota(I32, (sub, LANES), 1)

    def body(i, carry):
        r0 = pl.multiple_of(i * sub, sub)
        gu = _dot(x_ref[pl.ds(r0, sub), :], wg_s[...])
        a = gu[:, :D_EXPERT]
        hdn = (a * _sigmoid(a) * gu[:, D_EXPERT:]).astype(BF16)
        y = _dot(hdn, wd_s[...])
        c = jnp.sum(jnp.where(lane == e, comb_ref[pl.ds(r0, sub), :], 0.0), axis=-1, keepdims=True)
        o_ref[pl.ds(r0, sub), :] += c * y
        return carry

    lax.fori_loop(0, x_ref.shape[0] // sub, body, 0)


def _moe(h2, gain, rt, w_gu, w_down, layer, *, tm, sub):
    n, d = h2.shape
    xn, comb = _router(h2, gain, rt, tm=tm)
    return pl.pallas_call(
        functools.partial(_moe_kernel, sub=sub),
        grid=(n // tm, N_EXPERTS),
        in_specs=[
            pl.BlockSpec((tm, d), lambda i, e: (i, 0)),
            pl.BlockSpec((tm, LANES), lambda i, e: (i, 0)),
            pl.BlockSpec((tm, d), lambda i, e: (i, 0)),
            pl.BlockSpec((1, 1, d, 2 * D_EXPERT), lambda i, e: (layer, e, 0, 0)),
            pl.BlockSpec((1, 1, D_EXPERT, d), lambda i, e: (layer, e, 0, 0)),
        ],
        out_specs=pl.BlockSpec((tm, d), lambda i, e: (i, 0)),
        out_shape=jax.ShapeDtypeStruct((n, d), F32),
        scratch_shapes=[pltpu.VMEM((d, 2 * D_EXPERT), BF16), pltpu.VMEM((D_EXPERT, d), BF16)],
        compiler_params=_params(("arbitrary", "arbitrary"), VMEM_LIMIT),
        name="moe_experts",
    )(xn, comb, h2, w_gu, w_down)


def _head_norm_rope_t(z, gain, c, s):
    outs = []
    for g in range(z.shape[0] // HEAD_DIM):
        zh = z[g * HEAD_DIM:(g + 1) * HEAD_DIM]
        zn = zh * lax.rsqrt(jnp.mean(zh * zh, axis=0, keepdims=True) + EPS) * gain
        x1 = zn[:HALF_DIM]
        x2 = zn[HALF_DIM:]
        outs.append(x1 * c - x2 * s)
        outs.append(x2 * c + x1 * s)
    return jnp.concatenate(outs, axis=0)


def _kvproj_kernel(h_ref, g_ref, wt_ref, gsel_ref, gwin_ref, cos_ref, sin_ref,
                   kvt_ref, wint_ref, craw_ref, ksel_ref, kwin_ref, vselt_ref, vwint_ref, w_s):
    @pl.when((pl.program_id(0) == 0) & (pl.program_id(1) == 0))
    def _():
        w_s[...] = wt_ref[...].astype(BF16)

    xn = _rms(h_ref[0], g_ref[...]).astype(BF16)
    kvt = _nt_dot(w_s[...], xn)
    tt = xn.shape[0]
    c = cos_ref[...]
    s = sin_ref[...]
    gsel = jnp.concatenate([gsel_ref[...]] * (tt // LANES), axis=1)
    gwin = jnp.concatenate([gwin_ref[...]] * (tt // LANES), axis=1)
    ksel = _head_norm_rope_t(kvt[2 * KV_WIDTH:3 * KV_WIDTH], gsel, c, s)
    kwin = _head_norm_rope_t(kvt[4 * KV_WIDTH:5 * KV_WIDTH], gwin, c, s)
    kvt_ref[0] = jnp.concatenate([kvt[:2 * KV_WIDTH], ksel, kvt[3 * KV_WIDTH:4 * KV_WIDTH]], axis=0)
    wint_ref[0] = jnp.concatenate([kwin, kvt[5 * KV_WIDTH:]], axis=0)
    vselt_ref[...] = kvt[3 * KV_WIDTH:4 * KV_WIDTH].astype(BF16)
    vwint_ref[...] = kvt[5 * KV_WIDTH:].astype(BF16)
    raw = kvt[:2 * KV_WIDTH].T
    for k in range(2 * KV_WIDTH // LANES):
        craw_ref[k] = raw[:, k * LANES:(k + 1) * LANES]
    ksel_n = ksel.T
    kwin_n = kwin.T
    for g in range(N_KV_HEADS):
        ksel_ref[g] = ksel_n[:, g * HEAD_DIM:(g + 1) * HEAD_DIM].astype(BF16)
        kwin_ref[g] = kwin_n[:, g * HEAD_DIM:(g + 1) * HEAD_DIM].astype(BF16)


def _kvproj(x3, gain, w_kv, k_norm, cos_tt, sin_tt, *, tt):
    b, t, d = x3.shape
    n = b * t
    nt = t // tt
    wt = w_kv.T
    gsel = jnp.broadcast_to(k_norm[1][:, None], (HEAD_DIM, LANES))
    gwin = jnp.broadcast_to(k_norm[2][:, None], (HEAD_DIM, LANES))
    full = lambda a: pl.BlockSpec(a.shape, lambda i, j: (0,) * a.ndim)
    return pl.pallas_call(
        _kvproj_kernel,
        grid=(b, nt),
        in_specs=[
            pl.BlockSpec((1, tt, d), lambda i, j: (i, j, 0)),
            pl.BlockSpec((1, d), lambda i, j: (0, 0)),
            full(wt), full(gsel), full(gwin),
            pl.BlockSpec((HALF_DIM, tt), lambda i, j: (0, j)),
            pl.BlockSpec((HALF_DIM, tt), lambda i, j: (0, j)),
        ],
        out_specs=[
            pl.BlockSpec((1, 4 * KV_WIDTH, tt), lambda i, j: (i, 0, j)),
            pl.BlockSpec((1, 2 * KV_WIDTH, tt), lambda i, j: (i, 0, j)),
            pl.BlockSpec((2 * KV_WIDTH // LANES, tt, LANES), lambda i, j: (0, i * nt + j, 0)),
            pl.BlockSpec((N_KV_HEADS, tt, HEAD_DIM), lambda i, j: (0, i * nt + j, 0)),
            pl.BlockSpec((N_KV_HEADS, tt, HEAD_DIM), lambda i, j: (0, i * nt + j, 0)),
            pl.BlockSpec((KV_WIDTH, tt), lambda i, j: (0, i * nt + j)),
            pl.BlockSpec((KV_WIDTH, tt), lambda i, j: (0, i * nt + j)),
        ],
        out_shape=[
            jax.ShapeDtypeStruct((b, 4 * KV_WIDTH, t), F32),
            jax.ShapeDtypeStruct((b, 2 * KV_WIDTH, t), F32),
            jax.ShapeDtypeStruct((2 * KV_WIDTH // LANES, n, LANES), F32),
            jax.ShapeDtypeStruct((N_KV_HEADS, n, HEAD_DIM), BF16),
            jax.ShapeDtypeStruct((N_KV_HEADS, n, HEAD_DIM), BF16),
            jax.ShapeDtypeStruct((KV_WIDTH, n), BF16),
            jax.ShapeDtypeStruct((KV_WIDTH, n), BF16),
        ],
        scratch_shapes=[pltpu.VMEM(wt.shape, BF16)],
        compiler_params=_params(("arbitrary", "arbitrary"), VMEM_LIMIT),
        name="kv_proj",
    )(x3, gain.reshape(1, d), wt, gsel, gwin, cos_tt, sin_tt)


def _rope_tables_transposed(pos):
    inv = 1.0 / (ROPE_THETA ** (jnp.arange(HALF_DIM, dtype=F32) * (2.0 / HEAD_DIM)))
    ang = pos.astype(F32)[:, None] * inv[None, :]
    return jnp.cos(ang).T, jnp.sin(ang).T


def _cmp_ab_accumulate(load_rows, wab_ref, n_rows):
    accs = [[jnp.zeros((n_rows, 2 * CMP_HIDDEN), F32) for _ in range(N_KV_HEADS)] for _ in range(2)]
    heads_per_chunk = LANES // HEAD_DIM
    for r in range(CMP_STRIDE):
        for s in range(2):
            w = wab_ref[s, r].astype(BF16)
            for c in range(KV_WIDTH // LANES):
                xr = load_rows(r, s * (KV_WIDTH // LANES) + c)
                for k in range(heads_per_chunk):
                    g = c * heads_per_chunk + k
                    accs[s][g] = accs[s][g] + _dot(xr[:, k * HEAD_DIM:(k + 1) * HEAD_DIM].astype(BF16), w)
    return accs


def _cmpab_kernel(craw_ref, wab_ref, ab_ref, *, n_chunk):
    accs = _cmp_ab_accumulate(lambda r, c: craw_ref[c, pl.ds(r, n_chunk, stride=CMP_STRIDE), :], wab_ref, n_chunk)
    for s in range(2):
        for g in range(N_KV_HEADS):
            ab_ref[0, s, g] = accs[s][g]


def _cmpab_pages_kernel(pt_ref, *refs, n_pages):
    del pt_ref
    page_refs = refs[:n_pages]
    wab_ref, ab_ref, craw_s = refs[n_pages:]
    pairs = KV_WIDTH // LANES
    for j in range(n_pages):
        for s in range(2):
            for k in range(pairs):
                tile = jnp.concatenate([page_refs[j][0, s, 2 * k], page_refs[j][0, s, 2 * k + 1]], axis=0)
                craw_s[s * pairs + k, j * PAGE_SIZE:(j + 1) * PAGE_SIZE, :] = tile.T
    n_chunk = n_pages * PAGE_SIZE // CMP_STRIDE
    accs = _cmp_ab_accumulate(lambda r, c: craw_s[c, pl.ds(r, n_chunk, stride=CMP_STRIDE), :], wab_ref, n_chunk)
    for s in range(2):
        for g in range(N_KV_HEADS):
            ab_ref[0, s, g] = accs[s][g]


def _cmp_ab_pages(cache5, page_table, wab, *, n_pages):
    b, pages_per_seq = page_table.shape
    steps = pages_per_seq // n_pages
    rows = n_pages * PAGE_SIZE // CMP_STRIDE

    def page_spec(j):
        return pl.BlockSpec((1, 2, N_KV_HEADS, HEAD_DIM, PAGE_SIZE),
                            lambda i, q, pt: (pt[i, q * n_pages + j], 0, 0, 0, 0))

    return pl.pallas_call(
        functools.partial(_cmpab_pages_kernel, n_pages=n_pages),
        grid_spec=pltpu.PrefetchScalarGridSpec(
            num_scalar_prefetch=1,
            grid=(b, steps),
            in_specs=[page_spec(j) for j in range(n_pages)]
            + [pl.BlockSpec(wab.shape, lambda i, q, pt: (0, 0, 0, 0))],
            out_specs=pl.BlockSpec((1, 2, N_KV_HEADS, rows, 2 * CMP_HIDDEN), lambda i, q, pt: (i, 0, 0, q, 0)),
            scratch_shapes=[pltpu.VMEM((2 * KV_WIDTH // LANES, n_pages * PAGE_SIZE, LANES), F32)],
        ),
        out_shape=jax.ShapeDtypeStruct((b, 2, N_KV_HEADS, steps * rows, 2 * CMP_HIDDEN), F32),
        compiler_params=_params(("arbitrary", "arbitrary"), VMEM_LIMIT),
        name="cmp_ab_pages",
    )(page_table, *([cache5] * n_pages), wab)


def _cmp_weights(cmp_w1):
    w = cmp_w1.reshape(2, 2, CMP_STRIDE, HEAD_DIM, CMP_HIDDEN)
    return w.transpose(0, 2, 3, 1, 4).reshape(2, CMP_STRIDE, HEAD_DIM, 2 * CMP_HIDDEN)


def _cmp_ab_prompt(craw, wab, *, b, t):
    n_chunk = t // CMP_STRIDE
    return pl.pallas_call(
        functools.partial(_cmpab_kernel, n_chunk=n_chunk),
        grid=(b,),
        in_specs=[
            pl.BlockSpec((craw.shape[0], t, LANES), lambda i: (0, i, 0)),
            pl.BlockSpec(wab.shape, lambda i: (0, 0, 0, 0)),
        ],
        out_specs=pl.BlockSpec((1, 2, N_KV_HEADS, n_chunk, 2 * CMP_HIDDEN), lambda i: (i, 0, 0, 0, 0)),
        out_shape=jax.ShapeDtypeStruct((b, 2, N_KV_HEADS, n_chunk, 2 * CMP_HIDDEN), F32),
        compiler_params=_params(("arbitrary",)),
        name="cmp_ab_prompt",
    )(craw, wab)


def _cmpfin_kernel(ab_ref, pe_ref, w1_ref, b1_ref, w2_ref, gk_ref, ck_ref, cv_ref, *, n_row):
    for s in range(2):
        bias = _dot(pe_ref[s].astype(BF16), w1_ref[s].astype(BF16)) + b1_ref[s]
        w2 = w2_ref[s].astype(BF16)
        for g in range(N_KV_HEADS):
            ab = ab_ref[0, s, g]
            hid = ab[:, :CMP_HIDDEN] + pltpu.roll(ab[:, CMP_HIDDEN:], n_row - 1, 0) + bias
            cdf = 0.5 * (1.0 + jnp.tanh(0.7978845608028654 * (hid + 0.044715 * (hid * hid * hid))))
            out = _dot((hid * cdf).astype(BF16), w2)
            if s == 0:
                ck_ref[0, g] = _rms(out, gk_ref[...]).astype(BF16)
            else:
                cv_ref[0, g] = out.astype(BF16)


def _cmp_finish(ab, cmp_pe, cmp_w1, cmp_b1, cmp_w2, gk):
    b = ab.shape[0]
    n_row = ab.shape[3]
    pe = cmp_pe.reshape(2, 1, CMP_BLOCK * HEAD_DIM)
    full = lambda a: pl.BlockSpec(a.shape, lambda i: (0,) * a.ndim)
    b1 = cmp_b1.reshape(2, 1, CMP_HIDDEN)
    gk2 = gk.reshape(1, HEAD_DIM)
    return pl.pallas_call(
        functools.partial(_cmpfin_kernel, n_row=n_row),
        grid=(b,),
        in_specs=[pl.BlockSpec((1,) + ab.shape[1:], lambda i: (i, 0, 0, 0, 0)),
                  full(pe), full(cmp_w1), full(b1), full(cmp_w2), full(gk2)],
        out_specs=[pl.BlockSpec((1, N_KV_HEADS, n_row, HEAD_DIM), lambda i: (i, 0, 0, 0)),
                   pl.BlockSpec((1, N_KV_HEADS, n_row, HEAD_DIM), lambda i: (i, 0, 0, 0))],
        out_shape=[jax.ShapeDtypeStruct((b, N_KV_HEADS, n_row, HEAD_DIM), BF16),
                   jax.ShapeDtypeStruct((b, N_KV_HEADS, n_row, HEAD_DIM), BF16)],
        compiler_params=_params(("arbitrary",)),
        name="cmp_finish",
    )(ab, pe, cmp_w1, b1, cmp_w2, gk2)


def _qproj_kernel(h_ref, g_ref, wt_ref, gq_ref, cos_ref, sin_ref, qn_ref, qr_ref, gt_ref, w_s):
    @pl.when(pl.program_id(0) == 0)
    def _():
        w_s[...] = wt_ref[...].astype(BF16)

    xn = _rms(h_ref[...], g_ref[...]).astype(BF16)
    qg = _nt_dot(w_s[...], xn)
    tt = xn.shape[0]
    gq = jnp.concatenate([gq_ref[...]] * (tt // LANES), axis=1)
    c = cos_ref[...]
    s = sin_ref[...]
    for h in range(N_HEADS):
        qh = qg[h * HEAD_DIM:(h + 1) * HEAD_DIM]
        qn = qh * lax.rsqrt(jnp.mean(qh * qh, axis=0, keepdims=True) + EPS) * gq
        qn_ref[h * HEAD_DIM:(h + 1) * HEAD_DIM, :] = (qn * SCALE).astype(BF16)
        x1 = qn[:HALF_DIM]
        x2 = qn[HALF_DIM:]
        qr_ref[h * HEAD_DIM:h * HEAD_DIM + HALF_DIM, :] = ((x1 * c - x2 * s) * SCALE).astype(BF16)
        qr_ref[h * HEAD_DIM + HALF_DIM:(h + 1) * HEAD_DIM, :] = ((x2 * c + x1 * s) * SCALE).astype(BF16)
    gt_ref[...] = _sigmoid(qg[N_HEADS * HEAD_DIM:])


def _qg_weights(w_qg):
    nq = N_HEADS * HEAD_DIM
    gates = w_qg[:, nq:].reshape(D_MODEL, 3, N_KV_HEADS, HEADS_PER_KV).transpose(2, 1, 3, 0)
    gates = gates.reshape(N_KV_HEADS, 3 * HEADS_PER_KV, D_MODEL)
    gates = jnp.pad(gates, ((0, 0), (0, GATE_ROWS - 3 * HEADS_PER_KV), (0, 0)))
    return jnp.concatenate([w_qg[:, :nq].T, gates.reshape(N_KV_HEADS * GATE_ROWS, D_MODEL)], axis=0)


def _qproj(h2, gain, wt, q_norm, cos_tt, sin_tt, *, tt, pos_blocks):
    n, d = h2.shape
    nq = N_HEADS * HEAD_DIM
    ng = N_KV_HEADS * GATE_ROWS
    gq = jnp.broadcast_to(q_norm[:, None], (HEAD_DIM, LANES))
    full = lambda a: pl.BlockSpec(a.shape, lambda i: (0,) * a.ndim)
    return pl.pallas_call(
        _qproj_kernel,
        grid=(n // tt,),
        in_specs=[
            pl.BlockSpec((tt, d), lambda i: (i, 0)),
            pl.BlockSpec((1, d), lambda i: (0, 0)),
            full(wt), full(gq),
            pl.BlockSpec((HALF_DIM, tt), lambda i: (0, i % pos_blocks)),
            pl.BlockSpec((HALF_DIM, tt), lambda i: (0, i % pos_blocks)),
        ],
        out_specs=[
            pl.BlockSpec((nq, tt), lambda i: (0, i)),
            pl.BlockSpec((nq, tt), lambda i: (0, i)),
            pl.BlockSpec((ng, tt), lambda i: (0, i)),
        ],
        out_shape=[jax.ShapeDtypeStruct((nq, n), BF16), jax.ShapeDtypeStruct((nq, n), BF16),
                   jax.ShapeDtypeStruct((ng, n), F32)],
        scratch_shapes=[pltpu.VMEM(wt.shape, BF16)],
        compiler_params=_params(("arbitrary",), VMEM_LIMIT),
        name="q_proj",
    )(h2, gain.reshape(1, d), wt, gq, cos_tt, sin_tt)


def _attn_kernel(qn_ref, qr_ref, gt_ref, ck_ref, cvt_ref, ksel_ref, kwin_ref, vselt_ref, vwint_ref, et_ref,
                 o_ref, pg_s, mask_s, *, tq, n_cmp):
    hp_n = HEADS_PER_KV
    qt = pl.program_id(2)
    t0 = qt * tq
    tpos = t0 + lax.broadcasted_iota(I32, (1, tq), 1)
    tpos4 = jnp.concatenate([tpos] * hp_n, axis=1)
    qn4 = jnp.concatenate([qn_ref[h * HEAD_DIM:(h + 1) * HEAD_DIM, :] for h in range(hp_n)], axis=1)
    qr4 = jnp.concatenate([qr_ref[h * HEAD_DIM:(h + 1) * HEAD_DIM, :] for h in range(hp_n)], axis=1)

    n_row = ck_ref.shape[2]
    s = _dot(ck_ref[0, 0], qn4)
    ci = lax.broadcasted_iota(I32, (n_row, 1), 0)
    vis = (ci * CMP_STRIDE + (CMP_BLOCK - 1) <= tpos4) & (ci < n_cmp)
    s = jnp.where(vis, s, NEG)
    p = jnp.where(vis, jnp.exp(s - jnp.max(s, axis=0, keepdims=True)), 0.0)
    p = p / jnp.maximum(jnp.sum(p, axis=0, keepdims=True), TINY)
    o_cmp = _dot(cvt_ref[0, 0], p.astype(BF16))
    pg = p[:, 0:tq]
    for h in range(1, hp_n):
        pg = pg + p[:, h * tq:(h + 1) * tq]

    n_sb = et_ref.shape[1]
    ratio = SEL_BLOCK // CMP_STRIDE
    pg_s[0:8, :] = jnp.zeros((8, tq), F32)
    pg_s[8:8 + n_row, :] = pg
    score = pg_s[pl.ds(8 + 1 - CMP_BLOCK // CMP_STRIDE, n_sb, stride=ratio), :]
    for o in range(2 - CMP_BLOCK // CMP_STRIDE, ratio):
        score = score + pg_s[pl.ds(8 + o, n_sb, stride=ratio), :]
    jrow = lax.broadcasted_iota(I32, (n_sb, 1), 0)
    cur = tpos // SEL_BLOCK
    forced = (jrow == 0) | (jrow == cur) | (jrow == cur - 1)
    sc = jnp.where(jrow * SEL_BLOCK <= tpos, score + jnp.where(forced, BIG, 0.0), -BIG)
    rank = jnp.zeros((n_sb, tq), I32)
    for jp in range(n_sb):
        row = sc[jp:jp + 1, :]
        beats = (row > sc) | ((row == sc) & (jrow > jp))
        rank = rank + beats.astype(I32)
    sel = jnp.where(rank < min(N_SEL, n_sb), 1.0, 0.0).astype(BF16)
    mask_s[...] = _dot(et_ref[...], sel)

    def attend(k_ref, vt_ref, kt_lo, kt_hi, mask_fn):
        def body(kt, carry):
            m, l, acc = carry
            k0 = pl.multiple_of(kt * tq, tq)
            sk = _dot(k_ref[0, 0, pl.ds(k0, tq), :], qr4)
            kpos = k0 + lax.broadcasted_iota(I32, (tq, 1), 0)
            mk = mask_fn(k0, kpos)
            mk4 = jnp.concatenate([mk] * hp_n, axis=1) > 0.5
            sk = jnp.where(mk4, sk, NEG)
            m_new = jnp.maximum(m, jnp.max(sk, axis=0, keepdims=True))
            alpha = jnp.exp(m - m_new)
            pk = jnp.where(mk4, jnp.exp(sk - m_new), 0.0)
            l = l * alpha + jnp.sum(pk, axis=0, keepdims=True)
            acc = acc * alpha + _dot(vt_ref[:, pl.ds(k0, tq)], pk.astype(BF16))
            return m_new, l, acc

        init = (jnp.full((1, hp_n * tq), NEG, F32), jnp.zeros((1, hp_n * tq), F32),
                jnp.zeros((HEAD_DIM, hp_n * tq), F32))
        _, l, acc = lax.fori_loop(kt_lo, kt_hi, body, init)
        return acc / jnp.maximum(l, TINY)

    def sel_mask(k0, kpos):
        return jnp.where(kpos <= tpos, mask_s[pl.ds(k0, tq), :], 0.0)

    def win_mask(k0, kpos):
        dq = tpos - kpos
        return jnp.where((dq >= 0) & (dq < WINDOW), 1.0, 0.0)

    o_sel = attend(ksel_ref, vselt_ref, 0, qt + 1, sel_mask)
    o_win = attend(kwin_ref, vwint_ref, jnp.maximum(qt - WINDOW // tq, 0), qt + 1, win_mask)

    gt = gt_ref[...]
    for h in range(hp_n):
        sl = slice(h * tq, (h + 1) * tq)
        o = (gt[h:h + 1] * o_cmp[:, sl] + gt[hp_n + h:hp_n + h + 1] * o_sel[:, sl]
             + gt[2 * hp_n + h:2 * hp_n + h + 1] * o_win[:, sl])
        o_ref[h * HEAD_DIM:(h + 1) * HEAD_DIM, :] = o.astype(BF16)


def _attn_prompt(qn_t, qr_t, g_t, ck, cv_t, ksel, kwin, vsel_t, vwin_t, *, b, t, tq):
    nq = t // tq
    n_sb = t // SEL_BLOCK
    n_row = ck.shape[2]
    n_cmp = t // CMP_STRIDE - CMP_BLOCK // CMP_STRIDE + 1
    et = (jnp.arange(t)[:, None] // SEL_BLOCK == jnp.arange(n_sb)[None, :]).astype(BF16)
    kw = N_KV_HEADS * HEAD_DIM
    return pl.pallas_call(
        functools.partial(_attn_kernel, tq=tq, n_cmp=n_cmp),
        grid=(b, N_KV_HEADS, nq),
        in_specs=[
            pl.BlockSpec((kw, tq), lambda i, g, q: (g, i * nq + q)),
            pl.BlockSpec((kw, tq), lambda i, g, q: (g, i * nq + q)),
            pl.BlockSpec((GATE_ROWS, tq), lambda i, g, q: (g, i * nq + q)),
            pl.BlockSpec((1, 1, n_row, HEAD_DIM), lambda i, g, q: (i, g, 0, 0)),
            pl.BlockSpec((1, 1, HEAD_DIM, n_row), lambda i, g, q: (i, g, 0, 0)),
            pl.BlockSpec((1, 1, t, HEAD_DIM), lambda i, g, q: (g, i, 0, 0)),
            pl.BlockSpec((1, 1, t, HEAD_DIM), lambda i, g, q: (g, i, 0, 0)),
            pl.BlockSpec((HEAD_DIM, t), lambda i, g, q: (g, i)),
            pl.BlockSpec((HEAD_DIM, t), lambda i, g, q: (g, i)),
            pl.BlockSpec(et.shape, lambda i, g, q: (0, 0)),
        ],
        out_specs=pl.BlockSpec((kw, tq), lambda i, g, q: (g, i * nq + q)),
        out_shape=jax.ShapeDtypeStruct((N_HEADS * HEAD_DIM, b * t), BF16),
        scratch_shapes=[pltpu.VMEM((n_row + 8, tq), F32), pltpu.VMEM((t, tq), F32)],
        compiler_params=_params(("arbitrary", "arbitrary", "arbitrary"), VMEM_LIMIT),
        name="attn_prompt",
    )(qn_t, qr_t, g_t, ck, cv_t, ksel.reshape(N_KV_HEADS, b, t, HEAD_DIM), kwin.reshape(N_KV_HEADS, b, t, HEAD_DIM),
      vsel_t, vwin_t, et)


def _oproj_t_kernel(ot_ref, h_ref, w_ref, out_ref, w_s):
    @pl.when(pl.program_id(0) == 0)
    def _():
        w_s[...] = w_ref[...].astype(BF16)

    o = ot_ref[...].astype(F32).T.astype(BF16)
    out_ref[...] = h_ref[...] + _dot(o, w_s[...])


def _oproj_t(o_t, h2, w_o, *, tt):
    n, d = h2.shape
    return pl.pallas_call(
        _oproj_t_kernel,
        grid=(n // tt,),
        in_specs=[
            pl.BlockSpec((o_t.shape[0], tt), lambda i: (0, i)),
            pl.BlockSpec((tt, d), lambda i: (i, 0)),
            pl.BlockSpec(w_o.shape, lambda i: (0, 0)),
        ],
        out_specs=pl.BlockSpec((tt, d), lambda i: (i, 0)),
        out_shape=jax.ShapeDtypeStruct((n, d), F32),
        scratch_shapes=[pltpu.VMEM(w_o.shape, BF16)],
        compiler_params=_params(("arbitrary",), VMEM_LIMIT),
        name="o_proj",
    )(o_t, h2, w_o)


def _oproj_n_kernel(o_ref, h_ref, w_ref, out_ref):
    out_ref[...] = h_ref[...] + _dot(o_ref[...].astype(BF16), w_ref[...].astype(BF16))


def _oproj_n(o, h2, w_o):
    n, d = h2.shape
    return pl.pallas_call(
        _oproj_n_kernel,
        out_shape=jax.ShapeDtypeStruct((n, d), F32),
        compiler_params=_params((), VMEM_LIMIT),
        name="o_proj_sample",
    )(o, h2, w_o)


def _attn_sample_kernel(pt_ref, *refs, n_pages, n_cmp, n_sb, ts, past, n_buf):
    del pt_ref
    page_refs = refs[:n_pages]
    (qn_ref, qr_ref, ckt_ref, cvt_ref, win_ref, knew_ref, vnew_ref, kwnew_ref, vwnew_ref, gate_ref,
     sel_ref, e_ref, o_ref, mask_s, m_s, l_s, acc_s, ocmp_s) = refs[n_pages:]
    q_step = pl.program_id(1)
    rows = HEADS_PER_KV * N_KV_HEADS * ts
    grp_rows = N_KV_HEADS * ts
    row = lax.broadcasted_iota(I32, (rows, 1), 0)
    qpos = past + row % ts
    qr = qr_ref[0]

    def tile_rows(x):
        return jnp.concatenate([x] * HEADS_PER_KV, axis=0)

    def online_update(s, mk, v_dot):
        s = jnp.where(mk, s, NEG)
        m_new = jnp.maximum(m_s[...], jnp.max(s, axis=-1, keepdims=True))
        alpha = jnp.exp(m_s[...] - m_new)
        p = jnp.where(mk, jnp.exp(s - m_new), 0.0)
        l_s[...] = l_s[...] * alpha + jnp.sum(p, axis=-1, keepdims=True)
        acc_s[...] = acc_s[...] * alpha + v_dot(p.astype(BF16))
        m_s[...] = m_new

    @pl.when(q_step == 0)
    def _():
        n_row = ckt_ref.shape[2]
        s = _dot(qn_ref[0], ckt_ref[0])
        ci = lax.broadcasted_iota(I32, (1, n_row), 1)
        vis = (ci * CMP_STRIDE + (CMP_BLOCK - 1) <= qpos) & (ci < n_cmp)
        s = jnp.where(vis, s, NEG)
        p = jnp.where(vis, jnp.exp(s - jnp.max(s, axis=-1, keepdims=True)), 0.0)
        p = p / jnp.maximum(jnp.sum(p, axis=-1, keepdims=True), TINY)
        ocmp_s[...] = _nt_dot(p.astype(BF16), cvt_ref[0])
        pg = p[0:grp_rows]
        for h in range(1, HEADS_PER_KV):
            pg = pg + p[h * grp_rows:(h + 1) * grp_rows]
        score = _dot(pg, sel_ref[...], precision=HIGHEST)
        width = score.shape[1]
        j = lax.broadcasted_iota(I32, (1, width), 1)
        tq = qpos[0:grp_rows]
        cur = tq // SEL_BLOCK
        forced = (j == 0) | (j == cur) | (j == cur - 1)
        sc = jnp.where(j * SEL_BLOCK <= tq, score + jnp.where(forced, BIG, 0.0), -BIG)
        sc = jnp.where(j < n_sb, sc, -2.0 * BIG)
        rank = jnp.zeros((grp_rows, width), I32)
        for jp in range(n_sb):
            col = sc[:, jp:jp + 1]
            beats = (col > sc) | ((col == sc) & (j > jp))
            rank = rank + beats.astype(I32)
        sel = jnp.where((rank < min(N_SEL, n_sb)) & (j < n_sb), 1.0, 0.0).astype(BF16)
        mask_s[...] = _dot(sel, e_ref[...])
        m_s[...] = jnp.full(m_s.shape, NEG, F32)
        l_s[...] = jnp.zeros(l_s.shape, F32)
        acc_s[...] = jnp.zeros(acc_s.shape, F32)

    width = n_pages * PAGE_SIZE
    kt = jnp.concatenate([page_refs[i][0, 0].reshape(KV_WIDTH, PAGE_SIZE) for i in range(n_pages)], axis=1)
    vt = jnp.concatenate([page_refs[i][0, 1].reshape(KV_WIDTH, PAGE_SIZE) for i in range(n_pages)], axis=1)
    k0 = pl.multiple_of(q_step * width, width)
    mk = tile_rows(mask_s[:, pl.ds(k0, width)]) > 0.5
    online_update(_dot(qr, kt.astype(BF16)), mk, lambda p: _nt_dot(p, vt.astype(BF16)))

    @pl.when(q_step == pl.num_programs(1) - 1)
    def _():
        lane = lax.broadcasted_iota(I32, (1, LANES), 1)
        new_ok = (lane < ts) & (past + lane <= qpos)
        mk_new = (tile_rows(mask_s[:, past:past + LANES]) > 0.5) & new_ok
        online_update(_nt_dot(qr, knew_ref[0]), mk_new, lambda p: _dot(p, vnew_ref[0]))
        o_sel = acc_s[...] / jnp.maximum(l_s[...], TINY)
        bi = lax.broadcasted_iota(I32, (1, n_buf), 1)
        dq = qpos - (past - n_buf + bi)
        ok_buf = (dq >= 0) & (dq < WINDOW)
        dq_new = qpos - (past + lane)
        ok_new = (lane < ts) & (dq_new >= 0) & (dq_new < WINDOW)
        s_w = jnp.concatenate([_dot(qr, win_ref[0, 0].astype(BF16)), _nt_dot(qr, kwnew_ref[0])], axis=1)
        ok = jnp.concatenate([jnp.broadcast_to(ok_buf, (rows, n_buf)), jnp.broadcast_to(ok_new, (rows, LANES))], axis=1)
        s_w = jnp.where(ok, s_w, NEG)
        p_w = jnp.where(ok, jnp.exp(s_w - jnp.max(s_w, axis=-1, keepdims=True)), 0.0)
        p_w = (p_w / jnp.maximum(jnp.sum(p_w, axis=-1, keepdims=True), TINY)).astype(BF16)
        o_win = _nt_dot(p_w[:, :n_buf], win_ref[0, 1].astype(BF16)) + _dot(p_w[:, n_buf:], vwnew_ref[0])
        row_g = (row // ts) % N_KV_HEADS

        def own_group(o):
            out = jnp.zeros((rows, HEAD_DIM), F32)
            for g in range(N_KV_HEADS):
                out = out + jnp.where(row_g == g, o[:, g * HEAD_DIM:(g + 1) * HEAD_DIM], 0.0)
            return out

        o_ref[0] = (gate_ref[0, 0] * own_group(ocmp_s[...]) + gate_ref[0, 1] * own_group(o_sel)
                    + gate_ref[0, 2] * own_group(o_win))


def _attn_sample(cache5, page_table, qn_bd, qr_bd, ck_t, cv_t, win4, knew, vnew, kwnew, vwnew, gates,
                 *, n_pages, ts, past):
    b, pages_per_seq = page_table.shape
    steps = pages_per_seq // n_pages
    rows = qn_bd.shape[1]
    n_chunk = ck_t.shape[2]
    n_cmp = n_chunk - CMP_BLOCK // CMP_STRIDE + 1
    n_keys = past + LANES
    n_sb = -(-(past + ts) // SEL_BLOCK)
    n_sb_pad = -(-n_sb // LANES) * LANES
    n_buf = win4.shape[3]
    ratio = SEL_BLOCK // CMP_STRIDE
    ci = jnp.arange(n_chunk)[:, None]
    jb = jnp.arange(n_sb_pad)[None, :]
    sel_map = ((ci >= ratio * jb + 1 - CMP_BLOCK // CMP_STRIDE) & (ci < ratio * jb + ratio)
               & (ci < n_cmp) & (jb < n_sb)).astype(F32)
    expand = (jnp.arange(n_keys)[None, :] // SEL_BLOCK == jnp.arange(n_sb_pad)[:, None]).astype(BF16)

    def page_spec(j):
        return pl.BlockSpec((1, 2, N_KV_HEADS, HEAD_DIM, PAGE_SIZE),
                            lambda i, q, pt: (pt[i, q * n_pages + j], 1, 0, 0, 0))

    per_seq = lambda a: pl.BlockSpec((1,) + a.shape[1:], lambda i, q, pt: (i,) + (0,) * (a.ndim - 1))
    full = lambda a: pl.BlockSpec(a.shape, lambda i, q, pt: (0,) * a.ndim)
    return pl.pallas_call(
        functools.partial(_attn_sample_kernel, n_pages=n_pages, n_cmp=n_cmp, n_sb=n_sb, ts=ts, past=past,
                          n_buf=n_buf),
        grid_spec=pltpu.PrefetchScalarGridSpec(
            num_scalar_prefetch=1,
            grid=(b, steps),
            in_specs=[page_spec(j) for j in range(n_pages)]
            + [per_seq(a) for a in (qn_bd, qr_bd, ck_t, cv_t, win4, knew, vnew, kwnew, vwnew, gates)]
            + [full(sel_map), full(expand)],
            out_specs=pl.BlockSpec((1, rows, HEAD_DIM), lambda i, q, pt: (i, 0, 0)),
            scratch_shapes=[
                pltpu.VMEM((N_KV_HEADS * ts, n_keys), F32),
                pltpu.VMEM((rows, 1), F32), pltpu.VMEM((rows, 1), F32),
                pltpu.VMEM((rows, KV_WIDTH), F32), pltpu.VMEM((rows, KV_WIDTH), F32),
            ],
        ),
        out_shape=jax.ShapeDtypeStruct((b, rows, HEAD_DIM), F32),
        compiler_params=_params(("arbitrary", "arbitrary"), VMEM_LIMIT),
        name="attn_sample",
    )(page_table, *([cache5] * n_pages), qn_bd, qr_bd, ck_t, cv_t, win4, knew, vnew, kwnew, vwnew, gates,
      sel_map, expand)


def _router_weights(router_group, router_expert):
    rt = jnp.concatenate([router_group, router_expert], axis=1).T
    return jnp.pad(rt, ((0, 24 - rt.shape[0]), (0, 0)))


def _trunk_prompt(x, p, *, tt_pool=256, tm=1024, sub=256, tt=512, tq=128):
    (norm_mix, norm_ffn, pool_w, pool_scale, kv_norm, w_kv, k_norm, cmp_pe, cmp_w1, cmp_b1, cmp_w2,
     w_qg, q_norm, w_o, router_group, router_expert, w_gate_up, w_down) = p
    b, t, d = x.shape
    n = b * t
    pos = jnp.arange(t)
    h, new_pool = _pool_layer(x, jnp.zeros((b, POOL_HALO, d), F32), norm_mix[0], pool_w[0], pool_scale[0],
                              tt=tt_pool, clip=True)
    h = _moe(h.reshape(n, d), norm_ffn[0], _router_weights(router_group[0], router_expert[0]),
             w_gate_up, w_down, 0, tm=tm, sub=sub)
    cos_t, sin_t = _rope_tables_transposed(pos)
    kv_t, win_t, craw, ksel, kwin, vsel_t, vwin_t = _kvproj(h.reshape(b, t, d), kv_norm, w_kv, k_norm,
                                                            cos_t, sin_t, tt=tt)
    ab = _cmp_ab_prompt(craw, _cmp_weights(cmp_w1), b=b, t=t)
    ck, cv = _cmp_finish(ab, cmp_pe, cmp_w1, cmp_b1, cmp_w2, k_norm[0])
    qn_t, qr_t, g_t = _qproj(h, norm_mix[1], _qg_weights(w_qg[0]), q_norm[0], cos_t, sin_t,
                             tt=tt, pos_blocks=t // tt)
    o_t = _attn_prompt(qn_t, qr_t, g_t, ck, jnp.swapaxes(cv, 2, 3), ksel, kwin, vsel_t, vwin_t, b=b, t=t, tq=tq)
    h = _oproj_t(o_t, h, w_o[0], tt=tt)
    h = _moe(h, norm_ffn[1], _router_weights(router_group[1], router_expert[1]),
             w_gate_up, w_down, 1, tm=tm, sub=sub)
    n_win = min(WINDOW, t)
    kv_new = kv_t.reshape(b, 4, N_KV_HEADS, HEAD_DIM, t).transpose(0, 4, 1, 2, 3)
    win_new = win_t[:, :, t - n_win:].reshape(b, 2, N_KV_HEADS, HEAD_DIM, n_win).transpose(0, 4, 1, 2, 3)
    return h.reshape(b, t, d), new_pool[None], kv_new, win_new


def _trunk_sample(x, state_pool, cache_kv, page_table, state_win, p, *, n_pages=16):
    (norm_mix, norm_ffn, pool_w, pool_scale, kv_norm, w_kv, k_norm, cmp_pe, cmp_w1, cmp_b1, cmp_w2,
     w_qg, q_norm, w_o, router_group, router_expert, w_gate_up, w_down) = p
    b, ts, d = x.shape
    n = b * ts
    past = page_table.shape[1] * PAGE_SIZE
    n_buf = state_win.shape[1]
    prev16 = jnp.pad(state_pool[0], ((0, 0), (POOL_HALO - POOL_STATE, 0), (0, 0)))
    h, new_pool = _pool_layer(x, prev16, norm_mix[0], pool_w[0], pool_scale[0], tt=ts, clip=False)
    h = _moe(h.reshape(n, d), norm_ffn[0], _router_weights(router_group[0], router_expert[0]),
             w_gate_up, w_down, 0, tm=n, sub=n)
    cos_t, sin_t = _rope_tables_transposed(past + jnp.arange(ts))
    cos_t = jnp.tile(cos_t, (1, b))
    sin_t = jnp.tile(sin_t, (1, b))
    kv_t, win_t, _, _, _, _, _ = _kvproj(h.reshape(1, n, d), kv_norm, w_kv, k_norm, cos_t, sin_t, tt=n)
    kv_rows = kv_t[0].T
    win_rows = win_t[0].T

    cache5 = cache_kv.transpose(0, 2, 3, 4, 1)
    ab = _cmp_ab_pages(cache5, page_table, _cmp_weights(cmp_w1), n_pages=n_pages)
    ck, cv = _cmp_finish(ab, cmp_pe, cmp_w1, cmp_b1, cmp_w2, k_norm[0])
    n_chunk = ck.shape[2]
    ck_t = ck.transpose(0, 1, 3, 2).reshape(b, KV_WIDTH, n_chunk)
    cv_t = cv.transpose(0, 1, 3, 2).reshape(b, KV_WIDTH, n_chunk)

    qn_t, qr_t, g_t = _qproj(h, norm_mix[1], _qg_weights(w_qg[0]), q_norm[0], cos_t, sin_t, tt=n, pos_blocks=1)

    def block_diag_queries(q_t):
        q5 = q_t.reshape(N_KV_HEADS, HEADS_PER_KV, HEAD_DIM, b, ts).transpose(3, 1, 0, 4, 2)
        eye = jnp.eye(N_KV_HEADS, dtype=q_t.dtype)
        qbd = q5[:, :, :, :, None, :] * eye[None, None, :, None, :, None]
        return qbd.reshape(b, HEADS_PER_KV * N_KV_HEADS * ts, KV_WIDTH)

    gates = g_t.reshape(N_KV_HEADS, GATE_ROWS, b, ts)[:, :3 * HEADS_PER_KV]
    gates = gates.reshape(N_KV_HEADS, 3, HEADS_PER_KV, b, ts).transpose(3, 1, 2, 0, 4)
    gates = jnp.broadcast_to(gates.reshape(b, 3, HEADS_PER_KV * N_KV_HEADS * ts, 1),
                             (b, 3, HEADS_PER_KV * N_KV_HEADS * ts, HEAD_DIM))

    def new_rows(rows2):
        return jnp.pad(rows2.reshape(b, ts, KV_WIDTH), ((0, 0), (0, LANES - ts), (0, 0))).astype(BF16)

    win4 = state_win.transpose(0, 2, 3, 4, 1).reshape(b, 2, KV_WIDTH, n_buf)
    o = _attn_sample(cache5, page_table, block_diag_queries(qn_t), block_diag_queries(qr_t), ck_t, cv_t, win4,
                     new_rows(kv_rows[:, 2 * KV_WIDTH:3 * KV_WIDTH]), new_rows(kv_rows[:, 3 * KV_WIDTH:]),
                     new_rows(win_rows[:, :KV_WIDTH]), new_rows(win_rows[:, KV_WIDTH:]), gates,
                     n_pages=n_pages, ts=ts, past=past)
    o = o.reshape(b, HEADS_PER_KV, N_KV_HEADS, ts, HEAD_DIM).transpose(0, 3, 2, 1, 4).reshape(n, N_HEADS * HEAD_DIM)
    h = _oproj_n(o, h, w_o[0])
    h = _moe(h, norm_ffn[1], _router_weights(router_group[1], router_expert[1]),
             w_gate_up, w_down, 1, tm=n, sub=n)
    kv_new = kv_rows.reshape(b, ts, 4, N_KV_HEADS, HEAD_DIM)
    win_new = jnp.concatenate([state_win, win_rows.reshape(b, ts, 2, N_KV_HEADS, HEAD_DIM)], axis=1)[:, -n_buf:]
    return h.reshape(b, ts, d), new_pool[None], kv_new, win_new


def kernel(x_prompt, x_sample, state_pool, cache_kv, page_table, state_win, norm_mix, norm_ffn, pool_w, pool_scale, kv_norm, w_kv, k_norm, cmp_pe, cmp_w1, cmp_b1, cmp_w2, w_qg, q_norm, w_o, router_group, router_expert, w_gate_up, w_down):
    params = (norm_mix, norm_ffn, pool_w, pool_scale, kv_norm, w_kv, k_norm, cmp_pe, cmp_w1, cmp_b1,
              cmp_w2, w_qg, q_norm, w_o, router_group, router_expert, w_gate_up, w_down)
    y_p, pool_p, kv_p, win_p = _trunk_prompt(x_prompt, params)
    y_s, pool_s, kv_s, win_s = _trunk_sample(x_sample, state_pool, cache_kv, page_table, state_win, params)
    return y_p, y_s, pool_p, pool_s, kv_p, kv_s, win_p, win_s
```

```python
import functools

import jax
import jax.numpy as jnp
from jax import lax
from jax.experimental import pallas as pl
from jax.experimental.pallas import tpu as pltpu

F32 = jnp.float32
BF16 = jnp.bfloat16
I32 = jnp.int32
HIGHEST = lax.Precision.HIGHEST

D_MODEL = 1024
POOL_WINDOWS = (2, 4, 8, 16)
POOL_GROUP_DIM = D_MODEL // len(POOL_WINDOWS)
POOL_STATE = max(POOL_WINDOWS) - 1
POOL_HALO = 16
N_HEADS = 16
HEAD_DIM = 64
HALF_DIM = HEAD_DIM // 2
N_KV_HEADS = 4
HEADS_PER_KV = N_HEADS // N_KV_HEADS
KV_WIDTH = N_KV_HEADS * HEAD_DIM
CMP_BLOCK = 32
CMP_STRIDE = 16
CMP_HIDDEN = 2 * HEAD_DIM
SEL_BLOCK = 64
N_SEL = 16
WINDOW = 512
PAGE_SIZE = 128
ROPE_THETA = 10000.0
SCALE = HEAD_DIM ** -0.5
N_GROUPS = 4
EXPERTS_PER_GROUP = 4
N_EXPERTS = N_GROUPS * EXPERTS_PER_GROUP
D_EXPERT = 512
EPS = 1e-6
NEG = -1e30
TINY = 1e-30
BIG = 1e4

LANES = 128
GATE_ROWS = 16
VMEM_LIMIT = 56 * 1024 * 1024


def _params(sem, vmem=None):
    return pltpu.CompilerParams(dimension_semantics=sem, vmem_limit_bytes=vmem)


def _rms(x, g):
    return x * lax.rsqrt(jnp.mean(x * x, axis=-1, keepdims=True) + EPS) * g


def _nt_dot(a, b, precision=None):
    return lax.dot_general(a, b, (((1,), (1,)), ((), ())), precision=precision,
                           preferred_element_type=F32)


def _dot(a, b, precision=None):
    return jnp.dot(a, b, precision=precision, preferred_element_type=F32)


def _sigmoid(x):
    return 1.0 / (1.0 + jnp.exp(-x))


def _pool_kernel(h_ref, prev_ref, g_ref, w_ref, sc_ref, o_ref, np_ref, ext_ref, *, tt, clip):
    t = pl.program_id(1)
    x = h_ref[0]
    xn = _rms(x, g_ref[...])

    @pl.when(t == 0)
    def _():
        ext_ref[0:POOL_HALO, :] = prev_ref[0]

    @pl.when(t > 0)
    def _():
        ext_ref[0:POOL_HALO, :] = ext_ref[tt:tt + POOL_HALO, :]

    ext_ref[POOL_HALO:POOL_HALO + tt, :] = xn
    if clip:
        tpos = t * tt + lax.broadcasted_iota(I32, (tt, 1), 0)
    outs = []
    for gi, w in enumerate(POOL_WINDOWS):
        lo = gi * POOL_GROUP_DIM
        hi = lo + POOL_GROUP_DIM
        acc = ext_ref[POOL_HALO:POOL_HALO + tt, lo:hi]
        for j in range(1, w):
            acc = acc + ext_ref[POOL_HALO - j:POOL_HALO - j + tt, lo:hi]
        if clip:
            mean = acc / jnp.minimum(tpos + 1, w).astype(F32)
        else:
            mean = acc * (1.0 / w)
        d = (mean - xn[:, lo:hi]).astype(BF16)
        outs.append(_dot(d, w_ref[gi].astype(BF16)))
    o_ref[0] = x + jnp.concatenate(outs, axis=1) * sc_ref[...]

    @pl.when(t == pl.num_programs(1) - 1)
    def _():
        np_ref[0] = ext_ref[tt + POOL_HALO - POOL_STATE:tt + POOL_HALO, :]


def _pool_layer(h, prev16, gain, w_pool, scale, *, tt, clip):
    b, t, d = h.shape
    return pl.pallas_call(
        functools.partial(_pool_kernel, tt=tt, clip=clip),
        grid=(b, t // tt),
        in_specs=[
            pl.BlockSpec((1, tt, d), lambda i, j: (i, j, 0)),
            pl.BlockSpec((1, POOL_HALO, d), lambda i, j: (i, 0, 0)),
            pl.BlockSpec((1, d), lambda i, j: (0, 0)),
            pl.BlockSpec(w_pool.shape, lambda i, j: (0, 0, 0)),
            pl.BlockSpec((1, d), lambda i, j: (0, 0)),
        ],
        out_specs=[
            pl.BlockSpec((1, tt, d), lambda i, j: (i, j, 0)),
            pl.BlockSpec((1, POOL_STATE, d), lambda i, j: (i, 0, 0)),
        ],
        out_shape=[jax.ShapeDtypeStruct((b, t, d), F32),
                   jax.ShapeDtypeStruct((b, POOL_STATE, d), F32)],
        scratch_shapes=[pltpu.VMEM((tt + POOL_HALO, d), F32)],
        compiler_params=_params(("arbitrary", "arbitrary")),
        name="pool_layer",
    )(h, prev16, gain.reshape(1, d), w_pool, scale.reshape(1, d))


def _softmax_rows(rows):
    m = functools.reduce(jnp.maximum, rows)
    es = [jnp.exp(r - m) for r in rows]
    s = functools.reduce(lambda a, b: a + b, es)
    return [e / s for e in es]


def _router_kernel(h_ref, g_ref, rt_ref, xn_ref, comb_ref):
    xn = _rms(h_ref[...], g_ref[...])
    xn_ref[...] = xn.astype(BF16)
    lt = _nt_dot(rt_ref[...], xn, precision=HIGHEST)
    pg = _softmax_rows([lt[i:i + 1, :] for i in range(N_GROUPS)])
    g_val = functools.reduce(jnp.maximum, pg)
    g_idx = jnp.full(g_val.shape, N_GROUPS - 1, I32)
    for i in range(N_GROUPS - 2, -1, -1):
        g_idx = jnp.where(pg[i] == g_val, i, g_idx)
    le = []
    for j in range(EXPERTS_PER_GROUP):
        v = lt[N_GROUPS + (N_GROUPS - 1) * EXPERTS_PER_GROUP + j:N_GROUPS + (N_GROUPS - 1) * EXPERTS_PER_GROUP + j + 1, :]
        for gi in range(N_GROUPS - 2, -1, -1):
            r = N_GROUPS + gi * EXPERTS_PER_GROUP + j
            v = jnp.where(g_idx == gi, lt[r:r + 1, :], v)
        le.append(v)
    pe = _softmax_rows(le)
    ranks = []
    for j in range(EXPERTS_PER_GROUP):
        r = jnp.zeros(g_val.shape, I32)
        for i in range(EXPERTS_PER_GROUP):
            if i == j:
                continue
            beats = (pe[i] > pe[j]) | (pe[i] == pe[j]) if i < j else (pe[i] > pe[j])
            r = r + beats.astype(I32)
        ranks.append(r)
    vals, idxs = [], []
    for k in range(2):
        v = jnp.zeros(g_val.shape, F32)
        ix = jnp.zeros(g_val.shape, I32)
        for j in range(EXPERTS_PER_GROUP):
            hit = ranks[j] == k
            v = jnp.where(hit, pe[j], v)
            ix = jnp.where(hit, j, ix)
        vals.append(v)
        idxs.append(ix)
    tot = vals[0] + vals[1]
    erow = lax.broadcasted_iota(I32, (LANES, g_val.shape[1]), 0)
    comb_t = jnp.zeros(erow.shape, F32)
    for k in range(2):
        wk = g_val * (vals[k] / tot)
        comb_t = comb_t + jnp.where(erow == g_idx * EXPERTS_PER_GROUP + idxs[k], wk, 0.0)
    comb_ref[...] = comb_t.T


def _router(h2, gain, rt, *, tm):
    n, d = h2.shape
    return pl.pallas_call(
        _router_kernel,
        grid=(n // tm,),
        in_specs=[
            pl.BlockSpec((tm, d), lambda i: (i, 0)),
            pl.BlockSpec((1, d), lambda i: (0, 0)),
            pl.BlockSpec(rt.shape, lambda i: (0, 0)),
        ],
        out_specs=[
            pl.BlockSpec((tm, d), lambda i: (i, 0)),
            pl.BlockSpec((tm, LANES), lambda i: (i, 0)),
        ],
        out_shape=[jax.ShapeDtypeStruct((n, d), BF16), jax.ShapeDtypeStruct((n, LANES), F32)],
        compiler_params=_params(("arbitrary",)),
        name="moe_router",
    )(h2, gain.reshape(1, d), rt)


def _moe_kernel(x_ref, comb_ref, h_ref, wgu_ref, wd_ref, o_ref, wg_s, wd_s, *, sub):
    e = pl.program_id(1)
    wg_s[...] = wgu_ref[0, 0].astype(BF16)
    wd_s[...] = wd_ref[0, 0].astype(BF16)

    @pl.when(e == 0)
    def _():
        o_ref[...] = h_ref[...]

    lane = lax.broadcasted_iota(I32, (sub, LANES), 1)

    def body(i, carry):
        r0 = pl.multiple_of(i * sub, sub)
        gu = _dot(x_ref[pl.ds(r0, sub), :], wg_s[...])
        a = gu[:, :D_EXPERT]
        hdn = (a * _sigmoid(a) * gu[:, D_EXPERT:]).astype(BF16)
        y = _dot(hdn, wd_s[...])
        c = jnp.sum(jnp.where(lane == e, comb_ref[pl.ds(r0, sub), :], 0.0), axis=-1, keepdims=True)
        o_ref[pl.ds(r0, sub), :] += c * y
        return carry

    lax.fori_loop(0, x_ref.shape[0] // sub, body, 0)


def _moe(h2, gain, rt, w_gu, w_down, layer, *, tm, sub):
    n, d = h2.shape
    xn, comb = _router(h2, gain, rt, tm=tm)
    return pl.pallas_call(
        functools.partial(_moe_kernel, sub=sub),
        grid=(n // tm, N_EXPERTS),
        in_specs=[
            pl.BlockSpec((tm, d), lambda i, e: (i, 0)),
            pl.BlockSpec((tm, LANES), lambda i, e: (i, 0)),
            pl.BlockSpec((tm, d), lambda i, e: (i, 0)),
            pl.BlockSpec((1, 1, d, 2 * D_EXPERT), lambda i, e: (layer, e, 0, 0)),
            pl.BlockSpec((1, 1, D_EXPERT, d), lambda i, e: (layer, e, 0, 0)),
        ],
        out_specs=pl.BlockSpec((tm, d), lambda i, e: (i, 0)),
        out_shape=jax.ShapeDtypeStruct((n, d), F32),
        scratch_shapes=[pltpu.VMEM((d, 2 * D_EXPERT), BF16), pltpu.VMEM((D_EXPERT, d), BF16)],
        compiler_params=_params(("arbitrary", "arbitrary"), VMEM_LIMIT),
        name="moe_experts",
    )(xn, comb, h2, w_gu, w_down)


def _head_norm_rope_t(z, gain, c, s):
    outs = []
    for g in range(z.shape[0] // HEAD_DIM):
        zh = z[g * HEAD_DIM:(g + 1) * HEAD_DIM]
        zn = zh * lax.rsqrt(jnp.mean(zh * zh, axis=0, keepdims=True) + EPS) * gain
        x1 = zn[:HALF_DIM]
        x2 = zn[HALF_DIM:]
        outs.append(x1 * c - x2 * s)
        outs.append(x2 * c + x1 * s)
    return jnp.concatenate(outs, axis=0)


def _kvproj_kernel(h_ref, g_ref, wt_ref, gsel_ref, gwin_ref, cos_ref, sin_ref,
                   kvt_ref, wint_ref, craw_ref, ksel_ref, kwin_ref, vselt_ref, vwint_ref, w_s):
    @pl.when((pl.program_id(0) == 0) & (pl.program_id(1) == 0))
    def _():
        w_s[...] = wt_ref[...].astype(BF16)

    xn = _rms(h_ref[0], g_ref[...]).astype(BF16)
    kvt = _nt_dot(w_s[...], xn)
    tt = xn.shape[0]
    c = cos_ref[...]
    s = sin_ref[...]
    gsel = jnp.concatenate([gsel_ref[...]] * (tt // LANES), axis=1)
    gwin = jnp.concatenate([gwin_ref[...]] * (tt // LANES), axis=1)
    ksel = _head_norm_rope_t(kvt[2 * KV_WIDTH:3 * KV_WIDTH], gsel, c, s)
    kwin = _head_norm_rope_t(kvt[4 * KV_WIDTH:5 * KV_WIDTH], gwin, c, s)
    kvt_ref[0] = jnp.concatenate([kvt[:2 * KV_WIDTH], ksel, kvt[3 * KV_WIDTH:4 * KV_WIDTH]], axis=0)
    wint_ref[0] = jnp.concatenate([kwin, kvt[5 * KV_WIDTH:]], axis=0)
    vselt_ref[...] = kvt[3 * KV_WIDTH:4 * KV_WIDTH].astype(BF16)
    vwint_ref[...] = kvt[5 * KV_WIDTH:].astype(BF16)
    raw = kvt[:2 * KV_WIDTH].T
    for k in range(2 * KV_WIDTH // LANES):
        craw_ref[k] = raw[:, k * LANES:(k + 1) * LANES]
    ksel_n = ksel.T
    kwin_n = kwin.T
    for g in range(N_KV_HEADS):
        ksel_ref[g] = ksel_n[:, g * HEAD_DIM:(g + 1) * HEAD_DIM].astype(BF16)
        kwin_ref[g] = kwin_n[:, g * HEAD_DIM:(g + 1) * HEAD_DIM].astype(BF16)


def _kvproj(x3, gain, w_kv, k_norm, cos_tt, sin_tt, *, tt):
    b, t, d = x3.shape
    n = b * t
    nt = t // tt
    wt = w_kv.T
    gsel = jnp.broadcast_to(k_norm[1][:, None], (HEAD_DIM, LANES))
    gwin = jnp.broadcast_to(k_norm[2][:, None], (HEAD_DIM, LANES))
    full = lambda a: pl.BlockSpec(a.shape, lambda i, j: (0,) * a.ndim)
    return pl.pallas_call(
        _kvproj_kernel,
        grid=(b, nt),
        in_specs=[
            pl.BlockSpec((1, tt, d), lambda i, j: (i, j, 0)),
            pl.BlockSpec((1, d), lambda i, j: (0, 0)),
            full(wt), full(gsel), full(gwin),
            pl.BlockSpec((HALF_DIM, tt), lambda i, j: (0, j)),
            pl.BlockSpec((HALF_DIM, tt), lambda i, j: (0, j)),
        ],
        out_specs=[
            pl.BlockSpec((1, 4 * KV_WIDTH, tt), lambda i, j: (i, 0, j)),
            pl.BlockSpec((1, 2 * KV_WIDTH, tt), lambda i, j: (i, 0, j)),
            pl.BlockSpec((2 * KV_WIDTH // LANES, tt, LANES), lambda i, j: (0, i * nt + j, 0)),
            pl.BlockSpec((N_KV_HEADS, tt, HEAD_DIM), lambda i, j: (0, i * nt + j, 0)),
            pl.BlockSpec((N_KV_HEADS, tt, HEAD_DIM), lambda i, j: (0, i * nt + j, 0)),
            pl.BlockSpec((KV_WIDTH, tt), lambda i, j: (0, i * nt + j)),
            pl.BlockSpec((KV_WIDTH, tt), lambda i, j: (0, i * nt + j)),
        ],
        out_shape=[
            jax.ShapeDtypeStruct((b, 4 * KV_WIDTH, t), F32),
            jax.ShapeDtypeStruct((b, 2 * KV_WIDTH, t), F32),
            jax.ShapeDtypeStruct((2 * KV_WIDTH // LANES, n, LANES), F32),
            jax.ShapeDtypeStruct((N_KV_HEADS, n, HEAD_DIM), BF16),
            jax.ShapeDtypeStruct((N_KV_HEADS, n, HEAD_DIM), BF16),
            jax.ShapeDtypeStruct((KV_WIDTH, n), BF16),
            jax.ShapeDtypeStruct((KV_WIDTH, n), BF16),
        ],
        scratch_shapes=[pltpu.VMEM(wt.shape, BF16)],
        compiler_params=_params(("arbitrary", "arbitrary"), VMEM_LIMIT),
        name="kv_proj",
    )(x3, gain.reshape(1, d), wt, gsel, gwin, cos_tt, sin_tt)


def _rope_tables_transposed(pos):
    inv = 1.0 / (ROPE_THETA ** (jnp.arange(HALF_DIM, dtype=F32) * (2.0 / HEAD_DIM)))
    ang = pos.astype(F32)[:, None] * inv[None, :]
    return jnp.cos(ang).T, jnp.sin(ang).T


def _cmp_ab_accumulate(load_rows, wab_ref, n_rows):
    accs = [[jnp.zeros((n_rows, 2 * CMP_HIDDEN), F32) for _ in range(N_KV_HEADS)] for _ in range(2)]
    heads_per_chunk = LANES // HEAD_DIM
    for r in range(CMP_STRIDE):
        for s in range(2):
            w = wab_ref[s, r].astype(BF16)
            for c in range(KV_WIDTH // LANES):
                xr = load_rows(r, s * (KV_WIDTH // LANES) + c)
                for k in range(heads_per_chunk):
                    g = c * heads_per_chunk + k
                    accs[s][g] = accs[s][g] + _dot(xr[:, k * HEAD_DIM:(k + 1) * HEAD_DIM].astype(BF16), w)
    return accs


def _cmpab_kernel(craw_ref, wab_ref, ab_ref, *, n_chunk):
    accs = _cmp_ab_accumulate(lambda r, c: craw_ref[c, pl.ds(r, n_chunk, stride=CMP_STRIDE), :], wab_ref, n_chunk)
    for s in range(2):
        for g in range(N_KV_HEADS):
            ab_ref[0, s, g] = accs[s][g]


def _cmpab_pages_kernel(pt_ref, *refs, n_pages):
    del pt_ref
    page_refs = refs[:n_pages]
    wab_ref, ab_ref, craw_s = refs[n_pages:]
    pairs = KV_WIDTH // LANES
    for j in range(n_pages):
        for s in range(2):
            for k in range(pairs):
                tile = jnp.concatenate([page_refs[j][0, s, 2 * k], page_refs[j][0, s, 2 * k + 1]], axis=0)
                craw_s[s * pairs + k, j * PAGE_SIZE:(j + 1) * PAGE_SIZE, :] = tile.T
    n_chunk = n_pages * PAGE_SIZE // CMP_STRIDE
    accs = _cmp_ab_accumulate(lambda r, c: craw_s[c, pl.ds(r, n_chunk, stride=CMP_STRIDE), :], wab_ref, n_chunk)
    for s in range(2):
        for g in range(N_KV_HEADS):
            ab_ref[0, s, g] = accs[s][g]


def _cmp_ab_pages(cache5, page_table, wab, *, n_pages):
    b, pages_per_seq = page_table.shape
    steps = pages_per_seq // n_pages
    rows = n_pages * PAGE_SIZE // CMP_STRIDE

    def page_spec(j):
        return pl.BlockSpec((1, 2, N_KV_HEADS, HEAD_DIM, PAGE_SIZE),
                            lambda i, q, pt: (pt[i, q * n_pages + j], 0, 0, 0, 0))

    return pl.pallas_call(
        functools.partial(_cmpab_pages_kernel, n_pages=n_pages),
        grid_spec=pltpu.PrefetchScalarGridSpec(
            num_scalar_prefetch=1,
            grid=(b, steps),
            in_specs=[page_spec(j) for j in range(n_pages)]
            + [pl.BlockSpec(wab.shape, lambda i, q, pt: (0, 0, 0, 0))],
            out_specs=pl.BlockSpec((1, 2, N_KV_HEADS, rows, 2 * CMP_HIDDEN), lambda i, q, pt: (i, 0, 0, q, 0)),
            scratch_shapes=[pltpu.VMEM((2 * KV_WIDTH // LANES, n_pages * PAGE_SIZE, LANES), F32)],
        ),
        out_shape=jax.ShapeDtypeStruct((b, 2, N_KV_HEADS, steps * rows, 2 * CMP_HIDDEN), F32),
        compiler_params=_params(("arbitrary", "arbitrary"), VMEM_LIMIT),
        name="cmp_ab_pages",
    )(page_table, *([cache5] * n_pages), wab)


def _cmp_weights(cmp_w1):
    w = cmp_w1.reshape(2, 2, CMP_STRIDE, HEAD_DIM, CMP_HIDDEN)
    return w.transpose(0, 2, 3, 1, 4).reshape(2, CMP_STRIDE, HEAD_DIM, 2 * CMP_HIDDEN)


def _cmp_ab_prompt(craw, wab, *, b, t):
    n_chunk = t // CMP_STRIDE
    return pl.pallas_call(
        functools.partial(_cmpab_kernel, n_chunk=n_chunk),
        grid=(b,),
        in_specs=[
            pl.BlockSpec((craw.shape[0], t, LANES), lambda i: (0, i, 0)),
            pl.BlockSpec(wab.shape, lambda i: (0, 0, 0, 0)),
        ],
        out_specs=pl.BlockSpec((1, 2, N_KV_HEADS, n_chunk, 2 * CMP_HIDDEN), lambda i: (i, 0, 0, 0, 0)),
        out_shape=jax.ShapeDtypeStruct((b, 2, N_KV_HEADS, n_chunk, 2 * CMP_HIDDEN), F32),
        compiler_params=_params(("arbitrary",)),
        name="cmp_ab_prompt",
    )(craw, wab)


def _cmpfin_kernel(ab_ref, pe_ref, w1_ref, b1_ref, w2_ref, gk_ref, ck_ref, cv_ref, *, n_row):
    for s in range(2):
        bias = _dot(pe_ref[s].astype(BF16), w1_ref[s].astype(BF16)) + b1_ref[s]
        w2 = w2_ref[s].astype(BF16)
        for g in range(N_KV_HEADS):
            ab = ab_ref[0, s, g]
            hid = ab[:, :CMP_HIDDEN] + pltpu.roll(ab[:, CMP_HIDDEN:], n_row - 1, 0) + bias
            cdf = 0.5 * (1.0 + jnp.tanh(0.7978845608028654 * (hid + 0.044715 * (hid * hid * hid))))
            out = _dot((hid * cdf).astype(BF16), w2)
            if s == 0:
                ck_ref[0, g] = _rms(out, gk_ref[...]).astype(BF16)
            else:
                cv_ref[0, g] = out.astype(BF16)


def _cmp_finish(ab, cmp_pe, cmp_w1, cmp_b1, cmp_w2, gk):
    b = ab.shape[0]
    n_row = ab.shape[3]
    pe = cmp_pe.reshape(2, 1, CMP_BLOCK * HEAD_DIM)
    full = lambda a: pl.BlockSpec(a.shape, lambda i: (0,) * a.ndim)
    b1 = cmp_b1.reshape(2, 1, CMP_HIDDEN)
    gk2 = gk.reshape(1, HEAD_DIM)
    return pl.pallas_call(
        functools.partial(_cmpfin_kernel, n_row=n_row),
        grid=(b,),
        in_specs=[pl.BlockSpec((1,) + ab.shape[1:], lambda i: (i, 0, 0, 0, 0)),
                  full(pe), full(cmp_w1), full(b1), full(cmp_w2), full(gk2)],
        out_specs=[pl.BlockSpec((1, N_KV_HEADS, n_row, HEAD_DIM), lambda i: (i, 0, 0, 0)),
                   pl.BlockSpec((1, N_KV_HEADS, n_row, HEAD_DIM), lambda i: (i, 0, 0, 0))],
        out_shape=[jax.ShapeDtypeStruct((b, N_KV_HEADS, n_row, HEAD_DIM), BF16),
                   jax.ShapeDtypeStruct((b, N_KV_HEADS, n_row, HEAD_DIM), BF16)],
        compiler_params=_params(("arbitrary",)),
        name="cmp_finish",
    )(ab, pe, cmp_w1, b1, cmp_w2, gk2)


def _qproj_kernel(h_ref, g_ref, wt_ref, gq_ref, cos_ref, sin_ref, qn_ref, qr_ref, gt_ref, w_s):
    @pl.when(pl.program_id(0) == 0)
    def _():
        w_s[...] = wt_ref[...].astype(BF16)

    xn = _rms(h_ref[...], g_ref[...]).astype(BF16)
    qg = _nt_dot(w_s[...], xn)
    tt = xn.shape[0]
    gq = jnp.concatenate([gq_ref[...]] * (tt // LANES), axis=1)
    c = cos_ref[...]
    s = sin_ref[...]
    for h in range(N_HEADS):
        qh = qg[h * HEAD_DIM:(h + 1) * HEAD_DIM]
        qn = qh * lax.rsqrt(jnp.mean(qh * qh, axis=0, keepdims=True) + EPS) * gq
        qn_ref[h * HEAD_DIM:(h + 1) * HEAD_DIM, :] = (qn * SCALE).astype(BF16)
        x1 = qn[:HALF_DIM]
        x2 = qn[HALF_DIM:]
        qr_ref[h * HEAD_DIM:h * HEAD_DIM + HALF_DIM, :] = ((x1 * c - x2 * s) * SCALE).astype(BF16)
        qr_ref[h * HEAD_DIM + HALF_DIM:(h + 1) * HEAD_DIM, :] = ((x2 * c + x1 * s) * SCALE).astype(BF16)
    gt_ref[...] = _sigmoid(qg[N_HEADS * HEAD_DIM:])


def _qg_weights(w_qg):
    nq = N_HEADS * HEAD_DIM
    gates = w_qg[:, nq:].reshape(D_MODEL, 3, N_KV_HEADS, HEADS_PER_KV).transpose(2, 1, 3, 0)
    gates = gates.reshape(N_KV_HEADS, 3 * HEADS_PER_KV, D_MODEL)
    gates = jnp.pad(gates, ((0, 0), (0, GATE_ROWS - 3 * HEADS_PER_KV), (0, 0)))
    return jnp.concatenate([w_qg[:, :nq].T, gates.reshape(N_KV_HEADS * GATE_ROWS, D_MODEL)], axis=0)


def _qproj(h2, gain, wt, q_norm, cos_tt, sin_tt, *, tt, pos_blocks):
    n, d = h2.shape
    nq = N_HEADS * HEAD_DIM
    ng = N_KV_HEADS * GATE_ROWS
    gq = jnp.broadcast_to(q_norm[:, None], (HEAD_DIM, LANES))
    full = lambda a: pl.BlockSpec(a.shape, lambda i: (0,) * a.ndim)
    return pl.pallas_call(
        _qproj_kernel,
        grid=(n // tt,),
        in_specs=[
            pl.BlockSpec((tt, d), lambda i: (i, 0)),
            pl.BlockSpec((1, d), lambda i: (0, 0)),
            full(wt), full(gq),
            pl.BlockSpec((HALF_DIM, tt), lambda i: (0, i % pos_blocks)),
            pl.BlockSpec((HALF_DIM, tt), lambda i: (0, i % pos_blocks)),
        ],
        out_specs=[
            pl.BlockSpec((nq, tt), lambda i: (0, i)),
            pl.BlockSpec((nq, tt), lambda i: (0, i)),
            pl.BlockSpec((ng, tt), lambda i: (0, i)),
        ],
        out_shape=[jax.ShapeDtypeStruct((nq, n), BF16), jax.ShapeDtypeStruct((nq, n), BF16),
                   jax.ShapeDtypeStruct((ng, n), F32)],
        scratch_shapes=[pltpu.VMEM(wt.shape, BF16)],
        compiler_params=_params(("arbitrary",), VMEM_LIMIT),
        name="q_proj",
    )(h2, gain.reshape(1, d), wt, gq, cos_tt, sin_tt)


def _attn_kernel(qn_ref, qr_ref, gt_ref, ck_ref, cvt_ref, ksel_ref, kwin_ref, vselt_ref, vwint_ref, et_ref,
                 o_ref, pg_s, bias_s, *, tq, n_cmp):
    hp_n = HEADS_PER_KV
    qt = pl.program_id(2)
    t0 = qt * tq
    tpos = t0 + lax.broadcasted_iota(I32, (1, tq), 1)
    tpos4 = jnp.concatenate([tpos] * hp_n, axis=1)
    qn4 = jnp.concatenate([qn_ref[h * HEAD_DIM:(h + 1) * HEAD_DIM, :] for h in range(hp_n)], axis=1)
    qr4 = jnp.concatenate([qr_ref[h * HEAD_DIM:(h + 1) * HEAD_DIM, :] for h in range(hp_n)], axis=1)

    n_row = ck_ref.shape[2]
    s = _dot(ck_ref[0, 0], qn4)
    ci = lax.broadcasted_iota(I32, (n_row, 1), 0)
    vis = (ci * CMP_STRIDE + (CMP_BLOCK - 1) <= tpos4) & (ci < n_cmp)
    s = jnp.where(vis, s, NEG)
    p = jnp.where(vis, jnp.exp(s - jnp.max(s, axis=0, keepdims=True)), 0.0)
    p = p / jnp.maximum(jnp.sum(p, axis=0, keepdims=True), TINY)
    o_cmp = _dot(cvt_ref[0, 0], p.astype(BF16))
    pg = p[:, 0:tq]
    for h in range(1, hp_n):
        pg = pg + p[:, h * tq:(h + 1) * tq]

    n_sb = et_ref.shape[1]
    ratio = SEL_BLOCK // CMP_STRIDE
    scores = []
    for c in range(tq // LANES):
        pg_s[c, 0:8, :] = jnp.zeros((8, LANES), F32)
        pg_s[c, 8:8 + n_row, :] = pg[:, c * LANES:(c + 1) * LANES]
        sc_c = pg_s[c, pl.ds(8 + 1 - CMP_BLOCK // CMP_STRIDE, n_sb, stride=ratio), :]
        for o in range(2 - CMP_BLOCK // CMP_STRIDE, ratio):
            sc_c = sc_c + pg_s[c, pl.ds(8 + o, n_sb, stride=ratio), :]
        scores.append(sc_c)
    score = jnp.concatenate(scores, axis=1)
    jrow = lax.broadcasted_iota(I32, (n_sb, 1), 0)
    cur = tpos // SEL_BLOCK
    forced = (jrow == 0) | (jrow == cur) | (jrow == cur - 1)
    sc = jnp.where(jrow * SEL_BLOCK <= tpos, score + jnp.where(forced, BIG, 0.0), -BIG)
    rank = jnp.zeros((n_sb, tq), I32)
    for jp in range(n_sb):
        row = sc[jp:jp + 1, :]
        beats = (row > sc) | ((row == sc) & (jrow > jp))
        rank = rank + beats.astype(I32)
    sel_neg = jnp.where(rank < min(N_SEL, n_sb), 0.0, NEG).astype(BF16)
    bias_s[...] = _dot(et_ref[...], sel_neg)
    d0 = pl.multiple_of(t0, tq)
    kq_diag = lax.broadcasted_iota(I32, (tq, 1), 0) <= lax.broadcasted_iota(I32, (1, tq), 1)
    bias_s[pl.ds(d0, tq), :] = jnp.where(kq_diag, bias_s[pl.ds(d0, tq), :], NEG)

    def sel_body(kt, carry):
        m, l, acc = carry
        k0 = pl.multiple_of(kt * tq, tq)
        sk = _dot(ksel_ref[0, 0, pl.ds(k0, tq), :], qr4) + jnp.concatenate([bias_s[pl.ds(k0, tq), :]] * hp_n, axis=1)
        m_new = jnp.maximum(m, jnp.max(sk, axis=0, keepdims=True))
        alpha = jnp.exp(m - m_new)
        pk = jnp.exp(sk - m_new)
        l = l * alpha + jnp.sum(pk, axis=0, keepdims=True)
        acc = acc * alpha + _dot(vselt_ref[:, pl.ds(k0, tq)], pk.astype(BF16))
        return m_new, l, acc

    init = (jnp.full((1, hp_n * tq), NEG, F32), jnp.zeros((1, hp_n * tq), F32),
            jnp.zeros((HEAD_DIM, hp_n * tq), F32))
    _, l_sel, acc_sel = lax.fori_loop(0, qt + 1, sel_body, init)
    o_sel = acc_sel / jnp.maximum(l_sel, TINY)

    n_wt = WINDOW // tq + 1
    w0 = pl.multiple_of(jnp.clip(qt - (n_wt - 1), 0, pl.num_programs(2) - n_wt) * tq, tq)
    dq = tpos - (w0 + lax.broadcasted_iota(I32, (n_wt * tq, 1), 0))
    wbias = jnp.where((dq >= 0) & (dq < WINDOW), 0.0, NEG)
    sw = _dot(kwin_ref[0, 0, pl.ds(w0, n_wt * tq), :], qr4) + jnp.concatenate([wbias] * hp_n, axis=1)
    pw = jnp.exp(sw - jnp.max(sw, axis=0, keepdims=True))
    l_win = jnp.sum(pw, axis=0, keepdims=True)
    o_win = _dot(vwint_ref[:, pl.ds(w0, n_wt * tq)], pw.astype(BF16)) / jnp.maximum(l_win, TINY)

    gt = gt_ref[...]
    for h in range(hp_n):
        sl = slice(h * tq, (h + 1) * tq)
        o = (gt[h:h + 1] * o_cmp[:, sl] + gt[hp_n + h:hp_n + h + 1] * o_sel[:, sl]
             + gt[2 * hp_n + h:2 * hp_n + h + 1] * o_win[:, sl])
        o_ref[h * HEAD_DIM:(h + 1) * HEAD_DIM, :] = o.astype(BF16)


def _attn_prompt(qn_t, qr_t, g_t, ck, cv_t, ksel, kwin, vsel_t, vwin_t, *, b, t, tq):
    nq = t // tq
    n_sb = t // SEL_BLOCK
    n_row = ck.shape[2]
    n_cmp = t // CMP_STRIDE - CMP_BLOCK // CMP_STRIDE + 1
    et = (jnp.arange(t)[:, None] // SEL_BLOCK == jnp.arange(n_sb)[None, :]).astype(BF16)
    kw = N_KV_HEADS * HEAD_DIM
    return pl.pallas_call(
        functools.partial(_attn_kernel, tq=tq, n_cmp=n_cmp),
        grid=(b, N_KV_HEADS, nq),
        in_specs=[
            pl.BlockSpec((kw, tq), lambda i, g, q: (g, i * nq + q)),
            pl.BlockSpec((kw, tq), lambda i, g, q: (g, i * nq + q)),
            pl.BlockSpec((GATE_ROWS, tq), lambda i, g, q: (g, i * nq + q)),
            pl.BlockSpec((1, 1, n_row, HEAD_DIM), lambda i, g, q: (i, g, 0, 0)),
            pl.BlockSpec((1, 1, HEAD_DIM, n_row), lambda i, g, q: (i, g, 0, 0)),
            pl.BlockSpec((1, 1, t, HEAD_DIM), lambda i, g, q: (g, i, 0, 0)),
            pl.BlockSpec((1, 1, t, HEAD_DIM), lambda i, g, q: (g, i, 0, 0)),
            pl.BlockSpec((HEAD_DIM, t), lambda i, g, q: (g, i)),
            pl.BlockSpec((HEAD_DIM, t), lambda i, g, q: (g, i)),
            pl.BlockSpec(et.shape, lambda i, g, q: (0, 0)),
        ],
        out_specs=pl.BlockSpec((kw, tq), lambda i, g, q: (g, i * nq + q)),
        out_shape=jax.ShapeDtypeStruct((N_HEADS * HEAD_DIM, b * t), BF16),
        scratch_shapes=[pltpu.VMEM((tq // LANES, n_row + 8, LANES), F32), pltpu.VMEM((t, tq), F32)],
        compiler_params=_params(("arbitrary", "arbitrary", "arbitrary"), VMEM_LIMIT),
        name="attn_prompt",
    )(qn_t, qr_t, g_t, ck, cv_t, ksel.reshape(N_KV_HEADS, b, t, HEAD_DIM), kwin.reshape(N_KV_HEADS, b, t, HEAD_DIM),
      vsel_t, vwin_t, et)


def _oproj_t_kernel(ot_ref, h_ref, w_ref, out_ref, w_s):
    @pl.when(pl.program_id(0) == 0)
    def _():
        w_s[...] = w_ref[...].astype(BF16)

    o = ot_ref[...].astype(F32).T.astype(BF16)
    out_ref[...] = h_ref[...] + _dot(o, w_s[...])


def _oproj_t(o_t, h2, w_o, *, tt):
    n, d = h2.shape
    return pl.pallas_call(
        _oproj_t_kernel,
        grid=(n // tt,),
        in_specs=[
            pl.BlockSpec((o_t.shape[0], tt), lambda i: (0, i)),
            pl.BlockSpec((tt, d), lambda i: (i, 0)),
            pl.BlockSpec(w_o.shape, lambda i: (0, 0)),
        ],
        out_specs=pl.BlockSpec((tt, d), lambda i: (i, 0)),
        out_shape=jax.ShapeDtypeStruct((n, d), F32),
        scratch_shapes=[pltpu.VMEM(w_o.shape, BF16)],
        compiler_params=_params(("arbitrary",), VMEM_LIMIT),
        name="o_proj",
    )(o_t, h2, w_o)


def _oproj_n_kernel(o_ref, h_ref, w_ref, out_ref):
    out_ref[...] = h_ref[...] + _dot(o_ref[...].astype(BF16), w_ref[...].astype(BF16))


def _oproj_n(o, h2, w_o):
    n, d = h2.shape
    return pl.pallas_call(
        _oproj_n_kernel,
        out_shape=jax.ShapeDtypeStruct((n, d), F32),
        compiler_params=_params((), VMEM_LIMIT),
        name="o_proj_sample",
    )(o, h2, w_o)


def _attn_sample_kernel(pt_ref, *refs, n_pages, n_cmp, n_sb, ts, past, n_buf):
    del pt_ref
    page_refs = refs[:n_pages]
    (qn_ref, qr_ref, ckt_ref, cvt_ref, win_ref, knew_ref, vnew_ref, kwnew_ref, vwnew_ref, gate_ref,
     sel_ref, e_ref, o_ref, mask_s, m_s, l_s, acc_s, ocmp_s) = refs[n_pages:]
    q_step = pl.program_id(1)
    rows = HEADS_PER_KV * N_KV_HEADS * ts
    grp_rows = N_KV_HEADS * ts
    row = lax.broadcasted_iota(I32, (rows, 1), 0)
    qpos = past + row % ts
    qr = qr_ref[0]

    def tile_rows(x):
        return jnp.concatenate([x] * HEADS_PER_KV, axis=0)

    def online_update(s, mk, v_dot):
        s = jnp.where(mk, s, NEG)
        m_new = jnp.maximum(m_s[...], jnp.max(s, axis=-1, keepdims=True))
        alpha = jnp.exp(m_s[...] - m_new)
        p = jnp.where(mk, jnp.exp(s - m_new), 0.0)
        l_s[...] = l_s[...] * alpha + jnp.sum(p, axis=-1, keepdims=True)
        acc_s[...] = acc_s[...] * alpha + v_dot(p.astype(BF16))
        m_s[...] = m_new

    @pl.when(q_step == 0)
    def _():
        n_row = ckt_ref.shape[2]
        s = _dot(qn_ref[0], ckt_ref[0])
        ci = lax.broadcasted_iota(I32, (1, n_row), 1)
        vis = (ci * CMP_STRIDE + (CMP_BLOCK - 1) <= qpos) & (ci < n_cmp)
        s = jnp.where(vis, s, NEG)
        p = jnp.where(vis, jnp.exp(s - jnp.max(s, axis=-1, keepdims=True)), 0.0)
        p = p / jnp.maximum(jnp.sum(p, axis=-1, keepdims=True), TINY)
        ocmp_s[...] = _nt_dot(p.astype(BF16), cvt_ref[0])
        pg = p[0:grp_rows]
        for h in range(1, HEADS_PER_KV):
            pg = pg + p[h * grp_rows:(h + 1) * grp_rows]
        score = _dot(pg, sel_ref[...], precision=HIGHEST)
        width = score.shape[1]
        j = lax.broadcasted_iota(I32, (1, width), 1)
        tq = qpos[0:grp_rows]
        cur = tq // SEL_BLOCK
        forced = (j == 0) | (j == cur) | (j == cur - 1)
        sc = jnp.where(j * SEL_BLOCK <= tq, score + jnp.where(forced, BIG, 0.0), -BIG)
        sc = jnp.where(j < n_sb, sc, -2.0 * BIG)
        rank = jnp.zeros((grp_rows, width), I32)
        for jp in range(n_sb):
            col = sc[:, jp:jp + 1]
            beats = (col > sc) | ((col == sc) & (j > jp))
            rank = rank + beats.astype(I32)
        sel = jnp.where((rank < min(N_SEL, n_sb)) & (j < n_sb), 1.0, 0.0).astype(BF16)
        mask_s[...] = _dot(sel, e_ref[...])
        m_s[...] = jnp.full(m_s.shape, NEG, F32)
        l_s[...] = jnp.zeros(l_s.shape, F32)
        acc_s[...] = jnp.zeros(acc_s.shape, F32)

    width = n_pages * PAGE_SIZE
    kt = jnp.concatenate([page_refs[i][0, 0].reshape(KV_WIDTH, PAGE_SIZE) for i in range(n_pages)], axis=1)
    vt = jnp.concatenate([page_refs[i][0, 1].reshape(KV_WIDTH, PAGE_SIZE) for i in range(n_pages)], axis=1)
    k0 = pl.multiple_of(q_step * width, width)
    mk = tile_rows(mask_s[:, pl.ds(k0, width)]) > 0.5
    online_update(_dot(qr, kt.astype(BF16)), mk, lambda p: _nt_dot(p, vt.astype(BF16)))

    @pl.when(q_step == pl.num_programs(1) - 1)
    def _():
        lane = lax.broadcasted_iota(I32, (1, LANES), 1)
        new_ok = (lane < ts) & (past + lane <= qpos)
        mk_new = (tile_rows(mask_s[:, past:past + LANES]) > 0.5) & new_ok
        online_update(_nt_dot(qr, knew_ref[0]), mk_new, lambda p: _dot(p, vnew_ref[0]))
        o_sel = acc_s[...] / jnp.maximum(l_s[...], TINY)
        bi = lax.broadcasted_iota(I32, (1, n_buf), 1)
        dq = qpos - (past - n_buf + bi)
        ok_buf = (dq >= 0) & (dq < WINDOW)
        dq_new = qpos - (past + lane)
        ok_new = (lane < ts) & (dq_new >= 0) & (dq_new < WINDOW)
        s_w = jnp.concatenate([_dot(qr, win_ref[0, 0].astype(BF16)), _nt_dot(qr, kwnew_ref[0])], axis=1)
        ok = jnp.concatenate([jnp.broadcast_to(ok_buf, (rows, n_buf)), jnp.broadcast_to(ok_new, (rows, LANES))], axis=1)
        s_w = jnp.where(ok, s_w, NEG)
        p_w = jnp.where(ok, jnp.exp(s_w - jnp.max(s_w, axis=-1, keepdims=True)), 0.0)
        p_w = (p_w / jnp.maximum(jnp.sum(p_w, axis=-1, keepdims=True), TINY)).astype(BF16)
        o_win = _nt_dot(p_w[:, :n_buf], win_ref[0, 1].astype(BF16)) + _dot(p_w[:, n_buf:], vwnew_ref[0])
        row_g = (row // ts) % N_KV_HEADS

        def own_group(o):
            out = jnp.zeros((rows, HEAD_DIM), F32)
            for g in range(N_KV_HEADS):
                out = out + jnp.where(row_g == g, o[:, g * HEAD_DIM:(g + 1) * HEAD_DIM], 0.0)
            return out

        o_ref[0] = (gate_ref[0, 0] * own_group(ocmp_s[...]) + gate_ref[0, 1] * own_group(o_sel)
                    + gate_ref[0, 2] * own_group(o_win))


def _attn_sample(cache5, page_table, qn_bd, qr_bd, ck_t, cv_t, win4, knew, vnew, kwnew, vwnew, gates,
                 *, n_pages, ts, past):
    b, pages_per_seq = page_table.shape
    steps = pages_per_seq // n_pages
    rows = qn_bd.shape[1]
    n_chunk = ck_t.shape[2]
    n_cmp = n_chunk - CMP_BLOCK // CMP_STRIDE + 1
    n_keys = past + LANES
    n_sb = -(-(past + ts) // SEL_BLOCK)
    n_sb_pad = -(-n_sb // LANES) * LANES
    n_buf = win4.shape[3]
    ratio = SEL_BLOCK // CMP_STRIDE
    ci = jnp.arange(n_chunk)[:, None]
    jb = jnp.arange(n_sb_pad)[None, :]
    sel_map = ((ci >= ratio * jb + 1 - CMP_BLOCK // CMP_STRIDE) & (ci < ratio * jb + ratio)
               & (ci < n_cmp) & (jb < n_sb)).astype(F32)
    expand = (jnp.arange(n_keys)[None, :] // SEL_BLOCK == jnp.arange(n_sb_pad)[:, None]).astype(BF16)

    def page_spec(j):
        return pl.BlockSpec((1, 2, N_KV_HEADS, HEAD_DIM, PAGE_SIZE),
                            lambda i, q, pt: (pt[i, q * n_pages + j], 1, 0, 0, 0))

    per_seq = lambda a: pl.BlockSpec((1,) + a.shape[1:], lambda i, q, pt: (i,) + (0,) * (a.ndim - 1))
    full = lambda a: pl.BlockSpec(a.shape, lambda i, q, pt: (0,) * a.ndim)
    return pl.pallas_call(
        functools.partial(_attn_sample_kernel, n_pages=n_pages, n_cmp=n_cmp, n_sb=n_sb, ts=ts, past=past,
                          n_buf=n_buf),
        grid_spec=pltpu.PrefetchScalarGridSpec(
            num_scalar_prefetch=1,
            grid=(b, steps),
            in_specs=[page_spec(j) for j in range(n_pages)]
            + [per_seq(a) for a in (qn_bd, qr_bd, ck_t, cv_t, win4, knew, vnew, kwnew, vwnew, gates)]
            + [full(sel_map), full(expand)],
            out_specs=pl.BlockSpec((1, rows, HEAD_DIM), lambda i, q, pt: (i, 0, 0)),
            scratch_shapes=[
                pltpu.VMEM((N_KV_HEADS * ts, n_keys), F32),
                pltpu.VMEM((rows, 1), F32), pltpu.VMEM((rows, 1), F32),
                pltpu.VMEM((rows, KV_WIDTH), F32), pltpu.VMEM((rows, KV_WIDTH), F32),
            ],
        ),
        out_shape=jax.ShapeDtypeStruct((b, rows, HEAD_DIM), F32),
        compiler_params=_params(("arbitrary", "arbitrary"), VMEM_LIMIT),
        name="attn_sample",
    )(page_table, *([cache5] * n_pages), qn_bd, qr_bd, ck_t, cv_t, win4, knew, vnew, kwnew, vwnew, gates,
      sel_map, expand)


def _router_weights(router_group, router_expert):
    rt = jnp.concatenate([router_group, router_expert], axis=1).T
    return jnp.pad(rt, ((0, 24 - rt.shape[0]), (0, 0)))


def _trunk_prompt(x, p, *, tt_pool=256, tm=1024, sub=512, tt=512, tq=256):
    (norm_mix, norm_ffn, pool_w, pool_scale, kv_norm, w_kv, k_norm, cmp_pe, cmp_w1, cmp_b1, cmp_w2,
     w_qg, q_norm, w_o, router_group, router_expert, w_gate_up, w_down) = p
    b, t, d = x.shape
    n = b * t
    pos = jnp.arange(t)
    h, new_pool = _pool_layer(x, jnp.zeros((b, POOL_HALO, d), F32), norm_mix[0], pool_w[0], pool_scale[0],
                              tt=tt_pool, clip=True)
    h = _moe(h.reshape(n, d), norm_ffn[0], _router_weights(router_group[0], router_expert[0]),
             w_gate_up, w_down, 0, tm=tm, sub=sub)
    cos_t, sin_t = _rope_tables_transposed(pos)
    kv_t, win_t, craw, ksel, kwin, vsel_t, vwin_t = _kvproj(h.reshape(b, t, d), kv_norm, w_kv, k_norm,
                                                            cos_t, sin_t, tt=tt)
    ab = _cmp_ab_prompt(craw, _cmp_weights(cmp_w1), b=b, t=t)
    ck, cv = _cmp_finish(ab, cmp_pe, cmp_w1, cmp_b1, cmp_w2, k_norm[0])
    qn_t, qr_t, g_t = _qproj(h, norm_mix[1], _qg_weights(w_qg[0]), q_norm[0], cos_t, sin_t,
                             tt=tt, pos_blocks=t // tt)
    o_t = _attn_prompt(qn_t, qr_t, g_t, ck, jnp.swapaxes(cv, 2, 3), ksel, kwin, vsel_t, vwin_t, b=b, t=t, tq=tq)
    h = _oproj_t(o_t, h, w_o[0], tt=tt)
    h = _moe(h, norm_ffn[1], _router_weights(router_group[1], router_expert[1]),
             w_gate_up, w_down, 1, tm=tm, sub=sub)
    n_win = min(WINDOW, t)
    kv_new = kv_t.reshape(b, 4, N_KV_HEADS, HEAD_DIM, t).transpose(0, 4, 1, 2, 3)
    win_new = win_t[:, :, t - n_win:].reshape(b, 2, N_KV_HEADS, HEAD_DIM, n_win).transpose(0, 4, 1, 2, 3)
    return h.reshape(b, t, d), new_pool[None], kv_new, win_new


def _trunk_sample(x, state_pool, cache_kv, page_table, state_win, p, *, n_pages=16):
    (norm_mix, norm_ffn, pool_w, pool_scale, kv_norm, w_kv, k_norm, cmp_pe, cmp_w1, cmp_b1, cmp_w2,
     w_qg, q_norm, w_o, router_group, router_expert, w_gate_up, w_down) = p
    b, ts, d = x.shape
    n = b * ts
    past = page_table.shape[1] * PAGE_SIZE
    n_buf = state_win.shape[1]
    prev16 = jnp.pad(state_pool[0], ((0, 0), (POOL_HALO - POOL_STATE, 0), (0, 0)))
    h, new_pool = _pool_layer(x, prev16, norm_mix[0], pool_w[0], pool_scale[0], tt=ts, clip=False)
    h = _moe(h.reshape(n, d), norm_ffn[0], _router_weights(router_group[0], router_expert[0]),
             w_gate_up, w_down, 0, tm=n, sub=n)
    cos_t, sin_t = _rope_tables_transposed(past + jnp.arange(ts))
    cos_t = jnp.tile(cos_t, (1, b))
    sin_t = jnp.tile(sin_t, (1, b))
    kv_t, win_t, _, _, _, _, _ = _kvproj(h.reshape(1, n, d), kv_norm, w_kv, k_norm, cos_t, sin_t, tt=n)
    kv_rows = kv_t[0].T
    win_rows = win_t[0].T

    cache5 = cache_kv.transpose(0, 2, 3, 4, 1)
    ab = _cmp_ab_pages(cache5, page_table, _cmp_weights(cmp_w1), n_pages=n_pages)
    ck, cv = _cmp_finish(ab, cmp_pe, cmp_w1, cmp_b1, cmp_w2, k_norm[0])
    n_chunk = ck.shape[2]
    ck_t = ck.transpose(0, 1, 3, 2).reshape(b, KV_WIDTH, n_chunk)
    cv_t = cv.transpose(0, 1, 3, 2).reshape(b, KV_WIDTH, n_chunk)

    qn_t, qr_t, g_t = _qproj(h, norm_mix[1], _qg_weights(w_qg[0]), q_norm[0], cos_t, sin_t, tt=n, pos_blocks=1)

    def block_diag_queries(q_t):
        q5 = q_t.reshape(N_KV_HEADS, HEADS_PER_KV, HEAD_DIM, b, ts).transpose(3, 1, 0, 4, 2)
        eye = jnp.eye(N_KV_HEADS, dtype=q_t.dtype)
        qbd = q5[:, :, :, :, None, :] * eye[None, None, :, None, :, None]
        return qbd.reshape(b, HEADS_PER_KV * N_KV_HEADS * ts, KV_WIDTH)

    gates = g_t.reshape(N_KV_HEADS, GATE_ROWS, b, ts)[:, :3 * HEADS_PER_KV]
    gates = gates.reshape(N_KV_HEADS, 3, HEADS_PER_KV, b, ts).transpose(3, 1, 2, 0, 4)
    gates = jnp.broadcast_to(gates.reshape(b, 3, HEADS_PER_KV * N_KV_HEADS * ts, 1),
                             (b, 3, HEADS_PER_KV * N_KV_HEADS * ts, HEAD_DIM))

    def new_rows(rows2):
        return jnp.pad(rows2.reshape(b, ts, KV_WIDTH), ((0, 0), (0, LANES - ts), (0, 0))).astype(BF16)

    win4 = state_win.transpose(0, 2, 3, 4, 1).reshape(b, 2, KV_WIDTH, n_buf)
    o = _attn_sample(cache5, page_table, block_diag_queries(qn_t), block_diag_queries(qr_t), ck_t, cv_t, win4,
                     new_rows(kv_rows[:, 2 * KV_WIDTH:3 * KV_WIDTH]), new_rows(kv_rows[:, 3 * KV_WIDTH:]),
                     new_rows(win_rows[:, :KV_WIDTH]), new_rows(win_rows[:, KV_WIDTH:]), gates,
                     n_pages=n_pages, ts=ts, past=past)
    o = o.reshape(b, HEADS_PER_KV, N_KV_HEADS, ts, HEAD_DIM).transpose(0, 3, 2, 1, 4).reshape(n, N_HEADS * HEAD_DIM)
    h = _oproj_n(o, h, w_o[0])
    h = _moe(h, norm_ffn[1], _router_weights(router_group[1], router_expert[1]),
             w_gate_up, w_down, 1, tm=n, sub=n)
    kv_new = kv_rows.reshape(b, ts, 4, N_KV_HEADS, HEAD_DIM)
    win_new = jnp.concatenate([state_win, win_rows.reshape(b, ts, 2, N_KV_HEADS, HEAD_DIM)], axis=1)[:, -n_buf:]
    return h.reshape(b, ts, d), new_pool[None], kv_new, win_new


def kernel(x_prompt, x_sample, state_pool, cache_kv, page_table, state_win, norm_mix, norm_ffn, pool_w, pool_scale, kv_norm, w_kv, k_norm, cmp_pe, cmp_w1, cmp_b1, cmp_w2, w_qg, q_norm, w_o, router_group, router_expert, w_gate_up, w_down):
    params = (norm_mix, norm_ffn, pool_w, pool_scale, kv_norm, w_kv, k_norm, cmp_pe, cmp_w1, cmp_b1,
              cmp_w2, w_qg, q_norm, w_o, router_group, router_expert, w_gate_up, w_down)
    y_p, pool_p, kv_p, win_p = _trunk_prompt(x_prompt, params)
    y_s, pool_s, kv_s, win_s = _trunk_sample(x_sample, state_pool, cache_kv, page_table, state_win, params)
    return y_p, y_s, pool_p, pool_s, kv_p, kv_s, win_p, win_s
```

```python
import functools

import jax
import jax.numpy as jnp
from jax import lax
from jax.experimental import pallas as pl
from jax.experimental.pallas import tpu as pltpu

F32 = jnp.float32
BF16 = jnp.bfloat16
I32 = jnp.int32
HIGHEST = lax.Precision.HIGHEST

D_MODEL = 1024
POOL_WINDOWS = (2, 4, 8, 16)
POOL_GROUP_DIM = D_MODEL // len(POOL_WINDOWS)
POOL_STATE = max(POOL_WINDOWS) - 1
POOL_HALO = 16
N_HEADS = 16
HEAD_DIM = 64
HALF_DIM = HEAD_DIM // 2
N_KV_HEADS = 4
HEADS_PER_KV = N_HEADS // N_KV_HEADS
KV_WIDTH = N_KV_HEADS * HEAD_DIM
CMP_BLOCK = 32
CMP_STRIDE = 16
CMP_HIDDEN = 2 * HEAD_DIM
SEL_BLOCK = 64
N_SEL = 16
WINDOW = 512
PAGE_SIZE = 128
ROPE_THETA = 10000.0
SCALE = HEAD_DIM ** -0.5
N_GROUPS = 4
EXPERTS_PER_GROUP = 4
N_EXPERTS = N_GROUPS * EXPERTS_PER_GROUP
D_EXPERT = 512
EPS = 1e-6
NEG = -1e30
TINY = 1e-30
BIG = 1e4

LANES = 128
GATE_ROWS = 16
ATTN_CHAIN_HEADS = 4
V_PAD_ROWS = 16
VMEM_LIMIT = 56 * 1024 * 1024


def _params(sem, vmem=None):
    return pltpu.CompilerParams(dimension_semantics=sem, vmem_limit_bytes=vmem)


def _rms(x, g):
    return x * lax.rsqrt(jnp.mean(x * x, axis=-1, keepdims=True) + EPS) * g


def _nt_dot(a, b, precision=None):
    return lax.dot_general(a, b, (((1,), (1,)), ((), ())), precision=precision,
                           preferred_element_type=F32)


def _dot(a, b, precision=None):
    return jnp.dot(a, b, precision=precision, preferred_element_type=F32)


def _sigmoid(x):
    return 1.0 / (1.0 + jnp.exp(-x))


def _pool_kernel(h_ref, prev_ref, g_ref, w_ref, sc_ref, o_ref, np_ref, ext_ref, *, tt, clip):
    t = pl.program_id(1)
    x = h_ref[0]
    xn = _rms(x, g_ref[...])

    @pl.when(t == 0)
    def _():
        ext_ref[0:POOL_HALO, :] = prev_ref[0]

    @pl.when(t > 0)
    def _():
        ext_ref[0:POOL_HALO, :] = ext_ref[tt:tt + POOL_HALO, :]

    ext_ref[POOL_HALO:POOL_HALO + tt, :] = xn
    if clip:
        tpos = t * tt + lax.broadcasted_iota(I32, (tt, 1), 0)
    outs = []
    for gi, w in enumerate(POOL_WINDOWS):
        lo = gi * POOL_GROUP_DIM
        hi = lo + POOL_GROUP_DIM
        acc = ext_ref[POOL_HALO:POOL_HALO + tt, lo:hi]
        for j in range(1, w):
            acc = acc + ext_ref[POOL_HALO - j:POOL_HALO - j + tt, lo:hi]
        if clip:
            mean = acc / jnp.minimum(tpos + 1, w).astype(F32)
        else:
            mean = acc * (1.0 / w)
        d = (mean - xn[:, lo:hi]).astype(BF16)
        outs.append(_dot(d, w_ref[gi].astype(BF16)))
    o_ref[0] = x + jnp.concatenate(outs, axis=1) * sc_ref[...]

    @pl.when(t == pl.num_programs(1) - 1)
    def _():
        np_ref[0] = ext_ref[tt + POOL_HALO - POOL_STATE:tt + POOL_HALO, :]


def _pool_layer(h, prev16, gain, w_pool, scale, *, tt, clip):
    b, t, d = h.shape
    return pl.pallas_call(
        functools.partial(_pool_kernel, tt=tt, clip=clip),
        grid=(b, t // tt),
        in_specs=[
            pl.BlockSpec((1, tt, d), lambda i, j: (i, j, 0)),
            pl.BlockSpec((1, POOL_HALO, d), lambda i, j: (i, 0, 0)),
            pl.BlockSpec((1, d), lambda i, j: (0, 0)),
            pl.BlockSpec(w_pool.shape, lambda i, j: (0, 0, 0)),
            pl.BlockSpec((1, d), lambda i, j: (0, 0)),
        ],
        out_specs=[
            pl.BlockSpec((1, tt, d), lambda i, j: (i, j, 0)),
            pl.BlockSpec((1, POOL_STATE, d), lambda i, j: (i, 0, 0)),
        ],
        out_shape=[jax.ShapeDtypeStruct((b, t, d), F32),
                   jax.ShapeDtypeStruct((b, POOL_STATE, d), F32)],
        scratch_shapes=[pltpu.VMEM((tt + POOL_HALO, d), F32)],
        compiler_params=_params(("arbitrary", "arbitrary")),
        name="pool_layer",
    )(h, prev16, gain.reshape(1, d), w_pool, scale.reshape(1, d))


def _softmax_rows(rows):
    m = functools.reduce(jnp.maximum, rows)
    es = [jnp.exp(r - m) for r in rows]
    s = functools.reduce(lambda a, b: a + b, es)
    return [e / s for e in es]


def _router_kernel(h_ref, g_ref, rt_ref, xn_ref, comb_ref):
    xn = _rms(h_ref[...], g_ref[...])
    xn_ref[...] = xn.astype(BF16)
    lt = _nt_dot(rt_ref[...], xn, precision=HIGHEST)
    pg = _softmax_rows([lt[i:i + 1, :] for i in range(N_GROUPS)])
    g_val = functools.reduce(jnp.maximum, pg)
    g_idx = jnp.full(g_val.shape, N_GROUPS - 1, I32)
    for i in range(N_GROUPS - 2, -1, -1):
        g_idx = jnp.where(pg[i] == g_val, i, g_idx)
    le = []
    for j in range(EXPERTS_PER_GROUP):
        v = lt[N_GROUPS + (N_GROUPS - 1) * EXPERTS_PER_GROUP + j:N_GROUPS + (N_GROUPS - 1) * EXPERTS_PER_GROUP + j + 1, :]
        for gi in range(N_GROUPS - 2, -1, -1):
            r = N_GROUPS + gi * EXPERTS_PER_GROUP + j
            v = jnp.where(g_idx == gi, lt[r:r + 1, :], v)
        le.append(v)
    pe = _softmax_rows(le)
    ranks = []
    for j in range(EXPERTS_PER_GROUP):
        r = jnp.zeros(g_val.shape, I32)
        for i in range(EXPERTS_PER_GROUP):
            if i == j:
                continue
            beats = (pe[i] > pe[j]) | (pe[i] == pe[j]) if i < j else (pe[i] > pe[j])
            r = r + beats.astype(I32)
        ranks.append(r)
    vals, idxs = [], []
    for k in range(2):
        v = jnp.zeros(g_val.shape, F32)
        ix = jnp.zeros(g_val.shape, I32)
        for j in range(EXPERTS_PER_GROUP):
            hit = ranks[j] == k
            v = jnp.where(hit, pe[j], v)
            ix = jnp.where(hit, j, ix)
        vals.append(v)
        idxs.append(ix)
    tot = vals[0] + vals[1]
    erow = lax.broadcasted_iota(I32, (LANES, g_val.shape[1]), 0)
    comb_t = jnp.zeros(erow.shape, F32)
    for k in range(2):
        wk = g_val * (vals[k] / tot)
        comb_t = comb_t + jnp.where(erow == g_idx * EXPERTS_PER_GROUP + idxs[k], wk, 0.0)
    comb_ref[...] = comb_t.T


def _router(h2, gain, rt, *, tm):
    n, d = h2.shape
    return pl.pallas_call(
        _router_kernel,
        grid=(n // tm,),
        in_specs=[
            pl.BlockSpec((tm, d), lambda i: (i, 0)),
            pl.BlockSpec((1, d), lambda i: (0, 0)),
            pl.BlockSpec(rt.shape, lambda i: (0, 0)),
        ],
        out_specs=[
            pl.BlockSpec((tm, d), lambda i: (i, 0)),
            pl.BlockSpec((tm, LANES), lambda i: (i, 0)),
        ],
        out_shape=[jax.ShapeDtypeStruct((n, d), BF16), jax.ShapeDtypeStruct((n, LANES), F32)],
        compiler_params=_params(("arbitrary",)),
        name="moe_router",
    )(h2, gain.reshape(1, d), rt)


def _moe_kernel(x_ref, comb_ref, h_ref, wgu_ref, wd_ref, o_ref, wg_s, wd_s, *, sub):
    e = pl.program_id(1)
    wg_s[...] = wgu_ref[0, 0].astype(BF16)
    wd_s[...] = wd_ref[0, 0].astype(BF16)

    @pl.when(e == 0)
    def _():
        o_ref[...] = h_ref[...]

    lane = lax.broadcasted_iota(I32, (sub, LANES), 1)

    for i in range(x_ref.shape[0] // sub):
        rows = slice(i * sub, (i + 1) * sub)
        gu = _dot(x_ref[rows, :], wg_s[...])
        a = gu[:, :D_EXPERT]
        hdn = (a * _sigmoid(a) * gu[:, D_EXPERT:]).astype(BF16)
        y = _dot(hdn, wd_s[...])
        c = jnp.sum(jnp.where(lane == e, comb_ref[rows, :], 0.0), axis=-1, keepdims=True)
        o_ref[rows, :] += c * y


def _moe(h2, gain, rt, w_gu, w_down, layer, *, tm, sub):
    n, d = h2.shape
    xn, comb = _router(h2, gain, rt, tm=tm)
    return pl.pallas_call(
        functools.partial(_moe_kernel, sub=sub),
        grid=(n // tm, N_EXPERTS),
        in_specs=[
            pl.BlockSpec((tm, d), lambda i, e: (i, 0)),
            pl.BlockSpec((tm, LANES), lambda i, e: (i, 0)),
            pl.BlockSpec((tm, d), lambda i, e: (i, 0)),
            pl.BlockSpec((1, 1, d, 2 * D_EXPERT), lambda i, e: (layer, e, 0, 0)),
            pl.BlockSpec((1, 1, D_EXPERT, d), lambda i, e: (layer, e, 0, 0)),
        ],
        out_specs=pl.BlockSpec((tm, d), lambda i, e: (i, 0)),
        out_shape=jax.ShapeDtypeStruct((n, d), F32),
        scratch_shapes=[pltpu.VMEM((d, 2 * D_EXPERT), BF16), pltpu.VMEM((D_EXPERT, d), BF16)],
        compiler_params=_params(("arbitrary", "arbitrary"), VMEM_LIMIT),
        name="moe_experts",
    )(xn, comb, h2, w_gu, w_down)


def _head_norm_rope_t(z, gain, c, s):
    outs = []
    for g in range(z.shape[0] // HEAD_DIM):
        zh = z[g * HEAD_DIM:(g + 1) * HEAD_DIM]
        zn = zh * lax.rsqrt(jnp.mean(zh * zh, axis=0, keepdims=True) + EPS) * gain
        x1 = zn[:HALF_DIM]
        x2 = zn[HALF_DIM:]
        outs.append(x1 * c - x2 * s)
        outs.append(x2 * c + x1 * s)
    return jnp.concatenate(outs, axis=0)


def _kvproj_kernel(h_ref, g_ref, wt_ref, gsel_ref, gwin_ref, cos_ref, sin_ref,
                   kvt_ref, wint_ref, craw_ref, ksel_ref, kwin_ref, vselt_ref, vwint_ref, w_s):
    @pl.when((pl.program_id(0) == 0) & (pl.program_id(1) == 0))
    def _():
        w_s[...] = wt_ref[...].astype(BF16)

    xn = _rms(h_ref[0], g_ref[...]).astype(BF16)
    kvt = _nt_dot(w_s[...], xn)
    tt = xn.shape[0]
    c = cos_ref[...]
    s = sin_ref[...]
    gsel = jnp.concatenate([gsel_ref[...]] * (tt // LANES), axis=1)
    gwin = jnp.concatenate([gwin_ref[...]] * (tt // LANES), axis=1)
    ksel = _head_norm_rope_t(kvt[2 * KV_WIDTH:3 * KV_WIDTH], gsel, c, s)
    kwin = _head_norm_rope_t(kvt[4 * KV_WIDTH:5 * KV_WIDTH], gwin, c, s)
    kvt_ref[0] = jnp.concatenate([kvt[:2 * KV_WIDTH], ksel, kvt[3 * KV_WIDTH:4 * KV_WIDTH]], axis=0)
    wint_ref[0] = jnp.concatenate([kwin, kvt[5 * KV_WIDTH:]], axis=0)
    vselt_ref[...] = kvt[3 * KV_WIDTH:4 * KV_WIDTH].astype(BF16)
    vwint_ref[...] = kvt[5 * KV_WIDTH:].astype(BF16)
    raw = kvt[:2 * KV_WIDTH].T
    for k in range(2 * KV_WIDTH // LANES):
        craw_ref[k] = raw[:, k * LANES:(k + 1) * LANES]
    ksel_n = ksel.T
    kwin_n = kwin.T
    for g in range(N_KV_HEADS):
        ksel_ref[g] = ksel_n[:, g * HEAD_DIM:(g + 1) * HEAD_DIM].astype(BF16)
        kwin_ref[g] = kwin_n[:, g * HEAD_DIM:(g + 1) * HEAD_DIM].astype(BF16)


def _kvproj(x3, gain, w_kv, k_norm, cos_tt, sin_tt, *, tt):
    b, t, d = x3.shape
    n = b * t
    nt = t // tt
    wt = w_kv.T
    gsel = jnp.broadcast_to(k_norm[1][:, None], (HEAD_DIM, LANES))
    gwin = jnp.broadcast_to(k_norm[2][:, None], (HEAD_DIM, LANES))
    full = lambda a: pl.BlockSpec(a.shape, lambda i, j: (0,) * a.ndim)
    return pl.pallas_call(
        _kvproj_kernel,
        grid=(b, nt),
        in_specs=[
            pl.BlockSpec((1, tt, d), lambda i, j: (i, j, 0)),
            pl.BlockSpec((1, d), lambda i, j: (0, 0)),
            full(wt), full(gsel), full(gwin),
            pl.BlockSpec((HALF_DIM, tt), lambda i, j: (0, j)),
            pl.BlockSpec((HALF_DIM, tt), lambda i, j: (0, j)),
        ],
        out_specs=[
            pl.BlockSpec((1, 4 * KV_WIDTH, tt), lambda i, j: (i, 0, j)),
            pl.BlockSpec((1, 2 * KV_WIDTH, tt), lambda i, j: (i, 0, j)),
            pl.BlockSpec((2 * KV_WIDTH // LANES, tt, LANES), lambda i, j: (0, i * nt + j, 0)),
            pl.BlockSpec((N_KV_HEADS, tt, HEAD_DIM), lambda i, j: (0, i * nt + j, 0)),
            pl.BlockSpec((N_KV_HEADS, tt, HEAD_DIM), lambda i, j: (0, i * nt + j, 0)),
            pl.BlockSpec((KV_WIDTH, tt), lambda i, j: (0, i * nt + j)),
            pl.BlockSpec((KV_WIDTH, tt), lambda i, j: (0, i * nt + j)),
        ],
        out_shape=[
            jax.ShapeDtypeStruct((b, 4 * KV_WIDTH, t), F32),
            jax.ShapeDtypeStruct((b, 2 * KV_WIDTH, t), F32),
            jax.ShapeDtypeStruct((2 * KV_WIDTH // LANES, n, LANES), F32),
            jax.ShapeDtypeStruct((N_KV_HEADS, n, HEAD_DIM), BF16),
            jax.ShapeDtypeStruct((N_KV_HEADS, n, HEAD_DIM), BF16),
            jax.ShapeDtypeStruct((KV_WIDTH, n), BF16),
            jax.ShapeDtypeStruct((KV_WIDTH, n), BF16),
        ],
        scratch_shapes=[pltpu.VMEM(wt.shape, BF16)],
        compiler_params=_params(("arbitrary", "arbitrary"), VMEM_LIMIT),
        name="kv_proj",
    )(x3, gain.reshape(1, d), wt, gsel, gwin, cos_tt, sin_tt)


def _rope_tables_transposed(pos):
    inv = 1.0 / (ROPE_THETA ** (jnp.arange(HALF_DIM, dtype=F32) * (2.0 / HEAD_DIM)))
    ang = pos.astype(F32)[:, None] * inv[None, :]
    return jnp.cos(ang).T, jnp.sin(ang).T


def _cmp_ab_accumulate(load_rows, wab_ref, n_rows):
    del n_rows
    heads_per_chunk = LANES // HEAD_DIM
    accs = [[None] * N_KV_HEADS for _ in range(2)]
    for s in range(2):
        for c in range(KV_WIDTH // LANES):
            plane = s * (KV_WIDTH // LANES) + c
            lhs = jnp.concatenate([load_rows(r, plane).astype(BF16) for r in range(CMP_STRIDE)], axis=1)
            out = _dot(lhs, wab_ref[s])
            for k in range(heads_per_chunk):
                accs[s][c * heads_per_chunk + k] = out[:, k * 2 * CMP_HIDDEN:(k + 1) * 2 * CMP_HIDDEN]
    return accs


def _cmpab_kernel(craw_ref, wab_ref, ab_ref, *, n_chunk):
    accs = _cmp_ab_accumulate(lambda r, c: craw_ref[c, pl.ds(r, n_chunk, stride=CMP_STRIDE), :], wab_ref, n_chunk)
    for s in range(2):
        for g in range(N_KV_HEADS):
            ab_ref[0, s, g] = accs[s][g]


def _cmpab_pages_kernel(pt_ref, *refs, n_pages):
    del pt_ref
    page_refs = refs[:n_pages]
    wab_ref, ab_ref, craw_s = refs[n_pages:]
    pairs = KV_WIDTH // LANES
    for j in range(n_pages):
        for s in range(2):
            for k in range(pairs):
                tile = jnp.concatenate([page_refs[j][0, s, 2 * k], page_refs[j][0, s, 2 * k + 1]], axis=0)
                craw_s[s * pairs + k, j * PAGE_SIZE:(j + 1) * PAGE_SIZE, :] = tile.T
    n_chunk = n_pages * PAGE_SIZE // CMP_STRIDE
    accs = _cmp_ab_accumulate(lambda r, c: craw_s[c, pl.ds(r, n_chunk, stride=CMP_STRIDE), :], wab_ref, n_chunk)
    for s in range(2):
        for g in range(N_KV_HEADS):
            ab_ref[0, s, g] = accs[s][g]


def _cmp_ab_pages(cache5, page_table, wab, *, n_pages):
    b, pages_per_seq = page_table.shape
    steps = pages_per_seq // n_pages
    rows = n_pages * PAGE_SIZE // CMP_STRIDE

    def page_spec(j):
        return pl.BlockSpec((1, 2, N_KV_HEADS, HEAD_DIM, PAGE_SIZE),
                            lambda i, q, pt: (pt[i, q * n_pages + j], 0, 0, 0, 0))

    return pl.pallas_call(
        functools.partial(_cmpab_pages_kernel, n_pages=n_pages),
        grid_spec=pltpu.PrefetchScalarGridSpec(
            num_scalar_prefetch=1,
            grid=(b, steps),
            in_specs=[page_spec(j) for j in range(n_pages)]
            + [pl.BlockSpec(wab.shape, lambda i, q, pt: (0, 0, 0))],
            out_specs=pl.BlockSpec((1, 2, N_KV_HEADS, rows, 2 * CMP_HIDDEN), lambda i, q, pt: (i, 0, 0, q, 0)),
            scratch_shapes=[pltpu.VMEM((2 * KV_WIDTH // LANES, n_pages * PAGE_SIZE, LANES), F32)],
        ),
        out_shape=jax.ShapeDtypeStruct((b, 2, N_KV_HEADS, steps * rows, 2 * CMP_HIDDEN), F32),
        compiler_params=_params(("arbitrary", "arbitrary"), VMEM_LIMIT),
        name="cmp_ab_pages",
    )(page_table, *([cache5] * n_pages), wab)


def _cmp_weights(cmp_w1):
    w = cmp_w1.reshape(2, 2, CMP_STRIDE, HEAD_DIM, CMP_HIDDEN)
    w = w.transpose(0, 2, 3, 1, 4).reshape(2, CMP_STRIDE, HEAD_DIM, 2 * CMP_HIDDEN)
    eye = jnp.eye(LANES // HEAD_DIM, dtype=w.dtype)
    w = w[:, :, None, :, None, :] * eye[None, None, :, None, :, None]
    return w.reshape(2, CMP_STRIDE * LANES, (LANES // HEAD_DIM) * 2 * CMP_HIDDEN).astype(BF16)


def _cmp_ab_prompt(craw, wab, *, b, t):
    n_chunk = t // CMP_STRIDE
    return pl.pallas_call(
        functools.partial(_cmpab_kernel, n_chunk=n_chunk),
        grid=(b,),
        in_specs=[
            pl.BlockSpec((craw.shape[0], t, LANES), lambda i: (0, i, 0)),
            pl.BlockSpec(wab.shape, lambda i: (0, 0, 0)),
        ],
        out_specs=pl.BlockSpec((1, 2, N_KV_HEADS, n_chunk, 2 * CMP_HIDDEN), lambda i: (i, 0, 0, 0, 0)),
        out_shape=jax.ShapeDtypeStruct((b, 2, N_KV_HEADS, n_chunk, 2 * CMP_HIDDEN), F32),
        compiler_params=_params(("arbitrary",)),
        name="cmp_ab_prompt",
    )(craw, wab)


def _cmpfin_kernel(ab_ref, pe_ref, w1_ref, b1_ref, w2_ref, gk_ref, ck_ref, cv_ref, *, n_row):
    for s in range(2):
        bias = _dot(pe_ref[s].astype(BF16), w1_ref[s].astype(BF16)) + b1_ref[s]
        w2 = w2_ref[s].astype(BF16)
        for g in range(N_KV_HEADS):
            ab = ab_ref[0, s, g]
            hid = ab[:, :CMP_HIDDEN] + pltpu.roll(ab[:, CMP_HIDDEN:], n_row - 1, 0) + bias
            cdf = 0.5 * (1.0 + jnp.tanh(0.7978845608028654 * (hid + 0.044715 * (hid * hid * hid))))
            out = _dot((hid * cdf).astype(BF16), w2)
            if s == 0:
                ck_ref[0, g] = _rms(out, gk_ref[...]).astype(BF16)
            else:
                cv_ref[0, g] = out.astype(BF16)


def _cmp_finish(ab, cmp_pe, cmp_w1, cmp_b1, cmp_w2, gk):
    b = ab.shape[0]
    n_row = ab.shape[3]
    pe = cmp_pe.reshape(2, 1, CMP_BLOCK * HEAD_DIM)
    full = lambda a: pl.BlockSpec(a.shape, lambda i: (0,) * a.ndim)
    b1 = cmp_b1.reshape(2, 1, CMP_HIDDEN)
    gk2 = gk.reshape(1, HEAD_DIM)
    return pl.pallas_call(
        functools.partial(_cmpfin_kernel, n_row=n_row),
        grid=(b,),
        in_specs=[pl.BlockSpec((1,) + ab.shape[1:], lambda i: (i, 0, 0, 0, 0)),
                  full(pe), full(cmp_w1), full(b1), full(cmp_w2), full(gk2)],
        out_specs=[pl.BlockSpec((1, N_KV_HEADS, n_row, HEAD_DIM), lambda i: (i, 0, 0, 0)),
                   pl.BlockSpec((1, N_KV_HEADS, n_row, HEAD_DIM), lambda i: (i, 0, 0, 0))],
        out_shape=[jax.ShapeDtypeStruct((b, N_KV_HEADS, n_row, HEAD_DIM), BF16),
                   jax.ShapeDtypeStruct((b, N_KV_HEADS, n_row, HEAD_DIM), BF16)],
        compiler_params=_params(("arbitrary",)),
        name="cmp_finish",
    )(ab, pe, cmp_w1, b1, cmp_w2, gk2)


def _qproj_kernel(h_ref, g_ref, wt_ref, gq_ref, cos_ref, sin_ref, qn_ref, qr_ref, gt_ref, w_s):
    @pl.when(pl.program_id(0) == 0)
    def _():
        w_s[...] = wt_ref[...].astype(BF16)

    xn = _rms(h_ref[...], g_ref[...]).astype(BF16)
    qg = _nt_dot(w_s[...], xn)
    tt = xn.shape[0]
    gq = jnp.concatenate([gq_ref[...]] * (tt // LANES), axis=1)
    c = cos_ref[...]
    s = sin_ref[...]
    for h in range(N_HEADS):
        qh = qg[h * HEAD_DIM:(h + 1) * HEAD_DIM]
        qn = qh * lax.rsqrt(jnp.mean(qh * qh, axis=0, keepdims=True) + EPS) * gq
        qn_ref[h * HEAD_DIM:(h + 1) * HEAD_DIM, :] = (qn * SCALE).astype(BF16)
        x1 = qn[:HALF_DIM]
        x2 = qn[HALF_DIM:]
        qr_ref[h * HEAD_DIM:h * HEAD_DIM + HALF_DIM, :] = ((x1 * c - x2 * s) * SCALE).astype(BF16)
        qr_ref[h * HEAD_DIM + HALF_DIM:(h + 1) * HEAD_DIM, :] = ((x2 * c + x1 * s) * SCALE).astype(BF16)
    gt_ref[...] = _sigmoid(qg[N_HEADS * HEAD_DIM:])


def _qg_weights(w_qg):
    nq = N_HEADS * HEAD_DIM
    gates = w_qg[:, nq:].reshape(D_MODEL, 3, N_KV_HEADS, HEADS_PER_KV).transpose(2, 1, 3, 0)
    gates = gates.reshape(N_KV_HEADS, 3 * HEADS_PER_KV, D_MODEL)
    gates = jnp.pad(gates, ((0, 0), (0, GATE_ROWS - 3 * HEADS_PER_KV), (0, 0)))
    return jnp.concatenate([w_qg[:, :nq].T, gates.reshape(N_KV_HEADS * GATE_ROWS, D_MODEL)], axis=0)


def _qproj(h2, gain, wt, q_norm, cos_tt, sin_tt, *, tt, pos_blocks):
    n, d = h2.shape
    nq = N_HEADS * HEAD_DIM
    ng = N_KV_HEADS * GATE_ROWS
    gq = jnp.broadcast_to(q_norm[:, None], (HEAD_DIM, LANES))
    full = lambda a: pl.BlockSpec(a.shape, lambda i: (0,) * a.ndim)
    return pl.pallas_call(
        _qproj_kernel,
        grid=(n // tt,),
        in_specs=[
            pl.BlockSpec((tt, d), lambda i: (i, 0)),
            pl.BlockSpec((1, d), lambda i: (0, 0)),
            full(wt), full(gq),
            pl.BlockSpec((HALF_DIM, tt), lambda i: (0, i % pos_blocks)),
            pl.BlockSpec((HALF_DIM, tt), lambda i: (0, i % pos_blocks)),
        ],
        out_specs=[
            pl.BlockSpec((nq, tt), lambda i: (0, i)),
            pl.BlockSpec((nq, tt), lambda i: (0, i)),
            pl.BlockSpec((ng, tt), lambda i: (0, i)),
        ],
        out_shape=[jax.ShapeDtypeStruct((nq, n), BF16), jax.ShapeDtypeStruct((nq, n), BF16),
                   jax.ShapeDtypeStruct((ng, n), F32)],
        scratch_shapes=[pltpu.VMEM(wt.shape, BF16)],
        compiler_params=_params(("arbitrary",), VMEM_LIMIT),
        name="q_proj",
    )(h2, gain.reshape(1, d), wt, gq, cos_tt, sin_tt)


def _attn_kernel(qn_ref, qr_ref, gt_ref, ck_ref, cvt_ref, kaug_ref, kwin_ref, vaug_ref, vwaug_ref,
                 o_ref, pg_s, *, tq, n_cmp, n_sb):
    hp_n = HEADS_PER_KV
    qt = pl.program_id(2)
    t0 = qt * tq
    tpos = t0 + lax.broadcasted_iota(I32, (1, tq), 1)
    tpos4 = jnp.concatenate([tpos] * hp_n, axis=1)
    qn4 = jnp.concatenate([qn_ref[h * HEAD_DIM:(h + 1) * HEAD_DIM, :] for h in range(hp_n)], axis=1)
    qr4 = jnp.concatenate([qr_ref[h * HEAD_DIM:(h + 1) * HEAD_DIM, :] for h in range(hp_n)], axis=1)

    n_row = ck_ref.shape[2]
    s = _dot(ck_ref[0, 0], qn4)
    ci = lax.broadcasted_iota(I32, (n_row, 1), 0)
    vis = (ci * CMP_STRIDE + (CMP_BLOCK - 1) <= tpos4) & (ci < n_cmp)
    s = jnp.where(vis, s, NEG)
    p = jnp.where(vis, jnp.exp(s - jnp.max(s, axis=0, keepdims=True)), 0.0)
    p = p / jnp.maximum(jnp.sum(p, axis=0, keepdims=True), TINY)
    o_cmp = _dot(cvt_ref[0, 0], p.astype(BF16))
    pg = p[:, 0:tq]
    for h in range(1, hp_n):
        pg = pg + p[:, h * tq:(h + 1) * tq]

    ratio = SEL_BLOCK // CMP_STRIDE
    scores = []
    for c in range(tq // LANES):
        pg_s[c, 0:8, :] = jnp.zeros((8, LANES), F32)
        pg_s[c, 8:8 + n_row, :] = pg[:, c * LANES:(c + 1) * LANES]
        sc_c = pg_s[c, pl.ds(8 + 1 - CMP_BLOCK // CMP_STRIDE, n_sb, stride=ratio), :]
        for o in range(2 - CMP_BLOCK // CMP_STRIDE, ratio):
            sc_c = sc_c + pg_s[c, pl.ds(8 + o, n_sb, stride=ratio), :]
        scores.append(sc_c)
    score = jnp.concatenate(scores, axis=1)
    jrow = lax.broadcasted_iota(I32, (n_sb, 1), 0)
    cur = tpos // SEL_BLOCK
    forced = (jrow == 0) | (jrow == cur) | (jrow == cur - 1)
    sc = jnp.where(jrow * SEL_BLOCK <= tpos, score + jnp.where(forced, BIG, 0.0), -BIG)
    rank = jnp.zeros((n_sb, tq), I32)
    for jp in range(n_sb):
        row = sc[jp:jp + 1, :]
        beats = (row > sc) | ((row == sc) & (jrow > jp))
        rank = rank + beats.astype(I32)
    n_sbp = kaug_ref.shape[3] - HEAD_DIM
    sel_neg = jnp.where(rank < min(N_SEL, n_sb), 0.0, NEG)
    if n_sbp > n_sb:
        sel_neg = jnp.concatenate([sel_neg, jnp.zeros((n_sbp - n_sb, tq), F32)], axis=0)
    sel_neg = sel_neg.astype(BF16)
    n_chain = hp_n // ATTN_CHAIN_HEADS
    cw = ATTN_CHAIN_HEADS * tq

    def chain_lanes(per_head):
        return [jnp.concatenate(per_head[c * ATTN_CHAIN_HEADS:(c + 1) * ATTN_CHAIN_HEADS], axis=1)
                for c in range(n_chain)]

    q_rot = [qr_ref[h * HEAD_DIM:(h + 1) * HEAD_DIM, :] for h in range(hp_n)]
    q_aug = chain_lanes([jnp.concatenate([q, sel_neg], axis=0) for q in q_rot])
    q_win = chain_lanes(q_rot)

    def update(k_tile, v_tile, q, bias, m, acc):
        sk = _dot(k_tile, q)
        if bias is not None:
            sk = sk + bias
        m_new = jnp.maximum(m, jnp.max(sk, axis=0, keepdims=True))
        pk = jnp.exp(sk - m_new).astype(BF16)
        return m_new, acc * jnp.exp(m - m_new) + _dot(v_tile, pk)

    def sel_tile(kt, bias, carry):
        k0 = pl.multiple_of(kt * tq, tq)
        k_tile = kaug_ref[0, 0, pl.ds(k0, tq), :]
        v_tile = vaug_ref[:, pl.ds(k0, tq)]
        return tuple(update(k_tile, v_tile, q_aug[c], bias, *carry[c]) for c in range(n_chain))

    v_rows = vaug_ref.shape[0]
    init = tuple((jnp.full((1, cw), NEG, F32), jnp.zeros((v_rows, cw), F32)) for _ in range(n_chain))
    carry = lax.fori_loop(0, qt, lambda kt, c: sel_tile(kt, None, c), init)
    causal = jnp.where(lax.broadcasted_iota(I32, (tq, 1), 0) <= lax.broadcasted_iota(I32, (1, tq), 1), 0.0, NEG)
    carry = sel_tile(qt, jnp.concatenate([causal] * ATTN_CHAIN_HEADS, axis=1), carry)
    o_sel = jnp.concatenate([acc[:HEAD_DIM] / jnp.maximum(acc[HEAD_DIM:HEAD_DIM + 1], TINY) for _, acc in carry],
                            axis=1)

    n_wt = WINDOW // tq + 1
    w0 = pl.multiple_of(jnp.clip(qt - (n_wt - 1), 0, pl.num_programs(2) - n_wt) * tq, tq)
    dq = tpos - (w0 + lax.broadcasted_iota(I32, (n_wt * tq, 1), 0))
    wbias = jnp.where((dq >= 0) & (dq < WINDOW), 0.0, NEG)
    kw_tile = kwin_ref[0, 0, pl.ds(w0, n_wt * tq), :]
    vw_tile = vwaug_ref[:, pl.ds(w0, n_wt * tq)]
    wbias_c = jnp.concatenate([wbias] * ATTN_CHAIN_HEADS, axis=1)
    o_win = []
    for c in range(n_chain):
        sw = _dot(kw_tile, q_win[c]) + wbias_c
        pw = jnp.exp(sw - jnp.max(sw, axis=0, keepdims=True)).astype(BF16)
        acc_win = _dot(vw_tile, pw)
        o_win.append(acc_win[:HEAD_DIM] / jnp.maximum(acc_win[HEAD_DIM:HEAD_DIM + 1], TINY))
    o_win = jnp.concatenate(o_win, axis=1)
    gt = gt_ref[...]
    for h in range(hp_n):
        sl = slice(h * tq, (h + 1) * tq)
        o = (gt[h:h + 1] * o_cmp[:, sl] + gt[hp_n + h:hp_n + h + 1] * o_sel[:, sl]
             + gt[2 * hp_n + h:2 * hp_n + h + 1] * o_win[:, sl])
        o_ref[h * HEAD_DIM:(h + 1) * HEAD_DIM, :] = o.astype(BF16)


def _attn_prompt(qn_t, qr_t, g_t, ck, cv_t, ksel, kwin, vsel_t, vwin_t, *, b, t, tq):
    nq = t // tq
    n_sb = t // SEL_BLOCK
    n_row = ck.shape[2]
    n_cmp = t // CMP_STRIDE - CMP_BLOCK // CMP_STRIDE + 1
    n_sbp = -(-n_sb // 32) * 32
    kw = N_KV_HEADS * HEAD_DIM
    onehot = (jnp.arange(t)[:, None] // SEL_BLOCK == jnp.arange(n_sbp)[None, :]).astype(BF16)
    kaug = jnp.concatenate([ksel.reshape(N_KV_HEADS, b, t, HEAD_DIM),
                            jnp.broadcast_to(onehot, (N_KV_HEADS, b, t, n_sbp))], axis=-1)
    ones_rows = jnp.zeros((N_KV_HEADS, V_PAD_ROWS, b * t), BF16).at[:, 0].set(1.0)

    def with_ones(v_t):
        v3 = jnp.concatenate([v_t.reshape(N_KV_HEADS, HEAD_DIM, b * t), ones_rows], axis=1)
        return v3.reshape(N_KV_HEADS * (HEAD_DIM + V_PAD_ROWS), b * t)

    v_rows = HEAD_DIM + V_PAD_ROWS
    return pl.pallas_call(
        functools.partial(_attn_kernel, tq=tq, n_cmp=n_cmp, n_sb=n_sb),
        grid=(b, N_KV_HEADS, nq),
        in_specs=[
            pl.BlockSpec((kw, tq), lambda i, g, q: (g, i * nq + q)),
            pl.BlockSpec((kw, tq), lambda i, g, q: (g, i * nq + q)),
            pl.BlockSpec((GATE_ROWS, tq), lambda i, g, q: (g, i * nq + q)),
            pl.BlockSpec((1, 1, n_row, HEAD_DIM), lambda i, g, q: (i, g, 0, 0)),
            pl.BlockSpec((1, 1, HEAD_DIM, n_row), lambda i, g, q: (i, g, 0, 0)),
            pl.BlockSpec((1, 1, t, HEAD_DIM + n_sbp), lambda i, g, q: (g, i, 0, 0)),
            pl.BlockSpec((1, 1, t, HEAD_DIM), lambda i, g, q: (g, i, 0, 0)),
            pl.BlockSpec((v_rows, t), lambda i, g, q: (g, i)),
            pl.BlockSpec((v_rows, t), lambda i, g, q: (g, i)),
        ],
        out_specs=pl.BlockSpec((kw, tq), lambda i, g, q: (g, i * nq + q)),
        out_shape=jax.ShapeDtypeStruct((N_HEADS * HEAD_DIM, b * t), BF16),
        scratch_shapes=[pltpu.VMEM((tq // LANES, n_row + 8, LANES), F32)],
        compiler_params=_params(("arbitrary", "arbitrary", "arbitrary"), VMEM_LIMIT),
        name="attn_prompt",
    )(qn_t, qr_t, g_t, ck, cv_t, kaug, kwin.reshape(N_KV_HEADS, b, t, HEAD_DIM), with_ones(vsel_t), with_ones(vwin_t))


def _oproj_t_kernel(ot_ref, h_ref, w_ref, out_ref, w_s):
    @pl.when(pl.program_id(0) == 0)
    def _():
        w_s[...] = w_ref[...].astype(BF16)

    o = ot_ref[...].astype(F32).T.astype(BF16)
    out_ref[...] = h_ref[...] + _dot(o, w_s[...])


def _oproj_t(o_t, h2, w_o, *, tt):
    n, d = h2.shape
    return pl.pallas_call(
        _oproj_t_kernel,
        grid=(n // tt,),
        in_specs=[
            pl.BlockSpec((o_t.shape[0], tt), lambda i: (0, i)),
            pl.BlockSpec((tt, d), lambda i: (i, 0)),
            pl.BlockSpec(w_o.shape, lambda i: (0, 0)),
        ],
        out_specs=pl.BlockSpec((tt, d), lambda i: (i, 0)),
        out_shape=jax.ShapeDtypeStruct((n, d), F32),
        scratch_shapes=[pltpu.VMEM(w_o.shape, BF16)],
        compiler_params=_params(("arbitrary",), VMEM_LIMIT),
        name="o_proj",
    )(o_t, h2, w_o)


def _oproj_n_kernel(o_ref, h_ref, w_ref, out_ref):
    out_ref[...] = h_ref[...] + _dot(o_ref[...].astype(BF16), w_ref[...].astype(BF16))


def _oproj_n(o, h2, w_o):
    n, d = h2.shape
    return pl.pallas_call(
        _oproj_n_kernel,
        out_shape=jax.ShapeDtypeStruct((n, d), F32),
        compiler_params=_params((), VMEM_LIMIT),
        name="o_proj_sample",
    )(o, h2, w_o)


def _attn_sample_kernel(pt_ref, *refs, n_pages, n_cmp, n_sb, ts, past, n_buf):
    del pt_ref
    page_refs = refs[:n_pages]
    (qn_ref, qr_ref, ckt_ref, cvt_ref, win_ref, knew_ref, vnew_ref, kwnew_ref, vwnew_ref, gate_ref,
     sel_ref, e_ref, o_ref, mask_s, m_s, l_s, acc_s, ocmp_s) = refs[n_pages:]
    q_step = pl.program_id(1)
    rows = HEADS_PER_KV * N_KV_HEADS * ts
    grp_rows = N_KV_HEADS * ts
    row = lax.broadcasted_iota(I32, (rows, 1), 0)
    qpos = past + row % ts
    qr = qr_ref[0]

    def tile_rows(x):
        return jnp.concatenate([x] * HEADS_PER_KV, axis=0)

    def online_update(s, mk, v_dot):
        s = jnp.where(mk, s, NEG)
        m_new = jnp.maximum(m_s[...], jnp.max(s, axis=-1, keepdims=True))
        alpha = jnp.exp(m_s[...] - m_new)
        p = jnp.where(mk, jnp.exp(s - m_new), 0.0)
        l_s[...] = l_s[...] * alpha + jnp.sum(p, axis=-1, keepdims=True)
        acc_s[...] = acc_s[...] * alpha + v_dot(p.astype(BF16))
        m_s[...] = m_new

    @pl.when(q_step == 0)
    def _():
        n_row = ckt_ref.shape[2]
        s = _dot(qn_ref[0], ckt_ref[0])
        ci = lax.broadcasted_iota(I32, (1, n_row), 1)
        vis = (ci * CMP_STRIDE + (CMP_BLOCK - 1) <= qpos) & (ci < n_cmp)
        s = jnp.where(vis, s, NEG)
        p = jnp.where(vis, jnp.exp(s - jnp.max(s, axis=-1, keepdims=True)), 0.0)
        p = p / jnp.maximum(jnp.sum(p, axis=-1, keepdims=True), TINY)
        ocmp_s[...] = _nt_dot(p.astype(BF16), cvt_ref[0])
        pg = p[0:grp_rows]
        for h in range(1, HEADS_PER_KV):
            pg = pg + p[h * grp_rows:(h + 1) * grp_rows]
        score = _dot(pg, sel_ref[...], precision=HIGHEST)
        width = score.shape[1]
        j = lax.broadcasted_iota(I32, (1, width), 1)
        tq = qpos[0:grp_rows]
        cur = tq // SEL_BLOCK
        forced = (j == 0) | (j == cur) | (j == cur - 1)
        sc = jnp.where(j * SEL_BLOCK <= tq, score + jnp.where(forced, BIG, 0.0), -BIG)
        sc = jnp.where(j < n_sb, sc, -2.0 * BIG)
        rank = jnp.zeros((grp_rows, width), I32)
        for jp in range(n_sb):
            col = sc[:, jp:jp + 1]
            beats = (col > sc) | ((col == sc) & (j > jp))
            rank = rank + beats.astype(I32)
        sel = jnp.where((rank < min(N_SEL, n_sb)) & (j < n_sb), 1.0, 0.0).astype(BF16)
        mask_s[...] = _dot(sel, e_ref[...])
        m_s[...] = jnp.full(m_s.shape, NEG, F32)
        l_s[...] = jnp.zeros(l_s.shape, F32)
        acc_s[...] = jnp.zeros(acc_s.shape, F32)

    width = n_pages * PAGE_SIZE
    kt = jnp.concatenate([page_refs[i][0, 0].reshape(KV_WIDTH, PAGE_SIZE) for i in range(n_pages)], axis=1)
    vt = jnp.concatenate([page_refs[i][0, 1].reshape(KV_WIDTH, PAGE_SIZE) for i in range(n_pages)], axis=1)
    k0 = pl.multiple_of(q_step * width, width)
    mk = tile_rows(mask_s[:, pl.ds(k0, width)]) > 0.5
    online_update(_dot(qr, kt.astype(BF16)), mk, lambda p: _nt_dot(p, vt.astype(BF16)))

    @pl.when(q_step == pl.num_programs(1) - 1)
    def _():
        lane = lax.broadcasted_iota(I32, (1, LANES), 1)
        new_ok = (lane < ts) & (past + lane <= qpos)
        mk_new = (tile_rows(mask_s[:, past:past + LANES]) > 0.5) & new_ok
        online_update(_nt_dot(qr, knew_ref[0]), mk_new, lambda p: _dot(p, vnew_ref[0]))
        o_sel = acc_s[...] / jnp.maximum(l_s[...], TINY)
        bi = lax.broadcasted_iota(I32, (1, n_buf), 1)
        dq = qpos - (past - n_buf + bi)
        ok_buf = (dq >= 0) & (dq < WINDOW)
        dq_new = qpos - (past + lane)
        ok_new = (lane < ts) & (dq_new >= 0) & (dq_new < WINDOW)
        s_w = jnp.concatenate([_dot(qr, win_ref[0, 0].astype(BF16)), _nt_dot(qr, kwnew_ref[0])], axis=1)
        ok = jnp.concatenate([jnp.broadcast_to(ok_buf, (rows, n_buf)), jnp.broadcast_to(ok_new, (rows, LANES))], axis=1)
        s_w = jnp.where(ok, s_w, NEG)
        p_w = jnp.where(ok, jnp.exp(s_w - jnp.max(s_w, axis=-1, keepdims=True)), 0.0)
        p_w = (p_w / jnp.maximum(jnp.sum(p_w, axis=-1, keepdims=True), TINY)).astype(BF16)
        o_win = _nt_dot(p_w[:, :n_buf], win_ref[0, 1].astype(BF16)) + _dot(p_w[:, n_buf:], vwnew_ref[0])
        row_g = (row // ts) % N_KV_HEADS

        def own_group(o):
            out = jnp.zeros((rows, HEAD_DIM), F32)
            for g in range(N_KV_HEADS):
                out = out + jnp.where(row_g == g, o[:, g * HEAD_DIM:(g + 1) * HEAD_DIM], 0.0)
            return out

        o_ref[0] = (gate_ref[0, 0] * own_group(ocmp_s[...]) + gate_ref[0, 1] * own_group(o_sel)
                    + gate_ref[0, 2] * own_group(o_win))


def _attn_sample(cache5, page_table, qn_bd, qr_bd, ck_t, cv_t, win4, knew, vnew, kwnew, vwnew, gates,
                 *, n_pages, ts, past):
    b, pages_per_seq = page_table.shape
    steps = pages_per_seq // n_pages
    rows = qn_bd.shape[1]
    n_chunk = ck_t.shape[2]
    n_cmp = n_chunk - CMP_BLOCK // CMP_STRIDE + 1
    n_keys = past + LANES
    n_sb = -(-(past + ts) // SEL_BLOCK)
    n_sb_pad = -(-n_sb // LANES) * LANES
    n_buf = win4.shape[3]
    ratio = SEL_BLOCK // CMP_STRIDE
    ci = jnp.arange(n_chunk)[:, None]
    jb = jnp.arange(n_sb_pad)[None, :]
    sel_map = ((ci >= ratio * jb + 1 - CMP_BLOCK // CMP_STRIDE) & (ci < ratio * jb + ratio)
               & (ci < n_cmp) & (jb < n_sb)).astype(F32)
    expand = (jnp.arange(n_keys)[None, :] // SEL_BLOCK == jnp.arange(n_sb_pad)[:, None]).astype(BF16)

    def page_spec(j):
        return pl.BlockSpec((1, 2, N_KV_HEADS, HEAD_DIM, PAGE_SIZE),
                            lambda i, q, pt: (pt[i, q * n_pages + j], 1, 0, 0, 0))

    per_seq = lambda a: pl.BlockSpec((1,) + a.shape[1:], lambda i, q, pt: (i,) + (0,) * (a.ndim - 1))
    full = lambda a: pl.BlockSpec(a.shape, lambda i, q, pt: (0,) * a.ndim)
    return pl.pallas_call(
        functools.partial(_attn_sample_kernel, n_pages=n_pages, n_cmp=n_cmp, n_sb=n_sb, ts=ts, past=past,
                          n_buf=n_buf),
        grid_spec=pltpu.PrefetchScalarGridSpec(
            num_scalar_prefetch=1,
            grid=(b, steps),
            in_specs=[page_spec(j) for j in range(n_pages)]
            + [per_seq(a) for a in (qn_bd, qr_bd, ck_t, cv_t, win4, knew, vnew, kwnew, vwnew, gates)]
            + [full(sel_map), full(expand)],
            out_specs=pl.BlockSpec((1, rows, HEAD_DIM), lambda i, q, pt: (i, 0, 0)),
            scratch_shapes=[
                pltpu.VMEM((N_KV_HEADS * ts, n_keys), F32),
                pltpu.VMEM((rows, 1), F32), pltpu.VMEM((rows, 1), F32),
                pltpu.VMEM((rows, KV_WIDTH), F32), pltpu.VMEM((rows, KV_WIDTH), F32),
            ],
        ),
        out_shape=jax.ShapeDtypeStruct((b, rows, HEAD_DIM), F32),
        compiler_params=_params(("arbitrary", "arbitrary"), VMEM_LIMIT),
        name="attn_sample",
    )(page_table, *([cache5] * n_pages), qn_bd, qr_bd, ck_t, cv_t, win4, knew, vnew, kwnew, vwnew, gates,
      sel_map, expand)


def _router_weights(router_group, router_expert):
    rt = jnp.concatenate([router_group, router_expert], axis=1).T
    return jnp.pad(rt, ((0, 24 - rt.shape[0]), (0, 0)))


def _trunk_prompt(x, p, *, tt_pool=256, tm=1024, sub=512, tt=512, tq=256):
    (norm_mix, norm_ffn, pool_w, pool_scale, kv_norm, w_kv, k_norm, cmp_pe, cmp_w1, cmp_b1, cmp_w2,
     w_qg, q_norm, w_o, router_group, router_expert, w_gate_up, w_down) = p
    b, t, d = x.shape
    n = b * t
    pos = jnp.arange(t)
    h, new_pool = _pool_layer(x, jnp.zeros((b, POOL_HALO, d), F32), norm_mix[0], pool_w[0], pool_scale[0],
                              tt=tt_pool, clip=True)
    h = _moe(h.reshape(n, d), norm_ffn[0], _router_weights(router_group[0], router_expert[0]),
             w_gate_up, w_down, 0, tm=tm, sub=sub)
    cos_t, sin_t = _rope_tables_transposed(pos)
    kv_t, win_t, craw, ksel, kwin, vsel_t, vwin_t = _kvproj(h.reshape(b, t, d), kv_norm, w_kv, k_norm,
                                                            cos_t, sin_t, tt=tt)
    ab = _cmp_ab_prompt(craw, _cmp_weights(cmp_w1), b=b, t=t)
    ck, cv = _cmp_finish(ab, cmp_pe, cmp_w1, cmp_b1, cmp_w2, k_norm[0])
    qn_t, qr_t, g_t = _qproj(h, norm_mix[1], _qg_weights(w_qg[0]), q_norm[0], cos_t, sin_t,
                             tt=tt, pos_blocks=t // tt)
    o_t = _attn_prompt(qn_t, qr_t, g_t, ck, jnp.swapaxes(cv, 2, 3), ksel, kwin, vsel_t, vwin_t, b=b, t=t, tq=tq)
    h = _oproj_t(o_t, h, w_o[0], tt=tt)
    h = _moe(h, norm_ffn[1], _router_weights(router_group[1], router_expert[1]),
             w_gate_up, w_down, 1, tm=tm, sub=sub)
    n_win = min(WINDOW, t)
    kv_new = kv_t.reshape(b, 4, N_KV_HEADS, HEAD_DIM, t).transpose(0, 4, 1, 2, 3)
    win_new = win_t[:, :, t - n_win:].reshape(b, 2, N_KV_HEADS, HEAD_DIM, n_win).transpose(0, 4, 1, 2, 3)
    return h.reshape(b, t, d), new_pool[None], kv_new, win_new


def _trunk_sample(x, state_pool, cache_kv, page_table, state_win, p, *, n_pages=16):
    (norm_mix, norm_ffn, pool_w, pool_scale, kv_norm, w_kv, k_norm, cmp_pe, cmp_w1, cmp_b1, cmp_w2,
     w_qg, q_norm, w_o, router_group, router_expert, w_gate_up, w_down) = p
    b, ts, d = x.shape
    n = b * ts
    past = page_table.shape[1] * PAGE_SIZE
    n_buf = state_win.shape[1]
    prev16 = jnp.pad(state_pool[0], ((0, 0), (POOL_HALO - POOL_STATE, 0), (0, 0)))
    h, new_pool = _pool_layer(x, prev16, norm_mix[0], pool_w[0], pool_scale[0], tt=ts, clip=False)
    h = _moe(h.reshape(n, d), norm_ffn[0], _router_weights(router_group[0], router_expert[0]),
             w_gate_up, w_down, 0, tm=n, sub=n)
    cos_t, sin_t = _rope_tables_transposed(past + jnp.arange(ts))
    cos_t = jnp.tile(cos_t, (1, b))
    sin_t = jnp.tile(sin_t, (1, b))
    kv_t, win_t, _, _, _, _, _ = _kvproj(h.reshape(1, n, d), kv_norm, w_kv, k_norm, cos_t, sin_t, tt=n)
    kv_rows = kv_t[0].T
    win_rows = win_t[0].T

    cache5 = cache_kv.transpose(0, 2, 3, 4, 1)
    ab = _cmp_ab_pages(cache5, page_table, _cmp_weights(cmp_w1), n_pages=n_pages)
    ck, cv = _cmp_finish(ab, cmp_pe, cmp_w1, cmp_b1, cmp_w2, k_norm[0])
    n_chunk = ck.shape[2]
    ck_t = ck.transpose(0, 1, 3, 2).reshape(b, KV_WIDTH, n_chunk)
    cv_t = cv.transpose(0, 1, 3, 2).reshape(b, KV_WIDTH, n_chunk)

    qn_t, qr_t, g_t = _qproj(h, norm_mix[1], _qg_weights(w_qg[0]), q_norm[0], cos_t, sin_t, tt=n, pos_blocks=1)

    def block_diag_queries(q_t):
        q5 = q_t.reshape(N_KV_HEADS, HEADS_PER_KV, HEAD_DIM, b, ts).transpose(3, 1, 0, 4, 2)
        eye = jnp.eye(N_KV_HEADS, dtype=q_t.dtype)
        qbd = q5[:, :, :, :, None, :] * eye[None, None, :, None, :, None]
        return qbd.reshape(b, HEADS_PER_KV * N_KV_HEADS * ts, KV_WIDTH)

    gates = g_t.reshape(N_KV_HEADS, GATE_ROWS, b, ts)[:, :3 * HEADS_PER_KV]
    gates = gates.reshape(N_KV_HEADS, 3, HEADS_PER_KV, b, ts).transpose(3, 1, 2, 0, 4)
    gates = jnp.broadcast_to(gates.reshape(b, 3, HEADS_PER_KV * N_KV_HEADS * ts, 1),
                             (b, 3, HEADS_PER_KV * N_KV_HEADS * ts, HEAD_DIM))

    def new_rows(rows2):
        return jnp.pad(rows2.reshape(b, ts, KV_WIDTH), ((0, 0), (0, LANES - ts), (0, 0))).astype(BF16)

    win4 = state_win.transpose(0, 2, 3, 4, 1).reshape(b, 2, KV_WIDTH, n_buf)
    o = _attn_sample(cache5, page_table, block_diag_queries(qn_t), block_diag_queries(qr_t), ck_t, cv_t, win4,
                     new_rows(kv_rows[:, 2 * KV_WIDTH:3 * KV_WIDTH]), new_rows(kv_rows[:, 3 * KV_WIDTH:]),
                     new_rows(win_rows[:, :KV_WIDTH]), new_rows(win_rows[:, KV_WIDTH:]), gates,
                     n_pages=n_pages, ts=ts, past=past)
    o = o.reshape(b, HEADS_PER_KV, N_KV_HEADS, ts, HEAD_DIM).transpose(0, 3, 2, 1, 4).reshape(n, N_HEADS * HEAD_DIM)
    h = _oproj_n(o, h, w_o[0])
    h = _moe(h, norm_ffn[1], _router_weights(router_group[1], router_expert[1]),
             w_gate_up, w_down, 1, tm=n, sub=n)
    kv_new = kv_rows.reshape(b, ts, 4, N_KV_HEADS, HEAD_DIM)
    win_new = jnp.concatenate([state_win, win_rows.reshape(b, ts, 2, N_KV_HEADS, HEAD_DIM)], axis=1)[:, -n_buf:]
    return h.reshape(b, ts, d), new_pool[None], kv_new, win_new


def kernel(x_prompt, x_sample, state_pool, cache_kv, page_table, state_win, norm_mix, norm_ffn, pool_w, pool_scale, kv_norm, w_kv, k_norm, cmp_pe, cmp_w1, cmp_b1, cmp_w2, w_qg, q_norm, w_o, router_group, router_expert, w_gate_up, w_down):
    params = (norm_mix, norm_ffn, pool_w, pool_scale, kv_norm, w_kv, k_norm, cmp_pe, cmp_w1, cmp_b1,
              cmp_w2, w_qg, q_norm, w_o, router_group, router_expert, w_gate_up, w_down)
    y_p, pool_p, kv_p, win_p = _trunk_prompt(x_prompt, params)
    y_s, pool_s, kv_s, win_s = _trunk_sample(x_sample, state_pool, cache_kv, page_table, state_win, params)
    return y_p, y_s, pool_p, pool_s, kv_p, kv_s, win_p, win_s
```

```python
import functools

import jax
import jax.numpy as jnp
from jax import lax
from jax.experimental import pallas as pl
from jax.experimental.pallas import tpu as pltpu

F32 = jnp.float32
BF16 = jnp.bfloat16
I32 = jnp.int32
HIGHEST = lax.Precision.HIGHEST

D_MODEL = 1024
POOL_WINDOWS = (2, 4, 8, 16)
POOL_GROUP_DIM = D_MODEL // len(POOL_WINDOWS)
POOL_STATE = max(POOL_WINDOWS) - 1
POOL_HALO = 16
N_HEADS = 16
HEAD_DIM = 64
HALF_DIM = HEAD_DIM // 2
N_KV_HEADS = 4
HEADS_PER_KV = N_HEADS // N_KV_HEADS
KV_WIDTH = N_KV_HEADS * HEAD_DIM
CMP_BLOCK = 32
CMP_STRIDE = 16
CMP_HIDDEN = 2 * HEAD_DIM
SEL_BLOCK = 64
N_SEL = 16
WINDOW = 512
PAGE_SIZE = 128
ROPE_THETA = 10000.0
SCALE = HEAD_DIM ** -0.5
N_GROUPS = 4
EXPERTS_PER_GROUP = 4
N_EXPERTS = N_GROUPS * EXPERTS_PER_GROUP
D_EXPERT = 512
EPS = 1e-6
NEG = -1e30
TINY = 1e-30
BIG = 1e4

LANES = 128
GATE_ROWS = 16
ROUTE_GROUP_ROW = 3 * N_EXPERTS
MOE_SEG_ALIGN = 64
MOE_CHUNK = 128
MOE_STEP_EXPERTS = 2
ATTN_CHAIN_HEADS = 4
V_PAD_ROWS = 16
VMEM_LIMIT = 56 * 1024 * 1024


def _params(sem, vmem=None):
    return pltpu.CompilerParams(dimension_semantics=sem, vmem_limit_bytes=vmem)


def _rms(x, g):
    return x * lax.rsqrt(jnp.mean(x * x, axis=-1, keepdims=True) + EPS) * g


def _nt_dot(a, b, precision=None):
    return lax.dot_general(a, b, (((1,), (1,)), ((), ())), precision=precision,
                           preferred_element_type=F32)


def _dot(a, b, precision=None):
    return jnp.dot(a, b, precision=precision, preferred_element_type=F32)


def _sigmoid(x):
    return 1.0 / (1.0 + jnp.exp(-x))


def _pool_kernel(h_ref, prev_ref, g_ref, w_ref, sc_ref, o_ref, np_ref, ext_ref, *, tt, clip):
    t = pl.program_id(1)
    x = h_ref[0]
    xn = _rms(x, g_ref[...])

    @pl.when(t == 0)
    def _():
        ext_ref[0:POOL_HALO, :] = prev_ref[0]

    @pl.when(t > 0)
    def _():
        ext_ref[0:POOL_HALO, :] = ext_ref[tt:tt + POOL_HALO, :]

    ext_ref[POOL_HALO:POOL_HALO + tt, :] = xn
    if clip:
        tpos = t * tt + lax.broadcasted_iota(I32, (tt, 1), 0)
    outs = []
    for gi, w in enumerate(POOL_WINDOWS):
        lo = gi * POOL_GROUP_DIM
        hi = lo + POOL_GROUP_DIM
        acc = ext_ref[POOL_HALO:POOL_HALO + tt, lo:hi]
        for j in range(1, w):
            acc = acc + ext_ref[POOL_HALO - j:POOL_HALO - j + tt, lo:hi]
        if clip:
            mean = acc / jnp.minimum(tpos + 1, w).astype(F32)
        else:
            mean = acc * (1.0 / w)
        d = (mean - xn[:, lo:hi]).astype(BF16)
        outs.append(_dot(d, w_ref[gi].astype(BF16)))
    o_ref[0] = x + jnp.concatenate(outs, axis=1) * sc_ref[...]

    @pl.when(t == pl.num_programs(1) - 1)
    def _():
        np_ref[0] = ext_ref[tt + POOL_HALO - POOL_STATE:tt + POOL_HALO, :]


def _pool_layer(h, prev16, gain, w_pool, scale, *, tt, clip):
    b, t, d = h.shape
    return pl.pallas_call(
        functools.partial(_pool_kernel, tt=tt, clip=clip),
        grid=(b, t // tt),
        in_specs=[
            pl.BlockSpec((1, tt, d), lambda i, j: (i, j, 0)),
            pl.BlockSpec((1, POOL_HALO, d), lambda i, j: (i, 0, 0)),
            pl.BlockSpec((1, d), lambda i, j: (0, 0)),
            pl.BlockSpec(w_pool.shape, lambda i, j: (0, 0, 0)),
            pl.BlockSpec((1, d), lambda i, j: (0, 0)),
        ],
        out_specs=[
            pl.BlockSpec((1, tt, d), lambda i, j: (i, j, 0)),
            pl.BlockSpec((1, POOL_STATE, d), lambda i, j: (i, 0, 0)),
        ],
        out_shape=[jax.ShapeDtypeStruct((b, t, d), F32),
                   jax.ShapeDtypeStruct((b, POOL_STATE, d), F32)],
        scratch_shapes=[pltpu.VMEM((tt + POOL_HALO, d), F32)],
        compiler_params=_params(("arbitrary", "arbitrary")),
        name="pool_layer",
    )(h, prev16, gain.reshape(1, d), w_pool, scale.reshape(1, d))


def _softmax_rows(rows):
    m = functools.reduce(jnp.maximum, rows)
    es = [jnp.exp(r - m) for r in rows]
    s = functools.reduce(lambda a, b: a + b, es)
    return [e / s for e in es]


def _router_kernel(h_ref, g_ref, rt_ref, xn_ref, comb_ref, grp_ref):
    xn = _rms(h_ref[...], g_ref[...])
    xn_ref[...] = xn.astype(BF16)
    lt = _nt_dot(rt_ref[...], xn, precision=HIGHEST)
    pg = _softmax_rows([lt[i:i + 1, :] for i in range(N_GROUPS)])
    g_val = functools.reduce(jnp.maximum, pg)
    g_idx = jnp.full(g_val.shape, N_GROUPS - 1, I32)
    for i in range(N_GROUPS - 2, -1, -1):
        g_idx = jnp.where(pg[i] == g_val, i, g_idx)
    le = []
    for j in range(EXPERTS_PER_GROUP):
        v = lt[N_GROUPS + (N_GROUPS - 1) * EXPERTS_PER_GROUP + j:N_GROUPS + (N_GROUPS - 1) * EXPERTS_PER_GROUP + j + 1, :]
        for gi in range(N_GROUPS - 2, -1, -1):
            r = N_GROUPS + gi * EXPERTS_PER_GROUP + j
            v = jnp.where(g_idx == gi, lt[r:r + 1, :], v)
        le.append(v)
    pe = _softmax_rows(le)
    ranks = []
    for j in range(EXPERTS_PER_GROUP):
        r = jnp.zeros(g_val.shape, I32)
        for i in range(EXPERTS_PER_GROUP):
            if i == j:
                continue
            beats = (pe[i] > pe[j]) | (pe[i] == pe[j]) if i < j else (pe[i] > pe[j])
            r = r + beats.astype(I32)
        ranks.append(r)
    vals, idxs = [], []
    for k in range(2):
        v = jnp.zeros(g_val.shape, F32)
        ix = jnp.zeros(g_val.shape, I32)
        for j in range(EXPERTS_PER_GROUP):
            hit = ranks[j] == k
            v = jnp.where(hit, pe[j], v)
            ix = jnp.where(hit, j, ix)
        vals.append(v)
        idxs.append(ix)
    tot = vals[0] + vals[1]
    erow = lax.broadcasted_iota(I32, (LANES, g_val.shape[1]), 0)
    comb_t = jnp.where(erow == ROUTE_GROUP_ROW + g_idx, 1.0, 0.0)
    for k in range(2):
        wk = g_val * (vals[k] / tot)
        w_hi = wk.astype(BF16).astype(F32)
        w_mid = (wk - w_hi).astype(BF16).astype(F32)
        w_lo = (wk - w_hi - w_mid).astype(BF16).astype(F32)
        eid = g_idx * EXPERTS_PER_GROUP + idxs[k]
        for part, term in enumerate((w_hi, w_mid, w_lo)):
            comb_t = comb_t + jnp.where(erow == part * N_EXPERTS + eid, term, 0.0)
    comb_ref[...] = comb_t.T
    grp_ref[...] = comb_t[ROUTE_GROUP_ROW:ROUTE_GROUP_ROW + 8]


def _router(h2, gain, rt, *, tm):
    n, d = h2.shape
    return pl.pallas_call(
        _router_kernel,
        grid=(n // tm,),
        in_specs=[
            pl.BlockSpec((tm, d), lambda i: (i, 0)),
            pl.BlockSpec((1, d), lambda i: (0, 0)),
            pl.BlockSpec(rt.shape, lambda i: (0, 0)),
        ],
        out_specs=[
            pl.BlockSpec((tm, d), lambda i: (i, 0)),
            pl.BlockSpec((tm, LANES), lambda i: (i, 0)),
            pl.BlockSpec((8, tm), lambda i: (0, i)),
        ],
        out_shape=[jax.ShapeDtypeStruct((n, d), BF16), jax.ShapeDtypeStruct((n, LANES), F32),
                   jax.ShapeDtypeStruct((8, n), F32)],
        compiler_params=_params(("arbitrary",)),
        name="moe_router",
    )(h2, gain.reshape(1, d), rt)


def _cast_kernel(x_ref, o_ref):
    o_ref[...] = x_ref[...].astype(o_ref.dtype)


def _cast_bf16(w):
    blk = (1, 1) + w.shape[2:]
    return pl.pallas_call(
        _cast_kernel,
        grid=w.shape[:2],
        in_specs=[pl.BlockSpec(blk, lambda i, j: (i, j, 0, 0))],
        out_specs=pl.BlockSpec(blk, lambda i, j: (i, j, 0, 0)),
        out_shape=jax.ShapeDtypeStruct(w.shape, BF16),
        compiler_params=_params(("arbitrary", "arbitrary")),
        name="cast_weights",
    )(w)


def _moe_kernel(cnt_ref, base_ref, x_ref, comb_ref, grp_ref, h_ref, wgu_ref, wd_ref, o_ref,
                pt_s, z_s, y_s, cs_s, slot_s, *, r_max):
    w = pl.program_id(0)
    step = pl.program_id(1)
    nw = x_ref.shape[0]
    g = step // (EXPERTS_PER_GROUP // MOE_STEP_EXPERTS)

    @pl.when(step == 0)
    def _():
        lane = lax.broadcasted_iota(I32, (1, LANES), 1)
        in_grp = (lane >= ROUTE_GROUP_ROW) & (lane < ROUTE_GROUP_ROW + N_GROUPS)
        oh = jnp.where(in_grp, comb_ref[...], 0.0)
        base_row = jnp.zeros((1, LANES), F32)
        for gi in range(N_GROUPS):
            base_row = jnp.where(lane == ROUTE_GROUP_ROW + gi, base_ref[w, gi].astype(F32), base_row)
        ri = lax.broadcasted_iota(I32, (nw, nw), 0)
        ci = lax.broadcasted_iota(I32, (nw, nw), 1)
        before = _dot(jnp.where(ci < ri, 1.0, 0.0).astype(BF16), oh.astype(BF16))
        slot_s[...] = jnp.sum(oh * (before + base_row), axis=-1, keepdims=True)
        oh_t = grp_ref[...]
        before_t = _dot(oh_t.astype(BF16), jnp.where(ri < ci, 1.0, 0.0).astype(BF16))
        slot_row = jnp.zeros((1, nw), F32)
        for gi in range(N_GROUPS):
            slot_row = slot_row + oh_t[gi:gi + 1] * (before_t[gi:gi + 1] + base_ref[w, gi].astype(F32))
        rows = lax.broadcasted_iota(I32, (r_max, 1), 0)
        pt_s[...] = jnp.where(rows == slot_row.astype(I32), 1.0, 0.0).astype(BF16)
        z_s[...] = _dot(pt_s[...], x_ref[...]).astype(BF16)
        cs_s[...] = _dot(pt_s[...], comb_ref[...].astype(BF16))
        y_s[...] = jnp.zeros(y_s.shape, F32)

    seg0 = base_ref[w, g]
    seg1 = seg0 + cnt_ref[w, g]
    lane_c = lax.broadcasted_iota(I32, (MOE_CHUNK, LANES), 1)
    in_terms = lane_c < 3 * N_EXPERTS

    def chunk(c, carry):
        r0 = pl.multiple_of(seg0 + c * MOE_CHUNK, MOE_SEG_ALIGN)
        z = z_s[pl.ds(r0, MOE_CHUNK), :]
        cs = cs_s[pl.ds(r0, MOE_CHUNK), :]
        acc = None
        for k in range(MOE_STEP_EXPERTS):
            gu = _dot(z, wgu_ref[0, k])
            a = gu[:, :D_EXPERT]
            hdn = (a * _sigmoid(a) * gu[:, D_EXPERT:]).astype(BF16)
            y = _dot(hdn, wd_ref[0, k])
            is_e = in_terms & (lane_c % N_EXPERTS == step * MOE_STEP_EXPERTS + k)
            wgt = jnp.sum(jnp.where(is_e, cs, 0.0), axis=-1, keepdims=True)
            acc = wgt * y if acc is None else acc + wgt * y
        valid = r0 + lax.broadcasted_iota(I32, (MOE_CHUNK, 1), 0) < seg1
        y_s[pl.ds(r0, MOE_CHUNK), :] += jnp.where(valid, acc, 0.0)
        return carry

    lax.fori_loop(0, (cnt_ref[w, g] + MOE_CHUNK - 1) // MOE_CHUNK, chunk, 0)

    @pl.when(step == pl.num_programs(1) - 1)
    def _():
        cols = lax.broadcasted_iota(I32, (1, r_max), 1)
        p = jnp.where(cols == slot_s[...].astype(I32), 1.0, 0.0).astype(BF16)
        y = y_s[...]
        y_hi = y.astype(BF16)
        y_lo = (y - y_hi.astype(F32)).astype(BF16)
        o_ref[...] = h_ref[...] + (_dot(p, y_hi) + _dot(p, y_lo))


def _moe(h2, gain, rt, w_gu, w_down, layer, *, tm):
    n, d = h2.shape
    if w_gu.dtype != BF16:
        w_gu, w_down = _cast_bf16(w_gu), _cast_bf16(w_down)
    xn, comb, grp = _router(h2, gain, rt, tm=tm)
    n_win = n // tm
    cnt = jnp.sum(grp[:N_GROUPS].reshape(N_GROUPS, n_win, tm), axis=-1).T.astype(I32)
    seg = (cnt + MOE_SEG_ALIGN - 1) // MOE_SEG_ALIGN * MOE_SEG_ALIGN
    base = jnp.cumsum(seg, axis=1) - seg
    r_max = -(-(tm + N_GROUPS * MOE_SEG_ALIGN + MOE_CHUNK) // LANES) * LANES
    return pl.pallas_call(
        functools.partial(_moe_kernel, r_max=r_max),
        grid_spec=pltpu.PrefetchScalarGridSpec(
            num_scalar_prefetch=2,
            grid=(n_win, N_EXPERTS // MOE_STEP_EXPERTS),
            in_specs=[
                pl.BlockSpec((tm, d), lambda i, e, c, b: (i, 0)),
                pl.BlockSpec((tm, LANES), lambda i, e, c, b: (i, 0)),
                pl.BlockSpec((8, tm), lambda i, e, c, b: (0, i)),
                pl.BlockSpec((tm, d), lambda i, e, c, b: (i, 0)),
                pl.BlockSpec((1, MOE_STEP_EXPERTS, d, 2 * D_EXPERT), lambda i, e, c, b: (layer, e, 0, 0)),
                pl.BlockSpec((1, MOE_STEP_EXPERTS, D_EXPERT, d), lambda i, e, c, b: (layer, e, 0, 0)),
            ],
            out_specs=pl.BlockSpec((tm, d), lambda i, e, c, b: (i, 0)),
            scratch_shapes=[
                pltpu.VMEM((r_max, tm), BF16), pltpu.VMEM((r_max, d), BF16), pltpu.VMEM((r_max, d), F32),
                pltpu.VMEM((r_max, LANES), F32), pltpu.VMEM((tm, 1), F32),
            ],
        ),
        out_shape=jax.ShapeDtypeStruct((n, d), F32),
        compiler_params=_params(("arbitrary", "arbitrary"), VMEM_LIMIT),
        name="moe_experts",
    )(cnt, base, xn, comb, grp, h2, w_gu, w_down)


def _head_norm_rope_t(z, gain, c, s):
    outs = []
    for g in range(z.shape[0] // HEAD_DIM):
        zh = z[g * HEAD_DIM:(g + 1) * HEAD_DIM]
        zn = zh * lax.rsqrt(jnp.mean(zh * zh, axis=0, keepdims=True) + EPS) * gain
        x1 = zn[:HALF_DIM]
        x2 = zn[HALF_DIM:]
        outs.append(x1 * c - x2 * s)
        outs.append(x2 * c + x1 * s)
    return jnp.concatenate(outs, axis=0)


def _kvproj_kernel(h_ref, g_ref, wt_ref, gsel_ref, gwin_ref, cos_ref, sin_ref,
                   kvt_ref, wint_ref, craw_ref, ksel_ref, kwin_ref, vselt_ref, vwint_ref, w_s):
    @pl.when((pl.program_id(0) == 0) & (pl.program_id(1) == 0))
    def _():
        w_s[...] = wt_ref[...].astype(BF16)

    xn = _rms(h_ref[0], g_ref[...]).astype(BF16)
    kvt = _nt_dot(w_s[...], xn)
    tt = xn.shape[0]
    c = cos_ref[...]
    s = sin_ref[...]
    gsel = jnp.concatenate([gsel_ref[...]] * (tt // LANES), axis=1)
    gwin = jnp.concatenate([gwin_ref[...]] * (tt // LANES), axis=1)
    ksel = _head_norm_rope_t(kvt[2 * KV_WIDTH:3 * KV_WIDTH], gsel, c, s)
    kwin = _head_norm_rope_t(kvt[4 * KV_WIDTH:5 * KV_WIDTH], gwin, c, s)
    kvt_ref[0] = jnp.concatenate([kvt[:2 * KV_WIDTH], ksel, kvt[3 * KV_WIDTH:4 * KV_WIDTH]], axis=0)
    wint_ref[0] = jnp.concatenate([kwin, kvt[5 * KV_WIDTH:]], axis=0)
    vselt_ref[...] = kvt[3 * KV_WIDTH:4 * KV_WIDTH].astype(BF16)
    vwint_ref[...] = kvt[5 * KV_WIDTH:].astype(BF16)
    raw = kvt[:2 * KV_WIDTH].T
    for k in range(2 * KV_WIDTH // LANES):
        craw_ref[k] = raw[:, k * LANES:(k + 1) * LANES]
    ksel_n = ksel.T
    kwin_n = kwin.T
    for g in range(N_KV_HEADS):
        ksel_ref[g] = ksel_n[:, g * HEAD_DIM:(g + 1) * HEAD_DIM].astype(BF16)
        kwin_ref[g] = kwin_n[:, g * HEAD_DIM:(g + 1) * HEAD_DIM].astype(BF16)


def _kvproj(x3, gain, w_kv, k_norm, cos_tt, sin_tt, *, tt):
    b, t, d = x3.shape
    n = b * t
    nt = t // tt
    wt = w_kv.T
    gsel = jnp.broadcast_to(k_norm[1][:, None], (HEAD_DIM, LANES))
    gwin = jnp.broadcast_to(k_norm[2][:, None], (HEAD_DIM, LANES))
    full = lambda a: pl.BlockSpec(a.shape, lambda i, j: (0,) * a.ndim)
    return pl.pallas_call(
        _kvproj_kernel,
        grid=(b, nt),
        in_specs=[
            pl.BlockSpec((1, tt, d), lambda i, j: (i, j, 0)),
            pl.BlockSpec((1, d), lambda i, j: (0, 0)),
            full(wt), full(gsel), full(gwin),
            pl.BlockSpec((HALF_DIM, tt), lambda i, j: (0, j)),
            pl.BlockSpec((HALF_DIM, tt), lambda i, j: (0, j)),
        ],
        out_specs=[
            pl.BlockSpec((1, 4 * KV_WIDTH, tt), lambda i, j: (i, 0, j)),
            pl.BlockSpec((1, 2 * KV_WIDTH, tt), lambda i, j: (i, 0, j)),
            pl.BlockSpec((2 * KV_WIDTH // LANES, tt, LANES), lambda i, j: (0, i * nt + j, 0)),
            pl.BlockSpec((N_KV_HEADS, tt, HEAD_DIM), lambda i, j: (0, i * nt + j, 0)),
            pl.BlockSpec((N_KV_HEADS, tt, HEAD_DIM), lambda i, j: (0, i * nt + j, 0)),
            pl.BlockSpec((KV_WIDTH, tt), lambda i, j: (0, i * nt + j)),
            pl.BlockSpec((KV_WIDTH, tt), lambda i, j: (0, i * nt + j)),
        ],
        out_shape=[
            jax.ShapeDtypeStruct((b, 4 * KV_WIDTH, t), F32),
            jax.ShapeDtypeStruct((b, 2 * KV_WIDTH, t), F32),
            jax.ShapeDtypeStruct((2 * KV_WIDTH // LANES, n, LANES), F32),
            jax.ShapeDtypeStruct((N_KV_HEADS, n, HEAD_DIM), BF16),
            jax.ShapeDtypeStruct((N_KV_HEADS, n, HEAD_DIM), BF16),
            jax.ShapeDtypeStruct((KV_WIDTH, n), BF16),
            jax.ShapeDtypeStruct((KV_WIDTH, n), BF16),
        ],
        scratch_shapes=[pltpu.VMEM(wt.shape, BF16)],
        compiler_params=_params(("arbitrary", "arbitrary"), VMEM_LIMIT),
        name="kv_proj",
    )(x3, gain.reshape(1, d), wt, gsel, gwin, cos_tt, sin_tt)


def _rope_tables_transposed(pos):
    inv = 1.0 / (ROPE_THETA ** (jnp.arange(HALF_DIM, dtype=F32) * (2.0 / HEAD_DIM)))
    ang = pos.astype(F32)[:, None] * inv[None, :]
    return jnp.cos(ang).T, jnp.sin(ang).T


def _cmp_ab_accumulate(load_rows, wab_ref, n_rows):
    del n_rows
    heads_per_chunk = LANES // HEAD_DIM
    accs = [[None] * N_KV_HEADS for _ in range(2)]
    for s in range(2):
        for c in range(KV_WIDTH // LANES):
            plane = s * (KV_WIDTH // LANES) + c
            lhs = jnp.concatenate([load_rows(r, plane).astype(BF16) for r in range(CMP_STRIDE)], axis=1)
            out = _dot(lhs, wab_ref[s])
            for k in range(heads_per_chunk):
                accs[s][c * heads_per_chunk + k] = out[:, k * 2 * CMP_HIDDEN:(k + 1) * 2 * CMP_HIDDEN]
    return accs


def _cmpab_kernel(craw_ref, wab_ref, ab_ref, *, n_chunk):
    accs = _cmp_ab_accumulate(lambda r, c: craw_ref[c, pl.ds(r, n_chunk, stride=CMP_STRIDE), :], wab_ref, n_chunk)
    for s in range(2):
        for g in range(N_KV_HEADS):
            ab_ref[0, s, g] = accs[s][g]


def _cmpab_pages_kernel(pt_ref, *refs, n_pages):
    del pt_ref
    page_refs = refs[:n_pages]
    wab_ref, ab_ref, craw_s = refs[n_pages:]
    pairs = KV_WIDTH // LANES
    for j in range(n_pages):
        for s in range(2):
            for k in range(pairs):
                tile = jnp.concatenate([page_refs[j][0, s, 2 * k], page_refs[j][0, s, 2 * k + 1]], axis=0)
                craw_s[s * pairs + k, j * PAGE_SIZE:(j + 1) * PAGE_SIZE, :] = tile.T
    n_chunk = n_pages * PAGE_SIZE // CMP_STRIDE
    accs = _cmp_ab_accumulate(lambda r, c: craw_s[c, pl.ds(r, n_chunk, stride=CMP_STRIDE), :], wab_ref, n_chunk)
    for s in range(2):
        for g in range(N_KV_HEADS):
            ab_ref[0, s, g] = accs[s][g]


def _cmp_ab_pages(cache5, page_table, wab, *, n_pages):
    b, pages_per_seq = page_table.shape
    steps = pages_per_seq // n_pages
    rows = n_pages * PAGE_SIZE // CMP_STRIDE

    def page_spec(j):
        return pl.BlockSpec((1, 2, N_KV_HEADS, HEAD_DIM, PAGE_SIZE),
                            lambda i, q, pt: (pt[i, q * n_pages + j], 0, 0, 0, 0))

    return pl.pallas_call(
        functools.partial(_cmpab_pages_kernel, n_pages=n_pages),
        grid_spec=pltpu.PrefetchScalarGridSpec(
            num_scalar_prefetch=1,
            grid=(b, steps),
            in_specs=[page_spec(j) for j in range(n_pages)]
            + [pl.BlockSpec(wab.shape, lambda i, q, pt: (0, 0, 0))],
            out_specs=pl.BlockSpec((1, 2, N_KV_HEADS, rows, 2 * CMP_HIDDEN), lambda i, q, pt: (i, 0, 0, q, 0)),
            scratch_shapes=[pltpu.VMEM((2 * KV_WIDTH // LANES, n_pages * PAGE_SIZE, LANES), F32)],
        ),
        out_shape=jax.ShapeDtypeStruct((b, 2, N_KV_HEADS, steps * rows, 2 * CMP_HIDDEN), F32),
        compiler_params=_params(("arbitrary", "arbitrary"), VMEM_LIMIT),
        name="cmp_ab_pages",
    )(page_table, *([cache5] * n_pages), wab)


def _cmp_weights(cmp_w1):
    w = cmp_w1.reshape(2, 2, CMP_STRIDE, HEAD_DIM, CMP_HIDDEN)
    w = w.transpose(0, 2, 3, 1, 4).reshape(2, CMP_STRIDE, HEAD_DIM, 2 * CMP_HIDDEN)
    eye = jnp.eye(LANES // HEAD_DIM, dtype=w.dtype)
    w = w[:, :, None, :, None, :] * eye[None, None, :, None, :, None]
    return w.reshape(2, CMP_STRIDE * LANES, (LANES // HEAD_DIM) * 2 * CMP_HIDDEN).astype(BF16)


def _cmp_ab_prompt(craw, wab, *, b, t):
    n_chunk = t // CMP_STRIDE
    return pl.pallas_call(
        functools.partial(_cmpab_kernel, n_chunk=n_chunk),
        grid=(b,),
        in_specs=[
            pl.BlockSpec((craw.shape[0], t, LANES), lambda i: (0, i, 0)),
            pl.BlockSpec(wab.shape, lambda i: (0, 0, 0)),
        ],
        out_specs=pl.BlockSpec((1, 2, N_KV_HEADS, n_chunk, 2 * CMP_HIDDEN), lambda i: (i, 0, 0, 0, 0)),
        out_shape=jax.ShapeDtypeStruct((b, 2, N_KV_HEADS, n_chunk, 2 * CMP_HIDDEN), F32),
        compiler_params=_params(("arbitrary",)),
        name="cmp_ab_prompt",
    )(craw, wab)


def _cmpfin_kernel(ab_ref, pe_ref, w1_ref, b1_ref, w2_ref, gk_ref, ck_ref, cv_ref, *, n_row):
    for s in range(2):
        bias = _dot(pe_ref[s].astype(BF16), w1_ref[s].astype(BF16)) + b1_ref[s]
        w2 = w2_ref[s].astype(BF16)
        for g in range(N_KV_HEADS):
            ab = ab_ref[0, s, g]
            hid = ab[:, :CMP_HIDDEN] + pltpu.roll(ab[:, CMP_HIDDEN:], n_row - 1, 0) + bias
            cdf = 0.5 * (1.0 + jnp.tanh(0.7978845608028654 * (hid + 0.044715 * (hid * hid * hid))))
            out = _dot((hid * cdf).astype(BF16), w2)
            if s == 0:
                ck_ref[0, g] = _rms(out, gk_ref[...]).astype(BF16)
            else:
                cv_ref[0, g] = out.astype(BF16)


def _cmp_finish(ab, cmp_pe, cmp_w1, cmp_b1, cmp_w2, gk):
    b = ab.shape[0]
    n_row = ab.shape[3]
    pe = cmp_pe.reshape(2, 1, CMP_BLOCK * HEAD_DIM)
    full = lambda a: pl.BlockSpec(a.shape, lambda i: (0,) * a.ndim)
    b1 = cmp_b1.reshape(2, 1, CMP_HIDDEN)
    gk2 = gk.reshape(1, HEAD_DIM)
    return pl.pallas_call(
        functools.partial(_cmpfin_kernel, n_row=n_row),
        grid=(b,),
        in_specs=[pl.BlockSpec((1,) + ab.shape[1:], lambda i: (i, 0, 0, 0, 0)),
                  full(pe), full(cmp_w1), full(b1), full(cmp_w2), full(gk2)],
        out_specs=[pl.BlockSpec((1, N_KV_HEADS, n_row, HEAD_DIM), lambda i: (i, 0, 0, 0)),
                   pl.BlockSpec((1, N_KV_HEADS, n_row, HEAD_DIM), lambda i: (i, 0, 0, 0))],
        out_shape=[jax.ShapeDtypeStruct((b, N_KV_HEADS, n_row, HEAD_DIM), BF16),
                   jax.ShapeDtypeStruct((b, N_KV_HEADS, n_row, HEAD_DIM), BF16)],
        compiler_params=_params(("arbitrary",)),
        name="cmp_finish",
    )(ab, pe, cmp_w1, b1, cmp_w2, gk2)


def _qproj_kernel(h_ref, g_ref, wt_ref, gq_ref, cos_ref, sin_ref, qn_ref, qr_ref, gt_ref, w_s):
    @pl.when(pl.program_id(0) == 0)
    def _():
        w_s[...] = wt_ref[...].astype(BF16)

    xn = _rms(h_ref[...], g_ref[...]).astype(BF16)
    qg = _nt_dot(w_s[...], xn)
    tt = xn.shape[0]
    gq = jnp.concatenate([gq_ref[...]] * (tt // LANES), axis=1)
    c = cos_ref[...]
    s = sin_ref[...]
    for h in range(N_HEADS):
        qh = qg[h * HEAD_DIM:(h + 1) * HEAD_DIM]
        qn = qh * lax.rsqrt(jnp.mean(qh * qh, axis=0, keepdims=True) + EPS) * gq
        qn_ref[h * HEAD_DIM:(h + 1) * HEAD_DIM, :] = (qn * SCALE).astype(BF16)
        x1 = qn[:HALF_DIM]
        x2 = qn[HALF_DIM:]
        qr_ref[h * HEAD_DIM:h * HEAD_DIM + HALF_DIM, :] = ((x1 * c - x2 * s) * SCALE).astype(BF16)
        qr_ref[h * HEAD_DIM + HALF_DIM:(h + 1) * HEAD_DIM, :] = ((x2 * c + x1 * s) * SCALE).astype(BF16)
    gt_ref[...] = _sigmoid(qg[N_HEADS * HEAD_DIM:])


def _qg_weights(w_qg):
    nq = N_HEADS * HEAD_DIM
    gates = w_qg[:, nq:].reshape(D_MODEL, 3, N_KV_HEADS, HEADS_PER_KV).transpose(2, 1, 3, 0)
    gates = gates.reshape(N_KV_HEADS, 3 * HEADS_PER_KV, D_MODEL)
    gates = jnp.pad(gates, ((0, 0), (0, GATE_ROWS - 3 * HEADS_PER_KV), (0, 0)))
    return jnp.concatenate([w_qg[:, :nq].T, gates.reshape(N_KV_HEADS * GATE_ROWS, D_MODEL)], axis=0)


def _qproj(h2, gain, wt, q_norm, cos_tt, sin_tt, *, tt, pos_blocks):
    n, d = h2.shape
    nq = N_HEADS * HEAD_DIM
    ng = N_KV_HEADS * GATE_ROWS
    gq = jnp.broadcast_to(q_norm[:, None], (HEAD_DIM, LANES))
    full = lambda a: pl.BlockSpec(a.shape, lambda i: (0,) * a.ndim)
    return pl.pallas_call(
        _qproj_kernel,
        grid=(n // tt,),
        in_specs=[
            pl.BlockSpec((tt, d), lambda i: (i, 0)),
            pl.BlockSpec((1, d), lambda i: (0, 0)),
            full(wt), full(gq),
            pl.BlockSpec((HALF_DIM, tt), lambda i: (0, i % pos_blocks)),
            pl.BlockSpec((HALF_DIM, tt), lambda i: (0, i % pos_blocks)),
        ],
        out_specs=[
            pl.BlockSpec((nq, tt), lambda i: (0, i)),
            pl.BlockSpec((nq, tt), lambda i: (0, i)),
            pl.BlockSpec((ng, tt), lambda i: (0, i)),
        ],
        out_shape=[jax.ShapeDtypeStruct((nq, n), BF16), jax.ShapeDtypeStruct((nq, n), BF16),
                   jax.ShapeDtypeStruct((ng, n), F32)],
        scratch_shapes=[pltpu.VMEM(wt.shape, BF16)],
        compiler_params=_params(("arbitrary",), VMEM_LIMIT),
        name="q_proj",
    )(h2, gain.reshape(1, d), wt, gq, cos_tt, sin_tt)


def _attn_kernel(qn_ref, qr_ref, gt_ref, ck_ref, cvt_ref, kaug_ref, kwin_ref, vaug_ref, vwaug_ref,
                 o_ref, pg_s, *, tq, n_cmp, n_sb):
    hp_n = HEADS_PER_KV
    qt = pl.program_id(2)
    t0 = qt * tq
    tpos = t0 + lax.broadcasted_iota(I32, (1, tq), 1)
    tpos4 = jnp.concatenate([tpos] * hp_n, axis=1)
    qn4 = jnp.concatenate([qn_ref[h * HEAD_DIM:(h + 1) * HEAD_DIM, :] for h in range(hp_n)], axis=1)
    qr4 = jnp.concatenate([qr_ref[h * HEAD_DIM:(h + 1) * HEAD_DIM, :] for h in range(hp_n)], axis=1)

    n_row = ck_ref.shape[2]
    s = _dot(ck_ref[0, 0], qn4)
    ci = lax.broadcasted_iota(I32, (n_row, 1), 0)
    vis = (ci * CMP_STRIDE + (CMP_BLOCK - 1) <= tpos4) & (ci < n_cmp)
    s = jnp.where(vis, s, NEG)
    p = jnp.where(vis, jnp.exp(s - jnp.max(s, axis=0, keepdims=True)), 0.0)
    p = p / jnp.maximum(jnp.sum(p, axis=0, keepdims=True), TINY)
    o_cmp = _dot(cvt_ref[0, 0], p.astype(BF16))
    pg = p[:, 0:tq]
    for h in range(1, hp_n):
        pg = pg + p[:, h * tq:(h + 1) * tq]

    ratio = SEL_BLOCK // CMP_STRIDE
    scores = []
    for c in range(tq // LANES):
        pg_s[c, 0:8, :] = jnp.zeros((8, LANES), F32)
        pg_s[c, 8:8 + n_row, :] = pg[:, c * LANES:(c + 1) * LANES]
        sc_c = pg_s[c, pl.ds(8 + 1 - CMP_BLOCK // CMP_STRIDE, n_sb, stride=ratio), :]
        for o in range(2 - CMP_BLOCK // CMP_STRIDE, ratio):
            sc_c = sc_c + pg_s[c, pl.ds(8 + o, n_sb, stride=ratio), :]
        scores.append(sc_c)
    score = jnp.concatenate(scores, axis=1)
    jrow = lax.broadcasted_iota(I32, (n_sb, 1), 0)
    cur = tpos // SEL_BLOCK
    forced = (jrow == 0) | (jrow == cur) | (jrow == cur - 1)
    sc = jnp.where(jrow * SEL_BLOCK <= tpos, score + jnp.where(forced, BIG, 0.0), -BIG)
    rank = jnp.zeros((n_sb, tq), I32)
    for jp in range(n_sb):
        row = sc[jp:jp + 1, :]
        beats = (row > sc) | ((row == sc) & (jrow > jp))
        rank = rank + beats.astype(I32)
    n_sbp = kaug_ref.shape[3] - HEAD_DIM
    sel_neg = jnp.where(rank < min(N_SEL, n_sb), 0.0, NEG)
    if n_sbp > n_sb:
        sel_neg = jnp.concatenate([sel_neg, jnp.zeros((n_sbp - n_sb, tq), F32)], axis=0)
    sel_neg = sel_neg.astype(BF16)
    n_chain = hp_n // ATTN_CHAIN_HEADS
    cw = ATTN_CHAIN_HEADS * tq

    def chain_lanes(per_head):
        return [jnp.concatenate(per_head[c * ATTN_CHAIN_HEADS:(c + 1) * ATTN_CHAIN_HEADS], axis=1)
                for c in range(n_chain)]

    q_rot = [qr_ref[h * HEAD_DIM:(h + 1) * HEAD_DIM, :] for h in range(hp_n)]
    q_aug = chain_lanes([jnp.concatenate([q, sel_neg], axis=0) for q in q_rot])
    q_win = chain_lanes(q_rot)

    def update(k_tile, v_tile, q, bias, m, acc):
        sk = _dot(k_tile, q)
        if bias is not None:
            sk = sk + bias
        m_new = jnp.maximum(m, jnp.max(sk, axis=0, keepdims=True))
        pk = jnp.exp(sk - m_new).astype(BF16)
        return m_new, acc * jnp.exp(m - m_new) + _dot(v_tile, pk)

    def sel_tile(kt, bias, carry):
        k0 = pl.multiple_of(kt * tq, tq)
        k_tile = kaug_ref[0, 0, pl.ds(k0, tq), :]
        v_tile = vaug_ref[:, pl.ds(k0, tq)]
        return tuple(update(k_tile, v_tile, q_aug[c], bias, *carry[c]) for c in range(n_chain))

    v_rows = vaug_ref.shape[0]
    init = tuple((jnp.full((1, cw), NEG, F32), jnp.zeros((v_rows, cw), F32)) for _ in range(n_chain))
    carry = lax.fori_loop(0, qt, lambda kt, c: sel_tile(kt, None, c), init)
    causal = jnp.where(lax.broadcasted_iota(I32, (tq, 1), 0) <= lax.broadcasted_iota(I32, (1, tq), 1), 0.0, NEG)
    carry = sel_tile(qt, jnp.concatenate([causal] * ATTN_CHAIN_HEADS, axis=1), carry)
    o_sel = jnp.concatenate([acc[:HEAD_DIM] / jnp.maximum(acc[HEAD_DIM:HEAD_DIM + 1], TINY) for _, acc in carry],
                            axis=1)

    n_wt = WINDOW // tq + 1
    w0 = pl.multiple_of(jnp.clip(qt - (n_wt - 1), 0, pl.num_programs(2) - n_wt) * tq, tq)
    dq = tpos - (w0 + lax.broadcasted_iota(I32, (n_wt * tq, 1), 0))
    wbias = jnp.where((dq >= 0) & (dq < WINDOW), 0.0, NEG)
    kw_tile = kwin_ref[0, 0, pl.ds(w0, n_wt * tq), :]
    vw_tile = vwaug_ref[:, pl.ds(w0, n_wt * tq)]
    wbias_c = jnp.concatenate([wbias] * ATTN_CHAIN_HEADS, axis=1)
    o_win = []
    for c in range(n_chain):
        sw = _dot(kw_tile, q_win[c]) + wbias_c
        pw = jnp.exp(sw - jnp.max(sw, axis=0, keepdims=True)).astype(BF16)
        acc_win = _dot(vw_tile, pw)
        o_win.append(acc_win[:HEAD_DIM] / jnp.maximum(acc_win[HEAD_DIM:HEAD_DIM + 1], TINY))
    o_win = jnp.concatenate(o_win, axis=1)
    gt = gt_ref[...]
    for h in range(hp_n):
        sl = slice(h * tq, (h + 1) * tq)
        o = (gt[h:h + 1] * o_cmp[:, sl] + gt[hp_n + h:hp_n + h + 1] * o_sel[:, sl]
             + gt[2 * hp_n + h:2 * hp_n + h + 1] * o_win[:, sl])
        o_ref[h * HEAD_DIM:(h + 1) * HEAD_DIM, :] = o.astype(BF16)


def _attn_prompt(qn_t, qr_t, g_t, ck, cv_t, ksel, kwin, vsel_t, vwin_t, *, b, t, tq):
    nq = t // tq
    n_sb = t // SEL_BLOCK
    n_row = ck.shape[2]
    n_cmp = t // CMP_STRIDE - CMP_BLOCK // CMP_STRIDE + 1
    n_sbp = -(-n_sb // 32) * 32
    kw = N_KV_HEADS * HEAD_DIM
    onehot = (jnp.arange(t)[:, None] // SEL_BLOCK == jnp.arange(n_sbp)[None, :]).astype(BF16)
    kaug = jnp.concatenate([ksel.reshape(N_KV_HEADS, b, t, HEAD_DIM),
                            jnp.broadcast_to(onehot, (N_KV_HEADS, b, t, n_sbp))], axis=-1)
    ones_rows = jnp.zeros((N_KV_HEADS, V_PAD_ROWS, b * t), BF16).at[:, 0].set(1.0)

    def with_ones(v_t):
        v3 = jnp.concatenate([v_t.reshape(N_KV_HEADS, HEAD_DIM, b * t), ones_rows], axis=1)
        return v3.reshape(N_KV_HEADS * (HEAD_DIM + V_PAD_ROWS), b * t)

    v_rows = HEAD_DIM + V_PAD_ROWS
    return pl.pallas_call(
        functools.partial(_attn_kernel, tq=tq, n_cmp=n_cmp, n_sb=n_sb),
        grid=(b, N_KV_HEADS, nq),
        in_specs=[
            pl.BlockSpec((kw, tq), lambda i, g, q: (g, i * nq + q)),
            pl.BlockSpec((kw, tq), lambda i, g, q: (g, i * nq + q)),
            pl.BlockSpec((GATE_ROWS, tq), lambda i, g, q: (g, i * nq + q)),
            pl.BlockSpec((1, 1, n_row, HEAD_DIM), lambda i, g, q: (i, g, 0, 0)),
            pl.BlockSpec((1, 1, HEAD_DIM, n_row), lambda i, g, q: (i, g, 0, 0)),
            pl.BlockSpec((1, 1, t, HEAD_DIM + n_sbp), lambda i, g, q: (g, i, 0, 0)),
            pl.BlockSpec((1, 1, t, HEAD_DIM), lambda i, g, q: (g, i, 0, 0)),
            pl.BlockSpec((v_rows, t), lambda i, g, q: (g, i)),
            pl.BlockSpec((v_rows, t), lambda i, g, q: (g, i)),
        ],
        out_specs=pl.BlockSpec((kw, tq), lambda i, g, q: (g, i * nq + q)),
        out_shape=jax.ShapeDtypeStruct((N_HEADS * HEAD_DIM, b * t), BF16),
        scratch_shapes=[pltpu.VMEM((tq // LANES, n_row + 8, LANES), F32)],
        compiler_params=_params(("arbitrary", "arbitrary", "arbitrary"), VMEM_LIMIT),
        name="attn_prompt",
    )(qn_t, qr_t, g_t, ck, cv_t, kaug, kwin.reshape(N_KV_HEADS, b, t, HEAD_DIM), with_ones(vsel_t), with_ones(vwin_t))


def _oproj_t_kernel(ot_ref, h_ref, w_ref, out_ref, w_s):
    @pl.when(pl.program_id(0) == 0)
    def _():
        w_s[...] = w_ref[...].astype(BF16)

    o = ot_ref[...].astype(F32).T.astype(BF16)
    out_ref[...] = h_ref[...] + _dot(o, w_s[...])


def _oproj_t(o_t, h2, w_o, *, tt):
    n, d = h2.shape
    return pl.pallas_call(
        _oproj_t_kernel,
        grid=(n // tt,),
        in_specs=[
            pl.BlockSpec((o_t.shape[0], tt), lambda i: (0, i)),
            pl.BlockSpec((tt, d), lambda i: (i, 0)),
            pl.BlockSpec(w_o.shape, lambda i: (0, 0)),
        ],
        out_specs=pl.BlockSpec((tt, d), lambda i: (i, 0)),
        out_shape=jax.ShapeDtypeStruct((n, d), F32),
        scratch_shapes=[pltpu.VMEM(w_o.shape, BF16)],
        compiler_params=_params(("arbitrary",), VMEM_LIMIT),
        name="o_proj",
    )(o_t, h2, w_o)


def _oproj_n_kernel(o_ref, h_ref, w_ref, out_ref):
    out_ref[...] = h_ref[...] + _dot(o_ref[...].astype(BF16), w_ref[...].astype(BF16))


def _oproj_n(o, h2, w_o):
    n, d = h2.shape
    return pl.pallas_call(
        _oproj_n_kernel,
        out_shape=jax.ShapeDtypeStruct((n, d), F32),
        compiler_params=_params((), VMEM_LIMIT),
        name="o_proj_sample",
    )(o, h2, w_o)


def _attn_sample_kernel(pt_ref, *refs, n_pages, n_cmp, n_sb, ts, past, n_buf):
    del pt_ref
    page_refs = refs[:n_pages]
    (qn_ref, qr_ref, ckt_ref, cvt_ref, win_ref, knew_ref, vnew_ref, kwnew_ref, vwnew_ref, gate_ref,
     sel_ref, e_ref, o_ref, mask_s, m_s, l_s, acc_s, ocmp_s) = refs[n_pages:]
    q_step = pl.program_id(1)
    rows = HEADS_PER_KV * N_KV_HEADS * ts
    grp_rows = N_KV_HEADS * ts
    row = lax.broadcasted_iota(I32, (rows, 1), 0)
    qpos = past + row % ts
    qr = qr_ref[0]

    def tile_rows(x):
        return jnp.concatenate([x] * HEADS_PER_KV, axis=0)

    def online_update(s, mk, v_dot):
        s = jnp.where(mk, s, NEG)
        m_new = jnp.maximum(m_s[...], jnp.max(s, axis=-1, keepdims=True))
        alpha = jnp.exp(m_s[...] - m_new)
        p = jnp.where(mk, jnp.exp(s - m_new), 0.0)
        l_s[...] = l_s[...] * alpha + jnp.sum(p, axis=-1, keepdims=True)
        acc_s[...] = acc_s[...] * alpha + v_dot(p.astype(BF16))
        m_s[...] = m_new

    @pl.when(q_step == 0)
    def _():
        n_row = ckt_ref.shape[2]
        s = _dot(qn_ref[0], ckt_ref[0])
        ci = lax.broadcasted_iota(I32, (1, n_row), 1)
        vis = (ci * CMP_STRIDE + (CMP_BLOCK - 1) <= qpos) & (ci < n_cmp)
        s = jnp.where(vis, s, NEG)
        p = jnp.where(vis, jnp.exp(s - jnp.max(s, axis=-1, keepdims=True)), 0.0)
        p = p / jnp.maximum(jnp.sum(p, axis=-1, keepdims=True), TINY)
        ocmp_s[...] = _nt_dot(p.astype(BF16), cvt_ref[0])
        pg = p[0:grp_rows]
        for h in range(1, HEADS_PER_KV):
            pg = pg + p[h * grp_rows:(h + 1) * grp_rows]
        score = _dot(pg, sel_ref[...], precision=HIGHEST)
        width = score.shape[1]
        j = lax.broadcasted_iota(I32, (1, width), 1)
        tq = qpos[0:grp_rows]
        cur = tq // SEL_BLOCK
        forced = (j == 0) | (j == cur) | (j == cur - 1)
        sc = jnp.where(j * SEL_BLOCK <= tq, score + jnp.where(forced, BIG, 0.0), -BIG)
        sc = jnp.where(j < n_sb, sc, -2.0 * BIG)
        rank = jnp.zeros((grp_rows, width), I32)
        for jp in range(n_sb):
            col = sc[:, jp:jp + 1]
            beats = (col > sc) | ((col == sc) & (j > jp))
            rank = rank + beats.astype(I32)
        sel = jnp.where((rank < min(N_SEL, n_sb)) & (j < n_sb), 1.0, 0.0).astype(BF16)
        mask_s[...] = _dot(sel, e_ref[...])
        m_s[...] = jnp.full(m_s.shape, NEG, F32)
        l_s[...] = jnp.zeros(l_s.shape, F32)
        acc_s[...] = jnp.zeros(acc_s.shape, F32)

    width = n_pages * PAGE_SIZE
    kt = jnp.concatenate([page_refs[i][0, 0].reshape(KV_WIDTH, PAGE_SIZE) for i in range(n_pages)], axis=1)
    vt = jnp.concatenate([page_refs[i][0, 1].reshape(KV_WIDTH, PAGE_SIZE) for i in range(n_pages)], axis=1)
    k0 = pl.multiple_of(q_step * width, width)
    mk = tile_rows(mask_s[:, pl.ds(k0, width)]) > 0.5
    online_update(_dot(qr, kt.astype(BF16)), mk, lambda p: _nt_dot(p, vt.astype(BF16)))

    @pl.when(q_step == pl.num_programs(1) - 1)
    def _():
        lane = lax.broadcasted_iota(I32, (1, LANES), 1)
        new_ok = (lane < ts) & (past + lane <= qpos)
        mk_new = (tile_rows(mask_s[:, past:past + LANES]) > 0.5) & new_ok
        online_update(_nt_dot(qr, knew_ref[0]), mk_new, lambda p: _dot(p, vnew_ref[0]))
        o_sel = acc_s[...] / jnp.maximum(l_s[...], TINY)
        bi = lax.broadcasted_iota(I32, (1, n_buf), 1)
        dq = qpos - (past - n_buf + bi)
        ok_buf = (dq >= 0) & (dq < WINDOW)
        dq_new = qpos - (past + lane)
        ok_new = (lane < ts) & (dq_new >= 0) & (dq_new < WINDOW)
        s_w = jnp.concatenate([_dot(qr, win_ref[0, 0].astype(BF16)), _nt_dot(qr, kwnew_ref[0])], axis=1)
        ok = jnp.concatenate([jnp.broadcast_to(ok_buf, (rows, n_buf)), jnp.broadcast_to(ok_new, (rows, LANES))], axis=1)
        s_w = jnp.where(ok, s_w, NEG)
        p_w = jnp.where(ok, jnp.exp(s_w - jnp.max(s_w, axis=-1, keepdims=True)), 0.0)
        p_w = (p_w / jnp.maximum(jnp.sum(p_w, axis=-1, keepdims=True), TINY)).astype(BF16)
        o_win = _nt_dot(p_w[:, :n_buf], win_ref[0, 1].astype(BF16)) + _dot(p_w[:, n_buf:], vwnew_ref[0])
        row_g = (row // ts) % N_KV_HEADS

        def own_group(o):
            out = jnp.zeros((rows, HEAD_DIM), F32)
            for g in range(N_KV_HEADS):
                out = out + jnp.where(row_g == g, o[:, g * HEAD_DIM:(g + 1) * HEAD_DIM], 0.0)
            return out

        o_ref[0] = (gate_ref[0, 0] * own_group(ocmp_s[...]) + gate_ref[0, 1] * own_group(o_sel)
                    + gate_ref[0, 2] * own_group(o_win))


def _attn_sample(cache5, page_table, qn_bd, qr_bd, ck_t, cv_t, win4, knew, vnew, kwnew, vwnew, gates,
                 *, n_pages, ts, past):
    b, pages_per_seq = page_table.shape
    steps = pages_per_seq // n_pages
    rows = qn_bd.shape[1]
    n_chunk = ck_t.shape[2]
    n_cmp = n_chunk - CMP_BLOCK // CMP_STRIDE + 1
    n_keys = past + LANES
    n_sb = -(-(past + ts) // SEL_BLOCK)
    n_sb_pad = -(-n_sb // LANES) * LANES
    n_buf = win4.shape[3]
    ratio = SEL_BLOCK // CMP_STRIDE
    ci = jnp.arange(n_chunk)[:, None]
    jb = jnp.arange(n_sb_pad)[None, :]
    sel_map = ((ci >= ratio * jb + 1 - CMP_BLOCK // CMP_STRIDE) & (ci < ratio * jb + ratio)
               & (ci < n_cmp) & (jb < n_sb)).astype(F32)
    expand = (jnp.arange(n_keys)[None, :] // SEL_BLOCK == jnp.arange(n_sb_pad)[:, None]).astype(BF16)

    def page_spec(j):
        return pl.BlockSpec((1, 2, N_KV_HEADS, HEAD_DIM, PAGE_SIZE),
                            lambda i, q, pt: (pt[i, q * n_pages + j], 1, 0, 0, 0))

    per_seq = lambda a: pl.BlockSpec((1,) + a.shape[1:], lambda i, q, pt: (i,) + (0,) * (a.ndim - 1))
    full = lambda a: pl.BlockSpec(a.shape, lambda i, q, pt: (0,) * a.ndim)
    return pl.pallas_call(
        functools.partial(_attn_sample_kernel, n_pages=n_pages, n_cmp=n_cmp, n_sb=n_sb, ts=ts, past=past,
                          n_buf=n_buf),
        grid_spec=pltpu.PrefetchScalarGridSpec(
            num_scalar_prefetch=1,
            grid=(b, steps),
            in_specs=[page_spec(j) for j in range(n_pages)]
            + [per_seq(a) for a in (qn_bd, qr_bd, ck_t, cv_t, win4, knew, vnew, kwnew, vwnew, gates)]
            + [full(sel_map), full(expand)],
            out_specs=pl.BlockSpec((1, rows, HEAD_DIM), lambda i, q, pt: (i, 0, 0)),
            scratch_shapes=[
                pltpu.VMEM((N_KV_HEADS * ts, n_keys), F32),
                pltpu.VMEM((rows, 1), F32), pltpu.VMEM((rows, 1), F32),
                pltpu.VMEM((rows, KV_WIDTH), F32), pltpu.VMEM((rows, KV_WIDTH), F32),
            ],
        ),
        out_shape=jax.ShapeDtypeStruct((b, rows, HEAD_DIM), F32),
        compiler_params=_params(("arbitrary", "arbitrary"), VMEM_LIMIT),
        name="attn_sample",
    )(page_table, *([cache5] * n_pages), qn_bd, qr_bd, ck_t, cv_t, win4, knew, vnew, kwnew, vwnew, gates,
      sel_map, expand)


def _router_weights(router_group, router_expert):
    rt = jnp.concatenate([router_group, router_expert], axis=1).T
    return jnp.pad(rt, ((0, 24 - rt.shape[0]), (0, 0)))


def _trunk_prompt(x, p, *, tt_pool=256, tm=1024, tt=512, tq=256):
    (norm_mix, norm_ffn, pool_w, pool_scale, kv_norm, w_kv, k_norm, cmp_pe, cmp_w1, cmp_b1, cmp_w2,
     w_qg, q_norm, w_o, router_group, router_expert, w_gate_up, w_down) = p
    b, t, d = x.shape
    n = b * t
    pos = jnp.arange(t)
    h, new_pool = _pool_layer(x, jnp.zeros((b, POOL_HALO, d), F32), norm_mix[0], pool_w[0], pool_scale[0],
                              tt=tt_pool, clip=True)
    h = _moe(h.reshape(n, d), norm_ffn[0], _router_weights(router_group[0], router_expert[0]),
             w_gate_up, w_down, 0, tm=tm)
    cos_t, sin_t = _rope_tables_transposed(pos)
    kv_t, win_t, craw, ksel, kwin, vsel_t, vwin_t = _kvproj(h.reshape(b, t, d), kv_norm, w_kv, k_norm,
                                                            cos_t, sin_t, tt=tt)
    ab = _cmp_ab_prompt(craw, _cmp_weights(cmp_w1), b=b, t=t)
    ck, cv = _cmp_finish(ab, cmp_pe, cmp_w1, cmp_b1, cmp_w2, k_norm[0])
    qn_t, qr_t, g_t = _qproj(h, norm_mix[1], _qg_weights(w_qg[0]), q_norm[0], cos_t, sin_t,
                             tt=tt, pos_blocks=t // tt)
    o_t = _attn_prompt(qn_t, qr_t, g_t, ck, jnp.swapaxes(cv, 2, 3), ksel, kwin, vsel_t, vwin_t, b=b, t=t, tq=tq)
    h = _oproj_t(o_t, h, w_o[0], tt=tt)
    h = _moe(h, norm_ffn[1], _router_weights(router_group[1], router_expert[1]),
             w_gate_up, w_down, 1, tm=tm)
    n_win = min(WINDOW, t)
    kv_new = kv_t.reshape(b, 4, N_KV_HEADS, HEAD_DIM, t).transpose(0, 4, 1, 2, 3)
    win_new = win_t[:, :, t - n_win:].reshape(b, 2, N_KV_HEADS, HEAD_DIM, n_win).transpose(0, 4, 1, 2, 3)
    return h.reshape(b, t, d), new_pool[None], kv_new, win_new


def _trunk_sample(x, state_pool, cache_kv, page_table, state_win, p, *, n_pages=16):
    (norm_mix, norm_ffn, pool_w, pool_scale, kv_norm, w_kv, k_norm, cmp_pe, cmp_w1, cmp_b1, cmp_w2,
     w_qg, q_norm, w_o, router_group, router_expert, w_gate_up, w_down) = p
    b, ts, d = x.shape
    n = b * ts
    past = page_table.shape[1] * PAGE_SIZE
    n_buf = state_win.shape[1]
    prev16 = jnp.pad(state_pool[0], ((0, 0), (POOL_HALO - POOL_STATE, 0), (0, 0)))
    h, new_pool = _pool_layer(x, prev16, norm_mix[0], pool_w[0], pool_scale[0], tt=ts, clip=False)
    h = _moe(h.reshape(n, d), norm_ffn[0], _router_weights(router_group[0], router_expert[0]),
             w_gate_up, w_down, 0, tm=n)
    cos_t, sin_t = _rope_tables_transposed(past + jnp.arange(ts))
    cos_t = jnp.tile(cos_t, (1, b))
    sin_t = jnp.tile(sin_t, (1, b))
    kv_t, win_t, _, _, _, _, _ = _kvproj(h.reshape(1, n, d), kv_norm, w_kv, k_norm, cos_t, sin_t, tt=n)
    kv_rows = kv_t[0].T
    win_rows = win_t[0].T

    cache5 = cache_kv.transpose(0, 2, 3, 4, 1)
    ab = _cmp_ab_pages(cache5, page_table, _cmp_weights(cmp_w1), n_pages=n_pages)
    ck, cv = _cmp_finish(ab, cmp_pe, cmp_w1, cmp_b1, cmp_w2, k_norm[0])
    n_chunk = ck.shape[2]
    ck_t = ck.transpose(0, 1, 3, 2).reshape(b, KV_WIDTH, n_chunk)
    cv_t = cv.transpose(0, 1, 3, 2).reshape(b, KV_WIDTH, n_chunk)

    qn_t, qr_t, g_t = _qproj(h, norm_mix[1], _qg_weights(w_qg[0]), q_norm[0], cos_t, sin_t, tt=n, pos_blocks=1)

    def block_diag_queries(q_t):
        q5 = q_t.reshape(N_KV_HEADS, HEADS_PER_KV, HEAD_DIM, b, ts).transpose(3, 1, 0, 4, 2)
        eye = jnp.eye(N_KV_HEADS, dtype=q_t.dtype)
        qbd = q5[:, :, :, :, None, :] * eye[None, None, :, None, :, None]
        return qbd.reshape(b, HEADS_PER_KV * N_KV_HEADS * ts, KV_WIDTH)

    gates = g_t.reshape(N_KV_HEADS, GATE_ROWS, b, ts)[:, :3 * HEADS_PER_KV]
    gates = gates.reshape(N_KV_HEADS, 3, HEADS_PER_KV, b, ts).transpose(3, 1, 2, 0, 4)
    gates = jnp.broadcast_to(gates.reshape(b, 3, HEADS_PER_KV * N_KV_HEADS * ts, 1),
                             (b, 3, HEADS_PER_KV * N_KV_HEADS * ts, HEAD_DIM))

    def new_rows(rows2):
        return jnp.pad(rows2.reshape(b, ts, KV_WIDTH), ((0, 0), (0, LANES - ts), (0, 0))).astype(BF16)

    win4 = state_win.transpose(0, 2, 3, 4, 1).reshape(b, 2, KV_WIDTH, n_buf)
    o = _attn_sample(cache5, page_table, block_diag_queries(qn_t), block_diag_queries(qr_t), ck_t, cv_t, win4,
                     new_rows(kv_rows[:, 2 * KV_WIDTH:3 * KV_WIDTH]), new_rows(kv_rows[:, 3 * KV_WIDTH:]),
                     new_rows(win_rows[:, :KV_WIDTH]), new_rows(win_rows[:, KV_WIDTH:]), gates,
                     n_pages=n_pages, ts=ts, past=past)
    o = o.reshape(b, HEADS_PER_KV, N_KV_HEADS, ts, HEAD_DIM).transpose(0, 3, 2, 1, 4).reshape(n, N_HEADS * HEAD_DIM)
    h = _oproj_n(o, h, w_o[0])
    h = _moe(h, norm_ffn[1], _router_weights(router_group[1], router_expert[1]),
             w_gate_up, w_down, 1, tm=n)
    kv_new = kv_rows.reshape(b, ts, 4, N_KV_HEADS, HEAD_DIM)
    win_new = jnp.concatenate([state_win, win_rows.reshape(b, ts, 2, N_KV_HEADS, HEAD_DIM)], axis=1)[:, -n_buf:]
    return h.reshape(b, ts, d), new_pool[None], kv_new, win_new


def kernel(x_prompt, x_sample, state_pool, cache_kv, page_table, state_win, norm_mix, norm_ffn, pool_w, pool_scale, kv_norm, w_kv, k_norm, cmp_pe, cmp_w1, cmp_b1, cmp_w2, w_qg, q_norm, w_o, router_group, router_expert, w_gate_up, w_down):
    params = (norm_mix, norm_ffn, pool_w, pool_scale, kv_norm, w_kv, k_norm, cmp_pe, cmp_w1, cmp_b1,
              cmp_w2, w_qg, q_norm, w_o, router_group, router_expert, _cast_bf16(w_gate_up), _cast_bf16(w_down))
    y_p, pool_p, kv_p, win_p = _trunk_prompt(x_prompt, params)
    y_s, pool_s, kv_s, win_s = _trunk_sample(x_sample, state_pool, cache_kv, page_table, state_win, params)
    return y_p, y_s, pool_p, pool_s, kv_p, kv_s, win_p, win_s
```

```python
import functools

import jax
import jax.numpy as jnp
from jax import lax
from jax.experimental import pallas as pl
from jax.experimental.pallas import tpu as pltpu

F32 = jnp.float32
BF16 = jnp.bfloat16
I32 = jnp.int32
HIGHEST = lax.Precision.HIGHEST

D_MODEL = 1024
POOL_WINDOWS = (2, 4, 8, 16)
POOL_GROUP_DIM = D_MODEL // len(POOL_WINDOWS)
POOL_STATE = max(POOL_WINDOWS) - 1
POOL_HALO = 16
N_HEADS = 16
HEAD_DIM = 64
HALF_DIM = HEAD_DIM // 2
N_KV_HEADS = 4
HEADS_PER_KV = N_HEADS // N_KV_HEADS
KV_WIDTH = N_KV_HEADS * HEAD_DIM
CMP_BLOCK = 32
CMP_STRIDE = 16
CMP_HIDDEN = 2 * HEAD_DIM
SEL_BLOCK = 64
N_SEL = 16
WINDOW = 512
PAGE_SIZE = 128
ROPE_THETA = 10000.0
SCALE = HEAD_DIM ** -0.5
N_GROUPS = 4
EXPERTS_PER_GROUP = 4
N_EXPERTS = N_GROUPS * EXPERTS_PER_GROUP
D_EXPERT = 512
EPS = 1e-6
NEG = -1e30
TINY = 1e-30
BIG = 1e4

LANES = 128
GATE_ROWS = 16
ROUTE_GROUP_ROW = 3 * N_EXPERTS
MOE_SEG_ALIGN = 16
MOE_CHUNK = 144
MOE_STEP_EXPERTS = 2
ATTN_CHAIN_HEADS = 4
V_PAD_ROWS = 16
VMEM_LIMIT = 56 * 1024 * 1024


def _params(sem, vmem=None):
    return pltpu.CompilerParams(dimension_semantics=sem, vmem_limit_bytes=vmem)


def _rms(x, g):
    return x * lax.rsqrt(jnp.mean(x * x, axis=-1, keepdims=True) + EPS) * g


def _nt_dot(a, b, precision=None):
    return lax.dot_general(a, b, (((1,), (1,)), ((), ())), precision=precision,
                           preferred_element_type=F32)


def _dot(a, b, precision=None):
    return jnp.dot(a, b, precision=precision, preferred_element_type=F32)


def _sigmoid(x):
    return 1.0 / (1.0 + jnp.exp(-x))


def _pool_kernel(h_ref, prev_ref, g_ref, w_ref, sc_ref, o_ref, np_ref, ext_ref, lvl_ref, *, tt, clip):
    t = pl.program_id(1)
    x = h_ref[0]
    xn = _rms(x, g_ref[...])

    @pl.when(t == 0)
    def _():
        ext_ref[0:POOL_HALO, :] = prev_ref[0]

    @pl.when(t > 0)
    def _():
        ext_ref[0:POOL_HALO, :] = ext_ref[tt:tt + POOL_HALO, :]

    ext_ref[POOL_HALO:POOL_HALO + tt, :] = xn
    if clip:
        tpos = t * tt + lax.broadcasted_iota(I32, (tt, 1), 0)
    outs = []
    for gi, w in enumerate(POOL_WINDOWS):
        lo = gi * POOL_GROUP_DIM
        hi = lo + POOL_GROUP_DIM
        n = tt + POOL_HALO
        lvl_ref[...] = ext_ref[:, lo:hi]
        span = 1
        while span < w:
            lvl_ref[span:n, :] = lvl_ref[span:n, :] + lvl_ref[0:n - span, :]
            span *= 2
        acc = lvl_ref[POOL_HALO:n, :]
        if clip:
            mean = acc / jnp.minimum(tpos + 1, w).astype(F32)
        else:
            mean = acc * (1.0 / w)
        d = (mean - xn[:, lo:hi]).astype(BF16)
        outs.append(_dot(d, w_ref[gi].astype(BF16)))
    o_ref[0] = x + jnp.concatenate(outs, axis=1) * sc_ref[...]

    @pl.when(t == pl.num_programs(1) - 1)
    def _():
        np_ref[0] = ext_ref[tt + POOL_HALO - POOL_STATE:tt + POOL_HALO, :]


def _pool_layer(h, prev16, gain, w_pool, scale, *, tt, clip):
    b, t, d = h.shape
    return pl.pallas_call(
        functools.partial(_pool_kernel, tt=tt, clip=clip),
        grid=(b, t // tt),
        in_specs=[
            pl.BlockSpec((1, tt, d), lambda i, j: (i, j, 0)),
            pl.BlockSpec((1, POOL_HALO, d), lambda i, j: (i, 0, 0)),
            pl.BlockSpec((1, d), lambda i, j: (0, 0)),
            pl.BlockSpec(w_pool.shape, lambda i, j: (0, 0, 0)),
            pl.BlockSpec((1, d), lambda i, j: (0, 0)),
        ],
        out_specs=[
            pl.BlockSpec((1, tt, d), lambda i, j: (i, j, 0)),
            pl.BlockSpec((1, POOL_STATE, d), lambda i, j: (i, 0, 0)),
        ],
        out_shape=[jax.ShapeDtypeStruct((b, t, d), F32),
                   jax.ShapeDtypeStruct((b, POOL_STATE, d), F32)],
        scratch_shapes=[pltpu.VMEM((tt + POOL_HALO, d), F32), pltpu.VMEM((tt + POOL_HALO, POOL_GROUP_DIM), F32)],
        compiler_params=_params(("arbitrary", "arbitrary")),
        name="pool_layer",
    )(h, prev16, gain.reshape(1, d), w_pool, scale.reshape(1, d))


def _softmax_rows(rows):
    m = functools.reduce(jnp.maximum, rows)
    es = [jnp.exp(r - m) for r in rows]
    s = functools.reduce(lambda a, b: a + b, es)
    return [e / s for e in es]


def _router_kernel(h_ref, g_ref, rt_ref, xn_ref, comb_ref, grp_ref):
    xn = _rms(h_ref[...], g_ref[...])
    xn_ref[...] = xn.astype(BF16)
    lt = _nt_dot(rt_ref[...], xn, precision=HIGHEST)
    pg = _softmax_rows([lt[i:i + 1, :] for i in range(N_GROUPS)])
    g_val = functools.reduce(jnp.maximum, pg)
    g_idx = jnp.full(g_val.shape, N_GROUPS - 1, I32)
    for i in range(N_GROUPS - 2, -1, -1):
        g_idx = jnp.where(pg[i] == g_val, i, g_idx)
    le = []
    for j in range(EXPERTS_PER_GROUP):
        v = lt[N_GROUPS + (N_GROUPS - 1) * EXPERTS_PER_GROUP + j:N_GROUPS + (N_GROUPS - 1) * EXPERTS_PER_GROUP + j + 1, :]
        for gi in range(N_GROUPS - 2, -1, -1):
            r = N_GROUPS + gi * EXPERTS_PER_GROUP + j
            v = jnp.where(g_idx == gi, lt[r:r + 1, :], v)
        le.append(v)
    pe = _softmax_rows(le)
    ranks = []
    for j in range(EXPERTS_PER_GROUP):
        r = jnp.zeros(g_val.shape, I32)
        for i in range(EXPERTS_PER_GROUP):
            if i == j:
                continue
            beats = (pe[i] > pe[j]) | (pe[i] == pe[j]) if i < j else (pe[i] > pe[j])
            r = r + beats.astype(I32)
        ranks.append(r)
    vals, idxs = [], []
    for k in range(2):
        v = jnp.zeros(g_val.shape, F32)
        ix = jnp.zeros(g_val.shape, I32)
        for j in range(EXPERTS_PER_GROUP):
            hit = ranks[j] == k
            v = jnp.where(hit, pe[j], v)
            ix = jnp.where(hit, j, ix)
        vals.append(v)
        idxs.append(ix)
    tot = vals[0] + vals[1]
    erow = lax.broadcasted_iota(I32, (LANES, g_val.shape[1]), 0)
    comb_t = jnp.where(erow == ROUTE_GROUP_ROW + g_idx, 1.0, 0.0)
    for k in range(2):
        wk = g_val * (vals[k] / tot)
        w_hi = wk.astype(BF16).astype(F32)
        w_mid = (wk - w_hi).astype(BF16).astype(F32)
        w_lo = (wk - w_hi - w_mid).astype(BF16).astype(F32)
        eid = g_idx * EXPERTS_PER_GROUP + idxs[k]
        for part, term in enumerate((w_hi, w_mid, w_lo)):
            comb_t = comb_t + jnp.where(erow == part * N_EXPERTS + eid, term, 0.0)
    comb_ref[...] = comb_t.T
    grp_ref[...] = comb_t[ROUTE_GROUP_ROW:ROUTE_GROUP_ROW + 8]


def _router(h2, gain, rt, *, tm):
    n, d = h2.shape
    return pl.pallas_call(
        _router_kernel,
        grid=(n // tm,),
        in_specs=[
            pl.BlockSpec((tm, d), lambda i: (i, 0)),
            pl.BlockSpec((1, d), lambda i: (0, 0)),
            pl.BlockSpec(rt.shape, lambda i: (0, 0)),
        ],
        out_specs=[
            pl.BlockSpec((tm, d), lambda i: (i, 0)),
            pl.BlockSpec((tm, LANES), lambda i: (i, 0)),
            pl.BlockSpec((8, tm), lambda i: (0, i)),
        ],
        out_shape=[jax.ShapeDtypeStruct((n, d), BF16), jax.ShapeDtypeStruct((n, LANES), F32),
                   jax.ShapeDtypeStruct((8, n), F32)],
        compiler_params=_params(("arbitrary",)),
        name="moe_router",
    )(h2, gain.reshape(1, d), rt)


def _cast_kernel(x_ref, o_ref):
    o_ref[...] = x_ref[...].astype(o_ref.dtype)


def _cast_bf16(w):
    blk = (1, 1) + w.shape[2:]
    return pl.pallas_call(
        _cast_kernel,
        grid=w.shape[:2],
        in_specs=[pl.BlockSpec(blk, lambda i, j: (i, j, 0, 0))],
        out_specs=pl.BlockSpec(blk, lambda i, j: (i, j, 0, 0)),
        out_shape=jax.ShapeDtypeStruct(w.shape, BF16),
        compiler_params=_params(("arbitrary", "arbitrary")),
        name="cast_weights",
    )(w)


def _moe_kernel(cnt_ref, base_ref, x_ref, comb_ref, grp_ref, h_ref, wgu_ref, wd_ref, o_ref,
                pt_s, z_s, y_s, cs_s, slot_s, *, r_max):
    w = pl.program_id(0)
    step = pl.program_id(1)
    nw = x_ref.shape[0]
    g = step // (EXPERTS_PER_GROUP // MOE_STEP_EXPERTS)

    @pl.when(step == 0)
    def _():
        lane = lax.broadcasted_iota(I32, (1, LANES), 1)
        in_grp = (lane >= ROUTE_GROUP_ROW) & (lane < ROUTE_GROUP_ROW + N_GROUPS)
        oh = jnp.where(in_grp, comb_ref[...], 0.0)
        base_row = jnp.zeros((1, LANES), F32)
        for gi in range(N_GROUPS):
            base_row = jnp.where(lane == ROUTE_GROUP_ROW + gi, base_ref[w, gi].astype(F32), base_row)
        ri = lax.broadcasted_iota(I32, (nw, nw), 0)
        ci = lax.broadcasted_iota(I32, (nw, nw), 1)
        before = _dot(jnp.where(ci < ri, 1.0, 0.0).astype(BF16), oh.astype(BF16))
        slot_s[...] = jnp.sum(oh * (before + base_row), axis=-1, keepdims=True)
        oh_t = grp_ref[...]
        before_t = _dot(oh_t.astype(BF16), jnp.where(ri < ci, 1.0, 0.0).astype(BF16))
        slot_row = jnp.zeros((1, nw), F32)
        for gi in range(N_GROUPS):
            slot_row = slot_row + oh_t[gi:gi + 1] * (before_t[gi:gi + 1] + base_ref[w, gi].astype(F32))
        rows = lax.broadcasted_iota(I32, (r_max, 1), 0)
        pt_s[...] = jnp.where(rows == slot_row.astype(I32), 1.0, 0.0).astype(BF16)
        z_s[...] = _dot(pt_s[...], x_ref[...]).astype(BF16)
        cs_s[...] = _dot(pt_s[...], comb_ref[...].astype(BF16))
        y_s[...] = jnp.zeros(y_s.shape, F32)

    seg0 = base_ref[w, g]
    seg1 = seg0 + cnt_ref[w, g]
    lane_c = lax.broadcasted_iota(I32, (MOE_CHUNK, LANES), 1)
    in_terms = lane_c < 3 * N_EXPERTS

    def chunk(c, carry):
        r0 = pl.multiple_of(seg0 + c * MOE_CHUNK, MOE_SEG_ALIGN)
        z = z_s[pl.ds(r0, MOE_CHUNK), :]
        cs = cs_s[pl.ds(r0, MOE_CHUNK), :]
        acc = None
        for k in range(MOE_STEP_EXPERTS):
            gu = _dot(z, wgu_ref[0, k])
            a = gu[:, :D_EXPERT]
            hdn = (a * _sigmoid(a) * gu[:, D_EXPERT:]).astype(BF16)
            y = _dot(hdn, wd_ref[0, k])
            is_e = in_terms & (lane_c % N_EXPERTS == step * MOE_STEP_EXPERTS + k)
            wgt = jnp.sum(jnp.where(is_e, cs, 0.0), axis=-1, keepdims=True)
            acc = wgt * y if acc is None else acc + wgt * y
        valid = r0 + lax.broadcasted_iota(I32, (MOE_CHUNK, 1), 0) < seg1
        y_s[pl.ds(r0, MOE_CHUNK), :] += jnp.where(valid, acc, 0.0)
        return carry

    lax.fori_loop(0, (cnt_ref[w, g] + MOE_CHUNK - 1) // MOE_CHUNK, chunk, 0)

    @pl.when(step == pl.num_programs(1) - 1)
    def _():
        cols = lax.broadcasted_iota(I32, (1, r_max), 1)
        p = jnp.where(cols == slot_s[...].astype(I32), 1.0, 0.0).astype(BF16)
        o_ref[...] = h_ref[...] + _dot(p, y_s[...].astype(BF16))


def _moe(h2, gain, rt, w_gu, w_down, layer, *, tm):
    n, d = h2.shape
    if w_gu.dtype != BF16:
        w_gu, w_down = _cast_bf16(w_gu), _cast_bf16(w_down)
    xn, comb, grp = _router(h2, gain, rt, tm=tm)
    n_win = n // tm
    cnt = jnp.sum(grp[:N_GROUPS].reshape(N_GROUPS, n_win, tm), axis=-1).T.astype(I32)
    seg = (cnt + MOE_SEG_ALIGN - 1) // MOE_SEG_ALIGN * MOE_SEG_ALIGN
    base = jnp.cumsum(seg, axis=1) - seg
    r_max = -(-(tm + N_GROUPS * MOE_SEG_ALIGN + MOE_CHUNK) // LANES) * LANES
    return pl.pallas_call(
        functools.partial(_moe_kernel, r_max=r_max),
        grid_spec=pltpu.PrefetchScalarGridSpec(
            num_scalar_prefetch=2,
            grid=(n_win, N_EXPERTS // MOE_STEP_EXPERTS),
            in_specs=[
                pl.BlockSpec((tm, d), lambda i, e, c, b: (i, 0)),
                pl.BlockSpec((tm, LANES), lambda i, e, c, b: (i, 0)),
                pl.BlockSpec((8, tm), lambda i, e, c, b: (0, i)),
                pl.BlockSpec((tm, d), lambda i, e, c, b: (i, 0)),
                pl.BlockSpec((1, MOE_STEP_EXPERTS, d, 2 * D_EXPERT), lambda i, e, c, b: (layer, e, 0, 0)),
                pl.BlockSpec((1, MOE_STEP_EXPERTS, D_EXPERT, d), lambda i, e, c, b: (layer, e, 0, 0)),
            ],
            out_specs=pl.BlockSpec((tm, d), lambda i, e, c, b: (i, 0)),
            scratch_shapes=[
                pltpu.VMEM((r_max, tm), BF16), pltpu.VMEM((r_max, d), BF16), pltpu.VMEM((r_max, d), F32),
                pltpu.VMEM((r_max, LANES), F32), pltpu.VMEM((tm, 1), F32),
            ],
        ),
        out_shape=jax.ShapeDtypeStruct((n, d), F32),
        compiler_params=_params(("arbitrary", "arbitrary"), VMEM_LIMIT),
        name="moe_experts",
    )(cnt, base, xn, comb, grp, h2, w_gu, w_down)


def _head_norm_rope_t(z, gain, c, s):
    outs = []
    for g in range(z.shape[0] // HEAD_DIM):
        zh = z[g * HEAD_DIM:(g + 1) * HEAD_DIM]
        zn = zh * lax.rsqrt(jnp.mean(zh * zh, axis=0, keepdims=True) + EPS) * gain
        x1 = zn[:HALF_DIM]
        x2 = zn[HALF_DIM:]
        outs.append(x1 * c - x2 * s)
        outs.append(x2 * c + x1 * s)
    return jnp.concatenate(outs, axis=0)


def _kvproj_kernel(h_ref, g_ref, wt_ref, gsel_ref, gwin_ref, cos_ref, sin_ref,
                   kvt_ref, wint_ref, craw_ref, ksel_ref, kwin_ref, vselt_ref, vwint_ref, w_s):
    @pl.when((pl.program_id(0) == 0) & (pl.program_id(1) == 0))
    def _():
        w_s[...] = wt_ref[...].astype(BF16)

    xn = _rms(h_ref[0], g_ref[...]).astype(BF16)
    kvt = _nt_dot(w_s[...], xn)
    tt = xn.shape[0]
    c = cos_ref[...]
    s = sin_ref[...]
    gsel = jnp.concatenate([gsel_ref[...]] * (tt // LANES), axis=1)
    gwin = jnp.concatenate([gwin_ref[...]] * (tt // LANES), axis=1)
    ksel = _head_norm_rope_t(kvt[2 * KV_WIDTH:3 * KV_WIDTH], gsel, c, s)
    kwin = _head_norm_rope_t(kvt[4 * KV_WIDTH:5 * KV_WIDTH], gwin, c, s)
    kvt_ref[0] = jnp.concatenate([kvt[:2 * KV_WIDTH], ksel, kvt[3 * KV_WIDTH:4 * KV_WIDTH]], axis=0)
    wint_ref[0] = jnp.concatenate([kwin, kvt[5 * KV_WIDTH:]], axis=0)
    vselt_ref[...] = kvt[3 * KV_WIDTH:4 * KV_WIDTH].astype(BF16)
    vwint_ref[...] = kvt[5 * KV_WIDTH:].astype(BF16)
    raw = kvt[:2 * KV_WIDTH].T
    for k in range(2 * KV_WIDTH // LANES):
        craw_ref[k] = raw[:, k * LANES:(k + 1) * LANES]
    ksel_n = ksel.T
    kwin_n = kwin.T
    for g in range(N_KV_HEADS):
        ksel_ref[g] = ksel_n[:, g * HEAD_DIM:(g + 1) * HEAD_DIM].astype(BF16)
        kwin_ref[g] = kwin_n[:, g * HEAD_DIM:(g + 1) * HEAD_DIM].astype(BF16)


def _kvproj(x3, gain, w_kv, k_norm, cos_tt, sin_tt, *, tt):
    b, t, d = x3.shape
    n = b * t
    nt = t // tt
    wt = w_kv.T
    gsel = jnp.broadcast_to(k_norm[1][:, None], (HEAD_DIM, LANES))
    gwin = jnp.broadcast_to(k_norm[2][:, None], (HEAD_DIM, LANES))
    full = lambda a: pl.BlockSpec(a.shape, lambda i, j: (0,) * a.ndim)
    return pl.pallas_call(
        _kvproj_kernel,
        grid=(b, nt),
        in_specs=[
            pl.BlockSpec((1, tt, d), lambda i, j: (i, j, 0)),
            pl.BlockSpec((1, d), lambda i, j: (0, 0)),
            full(wt), full(gsel), full(gwin),
            pl.BlockSpec((HALF_DIM, tt), lambda i, j: (0, j)),
            pl.BlockSpec((HALF_DIM, tt), lambda i, j: (0, j)),
        ],
        out_specs=[
            pl.BlockSpec((1, 4 * KV_WIDTH, tt), lambda i, j: (i, 0, j)),
            pl.BlockSpec((1, 2 * KV_WIDTH, tt), lambda i, j: (i, 0, j)),
            pl.BlockSpec((2 * KV_WIDTH // LANES, tt, LANES), lambda i, j: (0, i * nt + j, 0)),
            pl.BlockSpec((N_KV_HEADS, tt, HEAD_DIM), lambda i, j: (0, i * nt + j, 0)),
            pl.BlockSpec((N_KV_HEADS, tt, HEAD_DIM), lambda i, j: (0, i * nt + j, 0)),
            pl.BlockSpec((KV_WIDTH, tt), lambda i, j: (0, i * nt + j)),
            pl.BlockSpec((KV_WIDTH, tt), lambda i, j: (0, i * nt + j)),
        ],
        out_shape=[
            jax.ShapeDtypeStruct((b, 4 * KV_WIDTH, t), F32),
            jax.ShapeDtypeStruct((b, 2 * KV_WIDTH, t), F32),
            jax.ShapeDtypeStruct((2 * KV_WIDTH // LANES, n, LANES), F32),
            jax.ShapeDtypeStruct((N_KV_HEADS, n, HEAD_DIM), BF16),
            jax.ShapeDtypeStruct((N_KV_HEADS, n, HEAD_DIM), BF16),
            jax.ShapeDtypeStruct((KV_WIDTH, n), BF16),
            jax.ShapeDtypeStruct((KV_WIDTH, n), BF16),
        ],
        scratch_shapes=[pltpu.VMEM(wt.shape, BF16)],
        compiler_params=_params(("arbitrary", "arbitrary"), VMEM_LIMIT),
        name="kv_proj",
    )(x3, gain.reshape(1, d), wt, gsel, gwin, cos_tt, sin_tt)


def _rope_tables_transposed(pos):
    inv = 1.0 / (ROPE_THETA ** (jnp.arange(HALF_DIM, dtype=F32) * (2.0 / HEAD_DIM)))
    ang = pos.astype(F32)[:, None] * inv[None, :]
    return jnp.cos(ang).T, jnp.sin(ang).T


def _cmp_ab_accumulate(load_rows, wab_ref, n_rows):
    del n_rows
    heads_per_chunk = LANES // HEAD_DIM
    accs = [[None] * N_KV_HEADS for _ in range(2)]
    for s in range(2):
        for c in range(KV_WIDTH // LANES):
            plane = s * (KV_WIDTH // LANES) + c
            lhs = jnp.concatenate([load_rows(r, plane).astype(BF16) for r in range(CMP_STRIDE)], axis=1)
            out = _dot(lhs, wab_ref[s])
            for k in range(heads_per_chunk):
                accs[s][c * heads_per_chunk + k] = out[:, k * 2 * CMP_HIDDEN:(k + 1) * 2 * CMP_HIDDEN]
    return accs


def _cmpab_kernel(craw_ref, wab_ref, ab_ref, *, n_chunk):
    accs = _cmp_ab_accumulate(lambda r, c: craw_ref[c, pl.ds(r, n_chunk, stride=CMP_STRIDE), :], wab_ref, n_chunk)
    for s in range(2):
        for g in range(N_KV_HEADS):
            ab_ref[0, s, g] = accs[s][g]


def _cmpab_pages_kernel(pt_ref, *refs, n_pages):
    del pt_ref
    page_refs = refs[:n_pages]
    wab_ref, ab_ref, craw_s = refs[n_pages:]
    pairs = KV_WIDTH // LANES
    for j in range(n_pages):
        for s in range(2):
            for k in range(pairs):
                tile = jnp.concatenate([page_refs[j][0, s, 2 * k], page_refs[j][0, s, 2 * k + 1]], axis=0)
                craw_s[s * pairs + k, j * PAGE_SIZE:(j + 1) * PAGE_SIZE, :] = tile.T
    n_chunk = n_pages * PAGE_SIZE // CMP_STRIDE
    accs = _cmp_ab_accumulate(lambda r, c: craw_s[c, pl.ds(r, n_chunk, stride=CMP_STRIDE), :], wab_ref, n_chunk)
    for s in range(2):
        for g in range(N_KV_HEADS):
            ab_ref[0, s, g] = accs[s][g]


def _cmp_ab_pages(cache5, page_table, wab, *, n_pages):
    b, pages_per_seq = page_table.shape
    steps = pages_per_seq // n_pages
    rows = n_pages * PAGE_SIZE // CMP_STRIDE

    def page_spec(j):
        return pl.BlockSpec((1, 2, N_KV_HEADS, HEAD_DIM, PAGE_SIZE),
                            lambda i, q, pt: (pt[i, q * n_pages + j], 0, 0, 0, 0))

    return pl.pallas_call(
        functools.partial(_cmpab_pages_kernel, n_pages=n_pages),
        grid_spec=pltpu.PrefetchScalarGridSpec(
            num_scalar_prefetch=1,
            grid=(b, steps),
            in_specs=[page_spec(j) for j in range(n_pages)]
            + [pl.BlockSpec(wab.shape, lambda i, q, pt: (0, 0, 0))],
            out_specs=pl.BlockSpec((1, 2, N_KV_HEADS, rows, 2 * CMP_HIDDEN), lambda i, q, pt: (i, 0, 0, q, 0)),
            scratch_shapes=[pltpu.VMEM((2 * KV_WIDTH // LANES, n_pages * PAGE_SIZE, LANES), F32)],
        ),
        out_shape=jax.ShapeDtypeStruct((b, 2, N_KV_HEADS, steps * rows, 2 * CMP_HIDDEN), F32),
        compiler_params=_params(("arbitrary", "arbitrary"), VMEM_LIMIT),
        name="cmp_ab_pages",
    )(page_table, *([cache5] * n_pages), wab)


def _cmp_weights(cmp_w1):
    w = cmp_w1.reshape(2, 2, CMP_STRIDE, HEAD_DIM, CMP_HIDDEN)
    w = w.transpose(0, 2, 3, 1, 4).reshape(2, CMP_STRIDE, HEAD_DIM, 2 * CMP_HIDDEN)
    eye = jnp.eye(LANES // HEAD_DIM, dtype=w.dtype)
    w = w[:, :, None, :, None, :] * eye[None, None, :, None, :, None]
    return w.reshape(2, CMP_STRIDE * LANES, (LANES // HEAD_DIM) * 2 * CMP_HIDDEN).astype(BF16)


def _cmp_ab_prompt(craw, wab, *, b, t):
    n_chunk = t // CMP_STRIDE
    return pl.pallas_call(
        functools.partial(_cmpab_kernel, n_chunk=n_chunk),
        grid=(b,),
        in_specs=[
            pl.BlockSpec((craw.shape[0], t, LANES), lambda i: (0, i, 0)),
            pl.BlockSpec(wab.shape, lambda i: (0, 0, 0)),
        ],
        out_specs=pl.BlockSpec((1, 2, N_KV_HEADS, n_chunk, 2 * CMP_HIDDEN), lambda i: (i, 0, 0, 0, 0)),
        out_shape=jax.ShapeDtypeStruct((b, 2, N_KV_HEADS, n_chunk, 2 * CMP_HIDDEN), F32),
        compiler_params=_params(("arbitrary",)),
        name="cmp_ab_prompt",
    )(craw, wab)


def _cmpfin_kernel(ab_ref, pe_ref, w1_ref, b1_ref, w2_ref, gk_ref, ck_ref, cv_ref, *, n_row):
    for s in range(2):
        bias = _dot(pe_ref[s].astype(BF16), w1_ref[s].astype(BF16)) + b1_ref[s]
        w2 = w2_ref[s].astype(BF16)
        for g in range(N_KV_HEADS):
            ab = ab_ref[0, s, g]
            hid = ab[:, :CMP_HIDDEN] + pltpu.roll(ab[:, CMP_HIDDEN:], n_row - 1, 0) + bias
            cdf = 0.5 * (1.0 + jnp.tanh(0.7978845608028654 * (hid + 0.044715 * (hid * hid * hid))))
            out = _dot((hid * cdf).astype(BF16), w2)
            if s == 0:
                ck_ref[0, g] = _rms(out, gk_ref[...]).astype(BF16)
            else:
                cv_ref[0, g] = out.astype(BF16)


def _cmp_finish(ab, cmp_pe, cmp_w1, cmp_b1, cmp_w2, gk):
    b = ab.shape[0]
    n_row = ab.shape[3]
    pe = cmp_pe.reshape(2, 1, CMP_BLOCK * HEAD_DIM)
    full = lambda a: pl.BlockSpec(a.shape, lambda i: (0,) * a.ndim)
    b1 = cmp_b1.reshape(2, 1, CMP_HIDDEN)
    gk2 = gk.reshape(1, HEAD_DIM)
    return pl.pallas_call(
        functools.partial(_cmpfin_kernel, n_row=n_row),
        grid=(b,),
        in_specs=[pl.BlockSpec((1,) + ab.shape[1:], lambda i: (i, 0, 0, 0, 0)),
                  full(pe), full(cmp_w1), full(b1), full(cmp_w2), full(gk2)],
        out_specs=[pl.BlockSpec((1, N_KV_HEADS, n_row, HEAD_DIM), lambda i: (i, 0, 0, 0)),
                   pl.BlockSpec((1, N_KV_HEADS, n_row, HEAD_DIM), lambda i: (i, 0, 0, 0))],
        out_shape=[jax.ShapeDtypeStruct((b, N_KV_HEADS, n_row, HEAD_DIM), BF16),
                   jax.ShapeDtypeStruct((b, N_KV_HEADS, n_row, HEAD_DIM), BF16)],
        compiler_params=_params(("arbitrary",)),
        name="cmp_finish",
    )(ab, pe, cmp_w1, b1, cmp_w2, gk2)


def _qproj_kernel(h_ref, g_ref, wt_ref, gq_ref, cos_ref, sin_ref, qn_ref, qr_ref, gt_ref, w_s):
    @pl.when(pl.program_id(0) == 0)
    def _():
        w_s[...] = wt_ref[...].astype(BF16)

    xn = _rms(h_ref[...], g_ref[...]).astype(BF16)
    qg = _nt_dot(w_s[...], xn)
    tt = xn.shape[0]
    gq = jnp.concatenate([gq_ref[...]] * (tt // LANES), axis=1)
    c = cos_ref[...]
    s = sin_ref[...]
    for h in range(N_HEADS):
        qh = qg[h * HEAD_DIM:(h + 1) * HEAD_DIM]
        qn = qh * lax.rsqrt(jnp.mean(qh * qh, axis=0, keepdims=True) + EPS) * gq
        qn_ref[h * HEAD_DIM:(h + 1) * HEAD_DIM, :] = (qn * SCALE).astype(BF16)
        x1 = qn[:HALF_DIM]
        x2 = qn[HALF_DIM:]
        qr_ref[h * HEAD_DIM:h * HEAD_DIM + HALF_DIM, :] = ((x1 * c - x2 * s) * SCALE).astype(BF16)
        qr_ref[h * HEAD_DIM + HALF_DIM:(h + 1) * HEAD_DIM, :] = ((x2 * c + x1 * s) * SCALE).astype(BF16)
    gt_ref[...] = _sigmoid(qg[N_HEADS * HEAD_DIM:])


def _qg_weights(w_qg):
    nq = N_HEADS * HEAD_DIM
    gates = w_qg[:, nq:].reshape(D_MODEL, 3, N_KV_HEADS, HEADS_PER_KV).transpose(2, 1, 3, 0)
    gates = gates.reshape(N_KV_HEADS, 3 * HEADS_PER_KV, D_MODEL)
    gates = jnp.pad(gates, ((0, 0), (0, GATE_ROWS - 3 * HEADS_PER_KV), (0, 0)))
    return jnp.concatenate([w_qg[:, :nq].T, gates.reshape(N_KV_HEADS * GATE_ROWS, D_MODEL)], axis=0)


def _qproj(h2, gain, wt, q_norm, cos_tt, sin_tt, *, tt, pos_blocks):
    n, d = h2.shape
    nq = N_HEADS * HEAD_DIM
    ng = N_KV_HEADS * GATE_ROWS
    gq = jnp.broadcast_to(q_norm[:, None], (HEAD_DIM, LANES))
    full = lambda a: pl.BlockSpec(a.shape, lambda i: (0,) * a.ndim)
    return pl.pallas_call(
        _qproj_kernel,
        grid=(n // tt,),
        in_specs=[
            pl.BlockSpec((tt, d), lambda i: (i, 0)),
            pl.BlockSpec((1, d), lambda i: (0, 0)),
            full(wt), full(gq),
            pl.BlockSpec((HALF_DIM, tt), lambda i: (0, i % pos_blocks)),
            pl.BlockSpec((HALF_DIM, tt), lambda i: (0, i % pos_blocks)),
        ],
        out_specs=[
            pl.BlockSpec((nq, tt), lambda i: (0, i)),
            pl.BlockSpec((nq, tt), lambda i: (0, i)),
            pl.BlockSpec((ng, tt), lambda i: (0, i)),
        ],
        out_shape=[jax.ShapeDtypeStruct((nq, n), BF16), jax.ShapeDtypeStruct((nq, n), BF16),
                   jax.ShapeDtypeStruct((ng, n), F32)],
        scratch_shapes=[pltpu.VMEM(wt.shape, BF16)],
        compiler_params=_params(("arbitrary",), VMEM_LIMIT),
        name="q_proj",
    )(h2, gain.reshape(1, d), wt, gq, cos_tt, sin_tt)


def _attn_kernel(qn_ref, qr_ref, gt_ref, ck_ref, cvt_ref, kaug_ref, kwin_ref, vaug_ref, vwaug_ref,
                 o_ref, pg_s, *, tq, n_cmp, n_sb):
    hp_n = HEADS_PER_KV
    qt = pl.program_id(2)
    t0 = qt * tq
    tpos = t0 + lax.broadcasted_iota(I32, (1, tq), 1)
    tpos4 = jnp.concatenate([tpos] * hp_n, axis=1)
    qn4 = jnp.concatenate([qn_ref[h * HEAD_DIM:(h + 1) * HEAD_DIM, :] for h in range(hp_n)], axis=1)
    qr4 = jnp.concatenate([qr_ref[h * HEAD_DIM:(h + 1) * HEAD_DIM, :] for h in range(hp_n)], axis=1)

    n_row = ck_ref.shape[2]
    s = _dot(ck_ref[0, 0], qn4)
    ci = lax.broadcasted_iota(I32, (n_row, 1), 0)
    vis = (ci * CMP_STRIDE + (CMP_BLOCK - 1) <= tpos4) & (ci < n_cmp)
    s = jnp.where(vis, s, NEG)
    p = jnp.where(vis, jnp.exp(s - jnp.max(s, axis=0, keepdims=True)), 0.0)
    p = p / jnp.maximum(jnp.sum(p, axis=0, keepdims=True), TINY)
    o_cmp = _dot(cvt_ref[0, 0], p.astype(BF16))
    pg = p[:, 0:tq]
    for h in range(1, hp_n):
        pg = pg + p[:, h * tq:(h + 1) * tq]

    ratio = SEL_BLOCK // CMP_STRIDE
    scores = []
    for c in range(tq // LANES):
        pg_s[c, 0:8, :] = jnp.zeros((8, LANES), F32)
        pg_s[c, 8:8 + n_row, :] = pg[:, c * LANES:(c + 1) * LANES]
        sc_c = pg_s[c, pl.ds(8 + 1 - CMP_BLOCK // CMP_STRIDE, n_sb, stride=ratio), :]
        for o in range(2 - CMP_BLOCK // CMP_STRIDE, ratio):
            sc_c = sc_c + pg_s[c, pl.ds(8 + o, n_sb, stride=ratio), :]
        scores.append(sc_c)
    score = jnp.concatenate(scores, axis=1)
    jrow = lax.broadcasted_iota(I32, (n_sb, 1), 0)
    cur = tpos // SEL_BLOCK
    forced = (jrow == 0) | (jrow == cur) | (jrow == cur - 1)
    sc = jnp.where(jrow * SEL_BLOCK <= tpos, score + jnp.where(forced, BIG, 0.0), -BIG)
    rank = jnp.zeros((n_sb, tq), I32)
    for jp in range(n_sb):
        row = sc[jp:jp + 1, :]
        beats = (row > sc) | ((row == sc) & (jrow > jp))
        rank = rank + beats.astype(I32)
    n_sbp = kaug_ref.shape[3] - HEAD_DIM
    sel_neg = jnp.where(rank < min(N_SEL, n_sb), 0.0, NEG)
    if n_sbp > n_sb:
        sel_neg = jnp.concatenate([sel_neg, jnp.zeros((n_sbp - n_sb, tq), F32)], axis=0)
    sel_neg = sel_neg.astype(BF16)
    n_chain = hp_n // ATTN_CHAIN_HEADS
    cw = ATTN_CHAIN_HEADS * tq

    def chain_lanes(per_head):
        return [jnp.concatenate(per_head[c * ATTN_CHAIN_HEADS:(c + 1) * ATTN_CHAIN_HEADS], axis=1)
                for c in range(n_chain)]

    q_rot = [qr_ref[h * HEAD_DIM:(h + 1) * HEAD_DIM, :] for h in range(hp_n)]
    q_aug = chain_lanes([jnp.concatenate([q, sel_neg], axis=0) for q in q_rot])
    q_win = chain_lanes(q_rot)

    def update(k_tile, v_tile, q, bias, m, acc):
        sk = _dot(k_tile, q)
        if bias is not None:
            sk = sk + bias
        m_new = jnp.maximum(m, jnp.max(sk, axis=0, keepdims=True))
        pk = jnp.exp(sk - m_new).astype(BF16)
        return m_new, acc * jnp.exp(m - m_new) + _dot(v_tile, pk)

    def sel_tile(kt, bias, carry):
        k0 = pl.multiple_of(kt * tq, tq)
        k_tile = kaug_ref[0, 0, pl.ds(k0, tq), :]
        v_tile = vaug_ref[:, pl.ds(k0, tq)]
        return tuple(update(k_tile, v_tile, q_aug[c], bias, *carry[c]) for c in range(n_chain))

    v_rows = vaug_ref.shape[0]
    init = tuple((jnp.full((1, cw), NEG, F32), jnp.zeros((v_rows, cw), F32)) for _ in range(n_chain))
    carry = lax.fori_loop(0, qt, lambda kt, c: sel_tile(kt, None, c), init)
    causal = jnp.where(lax.broadcasted_iota(I32, (tq, 1), 0) <= lax.broadcasted_iota(I32, (1, tq), 1), 0.0, NEG)
    carry = sel_tile(qt, jnp.concatenate([causal] * ATTN_CHAIN_HEADS, axis=1), carry)
    o_sel = jnp.concatenate([acc[:HEAD_DIM] / jnp.maximum(acc[HEAD_DIM:HEAD_DIM + 1], TINY) for _, acc in carry],
                            axis=1)

    n_wt = WINDOW // tq + 1
    w0 = pl.multiple_of(jnp.clip(qt - (n_wt - 1), 0, pl.num_programs(2) - n_wt) * tq, tq)
    dq = tpos - (w0 + lax.broadcasted_iota(I32, (n_wt * tq, 1), 0))
    wbias = jnp.where((dq >= 0) & (dq < WINDOW), 0.0, NEG)
    kw_tile = kwin_ref[0, 0, pl.ds(w0, n_wt * tq), :]
    vw_tile = vwaug_ref[:, pl.ds(w0, n_wt * tq)]
    wbias_c = jnp.concatenate([wbias] * ATTN_CHAIN_HEADS, axis=1)
    o_win = []
    for c in range(n_chain):
        sw = _dot(kw_tile, q_win[c]) + wbias_c
        pw = jnp.exp(sw - jnp.max(sw, axis=0, keepdims=True)).astype(BF16)
        acc_win = _dot(vw_tile, pw)
        o_win.append(acc_win[:HEAD_DIM] / jnp.maximum(acc_win[HEAD_DIM:HEAD_DIM + 1], TINY))
    o_win = jnp.concatenate(o_win, axis=1)
    gt = gt_ref[...]
    for h in range(hp_n):
        sl = slice(h * tq, (h + 1) * tq)
        o = (gt[h:h + 1] * o_cmp[:, sl] + gt[hp_n + h:hp_n + h + 1] * o_sel[:, sl]
             + gt[2 * hp_n + h:2 * hp_n + h + 1] * o_win[:, sl])
        o_ref[h * HEAD_DIM:(h + 1) * HEAD_DIM, :] = o.astype(BF16)


def _attn_prompt(qn_t, qr_t, g_t, ck, cv_t, ksel, kwin, vsel_t, vwin_t, *, b, t, tq):
    nq = t // tq
    n_sb = t // SEL_BLOCK
    n_row = ck.shape[2]
    n_cmp = t // CMP_STRIDE - CMP_BLOCK // CMP_STRIDE + 1
    n_sbp = -(-n_sb // 32) * 32
    kw = N_KV_HEADS * HEAD_DIM
    onehot = (jnp.arange(t)[:, None] // SEL_BLOCK == jnp.arange(n_sbp)[None, :]).astype(BF16)
    kaug = jnp.concatenate([ksel.reshape(N_KV_HEADS, b, t, HEAD_DIM),
                            jnp.broadcast_to(onehot, (N_KV_HEADS, b, t, n_sbp))], axis=-1)
    ones_rows = jnp.zeros((N_KV_HEADS, V_PAD_ROWS, b * t), BF16).at[:, 0].set(1.0)

    def with_ones(v_t):
        v3 = jnp.concatenate([v_t.reshape(N_KV_HEADS, HEAD_DIM, b * t), ones_rows], axis=1)
        return v3.reshape(N_KV_HEADS * (HEAD_DIM + V_PAD_ROWS), b * t)

    v_rows = HEAD_DIM + V_PAD_ROWS
    return pl.pallas_call(
        functools.partial(_attn_kernel, tq=tq, n_cmp=n_cmp, n_sb=n_sb),
        grid=(b, N_KV_HEADS, nq),
        in_specs=[
            pl.BlockSpec((kw, tq), lambda i, g, q: (g, i * nq + q)),
            pl.BlockSpec((kw, tq), lambda i, g, q: (g, i * nq + q)),
            pl.BlockSpec((GATE_ROWS, tq), lambda i, g, q: (g, i * nq + q)),
            pl.BlockSpec((1, 1, n_row, HEAD_DIM), lambda i, g, q: (i, g, 0, 0)),
            pl.BlockSpec((1, 1, HEAD_DIM, n_row), lambda i, g, q: (i, g, 0, 0)),
            pl.BlockSpec((1, 1, t, HEAD_DIM + n_sbp), lambda i, g, q: (g, i, 0, 0)),
            pl.BlockSpec((1, 1, t, HEAD_DIM), lambda i, g, q: (g, i, 0, 0)),
            pl.BlockSpec((v_rows, t), lambda i, g, q: (g, i)),
            pl.BlockSpec((v_rows, t), lambda i, g, q: (g, i)),
        ],
        out_specs=pl.BlockSpec((kw, tq), lambda i, g, q: (g, i * nq + q)),
        out_shape=jax.ShapeDtypeStruct((N_HEADS * HEAD_DIM, b * t), BF16),
        scratch_shapes=[pltpu.VMEM((tq // LANES, n_row + 8, LANES), F32)],
        compiler_params=_params(("arbitrary", "arbitrary", "arbitrary"), VMEM_LIMIT),
        name="attn_prompt",
    )(qn_t, qr_t, g_t, ck, cv_t, kaug, kwin.reshape(N_KV_HEADS, b, t, HEAD_DIM), with_ones(vsel_t), with_ones(vwin_t))


def _oproj_t_kernel(ot_ref, h_ref, w_ref, out_ref, w_s):
    @pl.when(pl.program_id(0) == 0)
    def _():
        w_s[...] = w_ref[...].astype(BF16)

    o = ot_ref[...].astype(F32).T.astype(BF16)
    out_ref[...] = h_ref[...] + _dot(o, w_s[...])


def _oproj_t(o_t, h2, w_o, *, tt):
    n, d = h2.shape
    return pl.pallas_call(
        _oproj_t_kernel,
        grid=(n // tt,),
        in_specs=[
            pl.BlockSpec((o_t.shape[0], tt), lambda i: (0, i)),
            pl.BlockSpec((tt, d), lambda i: (i, 0)),
            pl.BlockSpec(w_o.shape, lambda i: (0, 0)),
        ],
        out_specs=pl.BlockSpec((tt, d), lambda i: (i, 0)),
        out_shape=jax.ShapeDtypeStruct((n, d), F32),
        scratch_shapes=[pltpu.VMEM(w_o.shape, BF16)],
        compiler_params=_params(("arbitrary",), VMEM_LIMIT),
        name="o_proj",
    )(o_t, h2, w_o)


def _oproj_n_kernel(o_ref, h_ref, w_ref, out_ref):
    out_ref[...] = h_ref[...] + _dot(o_ref[...].astype(BF16), w_ref[...].astype(BF16))


def _oproj_n(o, h2, w_o):
    n, d = h2.shape
    return pl.pallas_call(
        _oproj_n_kernel,
        out_shape=jax.ShapeDtypeStruct((n, d), F32),
        compiler_params=_params((), VMEM_LIMIT),
        name="o_proj_sample",
    )(o, h2, w_o)


def _attn_sample_kernel(pt_ref, *refs, n_pages, n_cmp, n_sb, ts, past, n_buf):
    del pt_ref
    page_refs = refs[:n_pages]
    (qn_ref, qr_ref, ckt_ref, cvt_ref, win_ref, knew_ref, vnew_ref, kwnew_ref, vwnew_ref, gate_ref,
     sel_ref, e_ref, o_ref, mask_s, m_s, l_s, acc_s, ocmp_s) = refs[n_pages:]
    q_step = pl.program_id(1)
    rows = HEADS_PER_KV * N_KV_HEADS * ts
    grp_rows = N_KV_HEADS * ts
    row = lax.broadcasted_iota(I32, (rows, 1), 0)
    qpos = past + row % ts
    qr = qr_ref[0]

    def tile_rows(x):
        return jnp.concatenate([x] * HEADS_PER_KV, axis=0)

    def online_update(s, mk, v_dot):
        s = jnp.where(mk, s, NEG)
        m_new = jnp.maximum(m_s[...], jnp.max(s, axis=-1, keepdims=True))
        alpha = jnp.exp(m_s[...] - m_new)
        p = jnp.where(mk, jnp.exp(s - m_new), 0.0)
        l_s[...] = l_s[...] * alpha + jnp.sum(p, axis=-1, keepdims=True)
        acc_s[...] = acc_s[...] * alpha + v_dot(p.astype(BF16))
        m_s[...] = m_new

    @pl.when(q_step == 0)
    def _():
        n_row = ckt_ref.shape[2]
        s = _dot(qn_ref[0], ckt_ref[0])
        ci = lax.broadcasted_iota(I32, (1, n_row), 1)
        vis = (ci * CMP_STRIDE + (CMP_BLOCK - 1) <= qpos) & (ci < n_cmp)
        s = jnp.where(vis, s, NEG)
        p = jnp.where(vis, jnp.exp(s - jnp.max(s, axis=-1, keepdims=True)), 0.0)
        p = p / jnp.maximum(jnp.sum(p, axis=-1, keepdims=True), TINY)
        ocmp_s[...] = _nt_dot(p.astype(BF16), cvt_ref[0])
        pg = p[0:grp_rows]
        for h in range(1, HEADS_PER_KV):
            pg = pg + p[h * grp_rows:(h + 1) * grp_rows]
        score = _dot(pg, sel_ref[...], precision=HIGHEST)
        width = score.shape[1]
        j = lax.broadcasted_iota(I32, (1, width), 1)
        tq = qpos[0:grp_rows]
        cur = tq // SEL_BLOCK
        forced = (j == 0) | (j == cur) | (j == cur - 1)
        sc = jnp.where(j * SEL_BLOCK <= tq, score + jnp.where(forced, BIG, 0.0), -BIG)
        sc = jnp.where(j < n_sb, sc, -2.0 * BIG)
        rank = jnp.zeros((grp_rows, width), I32)
        for jp in range(n_sb):
            col = sc[:, jp:jp + 1]
            beats = (col > sc) | ((col == sc) & (j > jp))
            rank = rank + beats.astype(I32)
        sel = jnp.where((rank < min(N_SEL, n_sb)) & (j < n_sb), 1.0, 0.0).astype(BF16)
        mask_s[...] = _dot(sel, e_ref[...])
        m_s[...] = jnp.full(m_s.shape, NEG, F32)
        l_s[...] = jnp.zeros(l_s.shape, F32)
        acc_s[...] = jnp.zeros(acc_s.shape, F32)

    width = n_pages * PAGE_SIZE
    kt = jnp.concatenate([page_refs[i][0, 0].reshape(KV_WIDTH, PAGE_SIZE) for i in range(n_pages)], axis=1)
    vt = jnp.concatenate([page_refs[i][0, 1].reshape(KV_WIDTH, PAGE_SIZE) for i in range(n_pages)], axis=1)
    k0 = pl.multiple_of(q_step * width, width)
    mk = tile_rows(mask_s[:, pl.ds(k0, width)]) > 0.5
    online_update(_dot(qr, kt.astype(BF16)), mk, lambda p: _nt_dot(p, vt.astype(BF16)))

    @pl.when(q_step == pl.num_programs(1) - 1)
    def _():
        lane = lax.broadcasted_iota(I32, (1, LANES), 1)
        new_ok = (lane < ts) & (past + lane <= qpos)
        mk_new = (tile_rows(mask_s[:, past:past + LANES]) > 0.5) & new_ok
        online_update(_nt_dot(qr, knew_ref[0]), mk_new, lambda p: _dot(p, vnew_ref[0]))
        o_sel = acc_s[...] / jnp.maximum(l_s[...], TINY)
        bi = lax.broadcasted_iota(I32, (1, n_buf), 1)
        dq = qpos - (past - n_buf + bi)
        ok_buf = (dq >= 0) & (dq < WINDOW)
        dq_new = qpos - (past + lane)
        ok_new = (lane < ts) & (dq_new >= 0) & (dq_new < WINDOW)
        s_w = jnp.concatenate([_dot(qr, win_ref[0, 0].astype(BF16)), _nt_dot(qr, kwnew_ref[0])], axis=1)
        ok = jnp.concatenate([jnp.broadcast_to(ok_buf, (rows, n_buf)), jnp.broadcast_to(ok_new, (rows, LANES))], axis=1)
        s_w = jnp.where(ok, s_w, NEG)
        p_w = jnp.where(ok, jnp.exp(s_w - jnp.max(s_w, axis=-1, keepdims=True)), 0.0)
        p_w = (p_w / jnp.maximum(jnp.sum(p_w, axis=-1, keepdims=True), TINY)).astype(BF16)
        o_win = _nt_dot(p_w[:, :n_buf], win_ref[0, 1].astype(BF16)) + _dot(p_w[:, n_buf:], vwnew_ref[0])
        row_g = (row // ts) % N_KV_HEADS

        def own_group(o):
            out = jnp.zeros((rows, HEAD_DIM), F32)
            for g in range(N_KV_HEADS):
                out = out + jnp.where(row_g == g, o[:, g * HEAD_DIM:(g + 1) * HEAD_DIM], 0.0)
            return out

        o_ref[0] = (gate_ref[0, 0] * own_group(ocmp_s[...]) + gate_ref[0, 1] * own_group(o_sel)
                    + gate_ref[0, 2] * own_group(o_win))


def _attn_sample(cache5, page_table, qn_bd, qr_bd, ck_t, cv_t, win4, knew, vnew, kwnew, vwnew, gates,
                 *, n_pages, ts, past):
    b, pages_per_seq = page_table.shape
    steps = pages_per_seq // n_pages
    rows = qn_bd.shape[1]
    n_chunk = ck_t.shape[2]
    n_cmp = n_chunk - CMP_BLOCK // CMP_STRIDE + 1
    n_keys = past + LANES
    n_sb = -(-(past + ts) // SEL_BLOCK)
    n_sb_pad = -(-n_sb // LANES) * LANES
    n_buf = win4.shape[3]
    ratio = SEL_BLOCK // CMP_STRIDE
    ci = jnp.arange(n_chunk)[:, None]
    jb = jnp.arange(n_sb_pad)[None, :]
    sel_map = ((ci >= ratio * jb + 1 - CMP_BLOCK // CMP_STRIDE) & (ci < ratio * jb + ratio)
               & (ci < n_cmp) & (jb < n_sb)).astype(F32)
    expand = (jnp.arange(n_keys)[None, :] // SEL_BLOCK == jnp.arange(n_sb_pad)[:, None]).astype(BF16)

    def page_spec(j):
        return pl.BlockSpec((1, 2, N_KV_HEADS, HEAD_DIM, PAGE_SIZE),
                            lambda i, q, pt: (pt[i, q * n_pages + j], 1, 0, 0, 0))

    per_seq = lambda a: pl.BlockSpec((1,) + a.shape[1:], lambda i, q, pt: (i,) + (0,) * (a.ndim - 1))
    full = lambda a: pl.BlockSpec(a.shape, lambda i, q, pt: (0,) * a.ndim)
    return pl.pallas_call(
        functools.partial(_attn_sample_kernel, n_pages=n_pages, n_cmp=n_cmp, n_sb=n_sb, ts=ts, past=past,
                          n_buf=n_buf),
        grid_spec=pltpu.PrefetchScalarGridSpec(
            num_scalar_prefetch=1,
            grid=(b, steps),
            in_specs=[page_spec(j) for j in range(n_pages)]
            + [per_seq(a) for a in (qn_bd, qr_bd, ck_t, cv_t, win4, knew, vnew, kwnew, vwnew, gates)]
            + [full(sel_map), full(expand)],
            out_specs=pl.BlockSpec((1, rows, HEAD_DIM), lambda i, q, pt: (i, 0, 0)),
            scratch_shapes=[
                pltpu.VMEM((N_KV_HEADS * ts, n_keys), F32),
                pltpu.VMEM((rows, 1), F32), pltpu.VMEM((rows, 1), F32),
                pltpu.VMEM((rows, KV_WIDTH), F32), pltpu.VMEM((rows, KV_WIDTH), F32),
            ],
        ),
        out_shape=jax.ShapeDtypeStruct((b, rows, HEAD_DIM), F32),
        compiler_params=_params(("arbitrary", "arbitrary"), VMEM_LIMIT),
        name="attn_sample",
    )(page_table, *([cache5] * n_pages), qn_bd, qr_bd, ck_t, cv_t, win4, knew, vnew, kwnew, vwnew, gates,
      sel_map, expand)


def _router_weights(router_group, router_expert):
    rt = jnp.concatenate([router_group, router_expert], axis=1).T
    return jnp.pad(rt, ((0, 24 - rt.shape[0]), (0, 0)))


def _trunk_prompt(x, p, *, tt_pool=256, tm=1024, tt=512, tq=256):
    (norm_mix, norm_ffn, pool_w, pool_scale, kv_norm, w_kv, k_norm, cmp_pe, cmp_w1, cmp_b1, cmp_w2,
     w_qg, q_norm, w_o, router_group, router_expert, w_gate_up, w_down) = p
    b, t, d = x.shape
    n = b * t
    pos = jnp.arange(t)
    h, new_pool = _pool_layer(x, jnp.zeros((b, POOL_HALO, d), F32), norm_mix[0], pool_w[0], pool_scale[0],
                              tt=tt_pool, clip=True)
    h = _moe(h.reshape(n, d), norm_ffn[0], _router_weights(router_group[0], router_expert[0]),
             w_gate_up, w_down, 0, tm=tm)
    cos_t, sin_t = _rope_tables_transposed(pos)
    kv_t, win_t, craw, ksel, kwin, vsel_t, vwin_t = _kvproj(h.reshape(b, t, d), kv_norm, w_kv, k_norm,
                                                            cos_t, sin_t, tt=tt)
    ab = _cmp_ab_prompt(craw, _cmp_weights(cmp_w1), b=b, t=t)
    ck, cv = _cmp_finish(ab, cmp_pe, cmp_w1, cmp_b1, cmp_w2, k_norm[0])
    qn_t, qr_t, g_t = _qproj(h, norm_mix[1], _qg_weights(w_qg[0]), q_norm[0], cos_t, sin_t,
                             tt=tt, pos_blocks=t // tt)
    o_t = _attn_prompt(qn_t, qr_t, g_t, ck, jnp.swapaxes(cv, 2, 3), ksel, kwin, vsel_t, vwin_t, b=b, t=t, tq=tq)
    h = _oproj_t(o_t, h, w_o[0], tt=tt)
    h = _moe(h, norm_ffn[1], _router_weights(router_group[1], router_expert[1]),
             w_gate_up, w_down, 1, tm=tm)
    n_win = min(WINDOW, t)
    kv_new = kv_t.reshape(b, 4, N_KV_HEADS, HEAD_DIM, t).transpose(0, 4, 1, 2, 3)
    win_new = win_t[:, :, t - n_win:].reshape(b, 2, N_KV_HEADS, HEAD_DIM, n_win).transpose(0, 4, 1, 2, 3)
    return h.reshape(b, t, d), new_pool[None], kv_new, win_new


def _trunk_sample(x, state_pool, cache_kv, page_table, state_win, p, *, n_pages=16):
    (norm_mix, norm_ffn, pool_w, pool_scale, kv_norm, w_kv, k_norm, cmp_pe, cmp_w1, cmp_b1, cmp_w2,
     w_qg, q_norm, w_o, router_group, router_expert, w_gate_up, w_down) = p
    b, ts, d = x.shape
    n = b * ts
    past = page_table.shape[1] * PAGE_SIZE
    n_buf = state_win.shape[1]
    prev16 = jnp.pad(state_pool[0], ((0, 0), (POOL_HALO - POOL_STATE, 0), (0, 0)))
    h, new_pool = _pool_layer(x, prev16, norm_mix[0], pool_w[0], pool_scale[0], tt=ts, clip=False)
    h = _moe(h.reshape(n, d), norm_ffn[0], _router_weights(router_group[0], router_expert[0]),
             w_gate_up, w_down, 0, tm=n)
    cos_t, sin_t = _rope_tables_transposed(past + jnp.arange(ts))
    cos_t = jnp.tile(cos_t, (1, b))
    sin_t = jnp.tile(sin_t, (1, b))
    kv_t, win_t, _, _, _, _, _ = _kvproj(h.reshape(1, n, d), kv_norm, w_kv, k_norm, cos_t, sin_t, tt=n)
    kv_rows = kv_t[0].T
    win_rows = win_t[0].T

    cache5 = cache_kv.transpose(0, 2, 3, 4, 1)
    ab = _cmp_ab_pages(cache5, page_table, _cmp_weights(cmp_w1), n_pages=n_pages)
    ck, cv = _cmp_finish(ab, cmp_pe, cmp_w1, cmp_b1, cmp_w2, k_norm[0])
    n_chunk = ck.shape[2]
    ck_t = ck.transpose(0, 1, 3, 2).reshape(b, KV_WIDTH, n_chunk)
    cv_t = cv.transpose(0, 1, 3, 2).reshape(b, KV_WIDTH, n_chunk)

    qn_t, qr_t, g_t = _qproj(h, norm_mix[1], _qg_weights(w_qg[0]), q_norm[0], cos_t, sin_t, tt=n, pos_blocks=1)

    def block_diag_queries(q_t):
        q5 = q_t.reshape(N_KV_HEADS, HEADS_PER_KV, HEAD_DIM, b, ts).transpose(3, 1, 0, 4, 2)
        eye = jnp.eye(N_KV_HEADS, dtype=q_t.dtype)
        qbd = q5[:, :, :, :, None, :] * eye[None, None, :, None, :, None]
        return qbd.reshape(b, HEADS_PER_KV * N_KV_HEADS * ts, KV_WIDTH)

    gates = g_t.reshape(N_KV_HEADS, GATE_ROWS, b, ts)[:, :3 * HEADS_PER_KV]
    gates = gates.reshape(N_KV_HEADS, 3, HEADS_PER_KV, b, ts).transpose(3, 1, 2, 0, 4)
    gates = jnp.broadcast_to(gates.reshape(b, 3, HEADS_PER_KV * N_KV_HEADS * ts, 1),
                             (b, 3, HEADS_PER_KV * N_KV_HEADS * ts, HEAD_DIM))

    def new_rows(rows2):
        return jnp.pad(rows2.reshape(b, ts, KV_WIDTH), ((0, 0), (0, LANES - ts), (0, 0))).astype(BF16)

    win4 = state_win.transpose(0, 2, 3, 4, 1).reshape(b, 2, KV_WIDTH, n_buf)
    o = _attn_sample(cache5, page_table, block_diag_queries(qn_t), block_diag_queries(qr_t), ck_t, cv_t, win4,
                     new_rows(kv_rows[:, 2 * KV_WIDTH:3 * KV_WIDTH]), new_rows(kv_rows[:, 3 * KV_WIDTH:]),
                     new_rows(win_rows[:, :KV_WIDTH]), new_rows(win_rows[:, KV_WIDTH:]), gates,
                     n_pages=n_pages, ts=ts, past=past)
    o = o.reshape(b, HEADS_PER_KV, N_KV_HEADS, ts, HEAD_DIM).transpose(0, 3, 2, 1, 4).reshape(n, N_HEADS * HEAD_DIM)
    h = _oproj_n(o, h, w_o[0])
    h = _moe(h, norm_ffn[1], _router_weights(router_group[1], router_expert[1]),
             w_gate_up, w_down, 1, tm=n)
    kv_new = kv_rows.reshape(b, ts, 4, N_KV_HEADS, HEAD_DIM)
    win_new = jnp.concatenate([state_win, win_rows.reshape(b, ts, 2, N_KV_HEADS, HEAD_DIM)], axis=1)[:, -n_buf:]
    return h.reshape(b, ts, d), new_pool[None], kv_new, win_new


def kernel(x_prompt, x_sample, state_pool, cache_kv, page_table, state_win, norm_mix, norm_ffn, pool_w, pool_scale, kv_norm, w_kv, k_norm, cmp_pe, cmp_w1, cmp_b1, cmp_w2, w_qg, q_norm, w_o, router_group, router_expert, w_gate_up, w_down):
    params = (norm_mix, norm_ffn, pool_w, pool_scale, kv_norm, w_kv, k_norm, cmp_pe, cmp_w1, cmp_b1,
              cmp_w2, w_qg, q_norm, w_o, router_group, router_expert, _cast_bf16(w_gate_up), _cast_bf16(w_down))
    y_p, pool_p, kv_p, win_p = _trunk_prompt(x_prompt, params)
    y_s, pool_s, kv_s, win_s = _trunk_sample(x_sample, state_pool, cache_kv, page_table, state_win, params)
    return y_p, y_s, pool_p, pool_s, kv_p, kv_s, win_p, win_s
```

```python
import functools

import jax
import jax.numpy as jnp
from jax import lax
from jax.experimental import pallas as pl
from jax.experimental.pallas import tpu as pltpu

F32 = jnp.float32
BF16 = jnp.bfloat16
I32 = jnp.int32
HIGHEST = lax.Precision.HIGHEST

D_MODEL = 1024
POOL_WINDOWS = (2, 4, 8, 16)
POOL_GROUP_DIM = D_MODEL // len(POOL_WINDOWS)
POOL_STATE = max(POOL_WINDOWS) - 1
POOL_HALO = 16
N_HEADS = 16
HEAD_DIM = 64
HALF_DIM = HEAD_DIM // 2
N_KV_HEADS = 4
HEADS_PER_KV = N_HEADS // N_KV_HEADS
KV_WIDTH = N_KV_HEADS * HEAD_DIM
CMP_BLOCK = 32
CMP_STRIDE = 16
CMP_HIDDEN = 2 * HEAD_DIM
SEL_BLOCK = 64
N_SEL = 16
WINDOW = 512
PAGE_SIZE = 128
ROPE_THETA = 10000.0
SCALE = HEAD_DIM ** -0.5
N_GROUPS = 4
EXPERTS_PER_GROUP = 4
N_EXPERTS = N_GROUPS * EXPERTS_PER_GROUP
D_EXPERT = 512
EPS = 1e-6
NEG = -1e30
TINY = 1e-30
BIG = 1e4

LANES = 128
GATE_ROWS = 16
ROUTE_GROUP_ROW = 3 * N_EXPERTS
MOE_SEG_ALIGN = 16
MOE_CHUNK = 144
MOE_STEP_EXPERTS = 2
V_PAD_ROWS = 16
VMEM_LIMIT = 56 * 1024 * 1024


def _params(sem, vmem=None):
    return pltpu.CompilerParams(dimension_semantics=sem, vmem_limit_bytes=vmem)


def _rms(x, g):
    return x * lax.rsqrt(jnp.mean(x * x, axis=-1, keepdims=True) + EPS) * g


def _nt_dot(a, b, precision=None):
    return lax.dot_general(a, b, (((1,), (1,)), ((), ())), precision=precision,
                           preferred_element_type=F32)


def _dot(a, b, precision=None):
    return jnp.dot(a, b, precision=precision, preferred_element_type=F32)


def _sigmoid(x):
    return 1.0 / (1.0 + jnp.exp(-x))


def _pool_kernel(h_ref, prev_ref, g_ref, w_ref, sc_ref, o_ref, np_ref, ext_ref, lvl_ref, *, tt, clip):
    t = pl.program_id(1)
    x = h_ref[0]
    xn = _rms(x, g_ref[...])

    @pl.when(t == 0)
    def _():
        ext_ref[0:POOL_HALO, :] = prev_ref[0]

    @pl.when(t > 0)
    def _():
        ext_ref[0:POOL_HALO, :] = ext_ref[tt:tt + POOL_HALO, :]

    ext_ref[POOL_HALO:POOL_HALO + tt, :] = xn
    if clip:
        tpos = t * tt + lax.broadcasted_iota(I32, (tt, 1), 0)
    outs = []
    for gi, w in enumerate(POOL_WINDOWS):
        lo = gi * POOL_GROUP_DIM
        hi = lo + POOL_GROUP_DIM
        n = tt + POOL_HALO
        lvl_ref[...] = ext_ref[:, lo:hi]
        span = 1
        while span < w:
            lvl_ref[span:n, :] = lvl_ref[span:n, :] + lvl_ref[0:n - span, :]
            span *= 2
        acc = lvl_ref[POOL_HALO:n, :]
        if clip:
            mean = acc / jnp.minimum(tpos + 1, w).astype(F32)
        else:
            mean = acc * (1.0 / w)
        d = (mean - xn[:, lo:hi]).astype(BF16)
        outs.append(_dot(d, w_ref[gi].astype(BF16)))
    o_ref[0] = x + jnp.concatenate(outs, axis=1) * sc_ref[...]

    @pl.when(t == pl.num_programs(1) - 1)
    def _():
        np_ref[0] = ext_ref[tt + POOL_HALO - POOL_STATE:tt + POOL_HALO, :]


def _pool_layer(h, prev16, gain, w_pool, scale, *, tt, clip):
    b, t, d = h.shape
    return pl.pallas_call(
        functools.partial(_pool_kernel, tt=tt, clip=clip),
        grid=(b, t // tt),
        in_specs=[
            pl.BlockSpec((1, tt, d), lambda i, j: (i, j, 0)),
            pl.BlockSpec((1, POOL_HALO, d), lambda i, j: (i, 0, 0)),
            pl.BlockSpec((1, d), lambda i, j: (0, 0)),
            pl.BlockSpec(w_pool.shape, lambda i, j: (0, 0, 0)),
            pl.BlockSpec((1, d), lambda i, j: (0, 0)),
        ],
        out_specs=[
            pl.BlockSpec((1, tt, d), lambda i, j: (i, j, 0)),
            pl.BlockSpec((1, POOL_STATE, d), lambda i, j: (i, 0, 0)),
        ],
        out_shape=[jax.ShapeDtypeStruct((b, t, d), F32),
                   jax.ShapeDtypeStruct((b, POOL_STATE, d), F32)],
        scratch_shapes=[pltpu.VMEM((tt + POOL_HALO, d), F32), pltpu.VMEM((tt + POOL_HALO, POOL_GROUP_DIM), F32)],
        compiler_params=_params(("arbitrary", "arbitrary")),
        name="pool_layer",
    )(h, prev16, gain.reshape(1, d), w_pool, scale.reshape(1, d))


def _softmax_rows(rows):
    m = functools.reduce(jnp.maximum, rows)
    es = [jnp.exp(r - m) for r in rows]
    s = functools.reduce(lambda a, b: a + b, es)
    return [e / s for e in es]


def _router_kernel(h_ref, g_ref, rt_ref, xn_ref, comb_ref, grp_ref):
    xn = _rms(h_ref[...], g_ref[...])
    xn_ref[...] = xn.astype(BF16)
    lt = _nt_dot(rt_ref[...], xn, precision=HIGHEST)
    pg = _softmax_rows([lt[i:i + 1, :] for i in range(N_GROUPS)])
    g_val = functools.reduce(jnp.maximum, pg)
    g_idx = jnp.full(g_val.shape, N_GROUPS - 1, I32)
    for i in range(N_GROUPS - 2, -1, -1):
        g_idx = jnp.where(pg[i] == g_val, i, g_idx)
    le = []
    for j in range(EXPERTS_PER_GROUP):
        v = lt[N_GROUPS + (N_GROUPS - 1) * EXPERTS_PER_GROUP + j:N_GROUPS + (N_GROUPS - 1) * EXPERTS_PER_GROUP + j + 1, :]
        for gi in range(N_GROUPS - 2, -1, -1):
            r = N_GROUPS + gi * EXPERTS_PER_GROUP + j
            v = jnp.where(g_idx == gi, lt[r:r + 1, :], v)
        le.append(v)
    pe = _softmax_rows(le)
    ranks = []
    for j in range(EXPERTS_PER_GROUP):
        r = jnp.zeros(g_val.shape, I32)
        for i in range(EXPERTS_PER_GROUP):
            if i == j:
                continue
            beats = (pe[i] > pe[j]) | (pe[i] == pe[j]) if i < j else (pe[i] > pe[j])
            r = r + beats.astype(I32)
        ranks.append(r)
    vals, idxs = [], []
    for k in range(2):
        v = jnp.zeros(g_val.shape, F32)
        ix = jnp.zeros(g_val.shape, I32)
        for j in range(EXPERTS_PER_GROUP):
            hit = ranks[j] == k
            v = jnp.where(hit, pe[j], v)
            ix = jnp.where(hit, j, ix)
        vals.append(v)
        idxs.append(ix)
    tot = vals[0] + vals[1]
    erow = lax.broadcasted_iota(I32, (LANES, g_val.shape[1]), 0)
    comb_t = jnp.where(erow == ROUTE_GROUP_ROW + g_idx, 1.0, 0.0)
    for k in range(2):
        wk = g_val * (vals[k] / tot)
        w_hi = wk.astype(BF16).astype(F32)
        w_mid = (wk - w_hi).astype(BF16).astype(F32)
        w_lo = (wk - w_hi - w_mid).astype(BF16).astype(F32)
        eid = g_idx * EXPERTS_PER_GROUP + idxs[k]
        for part, term in enumerate((w_hi, w_mid, w_lo)):
            comb_t = comb_t + jnp.where(erow == part * N_EXPERTS + eid, term, 0.0)
    comb_ref[...] = comb_t.T
    grp_ref[...] = comb_t[ROUTE_GROUP_ROW:ROUTE_GROUP_ROW + 8]


def _router(h2, gain, rt, *, tm):
    n, d = h2.shape
    return pl.pallas_call(
        _router_kernel,
        grid=(n // tm,),
        in_specs=[
            pl.BlockSpec((tm, d), lambda i: (i, 0)),
            pl.BlockSpec((1, d), lambda i: (0, 0)),
            pl.BlockSpec(rt.shape, lambda i: (0, 0)),
        ],
        out_specs=[
            pl.BlockSpec((tm, d), lambda i: (i, 0)),
            pl.BlockSpec((tm, LANES), lambda i: (i, 0)),
            pl.BlockSpec((8, tm), lambda i: (0, i)),
        ],
        out_shape=[jax.ShapeDtypeStruct((n, d), BF16), jax.ShapeDtypeStruct((n, LANES), F32),
                   jax.ShapeDtypeStruct((8, n), F32)],
        compiler_params=_params(("arbitrary",)),
        name="moe_router",
    )(h2, gain.reshape(1, d), rt)


def _cast_kernel(x_ref, o_ref):
    o_ref[...] = x_ref[...].astype(o_ref.dtype)


def _cast_bf16(w):
    blk = (1, 1) + w.shape[2:]
    return pl.pallas_call(
        _cast_kernel,
        grid=w.shape[:2],
        in_specs=[pl.BlockSpec(blk, lambda i, j: (i, j, 0, 0))],
        out_specs=pl.BlockSpec(blk, lambda i, j: (i, j, 0, 0)),
        out_shape=jax.ShapeDtypeStruct(w.shape, BF16),
        compiler_params=_params(("arbitrary", "arbitrary")),
        name="cast_weights",
    )(w)


def _moe_kernel(cnt_ref, base_ref, x_ref, comb_ref, grp_ref, h_ref, wgu_ref, wd_ref, o_ref,
                pt_s, z_s, y_s, cs_s, slot_s, *, r_max):
    w = pl.program_id(0)
    step = pl.program_id(1)
    nw = x_ref.shape[0]
    g = step // (EXPERTS_PER_GROUP // MOE_STEP_EXPERTS)

    @pl.when(step == 0)
    def _():
        lane = lax.broadcasted_iota(I32, (1, LANES), 1)
        in_grp = (lane >= ROUTE_GROUP_ROW) & (lane < ROUTE_GROUP_ROW + N_GROUPS)
        oh = jnp.where(in_grp, comb_ref[...], 0.0)
        base_row = jnp.zeros((1, LANES), F32)
        for gi in range(N_GROUPS):
            base_row = jnp.where(lane == ROUTE_GROUP_ROW + gi, base_ref[w, gi].astype(F32), base_row)
        ri = lax.broadcasted_iota(I32, (nw, nw), 0)
        ci = lax.broadcasted_iota(I32, (nw, nw), 1)
        before = _dot(jnp.where(ci < ri, 1.0, 0.0).astype(BF16), oh.astype(BF16))
        slot_s[...] = jnp.sum(oh * (before + base_row), axis=-1, keepdims=True)
        oh_t = grp_ref[...]
        before_t = _dot(oh_t.astype(BF16), jnp.where(ri < ci, 1.0, 0.0).astype(BF16))
        slot_row = jnp.zeros((1, nw), F32)
        for gi in range(N_GROUPS):
            slot_row = slot_row + oh_t[gi:gi + 1] * (before_t[gi:gi + 1] + base_ref[w, gi].astype(F32))
        rows = lax.broadcasted_iota(I32, (r_max, 1), 0)
        pt_s[...] = jnp.where(rows == slot_row.astype(I32), 1.0, 0.0).astype(BF16)
        z_s[...] = _dot(pt_s[...], x_ref[...]).astype(BF16)
        cs_s[...] = _dot(pt_s[...], comb_ref[...].astype(BF16))
        y_s[...] = jnp.zeros(y_s.shape, F32)

    seg0 = base_ref[w, g]
    seg1 = seg0 + cnt_ref[w, g]
    lane_c = lax.broadcasted_iota(I32, (MOE_CHUNK, LANES), 1)
    in_terms = lane_c < 3 * N_EXPERTS

    def chunk(c, carry):
        r0 = pl.multiple_of(seg0 + c * MOE_CHUNK, MOE_SEG_ALIGN)
        z = z_s[pl.ds(r0, MOE_CHUNK), :]
        cs = cs_s[pl.ds(r0, MOE_CHUNK), :]
        acc = None
        for k in range(MOE_STEP_EXPERTS):
            gu = _dot(z, wgu_ref[0, k])
            a = gu[:, :D_EXPERT]
            hdn = (a * _sigmoid(a) * gu[:, D_EXPERT:]).astype(BF16)
            y = _dot(hdn, wd_ref[0, k])
            is_e = in_terms & (lane_c % N_EXPERTS == step * MOE_STEP_EXPERTS + k)
            wgt = jnp.sum(jnp.where(is_e, cs, 0.0), axis=-1, keepdims=True)
            acc = wgt * y if acc is None else acc + wgt * y
        valid = r0 + lax.broadcasted_iota(I32, (MOE_CHUNK, 1), 0) < seg1
        y_s[pl.ds(r0, MOE_CHUNK), :] += jnp.where(valid, acc, 0.0)
        return carry

    lax.fori_loop(0, (cnt_ref[w, g] + MOE_CHUNK - 1) // MOE_CHUNK, chunk, 0)

    @pl.when(step == pl.num_programs(1) - 1)
    def _():
        cols = lax.broadcasted_iota(I32, (1, r_max), 1)
        p = jnp.where(cols == slot_s[...].astype(I32), 1.0, 0.0).astype(BF16)
        o_ref[...] = h_ref[...] + _dot(p, y_s[...].astype(BF16))


def _moe(h2, gain, rt, w_gu, w_down, layer, *, tm):
    n, d = h2.shape
    if w_gu.dtype != BF16:
        w_gu, w_down = _cast_bf16(w_gu), _cast_bf16(w_down)
    xn, comb, grp = _router(h2, gain, rt, tm=tm)
    n_win = n // tm
    cnt = jnp.sum(grp[:N_GROUPS].reshape(N_GROUPS, n_win, tm), axis=-1).T.astype(I32)
    seg = (cnt + MOE_SEG_ALIGN - 1) // MOE_SEG_ALIGN * MOE_SEG_ALIGN
    base = jnp.cumsum(seg, axis=1) - seg
    r_max = -(-(tm + N_GROUPS * MOE_SEG_ALIGN + MOE_CHUNK) // LANES) * LANES
    return pl.pallas_call(
        functools.partial(_moe_kernel, r_max=r_max),
        grid_spec=pltpu.PrefetchScalarGridSpec(
            num_scalar_prefetch=2,
            grid=(n_win, N_EXPERTS // MOE_STEP_EXPERTS),
            in_specs=[
                pl.BlockSpec((tm, d), lambda i, e, c, b: (i, 0)),
                pl.BlockSpec((tm, LANES), lambda i, e, c, b: (i, 0)),
                pl.BlockSpec((8, tm), lambda i, e, c, b: (0, i)),
                pl.BlockSpec((tm, d), lambda i, e, c, b: (i, 0)),
                pl.BlockSpec((1, MOE_STEP_EXPERTS, d, 2 * D_EXPERT), lambda i, e, c, b: (layer, e, 0, 0)),
                pl.BlockSpec((1, MOE_STEP_EXPERTS, D_EXPERT, d), lambda i, e, c, b: (layer, e, 0, 0)),
            ],
            out_specs=pl.BlockSpec((tm, d), lambda i, e, c, b: (i, 0)),
            scratch_shapes=[
                pltpu.VMEM((r_max, tm), BF16), pltpu.VMEM((r_max, d), BF16), pltpu.VMEM((r_max, d), F32),
                pltpu.VMEM((r_max, LANES), F32), pltpu.VMEM((tm, 1), F32),
            ],
        ),
        out_shape=jax.ShapeDtypeStruct((n, d), F32),
        compiler_params=_params(("arbitrary", "arbitrary"), VMEM_LIMIT),
        name="moe_experts",
    )(cnt, base, xn, comb, grp, h2, w_gu, w_down)


def _head_norm_rope_t(z, gain, c, s):
    outs = []
    for g in range(z.shape[0] // HEAD_DIM):
        zh = z[g * HEAD_DIM:(g + 1) * HEAD_DIM]
        zn = zh * lax.rsqrt(jnp.mean(zh * zh, axis=0, keepdims=True) + EPS) * gain
        x1 = zn[:HALF_DIM]
        x2 = zn[HALF_DIM:]
        outs.append(x1 * c - x2 * s)
        outs.append(x2 * c + x1 * s)
    return jnp.concatenate(outs, axis=0)


def _kvproj_kernel(h_ref, g_ref, wt_ref, gsel_ref, gwin_ref, cos_ref, sin_ref,
                   kvt_ref, wint_ref, craw_ref, ksel_ref, kwin_ref, vselt_ref, vwint_ref, w_s):
    @pl.when((pl.program_id(0) == 0) & (pl.program_id(1) == 0))
    def _():
        w_s[...] = wt_ref[...].astype(BF16)

    xn = _rms(h_ref[0], g_ref[...]).astype(BF16)
    kvt = _nt_dot(w_s[...], xn)
    tt = xn.shape[0]
    c = cos_ref[...]
    s = sin_ref[...]
    gsel = jnp.concatenate([gsel_ref[...]] * (tt // LANES), axis=1)
    gwin = jnp.concatenate([gwin_ref[...]] * (tt // LANES), axis=1)
    ksel = _head_norm_rope_t(kvt[2 * KV_WIDTH:3 * KV_WIDTH], gsel, c, s)
    kwin = _head_norm_rope_t(kvt[4 * KV_WIDTH:5 * KV_WIDTH], gwin, c, s)
    kvt_ref[0] = jnp.concatenate([kvt[:2 * KV_WIDTH], ksel, kvt[3 * KV_WIDTH:4 * KV_WIDTH]], axis=0)
    wint_ref[0] = jnp.concatenate([kwin, kvt[5 * KV_WIDTH:]], axis=0)
    vselt_ref[...] = kvt[3 * KV_WIDTH:4 * KV_WIDTH].astype(BF16)
    vwint_ref[...] = kvt[5 * KV_WIDTH:].astype(BF16)
    raw = kvt[:2 * KV_WIDTH].T
    for k in range(2 * KV_WIDTH // LANES):
        craw_ref[k] = raw[:, k * LANES:(k + 1) * LANES]
    ksel_n = ksel.T
    kwin_n = kwin.T
    for g in range(N_KV_HEADS):
        ksel_ref[g] = ksel_n[:, g * HEAD_DIM:(g + 1) * HEAD_DIM].astype(BF16)
        kwin_ref[g] = kwin_n[:, g * HEAD_DIM:(g + 1) * HEAD_DIM].astype(BF16)


def _kvproj(x3, gain, w_kv, k_norm, cos_tt, sin_tt, *, tt):
    b, t, d = x3.shape
    n = b * t
    nt = t // tt
    wt = w_kv.T
    gsel = jnp.broadcast_to(k_norm[1][:, None], (HEAD_DIM, LANES))
    gwin = jnp.broadcast_to(k_norm[2][:, None], (HEAD_DIM, LANES))
    full = lambda a: pl.BlockSpec(a.shape, lambda i, j: (0,) * a.ndim)
    return pl.pallas_call(
        _kvproj_kernel,
        grid=(b, nt),
        in_specs=[
            pl.BlockSpec((1, tt, d), lambda i, j: (i, j, 0)),
            pl.BlockSpec((1, d), lambda i, j: (0, 0)),
            full(wt), full(gsel), full(gwin),
            pl.BlockSpec((HALF_DIM, tt), lambda i, j: (0, j)),
            pl.BlockSpec((HALF_DIM, tt), lambda i, j: (0, j)),
        ],
        out_specs=[
            pl.BlockSpec((1, 4 * KV_WIDTH, tt), lambda i, j: (i, 0, j)),
            pl.BlockSpec((1, 2 * KV_WIDTH, tt), lambda i, j: (i, 0, j)),
            pl.BlockSpec((2 * KV_WIDTH // LANES, tt, LANES), lambda i, j: (0, i * nt + j, 0)),
            pl.BlockSpec((N_KV_HEADS, tt, HEAD_DIM), lambda i, j: (0, i * nt + j, 0)),
            pl.BlockSpec((N_KV_HEADS, tt, HEAD_DIM), lambda i, j: (0, i * nt + j, 0)),
            pl.BlockSpec((KV_WIDTH, tt), lambda i, j: (0, i * nt + j)),
            pl.BlockSpec((KV_WIDTH, tt), lambda i, j: (0, i * nt + j)),
        ],
        out_shape=[
            jax.ShapeDtypeStruct((b, 4 * KV_WIDTH, t), F32),
            jax.ShapeDtypeStruct((b, 2 * KV_WIDTH, t), F32),
            jax.ShapeDtypeStruct((2 * KV_WIDTH // LANES, n, LANES), F32),
            jax.ShapeDtypeStruct((N_KV_HEADS, n, HEAD_DIM), BF16),
            jax.ShapeDtypeStruct((N_KV_HEADS, n, HEAD_DIM), BF16),
            jax.ShapeDtypeStruct((KV_WIDTH, n), BF16),
            jax.ShapeDtypeStruct((KV_WIDTH, n), BF16),
        ],
        scratch_shapes=[pltpu.VMEM(wt.shape, BF16)],
        compiler_params=_params(("arbitrary", "arbitrary"), VMEM_LIMIT),
        name="kv_proj",
    )(x3, gain.reshape(1, d), wt, gsel, gwin, cos_tt, sin_tt)


def _rope_tables_transposed(pos):
    inv = 1.0 / (ROPE_THETA ** (jnp.arange(HALF_DIM, dtype=F32) * (2.0 / HEAD_DIM)))
    ang = pos.astype(F32)[:, None] * inv[None, :]
    return jnp.cos(ang).T, jnp.sin(ang).T


def _cmp_ab_accumulate(load_rows, wab_ref, n_rows):
    del n_rows
    heads_per_chunk = LANES // HEAD_DIM
    accs = [[None] * N_KV_HEADS for _ in range(2)]
    for s in range(2):
        for c in range(KV_WIDTH // LANES):
            plane = s * (KV_WIDTH // LANES) + c
            lhs = jnp.concatenate([load_rows(r, plane).astype(BF16) for r in range(CMP_STRIDE)], axis=1)
            out = _dot(lhs, wab_ref[s])
            for k in range(heads_per_chunk):
                accs[s][c * heads_per_chunk + k] = out[:, k * 2 * CMP_HIDDEN:(k + 1) * 2 * CMP_HIDDEN]
    return accs


def _cmpab_kernel(craw_ref, wab_ref, ab_ref, *, n_chunk):
    accs = _cmp_ab_accumulate(lambda r, c: craw_ref[c, pl.ds(r, n_chunk, stride=CMP_STRIDE), :], wab_ref, n_chunk)
    for s in range(2):
        for g in range(N_KV_HEADS):
            ab_ref[0, s, g] = accs[s][g]


def _cmpab_pages_kernel(pt_ref, *refs, n_pages):
    del pt_ref
    page_refs = refs[:n_pages]
    wab_ref, ab_ref, craw_s = refs[n_pages:]
    pairs = KV_WIDTH // LANES
    for j in range(n_pages):
        for s in range(2):
            for k in range(pairs):
                tile = jnp.concatenate([page_refs[j][0, s, 2 * k], page_refs[j][0, s, 2 * k + 1]], axis=0)
                craw_s[s * pairs + k, j * PAGE_SIZE:(j + 1) * PAGE_SIZE, :] = tile.T
    n_chunk = n_pages * PAGE_SIZE // CMP_STRIDE
    accs = _cmp_ab_accumulate(lambda r, c: craw_s[c, pl.ds(r, n_chunk, stride=CMP_STRIDE), :], wab_ref, n_chunk)
    for s in range(2):
        for g in range(N_KV_HEADS):
            ab_ref[0, s, g] = accs[s][g]


def _cmp_ab_pages(cache5, page_table, wab, *, n_pages):
    b, pages_per_seq = page_table.shape
    steps = pages_per_seq // n_pages
    rows = n_pages * PAGE_SIZE // CMP_STRIDE

    def page_spec(j):
        return pl.BlockSpec((1, 2, N_KV_HEADS, HEAD_DIM, PAGE_SIZE),
                            lambda i, q, pt: (pt[i, q * n_pages + j], 0, 0, 0, 0))

    return pl.pallas_call(
        functools.partial(_cmpab_pages_kernel, n_pages=n_pages),
        grid_spec=pltpu.PrefetchScalarGridSpec(
            num_scalar_prefetch=1,
            grid=(b, steps),
            in_specs=[page_spec(j) for j in range(n_pages)]
            + [pl.BlockSpec(wab.shape, lambda i, q, pt: (0, 0, 0))],
            out_specs=pl.BlockSpec((1, 2, N_KV_HEADS, rows, 2 * CMP_HIDDEN), lambda i, q, pt: (i, 0, 0, q, 0)),
            scratch_shapes=[pltpu.VMEM((2 * KV_WIDTH // LANES, n_pages * PAGE_SIZE, LANES), F32)],
        ),
        out_shape=jax.ShapeDtypeStruct((b, 2, N_KV_HEADS, steps * rows, 2 * CMP_HIDDEN), F32),
        compiler_params=_params(("arbitrary", "arbitrary"), VMEM_LIMIT),
        name="cmp_ab_pages",
    )(page_table, *([cache5] * n_pages), wab)


def _cmp_weights(cmp_w1):
    w = cmp_w1.reshape(2, 2, CMP_STRIDE, HEAD_DIM, CMP_HIDDEN)
    w = w.transpose(0, 2, 3, 1, 4).reshape(2, CMP_STRIDE, HEAD_DIM, 2 * CMP_HIDDEN)
    eye = jnp.eye(LANES // HEAD_DIM, dtype=w.dtype)
    w = w[:, :, None, :, None, :] * eye[None, None, :, None, :, None]
    return w.reshape(2, CMP_STRIDE * LANES, (LANES // HEAD_DIM) * 2 * CMP_HIDDEN).astype(BF16)


def _cmp_ab_prompt(craw, wab, *, b, t):
    n_chunk = t // CMP_STRIDE
    return pl.pallas_call(
        functools.partial(_cmpab_kernel, n_chunk=n_chunk),
        grid=(b,),
        in_specs=[
            pl.BlockSpec((craw.shape[0], t, LANES), lambda i: (0, i, 0)),
            pl.BlockSpec(wab.shape, lambda i: (0, 0, 0)),
        ],
        out_specs=pl.BlockSpec((1, 2, N_KV_HEADS, n_chunk, 2 * CMP_HIDDEN), lambda i: (i, 0, 0, 0, 0)),
        out_shape=jax.ShapeDtypeStruct((b, 2, N_KV_HEADS, n_chunk, 2 * CMP_HIDDEN), F32),
        compiler_params=_params(("arbitrary",)),
        name="cmp_ab_prompt",
    )(craw, wab)


def _cmpfin_kernel(ab_ref, pe_ref, w1_ref, b1_ref, w2_ref, gk_ref, ck_ref, cv_ref, *, n_row):
    for s in range(2):
        bias = _dot(pe_ref[s].astype(BF16), w1_ref[s].astype(BF16)) + b1_ref[s]
        w2 = w2_ref[s].astype(BF16)
        for g in range(N_KV_HEADS):
            ab = ab_ref[0, s, g]
            hid = ab[:, :CMP_HIDDEN] + pltpu.roll(ab[:, CMP_HIDDEN:], n_row - 1, 0) + bias
            cdf = 0.5 * (1.0 + jnp.tanh(0.7978845608028654 * (hid + 0.044715 * (hid * hid * hid))))
            out = _dot((hid * cdf).astype(BF16), w2)
            if s == 0:
                ck_ref[0, g] = _rms(out, gk_ref[...]).astype(BF16)
            else:
                cv_ref[0, g] = out.astype(BF16)


def _cmp_finish(ab, cmp_pe, cmp_w1, cmp_b1, cmp_w2, gk):
    b = ab.shape[0]
    n_row = ab.shape[3]
    pe = cmp_pe.reshape(2, 1, CMP_BLOCK * HEAD_DIM)
    full = lambda a: pl.BlockSpec(a.shape, lambda i: (0,) * a.ndim)
    b1 = cmp_b1.reshape(2, 1, CMP_HIDDEN)
    gk2 = gk.reshape(1, HEAD_DIM)
    return pl.pallas_call(
        functools.partial(_cmpfin_kernel, n_row=n_row),
        grid=(b,),
        in_specs=[pl.BlockSpec((1,) + ab.shape[1:], lambda i: (i, 0, 0, 0, 0)),
                  full(pe), full(cmp_w1), full(b1), full(cmp_w2), full(gk2)],
        out_specs=[pl.BlockSpec((1, N_KV_HEADS, n_row, HEAD_DIM), lambda i: (i, 0, 0, 0)),
                   pl.BlockSpec((1, N_KV_HEADS, n_row, HEAD_DIM), lambda i: (i, 0, 0, 0))],
        out_shape=[jax.ShapeDtypeStruct((b, N_KV_HEADS, n_row, HEAD_DIM), BF16),
                   jax.ShapeDtypeStruct((b, N_KV_HEADS, n_row, HEAD_DIM), BF16)],
        compiler_params=_params(("arbitrary",)),
        name="cmp_finish",
    )(ab, pe, cmp_w1, b1, cmp_w2, gk2)


def _qproj_kernel(h_ref, g_ref, wt_ref, gq_ref, cos_ref, sin_ref, qn_ref, qr_ref, gt_ref, w_s):
    @pl.when(pl.program_id(0) == 0)
    def _():
        w_s[...] = wt_ref[...].astype(BF16)

    xn = _rms(h_ref[...], g_ref[...]).astype(BF16)
    qg = _nt_dot(w_s[...], xn)
    tt = xn.shape[0]
    gq = jnp.concatenate([gq_ref[...]] * (tt // LANES), axis=1)
    c = cos_ref[...]
    s = sin_ref[...]
    for h in range(N_HEADS):
        qh = qg[h * HEAD_DIM:(h + 1) * HEAD_DIM]
        qn = qh * lax.rsqrt(jnp.mean(qh * qh, axis=0, keepdims=True) + EPS) * gq
        qn_ref[h * HEAD_DIM:(h + 1) * HEAD_DIM, :] = (qn * SCALE).astype(BF16)
        x1 = qn[:HALF_DIM]
        x2 = qn[HALF_DIM:]
        qr_ref[h * HEAD_DIM:h * HEAD_DIM + HALF_DIM, :] = ((x1 * c - x2 * s) * SCALE).astype(BF16)
        qr_ref[h * HEAD_DIM + HALF_DIM:(h + 1) * HEAD_DIM, :] = ((x2 * c + x1 * s) * SCALE).astype(BF16)
    gt_ref[...] = _sigmoid(qg[N_HEADS * HEAD_DIM:])


def _qg_weights(w_qg):
    nq = N_HEADS * HEAD_DIM
    gates = w_qg[:, nq:].reshape(D_MODEL, 3, N_KV_HEADS, HEADS_PER_KV).transpose(2, 1, 3, 0)
    gates = gates.reshape(N_KV_HEADS, 3 * HEADS_PER_KV, D_MODEL)
    gates = jnp.pad(gates, ((0, 0), (0, GATE_ROWS - 3 * HEADS_PER_KV), (0, 0)))
    return jnp.concatenate([w_qg[:, :nq].T, gates.reshape(N_KV_HEADS * GATE_ROWS, D_MODEL)], axis=0)


def _qproj(h2, gain, wt, q_norm, cos_tt, sin_tt, *, tt, pos_blocks):
    n, d = h2.shape
    nq = N_HEADS * HEAD_DIM
    ng = N_KV_HEADS * GATE_ROWS
    gq = jnp.broadcast_to(q_norm[:, None], (HEAD_DIM, LANES))
    full = lambda a: pl.BlockSpec(a.shape, lambda i: (0,) * a.ndim)
    return pl.pallas_call(
        _qproj_kernel,
        grid=(n // tt,),
        in_specs=[
            pl.BlockSpec((tt, d), lambda i: (i, 0)),
            pl.BlockSpec((1, d), lambda i: (0, 0)),
            full(wt), full(gq),
            pl.BlockSpec((HALF_DIM, tt), lambda i: (0, i % pos_blocks)),
            pl.BlockSpec((HALF_DIM, tt), lambda i: (0, i % pos_blocks)),
        ],
        out_specs=[
            pl.BlockSpec((nq, tt), lambda i: (0, i)),
            pl.BlockSpec((nq, tt), lambda i: (0, i)),
            pl.BlockSpec((ng, tt), lambda i: (0, i)),
        ],
        out_shape=[jax.ShapeDtypeStruct((nq, n), BF16), jax.ShapeDtypeStruct((nq, n), BF16),
                   jax.ShapeDtypeStruct((ng, n), F32)],
        scratch_shapes=[pltpu.VMEM(wt.shape, BF16)],
        compiler_params=_params(("arbitrary",), VMEM_LIMIT),
        name="q_proj",
    )(h2, gain.reshape(1, d), wt, gq, cos_tt, sin_tt)


def _attn_kernel(qn_ref, qr_ref, gt_ref, ck_ref, cvt_ref, kaug_ref, kwin_ref, vaug_ref, vwaug_ref,
                 o_ref, pg_s, o_sel_s, *, tq, n_cmp, n_sb, n_qt):
    hp_n = HEADS_PER_KV
    qt = pl.program_id(2)
    t0 = qt * tq
    tpos = t0 + lax.broadcasted_iota(I32, (1, tq), 1)
    tpos4 = jnp.concatenate([tpos] * hp_n, axis=1)
    qn4 = jnp.concatenate([qn_ref[h * HEAD_DIM:(h + 1) * HEAD_DIM, :] for h in range(hp_n)], axis=1)
    qr4 = jnp.concatenate([qr_ref[h * HEAD_DIM:(h + 1) * HEAD_DIM, :] for h in range(hp_n)], axis=1)

    n_row = ck_ref.shape[2]
    s = _dot(ck_ref[0, 0], qn4)
    ci = lax.broadcasted_iota(I32, (n_row, 1), 0)
    vis = (ci * CMP_STRIDE + (CMP_BLOCK - 1) <= tpos4) & (ci < n_cmp)
    s = jnp.where(vis, s, NEG)
    p = jnp.where(vis, jnp.exp(s - jnp.max(s, axis=0, keepdims=True)), 0.0)
    p = p / jnp.maximum(jnp.sum(p, axis=0, keepdims=True), TINY)
    o_cmp = _dot(cvt_ref[0, 0], p.astype(BF16))
    pg = p[:, 0:tq]
    for h in range(1, hp_n):
        pg = pg + p[:, h * tq:(h + 1) * tq]

    ratio = SEL_BLOCK // CMP_STRIDE
    scores = []
    for c in range(tq // LANES):
        pg_s[c, 0:8, :] = jnp.zeros((8, LANES), F32)
        pg_s[c, 8:8 + n_row, :] = pg[:, c * LANES:(c + 1) * LANES]
        sc_c = pg_s[c, pl.ds(8 + 1 - CMP_BLOCK // CMP_STRIDE, n_sb, stride=ratio), :]
        for o in range(2 - CMP_BLOCK // CMP_STRIDE, ratio):
            sc_c = sc_c + pg_s[c, pl.ds(8 + o, n_sb, stride=ratio), :]
        scores.append(sc_c)
    score = jnp.concatenate(scores, axis=1)
    jrow = lax.broadcasted_iota(I32, (n_sb, 1), 0)
    cur = tpos // SEL_BLOCK
    forced = (jrow == 0) | (jrow == cur) | (jrow == cur - 1)
    sc = jnp.where(jrow * SEL_BLOCK <= tpos, score + jnp.where(forced, BIG, 0.0), -BIG)
    rank = jnp.zeros((n_sb, tq), I32)
    for jp in range(n_sb):
        row = sc[jp:jp + 1, :]
        beats = (row > sc) | ((row == sc) & (jrow > jp))
        rank = rank + beats.astype(I32)
    n_sbp = kaug_ref.shape[3] - HEAD_DIM
    sel_neg = jnp.where(rank < min(N_SEL, n_sb), 0.0, NEG)
    if n_sbp > n_sb:
        sel_neg = jnp.concatenate([sel_neg, jnp.zeros((n_sbp - n_sb, tq), F32)], axis=0)
    sel_neg = sel_neg.astype(BF16)
    q_rot = [qr_ref[h * HEAD_DIM:(h + 1) * HEAD_DIM, :] for h in range(hp_n)]
    q_aug = jnp.concatenate([jnp.concatenate([q, sel_neg], axis=0) for q in q_rot], axis=1)
    q_win = jnp.concatenate(q_rot, axis=1)
    causal = jnp.where(lax.broadcasted_iota(I32, (tq, 1), 0) <= lax.broadcasted_iota(I32, (1, tq), 1), 0.0, NEG)
    causal = jnp.concatenate([causal] * hp_n, axis=1)

    for k in range(n_qt):
        @pl.when(qt == k)
        def _(k=k):
            lo = k * tq
            s_diag = _dot(kaug_ref[0, 0, lo:lo + tq, :], q_aug) + causal
            m = jnp.max(s_diag, axis=0, keepdims=True)
            if k > 0:
                s_top = _dot(kaug_ref[0, 0, 0:lo, :], q_aug)
                m = jnp.maximum(m, jnp.max(s_top, axis=0, keepdims=True))
            acc = _dot(vaug_ref[:, lo:lo + tq], jnp.exp(s_diag - m).astype(BF16))
            if k > 0:
                acc = acc + _dot(vaug_ref[:, 0:lo], jnp.exp(s_top - m).astype(BF16))
            o_sel_s[...] = acc[:HEAD_DIM] / jnp.maximum(acc[HEAD_DIM:HEAD_DIM + 1], TINY)

    o_sel = o_sel_s[...]

    n_wt = WINDOW // tq + 1
    w0 = pl.multiple_of(jnp.clip(qt - (n_wt - 1), 0, pl.num_programs(2) - n_wt) * tq, tq)
    dq = tpos - (w0 + lax.broadcasted_iota(I32, (n_wt * tq, 1), 0))
    wbias = jnp.where((dq >= 0) & (dq < WINDOW), 0.0, NEG)
    kw_tile = kwin_ref[0, 0, pl.ds(w0, n_wt * tq), :]
    vw_tile = vwaug_ref[:, pl.ds(w0, n_wt * tq)]
    sw = _dot(kw_tile, q_win) + jnp.concatenate([wbias] * hp_n, axis=1)
    pw = jnp.exp(sw - jnp.max(sw, axis=0, keepdims=True)).astype(BF16)
    acc_win = _dot(vw_tile, pw)
    o_win = acc_win[:HEAD_DIM] / jnp.maximum(acc_win[HEAD_DIM:HEAD_DIM + 1], TINY)
    gt = gt_ref[...]
    for h in range(hp_n):
        sl = slice(h * tq, (h + 1) * tq)
        o = (gt[h:h + 1] * o_cmp[:, sl] + gt[hp_n + h:hp_n + h + 1] * o_sel[:, sl]
             + gt[2 * hp_n + h:2 * hp_n + h + 1] * o_win[:, sl])
        o_ref[h * HEAD_DIM:(h + 1) * HEAD_DIM, :] = o.astype(BF16)


def _attn_prompt(qn_t, qr_t, g_t, ck, cv_t, ksel, kwin, vsel_t, vwin_t, *, b, t, tq):
    nq = t // tq
    n_sb = t // SEL_BLOCK
    n_row = ck.shape[2]
    n_cmp = t // CMP_STRIDE - CMP_BLOCK // CMP_STRIDE + 1
    n_sbp = -(-n_sb // 32) * 32
    kw = N_KV_HEADS * HEAD_DIM
    onehot = (jnp.arange(t)[:, None] // SEL_BLOCK == jnp.arange(n_sbp)[None, :]).astype(BF16)
    kaug = jnp.concatenate([ksel.reshape(N_KV_HEADS, b, t, HEAD_DIM),
                            jnp.broadcast_to(onehot, (N_KV_HEADS, b, t, n_sbp))], axis=-1)
    ones_rows = jnp.zeros((N_KV_HEADS, V_PAD_ROWS, b * t), BF16).at[:, 0].set(1.0)

    def with_ones(v_t):
        v3 = jnp.concatenate([v_t.reshape(N_KV_HEADS, HEAD_DIM, b * t), ones_rows], axis=1)
        return v3.reshape(N_KV_HEADS * (HEAD_DIM + V_PAD_ROWS), b * t)

    v_rows = HEAD_DIM + V_PAD_ROWS
    return pl.pallas_call(
        functools.partial(_attn_kernel, tq=tq, n_cmp=n_cmp, n_sb=n_sb, n_qt=nq),
        grid=(b, N_KV_HEADS, nq),
        in_specs=[
            pl.BlockSpec((kw, tq), lambda i, g, q: (g, i * nq + q)),
            pl.BlockSpec((kw, tq), lambda i, g, q: (g, i * nq + q)),
            pl.BlockSpec((GATE_ROWS, tq), lambda i, g, q: (g, i * nq + q)),
            pl.BlockSpec((1, 1, n_row, HEAD_DIM), lambda i, g, q: (i, g, 0, 0)),
            pl.BlockSpec((1, 1, HEAD_DIM, n_row), lambda i, g, q: (i, g, 0, 0)),
            pl.BlockSpec((1, 1, t, HEAD_DIM + n_sbp), lambda i, g, q: (g, i, 0, 0)),
            pl.BlockSpec((1, 1, t, HEAD_DIM), lambda i, g, q: (g, i, 0, 0)),
            pl.BlockSpec((v_rows, t), lambda i, g, q: (g, i)),
            pl.BlockSpec((v_rows, t), lambda i, g, q: (g, i)),
        ],
        out_specs=pl.BlockSpec((kw, tq), lambda i, g, q: (g, i * nq + q)),
        out_shape=jax.ShapeDtypeStruct((N_HEADS * HEAD_DIM, b * t), BF16),
        scratch_shapes=[pltpu.VMEM((tq // LANES, n_row + 8, LANES), F32),
                        pltpu.VMEM((HEAD_DIM, HEADS_PER_KV * tq), F32)],
        compiler_params=_params(("arbitrary", "arbitrary", "arbitrary"), VMEM_LIMIT),
        name="attn_prompt",
    )(qn_t, qr_t, g_t, ck, cv_t, kaug, kwin.reshape(N_KV_HEADS, b, t, HEAD_DIM), with_ones(vsel_t), with_ones(vwin_t))


def _oproj_t_kernel(ot_ref, h_ref, w_ref, out_ref, w_s):
    @pl.when(pl.program_id(0) == 0)
    def _():
        w_s[...] = w_ref[...].astype(BF16)

    o = ot_ref[...].astype(F32).T.astype(BF16)
    out_ref[...] = h_ref[...] + _dot(o, w_s[...])


def _oproj_t(o_t, h2, w_o, *, tt):
    n, d = h2.shape
    return pl.pallas_call(
        _oproj_t_kernel,
        grid=(n // tt,),
        in_specs=[
            pl.BlockSpec((o_t.shape[0], tt), lambda i: (0, i)),
            pl.BlockSpec((tt, d), lambda i: (i, 0)),
            pl.BlockSpec(w_o.shape, lambda i: (0, 0)),
        ],
        out_specs=pl.BlockSpec((tt, d), lambda i: (i, 0)),
        out_shape=jax.ShapeDtypeStruct((n, d), F32),
        scratch_shapes=[pltpu.VMEM(w_o.shape, BF16)],
        compiler_params=_params(("arbitrary",), VMEM_LIMIT),
        name="o_proj",
    )(o_t, h2, w_o)


def _oproj_n_kernel(o_ref, h_ref, w_ref, out_ref):
    out_ref[...] = h_ref[...] + _dot(o_ref[...].astype(BF16), w_ref[...].astype(BF16))


def _oproj_n(o, h2, w_o):
    n, d = h2.shape
    return pl.pallas_call(
        _oproj_n_kernel,
        out_shape=jax.ShapeDtypeStruct((n, d), F32),
        compiler_params=_params((), VMEM_LIMIT),
        name="o_proj_sample",
    )(o, h2, w_o)


def _attn_sample_kernel(pt_ref, *refs, n_pages, n_cmp, n_sb, ts, past, n_buf):
    del pt_ref
    page_refs = refs[:n_pages]
    (qn_ref, qr_ref, ckt_ref, cvt_ref, win_ref, knew_ref, vnew_ref, kwnew_ref, vwnew_ref, gate_ref,
     sel_ref, e_ref, o_ref, mask_s, m_s, l_s, acc_s, ocmp_s) = refs[n_pages:]
    q_step = pl.program_id(1)
    rows = HEADS_PER_KV * N_KV_HEADS * ts
    grp_rows = N_KV_HEADS * ts
    row = lax.broadcasted_iota(I32, (rows, 1), 0)
    qpos = past + row % ts
    qr = qr_ref[0]

    def tile_rows(x):
        return jnp.concatenate([x] * HEADS_PER_KV, axis=0)

    def online_update(s, mk, v_dot):
        s = jnp.where(mk, s, NEG)
        m_new = jnp.maximum(m_s[...], jnp.max(s, axis=-1, keepdims=True))
        alpha = jnp.exp(m_s[...] - m_new)
        p = jnp.where(mk, jnp.exp(s - m_new), 0.0)
        l_s[...] = l_s[...] * alpha + jnp.sum(p, axis=-1, keepdims=True)
        acc_s[...] = acc_s[...] * alpha + v_dot(p.astype(BF16))
        m_s[...] = m_new

    @pl.when(q_step == 0)
    def _():
        n_row = ckt_ref.shape[2]
        s = _dot(qn_ref[0], ckt_ref[0])
        ci = lax.broadcasted_iota(I32, (1, n_row), 1)
        vis = (ci * CMP_STRIDE + (CMP_BLOCK - 1) <= qpos) & (ci < n_cmp)
        s = jnp.where(vis, s, NEG)
        p = jnp.where(vis, jnp.exp(s - jnp.max(s, axis=-1, keepdims=True)), 0.0)
        p = p / jnp.maximum(jnp.sum(p, axis=-1, keepdims=True), TINY)
        ocmp_s[...] = _nt_dot(p.astype(BF16), cvt_ref[0])
        pg = p[0:grp_rows]
        for h in range(1, HEADS_PER_KV):
            pg = pg + p[h * grp_rows:(h + 1) * grp_rows]
        score = _dot(pg, sel_ref[...], precision=HIGHEST)
        width = score.shape[1]
        j = lax.broadcasted_iota(I32, (1, width), 1)
        tq = qpos[0:grp_rows]
        cur = tq // SEL_BLOCK
        forced = (j == 0) | (j == cur) | (j == cur - 1)
        sc = jnp.where(j * SEL_BLOCK <= tq, score + jnp.where(forced, BIG, 0.0), -BIG)
        sc = jnp.where(j < n_sb, sc, -2.0 * BIG)
        rank = jnp.zeros((grp_rows, width), I32)
        for jp in range(n_sb):
            col = sc[:, jp:jp + 1]
            beats = (col > sc) | ((col == sc) & (j > jp))
            rank = rank + beats.astype(I32)
        sel = jnp.where((rank < min(N_SEL, n_sb)) & (j < n_sb), 1.0, 0.0).astype(BF16)
        mask_s[...] = _dot(sel, e_ref[...])
        m_s[...] = jnp.full(m_s.shape, NEG, F32)
        l_s[...] = jnp.zeros(l_s.shape, F32)
        acc_s[...] = jnp.zeros(acc_s.shape, F32)

    width = n_pages * PAGE_SIZE
    kt = jnp.concatenate([page_refs[i][0, 0].reshape(KV_WIDTH, PAGE_SIZE) for i in range(n_pages)], axis=1)
    vt = jnp.concatenate([page_refs[i][0, 1].reshape(KV_WIDTH, PAGE_SIZE) for i in range(n_pages)], axis=1)
    k0 = pl.multiple_of(q_step * width, width)
    mk = tile_rows(mask_s[:, pl.ds(k0, width)]) > 0.5
    online_update(_dot(qr, kt.astype(BF16)), mk, lambda p: _nt_dot(p, vt.astype(BF16)))

    @pl.when(q_step == pl.num_programs(1) - 1)
    def _():
        lane = lax.broadcasted_iota(I32, (1, LANES), 1)
        new_ok = (lane < ts) & (past + lane <= qpos)
        mk_new = (tile_rows(mask_s[:, past:past + LANES]) > 0.5) & new_ok
        online_update(_nt_dot(qr, knew_ref[0]), mk_new, lambda p: _dot(p, vnew_ref[0]))
        o_sel = acc_s[...] / jnp.maximum(l_s[...], TINY)
        bi = lax.broadcasted_iota(I32, (1, n_buf), 1)
        dq = qpos - (past - n_buf + bi)
        ok_buf = (dq >= 0) & (dq < WINDOW)
        dq_new = qpos - (past + lane)
        ok_new = (lane < ts) & (dq_new >= 0) & (dq_new < WINDOW)
        s_w = jnp.concatenate([_dot(qr, win_ref[0, 0].astype(BF16)), _nt_dot(qr, kwnew_ref[0])], axis=1)
        ok = jnp.concatenate([jnp.broadcast_to(ok_buf, (rows, n_buf)), jnp.broadcast_to(ok_new, (rows, LANES))], axis=1)
        s_w = jnp.where(ok, s_w, NEG)
        p_w = jnp.where(ok, jnp.exp(s_w - jnp.max(s_w, axis=-1, keepdims=True)), 0.0)
        p_w = (p_w / jnp.maximum(jnp.sum(p_w, axis=-1, keepdims=True), TINY)).astype(BF16)
        o_win = _nt_dot(p_w[:, :n_buf], win_ref[0, 1].astype(BF16)) + _dot(p_w[:, n_buf:], vwnew_ref[0])
        row_g = (row // ts) % N_KV_HEADS

        def own_group(o):
            out = jnp.zeros((rows, HEAD_DIM), F32)
            for g in range(N_KV_HEADS):
                out = out + jnp.where(row_g == g, o[:, g * HEAD_DIM:(g + 1) * HEAD_DIM], 0.0)
            return out

        o_ref[0] = (gate_ref[0, 0] * own_group(ocmp_s[...]) + gate_ref[0, 1] * own_group(o_sel)
                    + gate_ref[0, 2] * own_group(o_win))


def _attn_sample(cache5, page_table, qn_bd, qr_bd, ck_t, cv_t, win4, knew, vnew, kwnew, vwnew, gates,
                 *, n_pages, ts, past):
    b, pages_per_seq = page_table.shape
    steps = pages_per_seq // n_pages
    rows = qn_bd.shape[1]
    n_chunk = ck_t.shape[2]
    n_cmp = n_chunk - CMP_BLOCK // CMP_STRIDE + 1
    n_keys = past + LANES
    n_sb = -(-(past + ts) // SEL_BLOCK)
    n_sb_pad = -(-n_sb // LANES) * LANES
    n_buf = win4.shape[3]
    ratio = SEL_BLOCK // CMP_STRIDE
    ci = jnp.arange(n_chunk)[:, None]
    jb = jnp.arange(n_sb_pad)[None, :]
    sel_map = ((ci >= ratio * jb + 1 - CMP_BLOCK // CMP_STRIDE) & (ci < ratio * jb + ratio)
               & (ci < n_cmp) & (jb < n_sb)).astype(F32)
    expand = (jnp.arange(n_keys)[None, :] // SEL_BLOCK == jnp.arange(n_sb_pad)[:, None]).astype(BF16)

    def page_spec(j):
        return pl.BlockSpec((1, 2, N_KV_HEADS, HEAD_DIM, PAGE_SIZE),
                            lambda i, q, pt: (pt[i, q * n_pages + j], 1, 0, 0, 0))

    per_seq = lambda a: pl.BlockSpec((1,) + a.shape[1:], lambda i, q, pt: (i,) + (0,) * (a.ndim - 1))
    full = lambda a: pl.BlockSpec(a.shape, lambda i, q, pt: (0,) * a.ndim)
    return pl.pallas_call(
        functools.partial(_attn_sample_kernel, n_pages=n_pages, n_cmp=n_cmp, n_sb=n_sb, ts=ts, past=past,
                          n_buf=n_buf),
        grid_spec=pltpu.PrefetchScalarGridSpec(
            num_scalar_prefetch=1,
            grid=(b, steps),
            in_specs=[page_spec(j) for j in range(n_pages)]
            + [per_seq(a) for a in (qn_bd, qr_bd, ck_t, cv_t, win4, knew, vnew, kwnew, vwnew, gates)]
            + [full(sel_map), full(expand)],
            out_specs=pl.BlockSpec((1, rows, HEAD_DIM), lambda i, q, pt: (i, 0, 0)),
            scratch_shapes=[
                pltpu.VMEM((N_KV_HEADS * ts, n_keys), F32),
                pltpu.VMEM((rows, 1), F32), pltpu.VMEM((rows, 1), F32),
                pltpu.VMEM((rows, KV_WIDTH), F32), pltpu.VMEM((rows, KV_WIDTH), F32),
            ],
        ),
        out_shape=jax.ShapeDtypeStruct((b, rows, HEAD_DIM), F32),
        compiler_params=_params(("arbitrary", "arbitrary"), VMEM_LIMIT),
        name="attn_sample",
    )(page_table, *([cache5] * n_pages), qn_bd, qr_bd, ck_t, cv_t, win4, knew, vnew, kwnew, vwnew, gates,
      sel_map, expand)


def _router_weights(router_group, router_expert):
    rt = jnp.concatenate([router_group, router_expert], axis=1).T
    return jnp.pad(rt, ((0, 24 - rt.shape[0]), (0, 0)))


def _trunk_prompt(x, p, *, tt_pool=256, tm=1024, tt=512, tq=256):
    (norm_mix, norm_ffn, pool_w, pool_scale, kv_norm, w_kv, k_norm, cmp_pe, cmp_w1, cmp_b1, cmp_w2,
     w_qg, q_norm, w_o, router_group, router_expert, w_gate_up, w_down) = p
    b, t, d = x.shape
    n = b * t
    pos = jnp.arange(t)
    h, new_pool = _pool_layer(x, jnp.zeros((b, POOL_HALO, d), F32), norm_mix[0], pool_w[0], pool_scale[0],
                              tt=tt_pool, clip=True)
    h = _moe(h.reshape(n, d), norm_ffn[0], _router_weights(router_group[0], router_expert[0]),
             w_gate_up, w_down, 0, tm=tm)
    cos_t, sin_t = _rope_tables_transposed(pos)
    kv_t, win_t, craw, ksel, kwin, vsel_t, vwin_t = _kvproj(h.reshape(b, t, d), kv_norm, w_kv, k_norm,
                                                            cos_t, sin_t, tt=tt)
    ab = _cmp_ab_prompt(craw, _cmp_weights(cmp_w1), b=b, t=t)
    ck, cv = _cmp_finish(ab, cmp_pe, cmp_w1, cmp_b1, cmp_w2, k_norm[0])
    qn_t, qr_t, g_t = _qproj(h, norm_mix[1], _qg_weights(w_qg[0]), q_norm[0], cos_t, sin_t,
                             tt=tt, pos_blocks=t // tt)
    o_t = _attn_prompt(qn_t, qr_t, g_t, ck, jnp.swapaxes(cv, 2, 3), ksel, kwin, vsel_t, vwin_t, b=b, t=t, tq=tq)
    h = _oproj_t(o_t, h, w_o[0], tt=tt)
    h = _moe(h, norm_ffn[1], _router_weights(router_group[1], router_expert[1]),
             w_gate_up, w_down, 1, tm=tm)
    n_win = min(WINDOW, t)
    kv_new = kv_t.reshape(b, 4, N_KV_HEADS, HEAD_DIM, t).transpose(0, 4, 1, 2, 3)
    win_new = win_t[:, :, t - n_win:].reshape(b, 2, N_KV_HEADS, HEAD_DIM, n_win).transpose(0, 4, 1, 2, 3)
    return h.reshape(b, t, d), new_pool[None], kv_new, win_new


def _trunk_sample(x, state_pool, cache_kv, page_table, state_win, p, *, n_pages=16):
    (norm_mix, norm_ffn, pool_w, pool_scale, kv_norm, w_kv, k_norm, cmp_pe, cmp_w1, cmp_b1, cmp_w2,
     w_qg, q_norm, w_o, router_group, router_expert, w_gate_up, w_down) = p
    b, ts, d = x.shape
    n = b * ts
    past = page_table.shape[1] * PAGE_SIZE
    n_buf = state_win.shape[1]
    prev16 = jnp.pad(state_pool[0], ((0, 0), (POOL_HALO - POOL_STATE, 0), (0, 0)))
    h, new_pool = _pool_layer(x, prev16, norm_mix[0], pool_w[0], pool_scale[0], tt=ts, clip=False)
    h = _moe(h.reshape(n, d), norm_ffn[0], _router_weights(router_group[0], router_expert[0]),
             w_gate_up, w_down, 0, tm=n)
    cos_t, sin_t = _rope_tables_transposed(past + jnp.arange(ts))
    cos_t = jnp.tile(cos_t, (1, b))
    sin_t = jnp.tile(sin_t, (1, b))
    kv_t, win_t, _, _, _, _, _ = _kvproj(h.reshape(1, n, d), kv_norm, w_kv, k_norm, cos_t, sin_t, tt=n)
    kv_rows = kv_t[0].T
    win_rows = win_t[0].T

    cache5 = cache_kv.transpose(0, 2, 3, 4, 1)
    ab = _cmp_ab_pages(cache5, page_table, _cmp_weights(cmp_w1), n_pages=n_pages)
    ck, cv = _cmp_finish(ab, cmp_pe, cmp_w1, cmp_b1, cmp_w2, k_norm[0])
    n_chunk = ck.shape[2]
    ck_t = ck.transpose(0, 1, 3, 2).reshape(b, KV_WIDTH, n_chunk)
    cv_t = cv.transpose(0, 1, 3, 2).reshape(b, KV_WIDTH, n_chunk)

    qn_t, qr_t, g_t = _qproj(h, norm_mix[1], _qg_weights(w_qg[0]), q_norm[0], cos_t, sin_t, tt=n, pos_blocks=1)

    def block_diag_queries(q_t):
        q5 = q_t.reshape(N_KV_HEADS, HEADS_PER_KV, HEAD_DIM, b, ts).transpose(3, 1, 0, 4, 2)
        eye = jnp.eye(N_KV_HEADS, dtype=q_t.dtype)
        qbd = q5[:, :, :, :, None, :] * eye[None, None, :, None, :, None]
        return qbd.reshape(b, HEADS_PER_KV * N_KV_HEADS * ts, KV_WIDTH)

    gates = g_t.reshape(N_KV_HEADS, GATE_ROWS, b, ts)[:, :3 * HEADS_PER_KV]
    gates = gates.reshape(N_KV_HEADS, 3, HEADS_PER_KV, b, ts).transpose(3, 1, 2, 0, 4)
    gates = jnp.broadcast_to(gates.reshape(b, 3, HEADS_PER_KV * N_KV_HEADS * ts, 1),
                             (b, 3, HEADS_PER_KV * N_KV_HEADS * ts, HEAD_DIM))

    def new_rows(rows2):
        return jnp.pad(rows2.reshape(b, ts, KV_WIDTH), ((0, 0), (0, LANES - ts), (0, 0))).astype(BF16)

    win4 = state_win.transpose(0, 2, 3, 4, 1).reshape(b, 2, KV_WIDTH, n_buf)
    o = _attn_sample(cache5, page_table, block_diag_queries(qn_t), block_diag_queries(qr_t), ck_t, cv_t, win4,
                     new_rows(kv_rows[:, 2 * KV_WIDTH:3 * KV_WIDTH]), new_rows(kv_rows[:, 3 * KV_WIDTH:]),
                     new_rows(win_rows[:, :KV_WIDTH]), new_rows(win_rows[:, KV_WIDTH:]), gates,
                     n_pages=n_pages, ts=ts, past=past)
    o = o.reshape(b, HEADS_PER_KV, N_KV_HEADS, ts, HEAD_DIM).transpose(0, 3, 2, 1, 4).reshape(n, N_HEADS * HEAD_DIM)
    h = _oproj_n(o, h, w_o[0])
    h = _moe(h, norm_ffn[1], _router_weights(router_group[1], router_expert[1]),
             w_gate_up, w_down, 1, tm=n)
    kv_new = kv_rows.reshape(b, ts, 4, N_KV_HEADS, HEAD_DIM)
    win_new = jnp.concatenate([state_win, win_rows.reshape(b, ts, 2, N_KV_HEADS, HEAD_DIM)], axis=1)[:, -n_buf:]
    return h.reshape(b, ts, d), new_pool[None], kv_new, win_new


def kernel(x_prompt, x_sample, state_pool, cache_kv, page_table, state_win, norm_mix, norm_ffn, pool_w, pool_scale, kv_norm, w_kv, k_norm, cmp_pe, cmp_w1, cmp_b1, cmp_w2, w_qg, q_norm, w_o, router_group, router_expert, w_gate_up, w_down):
    params = (norm_mix, norm_ffn, pool_w, pool_scale, kv_norm, w_kv, k_norm, cmp_pe, cmp_w1, cmp_b1,
              cmp_w2, w_qg, q_norm, w_o, router_group, router_expert, _cast_bf16(w_gate_up), _cast_bf16(w_down))
    y_p, pool_p, kv_p, win_p = _trunk_prompt(x_prompt, params)
    y_s, pool_s, kv_s, win_s = _trunk_sample(x_sample, state_pool, cache_kv, page_table, state_win, params)
    return y_p, y_s, pool_p, pool_s, kv_p, kv_s, win_p, win_s
```

```python
import functools

import jax
import jax.numpy as jnp
from jax import lax
from jax.experimental import pallas as pl
from jax.experimental.pallas import tpu as pltpu

F32 = jnp.float32
BF16 = jnp.bfloat16
I32 = jnp.int32
HIGHEST = lax.Precision.HIGHEST

D_MODEL = 1024
POOL_WINDOWS = (2, 4, 8, 16)
POOL_GROUP_DIM = D_MODEL // len(POOL_WINDOWS)
POOL_STATE = max(POOL_WINDOWS) - 1
POOL_HALO = 16
N_HEADS = 16
HEAD_DIM = 64
HALF_DIM = HEAD_DIM // 2
N_KV_HEADS = 4
HEADS_PER_KV = N_HEADS // N_KV_HEADS
KV_WIDTH = N_KV_HEADS * HEAD_DIM
CMP_BLOCK = 32
CMP_STRIDE = 16
CMP_HIDDEN = 2 * HEAD_DIM
SEL_BLOCK = 64
N_SEL = 16
WINDOW = 512
PAGE_SIZE = 128
ROPE_THETA = 10000.0
SCALE = HEAD_DIM ** -0.5
N_GROUPS = 4
EXPERTS_PER_GROUP = 4
N_EXPERTS = N_GROUPS * EXPERTS_PER_GROUP
D_EXPERT = 512
EPS = 1e-6
NEG = -1e30
TINY = 1e-30
BIG = 1e4

LANES = 128
GATE_ROWS = 16
ROUTE_GROUP_ROW = 3 * N_EXPERTS
MOE_SEG_ALIGN = 16
MOE_CHUNK = 144
MOE_STEP_EXPERTS = 2
V_PAD_ROWS = 16
VMEM_LIMIT = 56 * 1024 * 1024


def _params(sem, vmem=None):
    return pltpu.CompilerParams(dimension_semantics=sem, vmem_limit_bytes=vmem)


def _rms(x, g):
    return x * lax.rsqrt(jnp.mean(x * x, axis=-1, keepdims=True) + EPS) * g


def _nt_dot(a, b, precision=None):
    return lax.dot_general(a, b, (((1,), (1,)), ((), ())), precision=precision,
                           preferred_element_type=F32)


def _dot(a, b, precision=None):
    return jnp.dot(a, b, precision=precision, preferred_element_type=F32)


def _sigmoid(x):
    return 1.0 / (1.0 + jnp.exp(-x))


def _pool_kernel(h_ref, prev_ref, g_ref, w_ref, sc_ref, o_ref, np_ref, ext_ref, lvl_ref, *, tt, clip):
    t = pl.program_id(1)
    x = h_ref[0]
    xn = _rms(x, g_ref[...])

    @pl.when(t == 0)
    def _():
        ext_ref[0:POOL_HALO, :] = prev_ref[0]

    @pl.when(t > 0)
    def _():
        ext_ref[0:POOL_HALO, :] = ext_ref[tt:tt + POOL_HALO, :]

    ext_ref[POOL_HALO:POOL_HALO + tt, :] = xn
    if clip:
        tpos = t * tt + lax.broadcasted_iota(I32, (tt, 1), 0)
    outs = []
    for gi, w in enumerate(POOL_WINDOWS):
        lo = gi * POOL_GROUP_DIM
        hi = lo + POOL_GROUP_DIM
        n = tt + POOL_HALO
        lvl_ref[...] = ext_ref[:, lo:hi]
        span = 1
        while span < w:
            lvl_ref[span:n, :] = lvl_ref[span:n, :] + lvl_ref[0:n - span, :]
            span *= 2
        acc = lvl_ref[POOL_HALO:n, :]
        if clip:
            mean = acc / jnp.minimum(tpos + 1, w).astype(F32)
        else:
            mean = acc * (1.0 / w)
        d = (mean - xn[:, lo:hi]).astype(BF16)
        outs.append(_dot(d, w_ref[gi].astype(BF16)))
    o_ref[0] = x + jnp.concatenate(outs, axis=1) * sc_ref[...]

    @pl.when(t == pl.num_programs(1) - 1)
    def _():
        np_ref[0] = ext_ref[tt + POOL_HALO - POOL_STATE:tt + POOL_HALO, :]


def _pool_layer(h, prev16, gain, w_pool, scale, *, tt, clip):
    b, t, d = h.shape
    return pl.pallas_call(
        functools.partial(_pool_kernel, tt=tt, clip=clip),
        grid=(b, t // tt),
        in_specs=[
            pl.BlockSpec((1, tt, d), lambda i, j: (i, j, 0)),
            pl.BlockSpec((1, POOL_HALO, d), lambda i, j: (i, 0, 0)),
            pl.BlockSpec((1, d), lambda i, j: (0, 0)),
            pl.BlockSpec(w_pool.shape, lambda i, j: (0, 0, 0)),
            pl.BlockSpec((1, d), lambda i, j: (0, 0)),
        ],
        out_specs=[
            pl.BlockSpec((1, tt, d), lambda i, j: (i, j, 0)),
            pl.BlockSpec((1, POOL_STATE, d), lambda i, j: (i, 0, 0)),
        ],
        out_shape=[jax.ShapeDtypeStruct((b, t, d), F32),
                   jax.ShapeDtypeStruct((b, POOL_STATE, d), F32)],
        scratch_shapes=[pltpu.VMEM((tt + POOL_HALO, d), F32), pltpu.VMEM((tt + POOL_HALO, POOL_GROUP_DIM), F32)],
        compiler_params=_params(("arbitrary", "arbitrary")),
        name="pool_layer",
    )(h, prev16, gain.reshape(1, d), w_pool, scale.reshape(1, d))


def _softmax_rows(rows):
    m = functools.reduce(jnp.maximum, rows)
    es = [jnp.exp(r - m) for r in rows]
    s = functools.reduce(lambda a, b: a + b, es)
    return [e / s for e in es]


def _router_kernel(h_ref, g_ref, rt_ref, xn_ref, comb_ref, grp_ref):
    xn = _rms(h_ref[...], g_ref[...])
    xn_ref[...] = xn.astype(BF16)
    lt = _nt_dot(rt_ref[...], xn, precision=HIGHEST)
    pg = _softmax_rows([lt[i:i + 1, :] for i in range(N_GROUPS)])
    g_val = functools.reduce(jnp.maximum, pg)
    g_idx = jnp.full(g_val.shape, N_GROUPS - 1, I32)
    for i in range(N_GROUPS - 2, -1, -1):
        g_idx = jnp.where(pg[i] == g_val, i, g_idx)
    le = []
    for j in range(EXPERTS_PER_GROUP):
        v = lt[N_GROUPS + (N_GROUPS - 1) * EXPERTS_PER_GROUP + j:N_GROUPS + (N_GROUPS - 1) * EXPERTS_PER_GROUP + j + 1, :]
        for gi in range(N_GROUPS - 2, -1, -1):
            r = N_GROUPS + gi * EXPERTS_PER_GROUP + j
            v = jnp.where(g_idx == gi, lt[r:r + 1, :], v)
        le.append(v)
    pe = _softmax_rows(le)
    ranks = []
    for j in range(EXPERTS_PER_GROUP):
        r = jnp.zeros(g_val.shape, I32)
        for i in range(EXPERTS_PER_GROUP):
            if i == j:
                continue
            beats = (pe[i] > pe[j]) | (pe[i] == pe[j]) if i < j else (pe[i] > pe[j])
            r = r + beats.astype(I32)
        ranks.append(r)
    vals, idxs = [], []
    for k in range(2):
        v = jnp.zeros(g_val.shape, F32)
        ix = jnp.zeros(g_val.shape, I32)
        for j in range(EXPERTS_PER_GROUP):
            hit = ranks[j] == k
            v = jnp.where(hit, pe[j], v)
            ix = jnp.where(hit, j, ix)
        vals.append(v)
        idxs.append(ix)
    tot = vals[0] + vals[1]
    erow = lax.broadcasted_iota(I32, (LANES, g_val.shape[1]), 0)
    comb_t = jnp.where(erow == ROUTE_GROUP_ROW + g_idx, 1.0, 0.0)
    for k in range(2):
        wk = g_val * (vals[k] / tot)
        w_hi = wk.astype(BF16).astype(F32)
        w_mid = (wk - w_hi).astype(BF16).astype(F32)
        w_lo = (wk - w_hi - w_mid).astype(BF16).astype(F32)
        eid = g_idx * EXPERTS_PER_GROUP + idxs[k]
        for part, term in enumerate((w_hi, w_mid, w_lo)):
            comb_t = comb_t + jnp.where(erow == part * N_EXPERTS + eid, term, 0.0)
    comb_ref[...] = comb_t.T
    grp_ref[...] = comb_t[ROUTE_GROUP_ROW:ROUTE_GROUP_ROW + 8]


def _router(h2, gain, rt, *, tm):
    n, d = h2.shape
    return pl.pallas_call(
        _router_kernel,
        grid=(n // tm,),
        in_specs=[
            pl.BlockSpec((tm, d), lambda i: (i, 0)),
            pl.BlockSpec((1, d), lambda i: (0, 0)),
            pl.BlockSpec(rt.shape, lambda i: (0, 0)),
        ],
        out_specs=[
            pl.BlockSpec((tm, d), lambda i: (i, 0)),
            pl.BlockSpec((tm, LANES), lambda i: (i, 0)),
            pl.BlockSpec((8, tm), lambda i: (0, i)),
        ],
        out_shape=[jax.ShapeDtypeStruct((n, d), BF16), jax.ShapeDtypeStruct((n, LANES), F32),
                   jax.ShapeDtypeStruct((8, n), F32)],
        compiler_params=_params(("arbitrary",)),
        name="moe_router",
    )(h2, gain.reshape(1, d), rt)


def _cast_kernel(x_ref, o_ref):
    o_ref[...] = x_ref[...].astype(o_ref.dtype)


def _cast_bf16(w):
    blk = (1, 1) + w.shape[2:]
    return pl.pallas_call(
        _cast_kernel,
        grid=w.shape[:2],
        in_specs=[pl.BlockSpec(blk, lambda i, j: (i, j, 0, 0))],
        out_specs=pl.BlockSpec(blk, lambda i, j: (i, j, 0, 0)),
        out_shape=jax.ShapeDtypeStruct(w.shape, BF16),
        compiler_params=_params(("arbitrary", "arbitrary")),
        name="cast_weights",
    )(w)


def _moe_kernel(cnt_ref, base_ref, x_ref, comb_ref, grp_ref, h_ref, wgu_ref, wd_ref, o_ref,
                pt_s, z_s, y_s, cs_s, slot_s, *, r_max, unroll):
    w = pl.program_id(0)
    step = pl.program_id(1)
    nw = x_ref.shape[0]
    g = step // (EXPERTS_PER_GROUP // MOE_STEP_EXPERTS)

    @pl.when(step == 0)
    def _():
        lane = lax.broadcasted_iota(I32, (1, LANES), 1)
        in_grp = (lane >= ROUTE_GROUP_ROW) & (lane < ROUTE_GROUP_ROW + N_GROUPS)
        oh = jnp.where(in_grp, comb_ref[...], 0.0)
        base_row = jnp.zeros((1, LANES), F32)
        for gi in range(N_GROUPS):
            base_row = jnp.where(lane == ROUTE_GROUP_ROW + gi, base_ref[w, gi].astype(F32), base_row)
        ri = lax.broadcasted_iota(I32, (nw, nw), 0)
        ci = lax.broadcasted_iota(I32, (nw, nw), 1)
        before = _dot(jnp.where(ci < ri, 1.0, 0.0).astype(BF16), oh.astype(BF16))
        slot_s[...] = jnp.sum(oh * (before + base_row), axis=-1, keepdims=True)
        oh_t = grp_ref[...]
        before_t = _dot(oh_t.astype(BF16), jnp.where(ri < ci, 1.0, 0.0).astype(BF16))
        slot_row = jnp.zeros((1, nw), F32)
        for gi in range(N_GROUPS):
            slot_row = slot_row + oh_t[gi:gi + 1] * (before_t[gi:gi + 1] + base_ref[w, gi].astype(F32))
        rows = lax.broadcasted_iota(I32, (r_max, 1), 0)
        pt_s[...] = jnp.where(rows == slot_row.astype(I32), 1.0, 0.0).astype(BF16)
        z_s[...] = _dot(pt_s[...], x_ref[...]).astype(BF16)
        cs_s[...] = _dot(pt_s[...], comb_ref[...].astype(BF16))
        y_s[...] = jnp.zeros(y_s.shape, F32)

    seg0 = base_ref[w, g]
    seg1 = seg0 + cnt_ref[w, g]
    lane_c = lax.broadcasted_iota(I32, (MOE_CHUNK, LANES), 1)
    in_terms = lane_c < 3 * N_EXPERTS

    def chunk(lo):
        r0 = pl.multiple_of(jnp.minimum(lo, r_max - MOE_CHUNK), MOE_SEG_ALIGN)
        z = z_s[pl.ds(r0, MOE_CHUNK), :]
        cs = cs_s[pl.ds(r0, MOE_CHUNK), :]
        acc = None
        for k in range(MOE_STEP_EXPERTS):
            gu = _dot(z, wgu_ref[0, k])
            a = gu[:, :D_EXPERT]
            hdn = (a * _sigmoid(a) * gu[:, D_EXPERT:]).astype(BF16)
            y = _dot(hdn, wd_ref[0, k])
            is_e = in_terms & (lane_c % N_EXPERTS == step * MOE_STEP_EXPERTS + k)
            wgt = jnp.sum(jnp.where(is_e, cs, 0.0), axis=-1, keepdims=True)
            acc = wgt * y if acc is None else acc + wgt * y
        rows = r0 + lax.broadcasted_iota(I32, (MOE_CHUNK, 1), 0)
        y_s[pl.ds(r0, MOE_CHUNK), :] += jnp.where((rows >= lo) & (rows < seg1), acc, 0.0)

    def chunks(c, carry):
        for u in range(unroll):
            chunk(seg0 + (c * unroll + u) * MOE_CHUNK)
        return carry

    span = unroll * MOE_CHUNK
    lax.fori_loop(0, (cnt_ref[w, g] + span - 1) // span, chunks, 0)

    @pl.when(step == pl.num_programs(1) - 1)
    def _():
        cols = lax.broadcasted_iota(I32, (1, r_max), 1)
        p = jnp.where(cols == slot_s[...].astype(I32), 1.0, 0.0).astype(BF16)
        o_ref[...] = h_ref[...] + _dot(p, y_s[...].astype(BF16))


def _moe(h2, gain, rt, w_gu, w_down, layer, *, tm):
    n, d = h2.shape
    if w_gu.dtype != BF16:
        w_gu, w_down = _cast_bf16(w_gu), _cast_bf16(w_down)
    xn, comb, grp = _router(h2, gain, rt, tm=tm)
    n_win = n // tm
    cnt = jnp.sum(grp[:N_GROUPS].reshape(N_GROUPS, n_win, tm), axis=-1).T.astype(I32)
    seg = (cnt + MOE_SEG_ALIGN - 1) // MOE_SEG_ALIGN * MOE_SEG_ALIGN
    base = jnp.cumsum(seg, axis=1) - seg
    r_max = -(-(tm + N_GROUPS * MOE_SEG_ALIGN + MOE_CHUNK) // LANES) * LANES
    return pl.pallas_call(
        functools.partial(_moe_kernel, r_max=r_max, unroll=max(1, -(-tm // (N_GROUPS * MOE_CHUNK)))),
        grid_spec=pltpu.PrefetchScalarGridSpec(
            num_scalar_prefetch=2,
            grid=(n_win, N_EXPERTS // MOE_STEP_EXPERTS),
            in_specs=[
                pl.BlockSpec((tm, d), lambda i, e, c, b: (i, 0)),
                pl.BlockSpec((tm, LANES), lambda i, e, c, b: (i, 0)),
                pl.BlockSpec((8, tm), lambda i, e, c, b: (0, i)),
                pl.BlockSpec((tm, d), lambda i, e, c, b: (i, 0)),
                pl.BlockSpec((1, MOE_STEP_EXPERTS, d, 2 * D_EXPERT), lambda i, e, c, b: (layer, e, 0, 0)),
                pl.BlockSpec((1, MOE_STEP_EXPERTS, D_EXPERT, d), lambda i, e, c, b: (layer, e, 0, 0)),
            ],
            out_specs=pl.BlockSpec((tm, d), lambda i, e, c, b: (i, 0)),
            scratch_shapes=[
                pltpu.VMEM((r_max, tm), BF16), pltpu.VMEM((r_max, d), BF16), pltpu.VMEM((r_max, d), F32),
                pltpu.VMEM((r_max, LANES), F32), pltpu.VMEM((tm, 1), F32),
            ],
        ),
        out_shape=jax.ShapeDtypeStruct((n, d), F32),
        compiler_params=_params(("arbitrary", "arbitrary"), VMEM_LIMIT),
        name="moe_experts",
    )(cnt, base, xn, comb, grp, h2, w_gu, w_down)


def _head_norm_rope_t(z, gain, c, s):
    outs = []
    for g in range(z.shape[0] // HEAD_DIM):
        zh = z[g * HEAD_DIM:(g + 1) * HEAD_DIM]
        zn = zh * lax.rsqrt(jnp.mean(zh * zh, axis=0, keepdims=True) + EPS) * gain
        x1 = zn[:HALF_DIM]
        x2 = zn[HALF_DIM:]
        outs.append(x1 * c - x2 * s)
        outs.append(x2 * c + x1 * s)
    return jnp.concatenate(outs, axis=0)


def _kvproj_kernel(h_ref, g_ref, wt_ref, gsel_ref, gwin_ref, cos_ref, sin_ref,
                   kvt_ref, wint_ref, craw_ref, ksel_ref, kwin_ref, vselt_ref, vwint_ref, w_s):
    @pl.when((pl.program_id(0) == 0) & (pl.program_id(1) == 0))
    def _():
        w_s[...] = wt_ref[...].astype(BF16)

    xn = _rms(h_ref[0], g_ref[...]).astype(BF16)
    kvt = _nt_dot(w_s[...], xn)
    tt = xn.shape[0]
    c = cos_ref[...]
    s = sin_ref[...]
    gsel = jnp.concatenate([gsel_ref[...]] * (tt // LANES), axis=1)
    gwin = jnp.concatenate([gwin_ref[...]] * (tt // LANES), axis=1)
    ksel = _head_norm_rope_t(kvt[2 * KV_WIDTH:3 * KV_WIDTH], gsel, c, s)
    kwin = _head_norm_rope_t(kvt[4 * KV_WIDTH:5 * KV_WIDTH], gwin, c, s)
    kvt_ref[0] = jnp.concatenate([kvt[:2 * KV_WIDTH], ksel, kvt[3 * KV_WIDTH:4 * KV_WIDTH]], axis=0)
    wint_ref[0] = jnp.concatenate([kwin, kvt[5 * KV_WIDTH:]], axis=0)
    vselt_ref[...] = kvt[3 * KV_WIDTH:4 * KV_WIDTH].astype(BF16)
    vwint_ref[...] = kvt[5 * KV_WIDTH:].astype(BF16)
    raw = kvt[:2 * KV_WIDTH].T
    for k in range(2 * KV_WIDTH // LANES):
        craw_ref[k] = raw[:, k * LANES:(k + 1) * LANES]
    ksel_n = ksel.T
    kwin_n = kwin.T
    for g in range(N_KV_HEADS):
        ksel_ref[g] = ksel_n[:, g * HEAD_DIM:(g + 1) * HEAD_DIM].astype(BF16)
        kwin_ref[g] = kwin_n[:, g * HEAD_DIM:(g + 1) * HEAD_DIM].astype(BF16)


def _kvproj(x3, gain, w_kv, k_norm, cos_tt, sin_tt, *, tt):
    b, t, d = x3.shape
    n = b * t
    nt = t // tt
    wt = w_kv.T
    gsel = jnp.broadcast_to(k_norm[1][:, None], (HEAD_DIM, LANES))
    gwin = jnp.broadcast_to(k_norm[2][:, None], (HEAD_DIM, LANES))
    full = lambda a: pl.BlockSpec(a.shape, lambda i, j: (0,) * a.ndim)
    return pl.pallas_call(
        _kvproj_kernel,
        grid=(b, nt),
        in_specs=[
            pl.BlockSpec((1, tt, d), lambda i, j: (i, j, 0)),
            pl.BlockSpec((1, d), lambda i, j: (0, 0)),
            full(wt), full(gsel), full(gwin),
            pl.BlockSpec((HALF_DIM, tt), lambda i, j: (0, j)),
            pl.BlockSpec((HALF_DIM, tt), lambda i, j: (0, j)),
        ],
        out_specs=[
            pl.BlockSpec((1, 4 * KV_WIDTH, tt), lambda i, j: (i, 0, j)),
            pl.BlockSpec((1, 2 * KV_WIDTH, tt), lambda i, j: (i, 0, j)),
            pl.BlockSpec((2 * KV_WIDTH // LANES, tt, LANES), lambda i, j: (0, i * nt + j, 0)),
            pl.BlockSpec((N_KV_HEADS, tt, HEAD_DIM), lambda i, j: (0, i * nt + j, 0)),
            pl.BlockSpec((N_KV_HEADS, tt, HEAD_DIM), lambda i, j: (0, i * nt + j, 0)),
            pl.BlockSpec((KV_WIDTH, tt), lambda i, j: (0, i * nt + j)),
            pl.BlockSpec((KV_WIDTH, tt), lambda i, j: (0, i * nt + j)),
        ],
        out_shape=[
            jax.ShapeDtypeStruct((b, 4 * KV_WIDTH, t), F32),
            jax.ShapeDtypeStruct((b, 2 * KV_WIDTH, t), F32),
            jax.ShapeDtypeStruct((2 * KV_WIDTH // LANES, n, LANES), F32),
            jax.ShapeDtypeStruct((N_KV_HEADS, n, HEAD_DIM), BF16),
            jax.ShapeDtypeStruct((N_KV_HEADS, n, HEAD_DIM), BF16),
            jax.ShapeDtypeStruct((KV_WIDTH, n), BF16),
            jax.ShapeDtypeStruct((KV_WIDTH, n), BF16),
        ],
        scratch_shapes=[pltpu.VMEM(wt.shape, BF16)],
        compiler_params=_params(("arbitrary", "arbitrary"), VMEM_LIMIT),
        name="kv_proj",
    )(x3, gain.reshape(1, d), wt, gsel, gwin, cos_tt, sin_tt)


def _rope_tables_transposed(pos):
    inv = 1.0 / (ROPE_THETA ** (jnp.arange(HALF_DIM, dtype=F32) * (2.0 / HEAD_DIM)))
    ang = pos.astype(F32)[:, None] * inv[None, :]
    return jnp.cos(ang).T, jnp.sin(ang).T


def _cmp_ab_accumulate(load_rows, wab_ref, n_rows):
    del n_rows
    heads_per_chunk = LANES // HEAD_DIM
    accs = [[None] * N_KV_HEADS for _ in range(2)]
    for s in range(2):
        for c in range(KV_WIDTH // LANES):
            plane = s * (KV_WIDTH // LANES) + c
            lhs = jnp.concatenate([load_rows(r, plane).astype(BF16) for r in range(CMP_STRIDE)], axis=1)
            out = _dot(lhs, wab_ref[s])
            for k in range(heads_per_chunk):
                accs[s][c * heads_per_chunk + k] = out[:, k * 2 * CMP_HIDDEN:(k + 1) * 2 * CMP_HIDDEN]
    return accs


def _cmpab_kernel(craw_ref, wab_ref, ab_ref, *, n_chunk):
    accs = _cmp_ab_accumulate(lambda r, c: craw_ref[c, pl.ds(r, n_chunk, stride=CMP_STRIDE), :], wab_ref, n_chunk)
    for s in range(2):
        for g in range(N_KV_HEADS):
            ab_ref[0, s, g] = accs[s][g]


def _cmpab_pages_kernel(pt_ref, *refs, n_pages):
    del pt_ref
    page_refs = refs[:n_pages]
    wab_ref, ab_ref, craw_s = refs[n_pages:]
    pairs = KV_WIDTH // LANES
    for j in range(n_pages):
        for s in range(2):
            for k in range(pairs):
                tile = jnp.concatenate([page_refs[j][0, s, 2 * k], page_refs[j][0, s, 2 * k + 1]], axis=0)
                craw_s[s * pairs + k, j * PAGE_SIZE:(j + 1) * PAGE_SIZE, :] = tile.T
    n_chunk = n_pages * PAGE_SIZE // CMP_STRIDE
    accs = _cmp_ab_accumulate(lambda r, c: craw_s[c, pl.ds(r, n_chunk, stride=CMP_STRIDE), :], wab_ref, n_chunk)
    for s in range(2):
        for g in range(N_KV_HEADS):
            ab_ref[0, s, g] = accs[s][g]


def _cmp_ab_pages(cache5, page_table, wab, *, n_pages):
    b, pages_per_seq = page_table.shape
    steps = pages_per_seq // n_pages
    rows = n_pages * PAGE_SIZE // CMP_STRIDE

    def page_spec(j):
        return pl.BlockSpec((1, 2, N_KV_HEADS, HEAD_DIM, PAGE_SIZE),
                            lambda i, q, pt: (pt[i, q * n_pages + j], 0, 0, 0, 0))

    return pl.pallas_call(
        functools.partial(_cmpab_pages_kernel, n_pages=n_pages),
        grid_spec=pltpu.PrefetchScalarGridSpec(
            num_scalar_prefetch=1,
            grid=(b, steps),
            in_specs=[page_spec(j) for j in range(n_pages)]
            + [pl.BlockSpec(wab.shape, lambda i, q, pt: (0, 0, 0))],
            out_specs=pl.BlockSpec((1, 2, N_KV_HEADS, rows, 2 * CMP_HIDDEN), lambda i, q, pt: (i, 0, 0, q, 0)),
            scratch_shapes=[pltpu.VMEM((2 * KV_WIDTH // LANES, n_pages * PAGE_SIZE, LANES), F32)],
        ),
        out_shape=jax.ShapeDtypeStruct((b, 2, N_KV_HEADS, steps * rows, 2 * CMP_HIDDEN), F32),
        compiler_params=_params(("arbitrary", "arbitrary"), VMEM_LIMIT),
        name="cmp_ab_pages",
    )(page_table, *([cache5] * n_pages), wab)


def _cmp_weights(cmp_w1):
    w = cmp_w1.reshape(2, 2, CMP_STRIDE, HEAD_DIM, CMP_HIDDEN)
    w = w.transpose(0, 2, 3, 1, 4).reshape(2, CMP_STRIDE, HEAD_DIM, 2 * CMP_HIDDEN)
    eye = jnp.eye(LANES // HEAD_DIM, dtype=w.dtype)
    w = w[:, :, None, :, None, :] * eye[None, None, :, None, :, None]
    return w.reshape(2, CMP_STRIDE * LANES, (LANES // HEAD_DIM) * 2 * CMP_HIDDEN).astype(BF16)


def _cmp_ab_prompt(craw, wab, *, b, t):
    n_chunk = t // CMP_STRIDE
    return pl.pallas_call(
        functools.partial(_cmpab_kernel, n_chunk=n_chunk),
        grid=(b,),
        in_specs=[
            pl.BlockSpec((craw.shape[0], t, LANES), lambda i: (0, i, 0)),
            pl.BlockSpec(wab.shape, lambda i: (0, 0, 0)),
        ],
        out_specs=pl.BlockSpec((1, 2, N_KV_HEADS, n_chunk, 2 * CMP_HIDDEN), lambda i: (i, 0, 0, 0, 0)),
        out_shape=jax.ShapeDtypeStruct((b, 2, N_KV_HEADS, n_chunk, 2 * CMP_HIDDEN), F32),
        compiler_params=_params(("arbitrary",)),
        name="cmp_ab_prompt",
    )(craw, wab)


def _cmpfin_kernel(ab_ref, pe_ref, w1_ref, b1_ref, w2_ref, gk_ref, ck_ref, cv_ref, *, n_row):
    for s in range(2):
        bias = _dot(pe_ref[s].astype(BF16), w1_ref[s].astype(BF16)) + b1_ref[s]
        w2 = w2_ref[s].astype(BF16)
        for g in range(N_KV_HEADS):
            ab = ab_ref[0, s, g]
            hid = ab[:, :CMP_HIDDEN] + pltpu.roll(ab[:, CMP_HIDDEN:], n_row - 1, 0) + bias
            cdf = 0.5 * (1.0 + jnp.tanh(0.7978845608028654 * (hid + 0.044715 * (hid * hid * hid))))
            out = _dot((hid * cdf).astype(BF16), w2)
            if s == 0:
                ck_ref[0, g] = _rms(out, gk_ref[...]).astype(BF16)
            else:
                cv_ref[0, g] = out.astype(BF16)


def _cmp_finish(ab, cmp_pe, cmp_w1, cmp_b1, cmp_w2, gk):
    b = ab.shape[0]
    n_row = ab.shape[3]
    pe = cmp_pe.reshape(2, 1, CMP_BLOCK * HEAD_DIM)
    full = lambda a: pl.BlockSpec(a.shape, lambda i: (0,) * a.ndim)
    b1 = cmp_b1.reshape(2, 1, CMP_HIDDEN)
    gk2 = gk.reshape(1, HEAD_DIM)
    return pl.pallas_call(
        functools.partial(_cmpfin_kernel, n_row=n_row),
        grid=(b,),
        in_specs=[pl.BlockSpec((1,) + ab.shape[1:], lambda i: (i, 0, 0, 0, 0)),
                  full(pe), full(cmp_w1), full(b1), full(cmp_w2), full(gk2)],
        out_specs=[pl.BlockSpec((1, N_KV_HEADS, n_row, HEAD_DIM), lambda i: (i, 0, 0, 0)),
                   pl.BlockSpec((1, N_KV_HEADS, n_row, HEAD_DIM), lambda i: (i, 0, 0, 0))],
        out_shape=[jax.ShapeDtypeStruct((b, N_KV_HEADS, n_row, HEAD_DIM), BF16),
                   jax.ShapeDtypeStruct((b, N_KV_HEADS, n_row, HEAD_DIM), BF16)],
        compiler_params=_params(("arbitrary",)),
        name="cmp_finish",
    )(ab, pe, cmp_w1, b1, cmp_w2, gk2)


def _qproj_kernel(h_ref, g_ref, wt_ref, gq_ref, cos_ref, sin_ref, qn_ref, qr_ref, gt_ref, w_s):
    @pl.when(pl.program_id(0) == 0)
    def _():
        w_s[...] = wt_ref[...].astype(BF16)

    xn = _rms(h_ref[...], g_ref[...]).astype(BF16)
    qg = _nt_dot(w_s[...], xn)
    tt = xn.shape[0]
    gq = jnp.concatenate([gq_ref[...]] * (tt // LANES), axis=1)
    c = cos_ref[...]
    s = sin_ref[...]
    for h in range(N_HEADS):
        qh = qg[h * HEAD_DIM:(h + 1) * HEAD_DIM]
        qn = qh * lax.rsqrt(jnp.mean(qh * qh, axis=0, keepdims=True) + EPS) * gq
        qn_ref[h * HEAD_DIM:(h + 1) * HEAD_DIM, :] = (qn * SCALE).astype(BF16)
        x1 = qn[:HALF_DIM]
        x2 = qn[HALF_DIM:]
        qr_ref[h * HEAD_DIM:h * HEAD_DIM + HALF_DIM, :] = ((x1 * c - x2 * s) * SCALE).astype(BF16)
        qr_ref[h * HEAD_DIM + HALF_DIM:(h + 1) * HEAD_DIM, :] = ((x2 * c + x1 * s) * SCALE).astype(BF16)
    gt_ref[...] = _sigmoid(qg[N_HEADS * HEAD_DIM:])


def _qg_weights(w_qg):
    nq = N_HEADS * HEAD_DIM
    gates = w_qg[:, nq:].reshape(D_MODEL, 3, N_KV_HEADS, HEADS_PER_KV).transpose(2, 1, 3, 0)
    gates = gates.reshape(N_KV_HEADS, 3 * HEADS_PER_KV, D_MODEL)
    gates = jnp.pad(gates, ((0, 0), (0, GATE_ROWS - 3 * HEADS_PER_KV), (0, 0)))
    return jnp.concatenate([w_qg[:, :nq].T, gates.reshape(N_KV_HEADS * GATE_ROWS, D_MODEL)], axis=0)


def _qproj(h2, gain, wt, q_norm, cos_tt, sin_tt, *, tt, pos_blocks):
    n, d = h2.shape
    nq = N_HEADS * HEAD_DIM
    ng = N_KV_HEADS * GATE_ROWS
    gq = jnp.broadcast_to(q_norm[:, None], (HEAD_DIM, LANES))
    full = lambda a: pl.BlockSpec(a.shape, lambda i: (0,) * a.ndim)
    return pl.pallas_call(
        _qproj_kernel,
        grid=(n // tt,),
        in_specs=[
            pl.BlockSpec((tt, d), lambda i: (i, 0)),
            pl.BlockSpec((1, d), lambda i: (0, 0)),
            full(wt), full(gq),
            pl.BlockSpec((HALF_DIM, tt), lambda i: (0, i % pos_blocks)),
            pl.BlockSpec((HALF_DIM, tt), lambda i: (0, i % pos_blocks)),
        ],
        out_specs=[
            pl.BlockSpec((nq, tt), lambda i: (0, i)),
            pl.BlockSpec((nq, tt), lambda i: (0, i)),
            pl.BlockSpec((ng, tt), lambda i: (0, i)),
        ],
        out_shape=[jax.ShapeDtypeStruct((nq, n), BF16), jax.ShapeDtypeStruct((nq, n), BF16),
                   jax.ShapeDtypeStruct((ng, n), F32)],
        scratch_shapes=[pltpu.VMEM(wt.shape, BF16)],
        compiler_params=_params(("arbitrary",), VMEM_LIMIT),
        name="q_proj",
    )(h2, gain.reshape(1, d), wt, gq, cos_tt, sin_tt)


def _attn_kernel(qn_ref, qr_ref, gt_ref, ck_ref, cvt_ref, kaug_ref, kwin_ref, vaug_ref, vwaug_ref,
                 o_ref, pg_s, o_sel_s, o_win_s, *, tq, n_cmp, n_sb, n_qt):
    hp_n = HEADS_PER_KV
    qt = pl.program_id(2)
    t0 = qt * tq
    tpos = t0 + lax.broadcasted_iota(I32, (1, tq), 1)
    tpos4 = jnp.concatenate([tpos] * hp_n, axis=1)
    qn4 = jnp.concatenate([qn_ref[h * HEAD_DIM:(h + 1) * HEAD_DIM, :] for h in range(hp_n)], axis=1)
    qr4 = jnp.concatenate([qr_ref[h * HEAD_DIM:(h + 1) * HEAD_DIM, :] for h in range(hp_n)], axis=1)

    n_row = ck_ref.shape[2]
    s = _dot(ck_ref[0, 0], qn4)
    ci = lax.broadcasted_iota(I32, (n_row, 1), 0)
    vis = (ci * CMP_STRIDE + (CMP_BLOCK - 1) <= tpos4) & (ci < n_cmp)
    s = jnp.where(vis, s, NEG)
    p = jnp.where(vis, jnp.exp(s - jnp.max(s, axis=0, keepdims=True)), 0.0)
    p = p / jnp.maximum(jnp.sum(p, axis=0, keepdims=True), TINY)
    o_cmp = _dot(cvt_ref[0, 0], p.astype(BF16))
    pg = p[:, 0:tq]
    for h in range(1, hp_n):
        pg = pg + p[:, h * tq:(h + 1) * tq]

    ratio = SEL_BLOCK // CMP_STRIDE
    scores = []
    for c in range(tq // LANES):
        pg_s[c, 0:8, :] = jnp.zeros((8, LANES), F32)
        pg_s[c, 8:8 + n_row, :] = pg[:, c * LANES:(c + 1) * LANES]
        sc_c = pg_s[c, pl.ds(8 + 1 - CMP_BLOCK // CMP_STRIDE, n_sb, stride=ratio), :]
        for o in range(2 - CMP_BLOCK // CMP_STRIDE, ratio):
            sc_c = sc_c + pg_s[c, pl.ds(8 + o, n_sb, stride=ratio), :]
        scores.append(sc_c)
    score = jnp.concatenate(scores, axis=1)
    jrow = lax.broadcasted_iota(I32, (n_sb, 1), 0)
    cur = tpos // SEL_BLOCK
    forced = (jrow == 0) | (jrow == cur) | (jrow == cur - 1)
    sc = jnp.where(jrow * SEL_BLOCK <= tpos, score + jnp.where(forced, BIG, 0.0), -BIG)
    rank = jnp.zeros((n_sb, tq), I32)
    for jp in range(n_sb):
        row = sc[jp:jp + 1, :]
        beats = (row > sc) | ((row == sc) & (jrow > jp))
        rank = rank + beats.astype(I32)
    n_sbp = kaug_ref.shape[3] - HEAD_DIM
    sel_neg = jnp.where(rank < min(N_SEL, n_sb), 0.0, NEG)
    if n_sbp > n_sb:
        sel_neg = jnp.concatenate([sel_neg, jnp.zeros((n_sbp - n_sb, tq), F32)], axis=0)
    sel_neg = sel_neg.astype(BF16)
    q_rot = [qr_ref[h * HEAD_DIM:(h + 1) * HEAD_DIM, :] for h in range(hp_n)]
    q_aug = jnp.concatenate([jnp.concatenate([q, sel_neg], axis=0) for q in q_rot], axis=1)
    q_win = jnp.concatenate(q_rot, axis=1)
    causal = jnp.where(lax.broadcasted_iota(I32, (tq, 1), 0) <= lax.broadcasted_iota(I32, (1, tq), 1), 0.0, NEG)
    causal = jnp.concatenate([causal] * hp_n, axis=1)

    for k in range(n_qt):
        @pl.when(qt == k)
        def _(k=k):
            lo = k * tq
            s_diag = _dot(kaug_ref[0, 0, lo:lo + tq, :], q_aug) + causal
            m = jnp.max(s_diag, axis=0, keepdims=True)
            if k > 0:
                s_top = _dot(kaug_ref[0, 0, 0:lo, :], q_aug)
                m = jnp.maximum(m, jnp.max(s_top, axis=0, keepdims=True))
            acc = _dot(vaug_ref[:, lo:lo + tq], jnp.exp(s_diag - m).astype(BF16))
            if k > 0:
                acc = acc + _dot(vaug_ref[:, 0:lo], jnp.exp(s_top - m).astype(BF16))
            o_sel_s[...] = acc[:HEAD_DIM] / jnp.maximum(acc[HEAD_DIM:HEAD_DIM + 1], TINY)

            parts = []
            for kt in range(max(0, k - -(-WINDOW // tq)), k + 1):
                s_t = _dot(kwin_ref[0, 0, kt * tq:(kt + 1) * tq, :], q_win)
                d_lo = (k - kt) * tq - (tq - 1)
                d_hi = (k - kt) * tq + (tq - 1)
                if d_lo < 0 or d_hi >= WINDOW:
                    dq = ((k - kt) * tq + lax.broadcasted_iota(I32, (1, tq), 1)
                          - lax.broadcasted_iota(I32, (tq, 1), 0))
                    wb = jnp.where((dq >= 0) & (dq < WINDOW), 0.0, NEG)
                    s_t = s_t + jnp.concatenate([wb] * hp_n, axis=1)
                parts.append((kt, s_t))
            m_w = functools.reduce(jnp.maximum, [jnp.max(s_t, axis=0, keepdims=True) for _, s_t in parts])
            acc_w = None
            for kt, s_t in parts:
                term = _dot(vwaug_ref[:, kt * tq:(kt + 1) * tq], jnp.exp(s_t - m_w).astype(BF16))
                acc_w = term if acc_w is None else acc_w + term
            o_win_s[...] = acc_w[:HEAD_DIM] / jnp.maximum(acc_w[HEAD_DIM:HEAD_DIM + 1], TINY)

    o_sel = o_sel_s[...]
    o_win = o_win_s[...]
    gt = gt_ref[...]
    for h in range(hp_n):
        sl = slice(h * tq, (h + 1) * tq)
        o = (gt[h:h + 1] * o_cmp[:, sl] + gt[hp_n + h:hp_n + h + 1] * o_sel[:, sl]
             + gt[2 * hp_n + h:2 * hp_n + h + 1] * o_win[:, sl])
        o_ref[h * HEAD_DIM:(h + 1) * HEAD_DIM, :] = o.astype(BF16)


def _attn_prompt(qn_t, qr_t, g_t, ck, cv_t, ksel, kwin, vsel_t, vwin_t, *, b, t, tq):
    nq = t // tq
    n_sb = t // SEL_BLOCK
    n_row = ck.shape[2]
    n_cmp = t // CMP_STRIDE - CMP_BLOCK // CMP_STRIDE + 1
    n_sbp = -(-n_sb // 32) * 32
    kw = N_KV_HEADS * HEAD_DIM
    onehot = (jnp.arange(t)[:, None] // SEL_BLOCK == jnp.arange(n_sbp)[None, :]).astype(BF16)
    kaug = jnp.concatenate([ksel.reshape(N_KV_HEADS, b, t, HEAD_DIM),
                            jnp.broadcast_to(onehot, (N_KV_HEADS, b, t, n_sbp))], axis=-1)
    ones_rows = jnp.zeros((N_KV_HEADS, V_PAD_ROWS, b * t), BF16).at[:, 0].set(1.0)

    def with_ones(v_t):
        v3 = jnp.concatenate([v_t.reshape(N_KV_HEADS, HEAD_DIM, b * t), ones_rows], axis=1)
        return v3.reshape(N_KV_HEADS * (HEAD_DIM + V_PAD_ROWS), b * t)

    v_rows = HEAD_DIM + V_PAD_ROWS
    return pl.pallas_call(
        functools.partial(_attn_kernel, tq=tq, n_cmp=n_cmp, n_sb=n_sb, n_qt=nq),
        grid=(b, N_KV_HEADS, nq),
        in_specs=[
            pl.BlockSpec((kw, tq), lambda i, g, q: (g, i * nq + q)),
            pl.BlockSpec((kw, tq), lambda i, g, q: (g, i * nq + q)),
            pl.BlockSpec((GATE_ROWS, tq), lambda i, g, q: (g, i * nq + q)),
            pl.BlockSpec((1, 1, n_row, HEAD_DIM), lambda i, g, q: (i, g, 0, 0)),
            pl.BlockSpec((1, 1, HEAD_DIM, n_row), lambda i, g, q: (i, g, 0, 0)),
            pl.BlockSpec((1, 1, t, HEAD_DIM + n_sbp), lambda i, g, q: (g, i, 0, 0)),
            pl.BlockSpec((1, 1, t, HEAD_DIM), lambda i, g, q: (g, i, 0, 0)),
            pl.BlockSpec((v_rows, t), lambda i, g, q: (g, i)),
            pl.BlockSpec((v_rows, t), lambda i, g, q: (g, i)),
        ],
        out_specs=pl.BlockSpec((kw, tq), lambda i, g, q: (g, i * nq + q)),
        out_shape=jax.ShapeDtypeStruct((N_HEADS * HEAD_DIM, b * t), BF16),
        scratch_shapes=[pltpu.VMEM((tq // LANES, n_row + 8, LANES), F32),
                        pltpu.VMEM((HEAD_DIM, HEADS_PER_KV * tq), F32),
                        pltpu.VMEM((HEAD_DIM, HEADS_PER_KV * tq), F32)],
        compiler_params=_params(("arbitrary", "arbitrary", "arbitrary"), VMEM_LIMIT),
        name="attn_prompt",
    )(qn_t, qr_t, g_t, ck, cv_t, kaug, kwin.reshape(N_KV_HEADS, b, t, HEAD_DIM), with_ones(vsel_t), with_ones(vwin_t))


def _oproj_t_kernel(ot_ref, h_ref, w_ref, out_ref, w_s):
    @pl.when(pl.program_id(0) == 0)
    def _():
        w_s[...] = w_ref[...].astype(BF16)

    o = ot_ref[...].astype(F32).T.astype(BF16)
    out_ref[...] = h_ref[...] + _dot(o, w_s[...])


def _oproj_t(o_t, h2, w_o, *, tt):
    n, d = h2.shape
    return pl.pallas_call(
        _oproj_t_kernel,
        grid=(n // tt,),
        in_specs=[
            pl.BlockSpec((o_t.shape[0], tt), lambda i: (0, i)),
            pl.BlockSpec((tt, d), lambda i: (i, 0)),
            pl.BlockSpec(w_o.shape, lambda i: (0, 0)),
        ],
        out_specs=pl.BlockSpec((tt, d), lambda i: (i, 0)),
        out_shape=jax.ShapeDtypeStruct((n, d), F32),
        scratch_shapes=[pltpu.VMEM(w_o.shape, BF16)],
        compiler_params=_params(("arbitrary",), VMEM_LIMIT),
        name="o_proj",
    )(o_t, h2, w_o)


def _oproj_n_kernel(o_ref, h_ref, w_ref, out_ref):
    out_ref[...] = h_ref[...] + _dot(o_ref[...].astype(BF16), w_ref[...].astype(BF16))


def _oproj_n(o, h2, w_o):
    n, d = h2.shape
    return pl.pallas_call(
        _oproj_n_kernel,
        out_shape=jax.ShapeDtypeStruct((n, d), F32),
        compiler_params=_params((), VMEM_LIMIT),
        name="o_proj_sample",
    )(o, h2, w_o)


def _attn_sample_kernel(pt_ref, *refs, n_pages, n_cmp, n_sb, ts, past, n_buf):
    del pt_ref
    page_refs = refs[:n_pages]
    (qn_ref, qr_ref, ckt_ref, cvt_ref, win_ref, knew_ref, vnew_ref, kwnew_ref, vwnew_ref, gate_ref,
     sel_ref, e_ref, o_ref, mask_s, m_s, l_s, acc_s, ocmp_s) = refs[n_pages:]
    q_step = pl.program_id(1)
    rows = HEADS_PER_KV * N_KV_HEADS * ts
    grp_rows = N_KV_HEADS * ts
    row = lax.broadcasted_iota(I32, (rows, 1), 0)
    qpos = past + row % ts
    qr = qr_ref[0]

    def tile_rows(x):
        return jnp.concatenate([x] * HEADS_PER_KV, axis=0)

    def online_update(s, mk, v_dot):
        s = jnp.where(mk, s, NEG)
        m_new = jnp.maximum(m_s[...], jnp.max(s, axis=-1, keepdims=True))
        alpha = jnp.exp(m_s[...] - m_new)
        p = jnp.where(mk, jnp.exp(s - m_new), 0.0)
        l_s[...] = l_s[...] * alpha + jnp.sum(p, axis=-1, keepdims=True)
        acc_s[...] = acc_s[...] * alpha + v_dot(p.astype(BF16))
        m_s[...] = m_new

    @pl.when(q_step == 0)
    def _():
        n_row = ckt_ref.shape[2]
        s = _dot(qn_ref[0], ckt_ref[0])
        ci = lax.broadcasted_iota(I32, (1, n_row), 1)
        vis = (ci * CMP_STRIDE + (CMP_BLOCK - 1) <= qpos) & (ci < n_cmp)
        s = jnp.where(vis, s, NEG)
        p = jnp.where(vis, jnp.exp(s - jnp.max(s, axis=-1, keepdims=True)), 0.0)
        p = p / jnp.maximum(jnp.sum(p, axis=-1, keepdims=True), TINY)
        ocmp_s[...] = _nt_dot(p.astype(BF16), cvt_ref[0])
        pg = p[0:grp_rows]
        for h in range(1, HEADS_PER_KV):
            pg = pg + p[h * grp_rows:(h + 1) * grp_rows]
        score = _dot(pg, sel_ref[...], precision=HIGHEST)
        width = score.shape[1]
        j = lax.broadcasted_iota(I32, (1, width), 1)
        tq = qpos[0:grp_rows]
        cur = tq // SEL_BLOCK
        forced = (j == 0) | (j == cur) | (j == cur - 1)
        sc = jnp.where(j * SEL_BLOCK <= tq, score + jnp.where(forced, BIG, 0.0), -BIG)
        sc = jnp.where(j < n_sb, sc, -2.0 * BIG)
        rank = jnp.zeros((grp_rows, width), I32)
        for jp in range(n_sb):
            col = sc[:, jp:jp + 1]
            beats = (col > sc) | ((col == sc) & (j > jp))
            rank = rank + beats.astype(I32)
        sel = jnp.where((rank < min(N_SEL, n_sb)) & (j < n_sb), 1.0, 0.0).astype(BF16)
        mask_s[...] = _dot(sel, e_ref[...])
        m_s[...] = jnp.full(m_s.shape, NEG, F32)
        l_s[...] = jnp.zeros(l_s.shape, F32)
        acc_s[...] = jnp.zeros(acc_s.shape, F32)

    width = n_pages * PAGE_SIZE
    kt = jnp.concatenate([page_refs[i][0, 0].reshape(KV_WIDTH, PAGE_SIZE) for i in range(n_pages)], axis=1)
    vt = jnp.concatenate([page_refs[i][0, 1].reshape(KV_WIDTH, PAGE_SIZE) for i in range(n_pages)], axis=1)
    k0 = pl.multiple_of(q_step * width, width)
    mk = tile_rows(mask_s[:, pl.ds(k0, width)]) > 0.5
    online_update(_dot(qr, kt.astype(BF16)), mk, lambda p: _nt_dot(p, vt.astype(BF16)))

    @pl.when(q_step == pl.num_programs(1) - 1)
    def _():
        lane = lax.broadcasted_iota(I32, (1, LANES), 1)
        new_ok = (lane < ts) & (past + lane <= qpos)
        mk_new = (tile_rows(mask_s[:, past:past + LANES]) > 0.5) & new_ok
        online_update(_nt_dot(qr, knew_ref[0]), mk_new, lambda p: _dot(p, vnew_ref[0]))
        o_sel = acc_s[...] / jnp.maximum(l_s[...], TINY)
        bi = lax.broadcasted_iota(I32, (1, n_buf), 1)
        dq = qpos - (past - n_buf + bi)
        ok_buf = (dq >= 0) & (dq < WINDOW)
        dq_new = qpos - (past + lane)
        ok_new = (lane < ts) & (dq_new >= 0) & (dq_new < WINDOW)
        s_w = jnp.concatenate([_dot(qr, win_ref[0, 0].astype(BF16)), _nt_dot(qr, kwnew_ref[0])], axis=1)
        ok = jnp.concatenate([jnp.broadcast_to(ok_buf, (rows, n_buf)), jnp.broadcast_to(ok_new, (rows, LANES))], axis=1)
        s_w = jnp.where(ok, s_w, NEG)
        p_w = jnp.where(ok, jnp.exp(s_w - jnp.max(s_w, axis=-1, keepdims=True)), 0.0)
        p_w = (p_w / jnp.maximum(jnp.sum(p_w, axis=-1, keepdims=True), TINY)).astype(BF16)
        o_win = _nt_dot(p_w[:, :n_buf], win_ref[0, 1].astype(BF16)) + _dot(p_w[:, n_buf:], vwnew_ref[0])
        row_g = (row // ts) % N_KV_HEADS

        def own_group(o):
            out = jnp.zeros((rows, HEAD_DIM), F32)
            for g in range(N_KV_HEADS):
                out = out + jnp.where(row_g == g, o[:, g * HEAD_DIM:(g + 1) * HEAD_DIM], 0.0)
            return out

        o_ref[0] = (gate_ref[0, 0] * own_group(ocmp_s[...]) + gate_ref[0, 1] * own_group(o_sel)
                    + gate_ref[0, 2] * own_group(o_win))


def _attn_sample(cache5, page_table, qn_bd, qr_bd, ck_t, cv_t, win4, knew, vnew, kwnew, vwnew, gates,
                 *, n_pages, ts, past):
    b, pages_per_seq = page_table.shape
    steps = pages_per_seq // n_pages
    rows = qn_bd.shape[1]
    n_chunk = ck_t.shape[2]
    n_cmp = n_chunk - CMP_BLOCK // CMP_STRIDE + 1
    n_keys = past + LANES
    n_sb = -(-(past + ts) // SEL_BLOCK)
    n_sb_pad = -(-n_sb // LANES) * LANES
    n_buf = win4.shape[3]
    ratio = SEL_BLOCK // CMP_STRIDE
    ci = jnp.arange(n_chunk)[:, None]
    jb = jnp.arange(n_sb_pad)[None, :]
    sel_map = ((ci >= ratio * jb + 1 - CMP_BLOCK // CMP_STRIDE) & (ci < ratio * jb + ratio)
               & (ci < n_cmp) & (jb < n_sb)).astype(F32)
    expand = (jnp.arange(n_keys)[None, :] // SEL_BLOCK == jnp.arange(n_sb_pad)[:, None]).astype(BF16)

    def page_spec(j):
        return pl.BlockSpec((1, 2, N_KV_HEADS, HEAD_DIM, PAGE_SIZE),
                            lambda i, q, pt: (pt[i, q * n_pages + j], 1, 0, 0, 0))

    per_seq = lambda a: pl.BlockSpec((1,) + a.shape[1:], lambda i, q, pt: (i,) + (0,) * (a.ndim - 1))
    full = lambda a: pl.BlockSpec(a.shape, lambda i, q, pt: (0,) * a.ndim)
    return pl.pallas_call(
        functools.partial(_attn_sample_kernel, n_pages=n_pages, n_cmp=n_cmp, n_sb=n_sb, ts=ts, past=past,
                          n_buf=n_buf),
        grid_spec=pltpu.PrefetchScalarGridSpec(
            num_scalar_prefetch=1,
            grid=(b, steps),
            in_specs=[page_spec(j) for j in range(n_pages)]
            + [per_seq(a) for a in (qn_bd, qr_bd, ck_t, cv_t, win4, knew, vnew, kwnew, vwnew, gates)]
            + [full(sel_map), full(expand)],
            out_specs=pl.BlockSpec((1, rows, HEAD_DIM), lambda i, q, pt: (i, 0, 0)),
            scratch_shapes=[
                pltpu.VMEM((N_KV_HEADS * ts, n_keys), F32),
                pltpu.VMEM((rows, 1), F32), pltpu.VMEM((rows, 1), F32),
                pltpu.VMEM((rows, KV_WIDTH), F32), pltpu.VMEM((rows, KV_WIDTH), F32),
            ],
        ),
        out_shape=jax.ShapeDtypeStruct((b, rows, HEAD_DIM), F32),
        compiler_params=_params(("arbitrary", "arbitrary"), VMEM_LIMIT),
        name="attn_sample",
    )(page_table, *([cache5] * n_pages), qn_bd, qr_bd, ck_t, cv_t, win4, knew, vnew, kwnew, vwnew, gates,
      sel_map, expand)


def _router_weights(router_group, router_expert):
    rt = jnp.concatenate([router_group, router_expert], axis=1).T
    return jnp.pad(rt, ((0, 24 - rt.shape[0]), (0, 0)))


def _trunk_prompt(x, p, *, tt_pool=256, tm=1024, tt=512, tq=256):
    (norm_mix, norm_ffn, pool_w, pool_scale, kv_norm, w_kv, k_norm, cmp_pe, cmp_w1, cmp_b1, cmp_w2,
     w_qg, q_norm, w_o, router_group, router_expert, w_gate_up, w_down) = p
    b, t, d = x.shape
    n = b * t
    pos = jnp.arange(t)
    h, new_pool = _pool_layer(x, jnp.zeros((b, POOL_HALO, d), F32), norm_mix[0], pool_w[0], pool_scale[0],
                              tt=tt_pool, clip=True)
    h = _moe(h.reshape(n, d), norm_ffn[0], _router_weights(router_group[0], router_expert[0]),
             w_gate_up, w_down, 0, tm=tm)
    cos_t, sin_t = _rope_tables_transposed(pos)
    kv_t, win_t, craw, ksel, kwin, vsel_t, vwin_t = _kvproj(h.reshape(b, t, d), kv_norm, w_kv, k_norm,
                                                            cos_t, sin_t, tt=tt)
    ab = _cmp_ab_prompt(craw, _cmp_weights(cmp_w1), b=b, t=t)
    ck, cv = _cmp_finish(ab, cmp_pe, cmp_w1, cmp_b1, cmp_w2, k_norm[0])
    qn_t, qr_t, g_t = _qproj(h, norm_mix[1], _qg_weights(w_qg[0]), q_norm[0], cos_t, sin_t,
                             tt=tt, pos_blocks=t // tt)
    o_t = _attn_prompt(qn_t, qr_t, g_t, ck, jnp.swapaxes(cv, 2, 3), ksel, kwin, vsel_t, vwin_t, b=b, t=t, tq=tq)
    h = _oproj_t(o_t, h, w_o[0], tt=tt)
    h = _moe(h, norm_ffn[1], _router_weights(router_group[1], router_expert[1]),
             w_gate_up, w_down, 1, tm=tm)
    n_win = min(WINDOW, t)
    kv_new = kv_t.reshape(b, 4, N_KV_HEADS, HEAD_DIM, t).transpose(0, 4, 1, 2, 3)
    win_new = win_t[:, :, t - n_win:].reshape(b, 2, N_KV_HEADS, HEAD_DIM, n_win).transpose(0, 4, 1, 2, 3)
    return h.reshape(b, t, d), new_pool[None], kv_new, win_new


def _trunk_sample(x, state_pool, cache_kv, page_table, state_win, p, *, n_pages=32):
    (norm_mix, norm_ffn, pool_w, pool_scale, kv_norm, w_kv, k_norm, cmp_pe, cmp_w1, cmp_b1, cmp_w2,
     w_qg, q_norm, w_o, router_group, router_expert, w_gate_up, w_down) = p
    b, ts, d = x.shape
    n = b * ts
    past = page_table.shape[1] * PAGE_SIZE
    n_buf = state_win.shape[1]
    prev16 = jnp.pad(state_pool[0], ((0, 0), (POOL_HALO - POOL_STATE, 0), (0, 0)))
    h, new_pool = _pool_layer(x, prev16, norm_mix[0], pool_w[0], pool_scale[0], tt=ts, clip=False)
    h = _moe(h.reshape(n, d), norm_ffn[0], _router_weights(router_group[0], router_expert[0]),
             w_gate_up, w_down, 0, tm=n)
    cos_t, sin_t = _rope_tables_transposed(past + jnp.arange(ts))
    cos_t = jnp.tile(cos_t, (1, b))
    sin_t = jnp.tile(sin_t, (1, b))
    kv_t, win_t, _, _, _, _, _ = _kvproj(h.reshape(1, n, d), kv_norm, w_kv, k_norm, cos_t, sin_t, tt=n)
    kv_rows = kv_t[0].T
    win_rows = win_t[0].T

    cache5 = cache_kv.transpose(0, 2, 3, 4, 1)
    ab = _cmp_ab_pages(cache5, page_table, _cmp_weights(cmp_w1), n_pages=n_pages)
    ck, cv = _cmp_finish(ab, cmp_pe, cmp_w1, cmp_b1, cmp_w2, k_norm[0])
    n_chunk = ck.shape[2]
    ck_t = ck.transpose(0, 1, 3, 2).reshape(b, KV_WIDTH, n_chunk)
    cv_t = cv.transpose(0, 1, 3, 2).reshape(b, KV_WIDTH, n_chunk)

    qn_t, qr_t, g_t = _qproj(h, norm_mix[1], _qg_weights(w_qg[0]), q_norm[0], cos_t, sin_t, tt=n, pos_blocks=1)

    def block_diag_queries(q_t):
        q5 = q_t.reshape(N_KV_HEADS, HEADS_PER_KV, HEAD_DIM, b, ts).transpose(3, 1, 0, 4, 2)
        eye = jnp.eye(N_KV_HEADS, dtype=q_t.dtype)
        qbd = q5[:, :, :, :, None, :] * eye[None, None, :, None, :, None]
        return qbd.reshape(b, HEADS_PER_KV * N_KV_HEADS * ts, KV_WIDTH)

    gates = g_t.reshape(N_KV_HEADS, GATE_ROWS, b, ts)[:, :3 * HEADS_PER_KV]
    gates = gates.reshape(N_KV_HEADS, 3, HEADS_PER_KV, b, ts).transpose(3, 1, 2, 0, 4)
    gates = jnp.broadcast_to(gates.reshape(b, 3, HEADS_PER_KV * N_KV_HEADS * ts, 1),
                             (b, 3, HEADS_PER_KV * N_KV_HEADS * ts, HEAD_DIM))

    def new_rows(rows2):
        return jnp.pad(rows2.reshape(b, ts, KV_WIDTH), ((0, 0), (0, LANES - ts), (0, 0))).astype(BF16)

    win4 = state_win.transpose(0, 2, 3, 4, 1).reshape(b, 2, KV_WIDTH, n_buf)
    o = _attn_sample(cache5, page_table, block_diag_queries(qn_t), block_diag_queries(qr_t), ck_t, cv_t, win4,
                     new_rows(kv_rows[:, 2 * KV_WIDTH:3 * KV_WIDTH]), new_rows(kv_rows[:, 3 * KV_WIDTH:]),
                     new_rows(win_rows[:, :KV_WIDTH]), new_rows(win_rows[:, KV_WIDTH:]), gates,
                     n_pages=n_pages, ts=ts, past=past)
    o = o.reshape(b, HEADS_PER_KV, N_KV_HEADS, ts, HEAD_DIM).transpose(0, 3, 2, 1, 4).reshape(n, N_HEADS * HEAD_DIM)
    h = _oproj_n(o, h, w_o[0])
    h = _moe(h, norm_ffn[1], _router_weights(router_group[1], router_expert[1]),
             w_gate_up, w_down, 1, tm=n)
    kv_new = kv_rows.reshape(b, ts, 4, N_KV_HEADS, HEAD_DIM)
    win_new = jnp.concatenate([state_win, win_rows.reshape(b, ts, 2, N_KV_HEADS, HEAD_DIM)], axis=1)[:, -n_buf:]
    return h.reshape(b, ts, d), new_pool[None], kv_new, win_new


def kernel(x_prompt, x_sample, state_pool, cache_kv, page_table, state_win, norm_mix, norm_ffn, pool_w, pool_scale, kv_norm, w_kv, k_norm, cmp_pe, cmp_w1, cmp_b1, cmp_w2, w_qg, q_norm, w_o, router_group, router_expert, w_gate_up, w_down):
    params = (norm_mix, norm_ffn, pool_w, pool_scale, kv_norm, w_kv, k_norm, cmp_pe, cmp_w1, cmp_b1,
              cmp_w2, w_qg, q_norm, w_o, router_group, router_expert, _cast_bf16(w_gate_up), _cast_bf16(w_down))
    y_p, pool_p, kv_p, win_p = _trunk_prompt(x_prompt, params)
    y_s, pool_s, kv_s, win_s = _trunk_sample(x_sample, state_pool, cache_kv, page_table, state_win, params)
    return y_p, y_s, pool_p, pool_s, kv_p, kv_s, win_p, win_s
```

```python
import functools

import jax
import jax.numpy as jnp
from jax import lax
from jax.experimental import pallas as pl
from jax.experimental.pallas import tpu as pltpu

F32 = jnp.float32
BF16 = jnp.bfloat16
I32 = jnp.int32
HIGHEST = lax.Precision.HIGHEST

D_MODEL = 1024
POOL_WINDOWS = (2, 4, 8, 16)
POOL_GROUP_DIM = D_MODEL // len(POOL_WINDOWS)
POOL_STATE = max(POOL_WINDOWS) - 1
POOL_HALO = 16
N_HEADS = 16
HEAD_DIM = 64
HALF_DIM = HEAD_DIM // 2
N_KV_HEADS = 4
HEADS_PER_KV = N_HEADS // N_KV_HEADS
KV_WIDTH = N_KV_HEADS * HEAD_DIM
CMP_BLOCK = 32
CMP_STRIDE = 16
CMP_HIDDEN = 2 * HEAD_DIM
SEL_BLOCK = 64
N_SEL = 16
WINDOW = 512
PAGE_SIZE = 128
ROPE_THETA = 10000.0
SCALE = HEAD_DIM ** -0.5
N_GROUPS = 4
EXPERTS_PER_GROUP = 4
N_EXPERTS = N_GROUPS * EXPERTS_PER_GROUP
D_EXPERT = 512
EPS = 1e-6
NEG = -1e30
TINY = 1e-30
BIG = 1e4

LANES = 128
GATE_ROWS = 16
ROUTE_GROUP_ROW = 3 * N_EXPERTS
MOE_SEG_ALIGN = 16
MOE_CHUNK = 144
MOE_STEP_EXPERTS = 2
V_PAD_ROWS = 16
VMEM_LIMIT = 56 * 1024 * 1024


def _params(sem, vmem=None):
    return pltpu.CompilerParams(dimension_semantics=sem, vmem_limit_bytes=vmem)


def _rms(x, g):
    return x * lax.rsqrt(jnp.mean(x * x, axis=-1, keepdims=True) + EPS) * g


def _nt_dot(a, b, precision=None):
    return lax.dot_general(a, b, (((1,), (1,)), ((), ())), precision=precision,
                           preferred_element_type=F32)


def _dot(a, b, precision=None):
    return jnp.dot(a, b, precision=precision, preferred_element_type=F32)


def _sigmoid(x):
    return 1.0 / (1.0 + jnp.exp(-x))


def _pool_kernel(h_ref, prev_ref, g_ref, w_ref, sc_ref, o_ref, np_ref, ext_ref, lvl_ref, *, tt, clip):
    t = pl.program_id(1)
    x = h_ref[0]
    xn = _rms(x, g_ref[...])

    @pl.when(t == 0)
    def _():
        ext_ref[0:POOL_HALO, :] = prev_ref[0]

    @pl.when(t > 0)
    def _():
        ext_ref[0:POOL_HALO, :] = ext_ref[tt:tt + POOL_HALO, :]

    ext_ref[POOL_HALO:POOL_HALO + tt, :] = xn
    if clip:
        tpos = t * tt + lax.broadcasted_iota(I32, (tt, 1), 0)
    outs = []
    for gi, w in enumerate(POOL_WINDOWS):
        lo = gi * POOL_GROUP_DIM
        hi = lo + POOL_GROUP_DIM
        n = tt + POOL_HALO
        lvl_ref[...] = ext_ref[:, lo:hi]
        span = 1
        while span < w:
            lvl_ref[span:n, :] = lvl_ref[span:n, :] + lvl_ref[0:n - span, :]
            span *= 2
        acc = lvl_ref[POOL_HALO:n, :]
        if clip:
            mean = acc / jnp.minimum(tpos + 1, w).astype(F32)
        else:
            mean = acc * (1.0 / w)
        d = (mean - xn[:, lo:hi]).astype(BF16)
        outs.append(_dot(d, w_ref[gi].astype(BF16)))
    o_ref[0] = x + jnp.concatenate(outs, axis=1) * sc_ref[...]

    @pl.when(t == pl.num_programs(1) - 1)
    def _():
        np_ref[0] = ext_ref[tt + POOL_HALO - POOL_STATE:tt + POOL_HALO, :]


def _pool_layer(h, prev16, gain, w_pool, scale, *, tt, clip):
    b, t, d = h.shape
    return pl.pallas_call(
        functools.partial(_pool_kernel, tt=tt, clip=clip),
        grid=(b, t // tt),
        in_specs=[
            pl.BlockSpec((1, tt, d), lambda i, j: (i, j, 0)),
            pl.BlockSpec((1, POOL_HALO, d), lambda i, j: (i, 0, 0)),
            pl.BlockSpec((1, d), lambda i, j: (0, 0)),
            pl.BlockSpec(w_pool.shape, lambda i, j: (0, 0, 0)),
            pl.BlockSpec((1, d), lambda i, j: (0, 0)),
        ],
        out_specs=[
            pl.BlockSpec((1, tt, d), lambda i, j: (i, j, 0)),
            pl.BlockSpec((1, POOL_STATE, d), lambda i, j: (i, 0, 0)),
        ],
        out_shape=[jax.ShapeDtypeStruct((b, t, d), F32),
                   jax.ShapeDtypeStruct((b, POOL_STATE, d), F32)],
        scratch_shapes=[pltpu.VMEM((tt + POOL_HALO, d), F32), pltpu.VMEM((tt + POOL_HALO, POOL_GROUP_DIM), F32)],
        compiler_params=_params(("arbitrary", "arbitrary")),
        name="pool_layer",
    )(h, prev16, gain.reshape(1, d), w_pool, scale.reshape(1, d))


def _softmax_rows(rows):
    m = functools.reduce(jnp.maximum, rows)
    es = [jnp.exp(r - m) for r in rows]
    s = functools.reduce(lambda a, b: a + b, es)
    return [e / s for e in es]


def _router_kernel(h_ref, g_ref, rt_ref, xn_ref, comb_ref, grp_ref):
    xn = _rms(h_ref[...], g_ref[...])
    x_hi = xn.astype(BF16)
    xn_ref[...] = x_hi
    x_lo = (xn - x_hi.astype(F32)).astype(BF16)
    r_hi = rt_ref[...].astype(BF16)
    r_lo = (rt_ref[...] - r_hi.astype(F32)).astype(BF16)
    lt = _nt_dot(r_hi, x_hi) + (_nt_dot(r_hi, x_lo) + _nt_dot(r_lo, x_hi))
    pg = _softmax_rows([lt[i:i + 1, :] for i in range(N_GROUPS)])
    g_val = functools.reduce(jnp.maximum, pg)
    g_idx = jnp.full(g_val.shape, N_GROUPS - 1, I32)
    for i in range(N_GROUPS - 2, -1, -1):
        g_idx = jnp.where(pg[i] == g_val, i, g_idx)
    le = []
    for j in range(EXPERTS_PER_GROUP):
        v = lt[N_GROUPS + (N_GROUPS - 1) * EXPERTS_PER_GROUP + j:N_GROUPS + (N_GROUPS - 1) * EXPERTS_PER_GROUP + j + 1, :]
        for gi in range(N_GROUPS - 2, -1, -1):
            r = N_GROUPS + gi * EXPERTS_PER_GROUP + j
            v = jnp.where(g_idx == gi, lt[r:r + 1, :], v)
        le.append(v)
    pe = _softmax_rows(le)
    ranks = []
    for j in range(EXPERTS_PER_GROUP):
        r = jnp.zeros(g_val.shape, I32)
        for i in range(EXPERTS_PER_GROUP):
            if i == j:
                continue
            beats = (pe[i] > pe[j]) | (pe[i] == pe[j]) if i < j else (pe[i] > pe[j])
            r = r + beats.astype(I32)
        ranks.append(r)
    vals, idxs = [], []
    for k in range(2):
        v = jnp.zeros(g_val.shape, F32)
        ix = jnp.zeros(g_val.shape, I32)
        for j in range(EXPERTS_PER_GROUP):
            hit = ranks[j] == k
            v = jnp.where(hit, pe[j], v)
            ix = jnp.where(hit, j, ix)
        vals.append(v)
        idxs.append(ix)
    tot = vals[0] + vals[1]
    erow = lax.broadcasted_iota(I32, (LANES, g_val.shape[1]), 0)
    comb_t = jnp.where(erow == ROUTE_GROUP_ROW + g_idx, 1.0, 0.0)
    for k in range(2):
        wk = g_val * (vals[k] / tot)
        w_hi = wk.astype(BF16).astype(F32)
        w_mid = (wk - w_hi).astype(BF16).astype(F32)
        w_lo = (wk - w_hi - w_mid).astype(BF16).astype(F32)
        eid = g_idx * EXPERTS_PER_GROUP + idxs[k]
        for part, term in enumerate((w_hi, w_mid, w_lo)):
            comb_t = comb_t + jnp.where(erow == part * N_EXPERTS + eid, term, 0.0)
    comb_ref[...] = comb_t.T
    grp_ref[...] = comb_t[ROUTE_GROUP_ROW:ROUTE_GROUP_ROW + 8]


def _router(h2, gain, rt, *, tm):
    n, d = h2.shape
    return pl.pallas_call(
        _router_kernel,
        grid=(n // tm,),
        in_specs=[
            pl.BlockSpec((tm, d), lambda i: (i, 0)),
            pl.BlockSpec((1, d), lambda i: (0, 0)),
            pl.BlockSpec(rt.shape, lambda i: (0, 0)),
        ],
        out_specs=[
            pl.BlockSpec((tm, d), lambda i: (i, 0)),
            pl.BlockSpec((tm, LANES), lambda i: (i, 0)),
            pl.BlockSpec((8, tm), lambda i: (0, i)),
        ],
        out_shape=[jax.ShapeDtypeStruct((n, d), BF16), jax.ShapeDtypeStruct((n, LANES), F32),
                   jax.ShapeDtypeStruct((8, n), F32)],
        compiler_params=_params(("arbitrary",)),
        name="moe_router",
    )(h2, gain.reshape(1, d), rt)


def _cast_kernel(x_ref, o_ref):
    o_ref[...] = x_ref[...].astype(o_ref.dtype)


def _cast_bf16(w):
    blk = (1, 1) + w.shape[2:]
    return pl.pallas_call(
        _cast_kernel,
        grid=w.shape[:2],
        in_specs=[pl.BlockSpec(blk, lambda i, j: (i, j, 0, 0))],
        out_specs=pl.BlockSpec(blk, lambda i, j: (i, j, 0, 0)),
        out_shape=jax.ShapeDtypeStruct(w.shape, BF16),
        compiler_params=_params(("arbitrary", "arbitrary")),
        name="cast_weights",
    )(w)


def _moe_kernel(cnt_ref, base_ref, x_ref, comb_ref, grp_ref, h_ref, wgu_ref, wd_ref, o_ref,
                pt_s, z_s, y_s, cs_s, slot_s, *, r_max, unroll):
    w = pl.program_id(0)
    step = pl.program_id(1)
    nw = x_ref.shape[0]
    g = step // (EXPERTS_PER_GROUP // MOE_STEP_EXPERTS)

    @pl.when(step == 0)
    def _():
        lane = lax.broadcasted_iota(I32, (1, LANES), 1)
        in_grp = (lane >= ROUTE_GROUP_ROW) & (lane < ROUTE_GROUP_ROW + N_GROUPS)
        oh = jnp.where(in_grp, comb_ref[...], 0.0)
        base_row = jnp.zeros((1, LANES), F32)
        for gi in range(N_GROUPS):
            base_row = jnp.where(lane == ROUTE_GROUP_ROW + gi, base_ref[w, gi].astype(F32), base_row)
        ri = lax.broadcasted_iota(I32, (nw, nw), 0)
        ci = lax.broadcasted_iota(I32, (nw, nw), 1)
        before = _dot(jnp.where(ci < ri, 1.0, 0.0).astype(BF16), oh.astype(BF16))
        slot_s[...] = jnp.sum(oh * (before + base_row), axis=-1, keepdims=True)
        oh_t = grp_ref[...]
        before_t = _dot(oh_t.astype(BF16), jnp.where(ri < ci, 1.0, 0.0).astype(BF16))
        slot_row = jnp.zeros((1, nw), F32)
        for gi in range(N_GROUPS):
            slot_row = slot_row + oh_t[gi:gi + 1] * (before_t[gi:gi + 1] + base_ref[w, gi].astype(F32))
        rows = lax.broadcasted_iota(I32, (r_max, 1), 0)
        pt_s[...] = jnp.where(rows == slot_row.astype(I32), 1.0, 0.0).astype(BF16)
        z_s[...] = _dot(pt_s[...], x_ref[...]).astype(BF16)
        cs_s[...] = _dot(pt_s[...], comb_ref[...].astype(BF16))
        y_s[...] = jnp.zeros(y_s.shape, F32)

    seg0 = base_ref[w, g]
    seg1 = seg0 + cnt_ref[w, g]
    lane_c = lax.broadcasted_iota(I32, (MOE_CHUNK, LANES), 1)
    in_terms = lane_c < 3 * N_EXPERTS

    def chunk(lo):
        r0 = pl.multiple_of(jnp.minimum(lo, r_max - MOE_CHUNK), MOE_SEG_ALIGN)
        z = z_s[pl.ds(r0, MOE_CHUNK), :]
        cs = cs_s[pl.ds(r0, MOE_CHUNK), :]
        acc = None
        for k in range(MOE_STEP_EXPERTS):
            gu = _dot(z, wgu_ref[0, k])
            a = gu[:, :D_EXPERT]
            hdn = (a * _sigmoid(a) * gu[:, D_EXPERT:]).astype(BF16)
            y = _dot(hdn, wd_ref[0, k])
            is_e = in_terms & (lane_c % N_EXPERTS == step * MOE_STEP_EXPERTS + k)
            wgt = jnp.sum(jnp.where(is_e, cs, 0.0), axis=-1, keepdims=True)
            acc = wgt * y if acc is None else acc + wgt * y
        rows = r0 + lax.broadcasted_iota(I32, (MOE_CHUNK, 1), 0)
        y_s[pl.ds(r0, MOE_CHUNK), :] += jnp.where((rows >= lo) & (rows < seg1), acc, 0.0)

    def chunks(c, carry):
        for u in range(unroll):
            chunk(seg0 + (c * unroll + u) * MOE_CHUNK)
        return carry

    span = unroll * MOE_CHUNK
    lax.fori_loop(0, (cnt_ref[w, g] + span - 1) // span, chunks, 0)

    @pl.when(step == pl.num_programs(1) - 1)
    def _():
        cols = lax.broadcasted_iota(I32, (1, r_max), 1)
        p = jnp.where(cols == slot_s[...].astype(I32), 1.0, 0.0).astype(BF16)
        o_ref[...] = h_ref[...] + _dot(p, y_s[...].astype(BF16))


def _moe(h2, gain, rt, w_gu, w_down, layer, *, tm):
    n, d = h2.shape
    if w_gu.dtype != BF16:
        w_gu, w_down = _cast_bf16(w_gu), _cast_bf16(w_down)
    xn, comb, grp = _router(h2, gain, rt, tm=tm)
    n_win = n // tm
    cnt = jnp.sum(grp[:N_GROUPS].reshape(N_GROUPS, n_win, tm), axis=-1).T.astype(I32)
    seg = (cnt + MOE_SEG_ALIGN - 1) // MOE_SEG_ALIGN * MOE_SEG_ALIGN
    base = jnp.cumsum(seg, axis=1) - seg
    r_max = -(-(tm + N_GROUPS * MOE_SEG_ALIGN + MOE_CHUNK) // LANES) * LANES
    return pl.pallas_call(
        functools.partial(_moe_kernel, r_max=r_max, unroll=max(1, -(-tm // (N_GROUPS * MOE_CHUNK)))),
        grid_spec=pltpu.PrefetchScalarGridSpec(
            num_scalar_prefetch=2,
            grid=(n_win, N_EXPERTS // MOE_STEP_EXPERTS),
            in_specs=[
                pl.BlockSpec((tm, d), lambda i, e, c, b: (i, 0)),
                pl.BlockSpec((tm, LANES), lambda i, e, c, b: (i, 0)),
                pl.BlockSpec((8, tm), lambda i, e, c, b: (0, i)),
                pl.BlockSpec((tm, d), lambda i, e, c, b: (i, 0)),
                pl.BlockSpec((1, MOE_STEP_EXPERTS, d, 2 * D_EXPERT), lambda i, e, c, b: (layer, e, 0, 0)),
                pl.BlockSpec((1, MOE_STEP_EXPERTS, D_EXPERT, d), lambda i, e, c, b: (layer, e, 0, 0)),
            ],
            out_specs=pl.BlockSpec((tm, d), lambda i, e, c, b: (i, 0)),
            scratch_shapes=[
                pltpu.VMEM((r_max, tm), BF16), pltpu.VMEM((r_max, d), BF16), pltpu.VMEM((r_max, d), F32),
                pltpu.VMEM((r_max, LANES), F32), pltpu.VMEM((tm, 1), F32),
            ],
        ),
        out_shape=jax.ShapeDtypeStruct((n, d), F32),
        compiler_params=_params(("arbitrary", "arbitrary"), VMEM_LIMIT),
        name="moe_experts",
    )(cnt, base, xn, comb, grp, h2, w_gu, w_down)


def _head_norm_rope_t(z, gain, c, s):
    outs = []
    for g in range(z.shape[0] // HEAD_DIM):
        zh = z[g * HEAD_DIM:(g + 1) * HEAD_DIM]
        zn = zh * lax.rsqrt(jnp.mean(zh * zh, axis=0, keepdims=True) + EPS) * gain
        x1 = zn[:HALF_DIM]
        x2 = zn[HALF_DIM:]
        outs.append(x1 * c - x2 * s)
        outs.append(x2 * c + x1 * s)
    return jnp.concatenate(outs, axis=0)


def _kvproj_kernel(h_ref, g_ref, wt_ref, gsel_ref, gwin_ref, cos_ref, sin_ref,
                   kvt_ref, wint_ref, craw_ref, ksel_ref, kwin_ref, vselt_ref, vwint_ref, w_s):
    @pl.when((pl.program_id(0) == 0) & (pl.program_id(1) == 0))
    def _():
        w_s[...] = wt_ref[...].astype(BF16)

    xn = _rms(h_ref[0], g_ref[...]).astype(BF16)
    kvt = _nt_dot(w_s[...], xn)
    tt = xn.shape[0]
    c = cos_ref[...]
    s = sin_ref[...]
    gsel = jnp.concatenate([gsel_ref[...]] * (tt // LANES), axis=1)
    gwin = jnp.concatenate([gwin_ref[...]] * (tt // LANES), axis=1)
    ksel = _head_norm_rope_t(kvt[2 * KV_WIDTH:3 * KV_WIDTH], gsel, c, s)
    kwin = _head_norm_rope_t(kvt[4 * KV_WIDTH:5 * KV_WIDTH], gwin, c, s)
    kvt_ref[0] = jnp.concatenate([kvt[:2 * KV_WIDTH], ksel, kvt[3 * KV_WIDTH:4 * KV_WIDTH]], axis=0)
    wint_ref[0] = jnp.concatenate([kwin, kvt[5 * KV_WIDTH:]], axis=0)
    vselt_ref[...] = kvt[3 * KV_WIDTH:4 * KV_WIDTH].astype(BF16)
    vwint_ref[...] = kvt[5 * KV_WIDTH:].astype(BF16)
    raw = kvt[:2 * KV_WIDTH].T
    for k in range(2 * KV_WIDTH // LANES):
        craw_ref[k] = raw[:, k * LANES:(k + 1) * LANES]
    ksel_n = ksel.T
    kwin_n = kwin.T
    for g in range(N_KV_HEADS):
        ksel_ref[g] = ksel_n[:, g * HEAD_DIM:(g + 1) * HEAD_DIM].astype(BF16)
        kwin_ref[g] = kwin_n[:, g * HEAD_DIM:(g + 1) * HEAD_DIM].astype(BF16)


def _kvproj(x3, gain, w_kv, k_norm, cos_tt, sin_tt, *, tt):
    b, t, d = x3.shape
    n = b * t
    nt = t // tt
    wt = w_kv.T
    gsel = jnp.broadcast_to(k_norm[1][:, None], (HEAD_DIM, LANES))
    gwin = jnp.broadcast_to(k_norm[2][:, None], (HEAD_DIM, LANES))
    full = lambda a: pl.BlockSpec(a.shape, lambda i, j: (0,) * a.ndim)
    return pl.pallas_call(
        _kvproj_kernel,
        grid=(b, nt),
        in_specs=[
            pl.BlockSpec((1, tt, d), lambda i, j: (i, j, 0)),
            pl.BlockSpec((1, d), lambda i, j: (0, 0)),
            full(wt), full(gsel), full(gwin),
            pl.BlockSpec((HALF_DIM, tt), lambda i, j: (0, j)),
            pl.BlockSpec((HALF_DIM, tt), lambda i, j: (0, j)),
        ],
        out_specs=[
            pl.BlockSpec((1, 4 * KV_WIDTH, tt), lambda i, j: (i, 0, j)),
            pl.BlockSpec((1, 2 * KV_WIDTH, tt), lambda i, j: (i, 0, j)),
            pl.BlockSpec((2 * KV_WIDTH // LANES, tt, LANES), lambda i, j: (0, i * nt + j, 0)),
            pl.BlockSpec((N_KV_HEADS, tt, HEAD_DIM), lambda i, j: (0, i * nt + j, 0)),
            pl.BlockSpec((N_KV_HEADS, tt, HEAD_DIM), lambda i, j: (0, i * nt + j, 0)),
            pl.BlockSpec((KV_WIDTH, tt), lambda i, j: (0, i * nt + j)),
            pl.BlockSpec((KV_WIDTH, tt), lambda i, j: (0, i * nt + j)),
        ],
        out_shape=[
            jax.ShapeDtypeStruct((b, 4 * KV_WIDTH, t), F32),
            jax.ShapeDtypeStruct((b, 2 * KV_WIDTH, t), F32),
            jax.ShapeDtypeStruct((2 * KV_WIDTH // LANES, n, LANES), F32),
            jax.ShapeDtypeStruct((N_KV_HEADS, n, HEAD_DIM), BF16),
            jax.ShapeDtypeStruct((N_KV_HEADS, n, HEAD_DIM), BF16),
            jax.ShapeDtypeStruct((KV_WIDTH, n), BF16),
            jax.ShapeDtypeStruct((KV_WIDTH, n), BF16),
        ],
        scratch_shapes=[pltpu.VMEM(wt.shape, BF16)],
        compiler_params=_params(("arbitrary", "arbitrary"), VMEM_LIMIT),
        name="kv_proj",
    )(x3, gain.reshape(1, d), wt, gsel, gwin, cos_tt, sin_tt)


def _rope_tables_transposed(pos):
    inv = 1.0 / (ROPE_THETA ** (jnp.arange(HALF_DIM, dtype=F32) * (2.0 / HEAD_DIM)))
    ang = pos.astype(F32)[:, None] * inv[None, :]
    return jnp.cos(ang).T, jnp.sin(ang).T


def _cmp_ab_accumulate(load_rows, wab_ref, n_rows):
    del n_rows
    heads_per_chunk = LANES // HEAD_DIM
    accs = [[None] * N_KV_HEADS for _ in range(2)]
    for s in range(2):
        for c in range(KV_WIDTH // LANES):
            plane = s * (KV_WIDTH // LANES) + c
            lhs = jnp.concatenate([load_rows(r, plane).astype(BF16) for r in range(CMP_STRIDE)], axis=1)
            out = _dot(lhs, wab_ref[s])
            for k in range(heads_per_chunk):
                accs[s][c * heads_per_chunk + k] = out[:, k * 2 * CMP_HIDDEN:(k + 1) * 2 * CMP_HIDDEN]
    return accs


def _cmp_finish_rows(load_ab, n_row, pe_ref, w1_ref, b1_ref, w2_ref, gk_ref, ck_ref, cv_ref):
    for s in range(2):
        bias = _dot(pe_ref[s].astype(BF16), w1_ref[s].astype(BF16)) + b1_ref[s]
        w2 = w2_ref[s].astype(BF16)
        for g in range(N_KV_HEADS):
            ab = load_ab(s, g)
            hid = ab[:, :CMP_HIDDEN] + pltpu.roll(ab[:, CMP_HIDDEN:], n_row - 1, 0) + bias
            cdf = 0.5 * (1.0 + jnp.tanh(0.7978845608028654 * (hid + 0.044715 * (hid * hid * hid))))
            out = _dot((hid * cdf).astype(BF16), w2)
            if s == 0:
                ck_ref[0, g] = _rms(out, gk_ref[...]).astype(BF16)
            else:
                cv_ref[0, g] = out.astype(BF16)


def _cmp_prompt_kernel(craw_ref, wab_ref, pe_ref, w1_ref, b1_ref, w2_ref, gk_ref, ck_ref, cv_ref, *, n_chunk):
    accs = _cmp_ab_accumulate(lambda r, c: craw_ref[c, pl.ds(r, n_chunk, stride=CMP_STRIDE), :], wab_ref, n_chunk)
    _cmp_finish_rows(lambda s, g: accs[s][g], n_chunk, pe_ref, w1_ref, b1_ref, w2_ref, gk_ref, ck_ref, cv_ref)


def _cmp_pages_kernel(pt_ref, *refs, n_pages):
    del pt_ref
    page_refs = refs[:n_pages]
    wab_ref, pe_ref, w1_ref, b1_ref, w2_ref, gk_ref, ck_ref, cv_ref, craw_s, ab_s = refs[n_pages:]
    q = pl.program_id(1)
    pairs = KV_WIDTH // LANES
    for j in range(n_pages):
        for s in range(2):
            for k in range(pairs):
                tile = jnp.concatenate([page_refs[j][0, s, 2 * k], page_refs[j][0, s, 2 * k + 1]], axis=0)
                craw_s[s * pairs + k, j * PAGE_SIZE:(j + 1) * PAGE_SIZE, :] = tile.T
    n_chunk = n_pages * PAGE_SIZE // CMP_STRIDE
    accs = _cmp_ab_accumulate(lambda r, c: craw_s[c, pl.ds(r, n_chunk, stride=CMP_STRIDE), :], wab_ref, n_chunk)
    r0 = pl.multiple_of(q * n_chunk, n_chunk)
    for s in range(2):
        for g in range(N_KV_HEADS):
            ab_s[s, g, pl.ds(r0, n_chunk), :] = accs[s][g]

    @pl.when(q == pl.num_programs(1) - 1)
    def _():
        _cmp_finish_rows(lambda s, g: ab_s[s, g], ab_s.shape[2], pe_ref, w1_ref, b1_ref, w2_ref, gk_ref,
                         ck_ref, cv_ref)


def _cmp_small_operands(cmp_pe, cmp_w1, cmp_b1, cmp_w2, gk):
    return (cmp_pe.reshape(2, 1, CMP_BLOCK * HEAD_DIM), cmp_w1, cmp_b1.reshape(2, 1, CMP_HIDDEN), cmp_w2,
            gk.reshape(1, HEAD_DIM))


def _compress_pages(cache5, page_table, cmp_pe, cmp_w1, cmp_b1, cmp_w2, gk, *, n_pages):
    b, pages_per_seq = page_table.shape
    steps = pages_per_seq // n_pages
    n_row = pages_per_seq * PAGE_SIZE // CMP_STRIDE
    wab = _cmp_weights(cmp_w1)
    small = _cmp_small_operands(cmp_pe, cmp_w1, cmp_b1, cmp_w2, gk)

    def page_spec(j):
        return pl.BlockSpec((1, 2, N_KV_HEADS, HEAD_DIM, PAGE_SIZE),
                            lambda i, q, pt: (pt[i, q * n_pages + j], 0, 0, 0, 0))

    full = lambda a: pl.BlockSpec(a.shape, lambda i, q, pt: (0,) * a.ndim)
    out_spec = pl.BlockSpec((1, N_KV_HEADS, n_row, HEAD_DIM), lambda i, q, pt: (i, 0, 0, 0))
    return pl.pallas_call(
        functools.partial(_cmp_pages_kernel, n_pages=n_pages),
        grid_spec=pltpu.PrefetchScalarGridSpec(
            num_scalar_prefetch=1,
            grid=(b, steps),
            in_specs=[page_spec(j) for j in range(n_pages)] + [full(wab)] + [full(a) for a in small],
            out_specs=[out_spec, out_spec],
            scratch_shapes=[pltpu.VMEM((2 * KV_WIDTH // LANES, n_pages * PAGE_SIZE, LANES), F32),
                            pltpu.VMEM((2, N_KV_HEADS, n_row, 2 * CMP_HIDDEN), F32)],
        ),
        out_shape=[jax.ShapeDtypeStruct((b, N_KV_HEADS, n_row, HEAD_DIM), BF16)] * 2,
        compiler_params=_params(("arbitrary", "arbitrary"), VMEM_LIMIT),
        name="cmp_pages",
    )(page_table, *([cache5] * n_pages), wab, *small)


def _cmp_weights(cmp_w1):
    w = cmp_w1.reshape(2, 2, CMP_STRIDE, HEAD_DIM, CMP_HIDDEN)
    w = w.transpose(0, 2, 3, 1, 4).reshape(2, CMP_STRIDE, HEAD_DIM, 2 * CMP_HIDDEN)
    eye = jnp.eye(LANES // HEAD_DIM, dtype=w.dtype)
    w = w[:, :, None, :, None, :] * eye[None, None, :, None, :, None]
    return w.reshape(2, CMP_STRIDE * LANES, (LANES // HEAD_DIM) * 2 * CMP_HIDDEN).astype(BF16)


def _compress_prompt(craw, cmp_pe, cmp_w1, cmp_b1, cmp_w2, gk, *, b, t):
    n_chunk = t // CMP_STRIDE
    wab = _cmp_weights(cmp_w1)
    small = _cmp_small_operands(cmp_pe, cmp_w1, cmp_b1, cmp_w2, gk)
    full = lambda a: pl.BlockSpec(a.shape, lambda i: (0,) * a.ndim)
    out_spec = pl.BlockSpec((1, N_KV_HEADS, n_chunk, HEAD_DIM), lambda i: (i, 0, 0, 0))
    return pl.pallas_call(
        functools.partial(_cmp_prompt_kernel, n_chunk=n_chunk),
        grid=(b,),
        in_specs=[pl.BlockSpec((craw.shape[0], t, LANES), lambda i: (0, i, 0)), full(wab)]
        + [full(a) for a in small],
        out_specs=[out_spec, out_spec],
        out_shape=[jax.ShapeDtypeStruct((b, N_KV_HEADS, n_chunk, HEAD_DIM), BF16)] * 2,
        compiler_params=_params(("arbitrary",), VMEM_LIMIT),
        name="cmp_prompt",
    )(craw, wab, *small)


def _qproj_kernel(h_ref, g_ref, wt_ref, gq_ref, cos_ref, sin_ref, qn_ref, qr_ref, gt_ref, w_s):
    @pl.when(pl.program_id(0) == 0)
    def _():
        w_s[...] = wt_ref[...].astype(BF16)

    xn = _rms(h_ref[...], g_ref[...]).astype(BF16)
    qg = _nt_dot(w_s[...], xn)
    tt = xn.shape[0]
    gq = jnp.concatenate([gq_ref[...]] * (tt // LANES), axis=1)
    c = cos_ref[...]
    s = sin_ref[...]
    for h in range(N_HEADS):
        qh = qg[h * HEAD_DIM:(h + 1) * HEAD_DIM]
        qn = qh * lax.rsqrt(jnp.mean(qh * qh, axis=0, keepdims=True) + EPS) * gq
        qn_ref[h * HEAD_DIM:(h + 1) * HEAD_DIM, :] = (qn * SCALE).astype(BF16)
        x1 = qn[:HALF_DIM]
        x2 = qn[HALF_DIM:]
        qr_ref[h * HEAD_DIM:h * HEAD_DIM + HALF_DIM, :] = ((x1 * c - x2 * s) * SCALE).astype(BF16)
        qr_ref[h * HEAD_DIM + HALF_DIM:(h + 1) * HEAD_DIM, :] = ((x2 * c + x1 * s) * SCALE).astype(BF16)
    gt_ref[...] = _sigmoid(qg[N_HEADS * HEAD_DIM:])


def _qg_weights(w_qg):
    nq = N_HEADS * HEAD_DIM
    gates = w_qg[:, nq:].reshape(D_MODEL, 3, N_KV_HEADS, HEADS_PER_KV).transpose(2, 1, 3, 0)
    gates = gates.reshape(N_KV_HEADS, 3 * HEADS_PER_KV, D_MODEL)
    gates = jnp.pad(gates, ((0, 0), (0, GATE_ROWS - 3 * HEADS_PER_KV), (0, 0)))
    return jnp.concatenate([w_qg[:, :nq].T, gates.reshape(N_KV_HEADS * GATE_ROWS, D_MODEL)], axis=0)


def _qproj(h2, gain, wt, q_norm, cos_tt, sin_tt, *, tt, pos_blocks):
    n, d = h2.shape
    nq = N_HEADS * HEAD_DIM
    ng = N_KV_HEADS * GATE_ROWS
    gq = jnp.broadcast_to(q_norm[:, None], (HEAD_DIM, LANES))
    full = lambda a: pl.BlockSpec(a.shape, lambda i: (0,) * a.ndim)
    return pl.pallas_call(
        _qproj_kernel,
        grid=(n // tt,),
        in_specs=[
            pl.BlockSpec((tt, d), lambda i: (i, 0)),
            pl.BlockSpec((1, d), lambda i: (0, 0)),
            full(wt), full(gq),
            pl.BlockSpec((HALF_DIM, tt), lambda i: (0, i % pos_blocks)),
            pl.BlockSpec((HALF_DIM, tt), lambda i: (0, i % pos_blocks)),
        ],
        out_specs=[
            pl.BlockSpec((nq, tt), lambda i: (0, i)),
            pl.BlockSpec((nq, tt), lambda i: (0, i)),
            pl.BlockSpec((ng, tt), lambda i: (0, i)),
        ],
        out_shape=[jax.ShapeDtypeStruct((nq, n), BF16), jax.ShapeDtypeStruct((nq, n), BF16),
                   jax.ShapeDtypeStruct((ng, n), F32)],
        scratch_shapes=[pltpu.VMEM(wt.shape, BF16)],
        compiler_params=_params(("arbitrary",), VMEM_LIMIT),
        name="q_proj",
    )(h2, gain.reshape(1, d), wt, gq, cos_tt, sin_tt)


def _attn_kernel(qn_ref, qr_ref, gt_ref, ck_ref, cvt_ref, kaug_ref, kwin_ref, vaug_ref, vwaug_ref,
                 o_ref, pg_s, o_sel_s, o_win_s, *, tq, n_cmp, n_sb, n_qt):
    hp_n = HEADS_PER_KV
    qt = pl.program_id(2)
    t0 = qt * tq
    tpos = t0 + lax.broadcasted_iota(I32, (1, tq), 1)
    tpos4 = jnp.concatenate([tpos] * hp_n, axis=1)
    qn4 = jnp.concatenate([qn_ref[h * HEAD_DIM:(h + 1) * HEAD_DIM, :] for h in range(hp_n)], axis=1)
    qr4 = jnp.concatenate([qr_ref[h * HEAD_DIM:(h + 1) * HEAD_DIM, :] for h in range(hp_n)], axis=1)

    n_row = ck_ref.shape[2]
    s = _dot(ck_ref[0, 0], qn4)
    ci = lax.broadcasted_iota(I32, (n_row, 1), 0)
    vis = (ci * CMP_STRIDE + (CMP_BLOCK - 1) <= tpos4) & (ci < n_cmp)
    s = jnp.where(vis, s, NEG)
    p = jnp.where(vis, jnp.exp(s - jnp.max(s, axis=0, keepdims=True)), 0.0)
    p = p / jnp.maximum(jnp.sum(p, axis=0, keepdims=True), TINY)
    o_cmp = _dot(cvt_ref[0, 0], p.astype(BF16))
    pg = p[:, 0:tq]
    for h in range(1, hp_n):
        pg = pg + p[:, h * tq:(h + 1) * tq]

    ratio = SEL_BLOCK // CMP_STRIDE
    scores = []
    for c in range(tq // LANES):
        pg_s[c, 0:8, :] = jnp.zeros((8, LANES), F32)
        pg_s[c, 8:8 + n_row, :] = pg[:, c * LANES:(c + 1) * LANES]
        sc_c = pg_s[c, pl.ds(8 + 1 - CMP_BLOCK // CMP_STRIDE, n_sb, stride=ratio), :]
        for o in range(2 - CMP_BLOCK // CMP_STRIDE, ratio):
            sc_c = sc_c + pg_s[c, pl.ds(8 + o, n_sb, stride=ratio), :]
        scores.append(sc_c)
    score = jnp.concatenate(scores, axis=1)
    jrow = lax.broadcasted_iota(I32, (n_sb, 1), 0)
    cur = tpos // SEL_BLOCK
    forced = (jrow == 0) | (jrow == cur) | (jrow == cur - 1)
    sc = jnp.where(jrow * SEL_BLOCK <= tpos, score + jnp.where(forced, BIG, 0.0), -BIG)
    rank = jnp.zeros((n_sb, tq), I32)
    for jp in range(n_sb):
        row = sc[jp:jp + 1, :]
        beats = (row > sc) | ((row == sc) & (jrow > jp))
        rank = rank + beats.astype(I32)
    n_sbp = kaug_ref.shape[3] - HEAD_DIM
    sel_neg = jnp.where(rank < min(N_SEL, n_sb), 0.0, NEG)
    if n_sbp > n_sb:
        sel_neg = jnp.concatenate([sel_neg, jnp.zeros((n_sbp - n_sb, tq), F32)], axis=0)
    sel_neg = sel_neg.astype(BF16)
    q_rot = [qr_ref[h * HEAD_DIM:(h + 1) * HEAD_DIM, :] for h in range(hp_n)]
    q_aug = jnp.concatenate([jnp.concatenate([q, sel_neg], axis=0) for q in q_rot], axis=1)
    q_win = jnp.concatenate(q_rot, axis=1)
    causal = jnp.where(lax.broadcasted_iota(I32, (tq, 1), 0) <= lax.broadcasted_iota(I32, (1, tq), 1), 0.0, NEG)
    causal = jnp.concatenate([causal] * hp_n, axis=1)

    for k in range(n_qt):
        @pl.when(qt == k)
        def _(k=k):
            lo = k * tq
            s_diag = _dot(kaug_ref[0, 0, lo:lo + tq, :], q_aug) + causal
            m = jnp.max(s_diag, axis=0, keepdims=True)
            if k > 0:
                s_top = _dot(kaug_ref[0, 0, 0:lo, :], q_aug)
                m = jnp.maximum(m, jnp.max(s_top, axis=0, keepdims=True))
            acc = _dot(vaug_ref[:, lo:lo + tq], jnp.exp(s_diag - m).astype(BF16))
            if k > 0:
                acc = acc + _dot(vaug_ref[:, 0:lo], jnp.exp(s_top - m).astype(BF16))
            o_sel_s[...] = acc[:HEAD_DIM] / jnp.maximum(acc[HEAD_DIM:HEAD_DIM + 1], TINY)

            parts = []
            for kt in range(max(0, k - -(-WINDOW // tq)), k + 1):
                s_t = _dot(kwin_ref[0, 0, kt * tq:(kt + 1) * tq, :], q_win)
                d_lo = (k - kt) * tq - (tq - 1)
                d_hi = (k - kt) * tq + (tq - 1)
                if d_lo < 0 or d_hi >= WINDOW:
                    dq = ((k - kt) * tq + lax.broadcasted_iota(I32, (1, tq), 1)
                          - lax.broadcasted_iota(I32, (tq, 1), 0))
                    wb = jnp.where((dq >= 0) & (dq < WINDOW), 0.0, NEG)
                    s_t = s_t + jnp.concatenate([wb] * hp_n, axis=1)
                parts.append((kt, s_t))
            m_w = functools.reduce(jnp.maximum, [jnp.max(s_t, axis=0, keepdims=True) for _, s_t in parts])
            acc_w = None
            for kt, s_t in parts:
                term = _dot(vwaug_ref[:, kt * tq:(kt + 1) * tq], jnp.exp(s_t - m_w).astype(BF16))
                acc_w = term if acc_w is None else acc_w + term
            o_win_s[...] = acc_w[:HEAD_DIM] / jnp.maximum(acc_w[HEAD_DIM:HEAD_DIM + 1], TINY)

    o_sel = o_sel_s[...]
    o_win = o_win_s[...]
    gt = gt_ref[...]
    for h in range(hp_n):
        sl = slice(h * tq, (h + 1) * tq)
        o = (gt[h:h + 1] * o_cmp[:, sl] + gt[hp_n + h:hp_n + h + 1] * o_sel[:, sl]
             + gt[2 * hp_n + h:2 * hp_n + h + 1] * o_win[:, sl])
        o_ref[h * HEAD_DIM:(h + 1) * HEAD_DIM, :] = o.astype(BF16)


def _attn_prompt(qn_t, qr_t, g_t, ck, cv_t, ksel, kwin, vsel_t, vwin_t, *, b, t, tq):
    nq = t // tq
    n_sb = t // SEL_BLOCK
    n_row = ck.shape[2]
    n_cmp = t // CMP_STRIDE - CMP_BLOCK // CMP_STRIDE + 1
    n_sbp = -(-n_sb // 32) * 32
    kw = N_KV_HEADS * HEAD_DIM
    onehot = (jnp.arange(t)[:, None] // SEL_BLOCK == jnp.arange(n_sbp)[None, :]).astype(BF16)
    kaug = jnp.concatenate([ksel.reshape(N_KV_HEADS, b, t, HEAD_DIM),
                            jnp.broadcast_to(onehot, (N_KV_HEADS, b, t, n_sbp))], axis=-1)
    ones_rows = jnp.zeros((N_KV_HEADS, V_PAD_ROWS, b * t), BF16).at[:, 0].set(1.0)

    def with_ones(v_t):
        v3 = jnp.concatenate([v_t.reshape(N_KV_HEADS, HEAD_DIM, b * t), ones_rows], axis=1)
        return v3.reshape(N_KV_HEADS * (HEAD_DIM + V_PAD_ROWS), b * t)

    v_rows = HEAD_DIM + V_PAD_ROWS
    return pl.pallas_call(
        functools.partial(_attn_kernel, tq=tq, n_cmp=n_cmp, n_sb=n_sb, n_qt=nq),
        grid=(b, N_KV_HEADS, nq),
        in_specs=[
            pl.BlockSpec((kw, tq), lambda i, g, q: (g, i * nq + q)),
            pl.BlockSpec((kw, tq), lambda i, g, q: (g, i * nq + q)),
            pl.BlockSpec((GATE_ROWS, tq), lambda i, g, q: (g, i * nq + q)),
            pl.BlockSpec((1, 1, n_row, HEAD_DIM), lambda i, g, q: (i, g, 0, 0)),
            pl.BlockSpec((1, 1, HEAD_DIM, n_row), lambda i, g, q: (i, g, 0, 0)),
            pl.BlockSpec((1, 1, t, HEAD_DIM + n_sbp), lambda i, g, q: (g, i, 0, 0)),
            pl.BlockSpec((1, 1, t, HEAD_DIM), lambda i, g, q: (g, i, 0, 0)),
            pl.BlockSpec((v_rows, t), lambda i, g, q: (g, i)),
            pl.BlockSpec((v_rows, t), lambda i, g, q: (g, i)),
        ],
        out_specs=pl.BlockSpec((kw, tq), lambda i, g, q: (g, i * nq + q)),
        out_shape=jax.ShapeDtypeStruct((N_HEADS * HEAD_DIM, b * t), BF16),
        scratch_shapes=[pltpu.VMEM((tq // LANES, n_row + 8, LANES), F32),
                        pltpu.VMEM((HEAD_DIM, HEADS_PER_KV * tq), F32),
                        pltpu.VMEM((HEAD_DIM, HEADS_PER_KV * tq), F32)],
        compiler_params=_params(("arbitrary", "arbitrary", "arbitrary"), VMEM_LIMIT),
        name="attn_prompt",
    )(qn_t, qr_t, g_t, ck, cv_t, kaug, kwin.reshape(N_KV_HEADS, b, t, HEAD_DIM), with_ones(vsel_t), with_ones(vwin_t))


def _oproj_t_kernel(ot_ref, h_ref, w_ref, out_ref, w_s):
    @pl.when(pl.program_id(0) == 0)
    def _():
        w_s[...] = w_ref[...].astype(BF16)

    o = ot_ref[...].astype(F32).T.astype(BF16)
    out_ref[...] = h_ref[...] + _dot(o, w_s[...])


def _oproj_t(o_t, h2, w_o, *, tt):
    n, d = h2.shape
    return pl.pallas_call(
        _oproj_t_kernel,
        grid=(n // tt,),
        in_specs=[
            pl.BlockSpec((o_t.shape[0], tt), lambda i: (0, i)),
            pl.BlockSpec((tt, d), lambda i: (i, 0)),
            pl.BlockSpec(w_o.shape, lambda i: (0, 0)),
        ],
        out_specs=pl.BlockSpec((tt, d), lambda i: (i, 0)),
        out_shape=jax.ShapeDtypeStruct((n, d), F32),
        scratch_shapes=[pltpu.VMEM(w_o.shape, BF16)],
        compiler_params=_params(("arbitrary",), VMEM_LIMIT),
        name="o_proj",
    )(o_t, h2, w_o)


def _oproj_n_kernel(o_ref, h_ref, w_ref, out_ref):
    out_ref[...] = h_ref[...] + _dot(o_ref[...].astype(BF16), w_ref[...].astype(BF16))


def _oproj_n(o, h2, w_o):
    n, d = h2.shape
    return pl.pallas_call(
        _oproj_n_kernel,
        out_shape=jax.ShapeDtypeStruct((n, d), F32),
        compiler_params=_params((), VMEM_LIMIT),
        name="o_proj_sample",
    )(o, h2, w_o)


def _attn_sample_kernel(pt_ref, *refs, n_pages, n_cmp, n_sb, ts, past, n_buf):
    del pt_ref
    page_refs = refs[:n_pages]
    (qn_ref, qr_ref, ckt_ref, cvt_ref, win_ref, knew_ref, vnew_ref, kwnew_ref, vwnew_ref, gate_ref,
     sel_ref, e_ref, o_ref, mask_s, m_s, l_s, acc_s, ocmp_s) = refs[n_pages:]
    q_step = pl.program_id(1)
    rows = HEADS_PER_KV * N_KV_HEADS * ts
    grp_rows = N_KV_HEADS * ts
    row = lax.broadcasted_iota(I32, (rows, 1), 0)
    qpos = past + row % ts
    qr = qr_ref[0]

    def tile_rows(x):
        return jnp.concatenate([x] * HEADS_PER_KV, axis=0)

    def online_update(s, mk, v_dot):
        s = jnp.where(mk, s, NEG)
        m_new = jnp.maximum(m_s[...], jnp.max(s, axis=-1, keepdims=True))
        alpha = jnp.exp(m_s[...] - m_new)
        p = jnp.where(mk, jnp.exp(s - m_new), 0.0)
        l_s[...] = l_s[...] * alpha + jnp.sum(p, axis=-1, keepdims=True)
        acc_s[...] = acc_s[...] * alpha + v_dot(p.astype(BF16))
        m_s[...] = m_new

    @pl.when(q_step == 0)
    def _():
        n_row = ckt_ref.shape[2]
        s = _dot(qn_ref[0], ckt_ref[0])
        ci = lax.broadcasted_iota(I32, (1, n_row), 1)
        vis = (ci * CMP_STRIDE + (CMP_BLOCK - 1) <= qpos) & (ci < n_cmp)
        s = jnp.where(vis, s, NEG)
        p = jnp.where(vis, jnp.exp(s - jnp.max(s, axis=-1, keepdims=True)), 0.0)
        p = p / jnp.maximum(jnp.sum(p, axis=-1, keepdims=True), TINY)
        ocmp_s[...] = _nt_dot(p.astype(BF16), cvt_ref[0])
        pg = p[0:grp_rows]
        for h in range(1, HEADS_PER_KV):
            pg = pg + p[h * grp_rows:(h + 1) * grp_rows]
        score = _dot(pg, sel_ref[...], precision=HIGHEST)
        width = score.shape[1]
        j = lax.broadcasted_iota(I32, (1, width), 1)
        tq = qpos[0:grp_rows]
        cur = tq // SEL_BLOCK
        forced = (j == 0) | (j == cur) | (j == cur - 1)
        sc = jnp.where(j * SEL_BLOCK <= tq, score + jnp.where(forced, BIG, 0.0), -BIG)
        sc = jnp.where(j < n_sb, sc, -2.0 * BIG)
        rank = jnp.zeros((grp_rows, width), I32)
        for jp in range(n_sb):
            col = sc[:, jp:jp + 1]
            beats = (col > sc) | ((col == sc) & (j > jp))
            rank = rank + beats.astype(I32)
        sel = jnp.where((rank < min(N_SEL, n_sb)) & (j < n_sb), 1.0, 0.0).astype(BF16)
        mask_s[...] = _dot(sel, e_ref[...])
        m_s[...] = jnp.full(m_s.shape, NEG, F32)
        l_s[...] = jnp.zeros(l_s.shape, F32)
        acc_s[...] = jnp.zeros(acc_s.shape, F32)

    width = n_pages * PAGE_SIZE
    kt = jnp.concatenate([page_refs[i][0, 0].reshape(KV_WIDTH, PAGE_SIZE) for i in range(n_pages)], axis=1)
    vt = jnp.concatenate([page_refs[i][0, 1].reshape(KV_WIDTH, PAGE_SIZE) for i in range(n_pages)], axis=1)
    k0 = pl.multiple_of(q_step * width, width)
    mk = tile_rows(mask_s[:, pl.ds(k0, width)]) > 0.5
    online_update(_dot(qr, kt.astype(BF16)), mk, lambda p: _nt_dot(p, vt.astype(BF16)))

    @pl.when(q_step == pl.num_programs(1) - 1)
    def _():
        lane = lax.broadcasted_iota(I32, (1, LANES), 1)
        new_ok = (lane < ts) & (past + lane <= qpos)
        mk_new = (tile_rows(mask_s[:, past:past + LANES]) > 0.5) & new_ok
        online_update(_nt_dot(qr, knew_ref[0]), mk_new, lambda p: _dot(p, vnew_ref[0]))
        o_sel = acc_s[...] / jnp.maximum(l_s[...], TINY)
        bi = lax.broadcasted_iota(I32, (1, n_buf), 1)
        dq = qpos - (past - n_buf + bi)
        ok_buf = (dq >= 0) & (dq < WINDOW)
        dq_new = qpos - (past + lane)
        ok_new = (lane < ts) & (dq_new >= 0) & (dq_new < WINDOW)
        s_w = jnp.concatenate([_dot(qr, win_ref[0, 0].astype(BF16)), _nt_dot(qr, kwnew_ref[0])], axis=1)
        ok = jnp.concatenate([jnp.broadcast_to(ok_buf, (rows, n_buf)), jnp.broadcast_to(ok_new, (rows, LANES))], axis=1)
        s_w = jnp.where(ok, s_w, NEG)
        p_w = jnp.where(ok, jnp.exp(s_w - jnp.max(s_w, axis=-1, keepdims=True)), 0.0)
        p_w = (p_w / jnp.maximum(jnp.sum(p_w, axis=-1, keepdims=True), TINY)).astype(BF16)
        o_win = _nt_dot(p_w[:, :n_buf], win_ref[0, 1].astype(BF16)) + _dot(p_w[:, n_buf:], vwnew_ref[0])
        row_g = (row // ts) % N_KV_HEADS

        def own_group(o):
            out = jnp.zeros((rows, HEAD_DIM), F32)
            for g in range(N_KV_HEADS):
                out = out + jnp.where(row_g == g, o[:, g * HEAD_DIM:(g + 1) * HEAD_DIM], 0.0)
            return out

        o_ref[0] = (gate_ref[0, 0] * own_group(ocmp_s[...]) + gate_ref[0, 1] * own_group(o_sel)
                    + gate_ref[0, 2] * own_group(o_win))


def _attn_sample(cache5, page_table, qn_bd, qr_bd, ck_t, cv_t, win4, knew, vnew, kwnew, vwnew, gates,
                 *, n_pages, ts, past):
    b, pages_per_seq = page_table.shape
    steps = pages_per_seq // n_pages
    rows = qn_bd.shape[1]
    n_chunk = ck_t.shape[2]
    n_cmp = n_chunk - CMP_BLOCK // CMP_STRIDE + 1
    n_keys = past + LANES
    n_sb = -(-(past + ts) // SEL_BLOCK)
    n_sb_pad = -(-n_sb // LANES) * LANES
    n_buf = win4.shape[3]
    ratio = SEL_BLOCK // CMP_STRIDE
    ci = jnp.arange(n_chunk)[:, None]
    jb = jnp.arange(n_sb_pad)[None, :]
    sel_map = ((ci >= ratio * jb + 1 - CMP_BLOCK // CMP_STRIDE) & (ci < ratio * jb + ratio)
               & (ci < n_cmp) & (jb < n_sb)).astype(F32)
    expand = (jnp.arange(n_keys)[None, :] // SEL_BLOCK == jnp.arange(n_sb_pad)[:, None]).astype(BF16)

    def page_spec(j):
        return pl.BlockSpec((1, 2, N_KV_HEADS, HEAD_DIM, PAGE_SIZE),
                            lambda i, q, pt: (pt[i, q * n_pages + j], 1, 0, 0, 0))

    per_seq = lambda a: pl.BlockSpec((1,) + a.shape[1:], lambda i, q, pt: (i,) + (0,) * (a.ndim - 1))
    full = lambda a: pl.BlockSpec(a.shape, lambda i, q, pt: (0,) * a.ndim)
    return pl.pallas_call(
        functools.partial(_attn_sample_kernel, n_pages=n_pages, n_cmp=n_cmp, n_sb=n_sb, ts=ts, past=past,
                          n_buf=n_buf),
        grid_spec=pltpu.PrefetchScalarGridSpec(
            num_scalar_prefetch=1,
            grid=(b, steps),
            in_specs=[page_spec(j) for j in range(n_pages)]
            + [per_seq(a) for a in (qn_bd, qr_bd, ck_t, cv_t, win4, knew, vnew, kwnew, vwnew, gates)]
            + [full(sel_map), full(expand)],
            out_specs=pl.BlockSpec((1, rows, HEAD_DIM), lambda i, q, pt: (i, 0, 0)),
            scratch_shapes=[
                pltpu.VMEM((N_KV_HEADS * ts, n_keys), F32),
                pltpu.VMEM((rows, 1), F32), pltpu.VMEM((rows, 1), F32),
                pltpu.VMEM((rows, KV_WIDTH), F32), pltpu.VMEM((rows, KV_WIDTH), F32),
            ],
        ),
        out_shape=jax.ShapeDtypeStruct((b, rows, HEAD_DIM), F32),
        compiler_params=_params(("arbitrary", "arbitrary"), VMEM_LIMIT),
        name="attn_sample",
    )(page_table, *([cache5] * n_pages), qn_bd, qr_bd, ck_t, cv_t, win4, knew, vnew, kwnew, vwnew, gates,
      sel_map, expand)


def _router_weights(router_group, router_expert):
    rt = jnp.concatenate([router_group, router_expert], axis=1).T
    return jnp.pad(rt, ((0, 24 - rt.shape[0]), (0, 0)))


def _trunk_prompt(x, p, *, tt_pool=256, tm=1024, tt=512, tq=256):
    (norm_mix, norm_ffn, pool_w, pool_scale, kv_norm, w_kv, k_norm, cmp_pe, cmp_w1, cmp_b1, cmp_w2,
     w_qg, q_norm, w_o, router_group, router_expert, w_gate_up, w_down) = p
    b, t, d = x.shape
    n = b * t
    pos = jnp.arange(t)
    h, new_pool = _pool_layer(x, jnp.zeros((b, POOL_HALO, d), F32), norm_mix[0], pool_w[0], pool_scale[0],
                              tt=tt_pool, clip=True)
    h = _moe(h.reshape(n, d), norm_ffn[0], _router_weights(router_group[0], router_expert[0]),
             w_gate_up, w_down, 0, tm=tm)
    cos_t, sin_t = _rope_tables_transposed(pos)
    kv_t, win_t, craw, ksel, kwin, vsel_t, vwin_t = _kvproj(h.reshape(b, t, d), kv_norm, w_kv, k_norm,
                                                            cos_t, sin_t, tt=tt)
    ck, cv = _compress_prompt(craw, cmp_pe, cmp_w1, cmp_b1, cmp_w2, k_norm[0], b=b, t=t)
    qn_t, qr_t, g_t = _qproj(h, norm_mix[1], _qg_weights(w_qg[0]), q_norm[0], cos_t, sin_t,
                             tt=tt, pos_blocks=t // tt)
    o_t = _attn_prompt(qn_t, qr_t, g_t, ck, jnp.swapaxes(cv, 2, 3), ksel, kwin, vsel_t, vwin_t, b=b, t=t, tq=tq)
    h = _oproj_t(o_t, h, w_o[0], tt=tt)
    h = _moe(h, norm_ffn[1], _router_weights(router_group[1], router_expert[1]),
             w_gate_up, w_down, 1, tm=tm)
    n_win = min(WINDOW, t)
    kv_new = kv_t.reshape(b, 4, N_KV_HEADS, HEAD_DIM, t).transpose(0, 4, 1, 2, 3)
    win_new = win_t[:, :, t - n_win:].reshape(b, 2, N_KV_HEADS, HEAD_DIM, n_win).transpose(0, 4, 1, 2, 3)
    return h.reshape(b, t, d), new_pool[None], kv_new, win_new


def _trunk_sample(x, state_pool, cache_kv, page_table, state_win, p, *, n_pages=32):
    (norm_mix, norm_ffn, pool_w, pool_scale, kv_norm, w_kv, k_norm, cmp_pe, cmp_w1, cmp_b1, cmp_w2,
     w_qg, q_norm, w_o, router_group, router_expert, w_gate_up, w_down) = p
    b, ts, d = x.shape
    n = b * ts
    past = page_table.shape[1] * PAGE_SIZE
    n_buf = state_win.shape[1]
    prev16 = jnp.pad(state_pool[0], ((0, 0), (POOL_HALO - POOL_STATE, 0), (0, 0)))
    h, new_pool = _pool_layer(x, prev16, norm_mix[0], pool_w[0], pool_scale[0], tt=ts, clip=False)
    h = _moe(h.reshape(n, d), norm_ffn[0], _router_weights(router_group[0], router_expert[0]),
             w_gate_up, w_down, 0, tm=n)
    cos_t, sin_t = _rope_tables_transposed(past + jnp.arange(ts))
    cos_t = jnp.tile(cos_t, (1, b))
    sin_t = jnp.tile(sin_t, (1, b))
    kv_t, win_t, _, _, _, _, _ = _kvproj(h.reshape(1, n, d), kv_norm, w_kv, k_norm, cos_t, sin_t, tt=n)
    kv_rows = kv_t[0].T
    win_rows = win_t[0].T

    cache5 = cache_kv.transpose(0, 2, 3, 4, 1)
    ck, cv = _compress_pages(cache5, page_table, cmp_pe, cmp_w1, cmp_b1, cmp_w2, k_norm[0], n_pages=n_pages)
    n_chunk = ck.shape[2]
    ck_t = ck.transpose(0, 1, 3, 2).reshape(b, KV_WIDTH, n_chunk)
    cv_t = cv.transpose(0, 1, 3, 2).reshape(b, KV_WIDTH, n_chunk)

    qn_t, qr_t, g_t = _qproj(h, norm_mix[1], _qg_weights(w_qg[0]), q_norm[0], cos_t, sin_t, tt=n, pos_blocks=1)

    def block_diag_queries(q_t):
        q5 = q_t.reshape(N_KV_HEADS, HEADS_PER_KV, HEAD_DIM, b, ts).transpose(3, 1, 0, 4, 2)
        eye = jnp.eye(N_KV_HEADS, dtype=q_t.dtype)
        qbd = q5[:, :, :, :, None, :] * eye[None, None, :, None, :, None]
        return qbd.reshape(b, HEADS_PER_KV * N_KV_HEADS * ts, KV_WIDTH)

    gates = g_t.reshape(N_KV_HEADS, GATE_ROWS, b, ts)[:, :3 * HEADS_PER_KV]
    gates = gates.reshape(N_KV_HEADS, 3, HEADS_PER_KV, b, ts).transpose(3, 1, 2, 0, 4)
    gates = jnp.broadcast_to(gates.reshape(b, 3, HEADS_PER_KV * N_KV_HEADS * ts, 1),
                             (b, 3, HEADS_PER_KV * N_KV_HEADS * ts, HEAD_DIM))

    def new_rows(rows2):
        return jnp.pad(rows2.reshape(b, ts, KV_WIDTH), ((0, 0), (0, LANES - ts), (0, 0))).astype(BF16)

    win4 = state_win.transpose(0, 2, 3, 4, 1).reshape(b, 2, KV_WIDTH, n_buf)
    o = _attn_sample(cache5, page_table, block_diag_queries(qn_t), block_diag_queries(qr_t), ck_t, cv_t, win4,
                     new_rows(kv_rows[:, 2 * KV_WIDTH:3 * KV_WIDTH]), new_rows(kv_rows[:, 3 * KV_WIDTH:]),
                     new_rows(win_rows[:, :KV_WIDTH]), new_rows(win_rows[:, KV_WIDTH:]), gates,
                     n_pages=n_pages, ts=ts, past=past)
    o = o.reshape(b, HEADS_PER_KV, N_KV_HEADS, ts, HEAD_DIM).transpose(0, 3, 2, 1, 4).reshape(n, N_HEADS * HEAD_DIM)
    h = _oproj_n(o, h, w_o[0])
    h = _moe(h, norm_ffn[1], _router_weights(router_group[1], router_expert[1]),
             w_gate_up, w_down, 1, tm=n)
    kv_new = kv_rows.reshape(b, ts, 4, N_KV_HEADS, HEAD_DIM)
    win_new = jnp.concatenate([state_win, win_rows.reshape(b, ts, 2, N_KV_HEADS, HEAD_DIM)], axis=1)[:, -n_buf:]
    return h.reshape(b, ts, d), new_pool[None], kv_new, win_new


def kernel(x_prompt, x_sample, state_pool, cache_kv, page_table, state_win, norm_mix, norm_ffn, pool_w, pool_scale, kv_norm, w_kv, k_norm, cmp_pe, cmp_w1, cmp_b1, cmp_w2, w_qg, q_norm, w_o, router_group, router_expert, w_gate_up, w_down):
    params = (norm_mix, norm_ffn, pool_w, pool_scale, kv_norm, w_kv, k_norm, cmp_pe, cmp_w1, cmp_b1,
              cmp_w2, w_qg, q_norm, w_o, router_group, router_expert, _cast_bf16(w_gate_up), _cast_bf16(w_down))
    y_p, pool_p, kv_p, win_p = _trunk_prompt(x_prompt, params)
    y_s, pool_s, kv_s, win_s = _trunk_sample(x_sample, state_pool, cache_kv, page_table, state_win, params)
    return y_p, y_s, pool_p, pool_s, kv_p, kv_s, win_p, win_s
```

```python
import functools

import jax
import jax.numpy as jnp
from jax import lax
from jax.experimental import pallas as pl
from jax.experimental.pallas import tpu as pltpu

F32 = jnp.float32
BF16 = jnp.bfloat16
I32 = jnp.int32
HIGHEST = lax.Precision.HIGHEST

D_MODEL = 1024
POOL_WINDOWS = (2, 4, 8, 16)
POOL_GROUP_DIM = D_MODEL // len(POOL_WINDOWS)
POOL_STATE = max(POOL_WINDOWS) - 1
POOL_HALO = 16
N_HEADS = 16
HEAD_DIM = 64
HALF_DIM = HEAD_DIM // 2
N_KV_HEADS = 4
HEADS_PER_KV = N_HEADS // N_KV_HEADS
KV_WIDTH = N_KV_HEADS * HEAD_DIM
CMP_BLOCK = 32
CMP_STRIDE = 16
CMP_HIDDEN = 2 * HEAD_DIM
SEL_BLOCK = 64
N_SEL = 16
WINDOW = 512
PAGE_SIZE = 128
ROPE_THETA = 10000.0
SCALE = HEAD_DIM ** -0.5
Q_SCALE = SCALE * 1.4426950408889634
N_GROUPS = 4
EXPERTS_PER_GROUP = 4
N_EXPERTS = N_GROUPS * EXPERTS_PER_GROUP
D_EXPERT = 512
EPS = 1e-6
NEG = -1e30
TINY = 1e-30
BIG = 1e4

LANES = 128
GATE_ROWS = 16
ROUTE_GROUP_ROW = 3 * N_EXPERTS
MOE_SEG_ALIGN = 16
MOE_CHUNK = 144
MOE_STEP_EXPERTS = 2
V_PAD_ROWS = 16
VMEM_LIMIT = 56 * 1024 * 1024


def _params(sem, vmem=None):
    return pltpu.CompilerParams(dimension_semantics=sem, vmem_limit_bytes=vmem)


def _rms(x, g):
    return x * lax.rsqrt(jnp.mean(x * x, axis=-1, keepdims=True) + EPS) * g


def _nt_dot(a, b, precision=None):
    return lax.dot_general(a, b, (((1,), (1,)), ((), ())), precision=precision,
                           preferred_element_type=F32)


def _dot(a, b, precision=None):
    return jnp.dot(a, b, precision=precision, preferred_element_type=F32)


def _sigmoid(x):
    return 1.0 / (1.0 + jnp.exp(-x))


def _pool_kernel(h_ref, prev_ref, g_ref, w_ref, sc_ref, o_ref, np_ref, ext_ref, lvl_ref, *, tt, clip):
    t = pl.program_id(1)
    x = h_ref[0]
    xn = _rms(x, g_ref[...])

    @pl.when(t == 0)
    def _():
        ext_ref[0:POOL_HALO, :] = prev_ref[0]

    @pl.when(t > 0)
    def _():
        ext_ref[0:POOL_HALO, :] = ext_ref[tt:tt + POOL_HALO, :]

    ext_ref[POOL_HALO:POOL_HALO + tt, :] = xn
    if clip:
        tpos = t * tt + lax.broadcasted_iota(I32, (tt, 1), 0)
    outs = []
    for gi, w in enumerate(POOL_WINDOWS):
        lo = gi * POOL_GROUP_DIM
        hi = lo + POOL_GROUP_DIM
        n = tt + POOL_HALO
        lvl_ref[...] = ext_ref[:, lo:hi]
        span = 1
        while span < w:
            lvl_ref[span:n, :] = lvl_ref[span:n, :] + lvl_ref[0:n - span, :]
            span *= 2
        acc = lvl_ref[POOL_HALO:n, :]
        if clip:
            mean = acc / jnp.minimum(tpos + 1, w).astype(F32)
        else:
            mean = acc * (1.0 / w)
        d = (mean - xn[:, lo:hi]).astype(BF16)
        outs.append(_dot(d, w_ref[gi].astype(BF16)))
    o_ref[0] = x + jnp.concatenate(outs, axis=1) * sc_ref[...]

    @pl.when(t == pl.num_programs(1) - 1)
    def _():
        np_ref[0] = ext_ref[tt + POOL_HALO - POOL_STATE:tt + POOL_HALO, :]


def _pool_layer(h, prev16, gain, w_pool, scale, *, tt, clip):
    b, t, d = h.shape
    return pl.pallas_call(
        functools.partial(_pool_kernel, tt=tt, clip=clip),
        grid=(b, t // tt),
        in_specs=[
            pl.BlockSpec((1, tt, d), lambda i, j: (i, j, 0)),
            pl.BlockSpec((1, POOL_HALO, d), lambda i, j: (i, 0, 0)),
            pl.BlockSpec((1, d), lambda i, j: (0, 0)),
            pl.BlockSpec(w_pool.shape, lambda i, j: (0, 0, 0)),
            pl.BlockSpec((1, d), lambda i, j: (0, 0)),
        ],
        out_specs=[
            pl.BlockSpec((1, tt, d), lambda i, j: (i, j, 0)),
            pl.BlockSpec((1, POOL_STATE, d), lambda i, j: (i, 0, 0)),
        ],
        out_shape=[jax.ShapeDtypeStruct((b, t, d), F32),
                   jax.ShapeDtypeStruct((b, POOL_STATE, d), F32)],
        scratch_shapes=[pltpu.VMEM((tt + POOL_HALO, d), F32), pltpu.VMEM((tt + POOL_HALO, POOL_GROUP_DIM), F32)],
        compiler_params=_params(("arbitrary", "arbitrary")),
        name="pool_layer",
    )(h, prev16, gain.reshape(1, d), w_pool, scale.reshape(1, d))


def _softmax_rows(rows):
    m = functools.reduce(jnp.maximum, rows)
    es = [jnp.exp(r - m) for r in rows]
    s = functools.reduce(lambda a, b: a + b, es)
    return [e / s for e in es]


def _router_kernel(h_ref, g_ref, rt_ref, xn_ref, comb_ref, grp_ref):
    xn = _rms(h_ref[...], g_ref[...])
    x_hi = xn.astype(BF16)
    xn_ref[...] = x_hi
    x_lo = (xn - x_hi.astype(F32)).astype(BF16)
    r_hi = rt_ref[...].astype(BF16)
    r_lo = (rt_ref[...] - r_hi.astype(F32)).astype(BF16)
    lt = _nt_dot(r_hi, x_hi) + (_nt_dot(r_hi, x_lo) + _nt_dot(r_lo, x_hi))
    pg = _softmax_rows([lt[i:i + 1, :] for i in range(N_GROUPS)])
    g_val = functools.reduce(jnp.maximum, pg)
    g_idx = jnp.full(g_val.shape, N_GROUPS - 1, I32)
    for i in range(N_GROUPS - 2, -1, -1):
        g_idx = jnp.where(pg[i] == g_val, i, g_idx)
    le = []
    for j in range(EXPERTS_PER_GROUP):
        v = lt[N_GROUPS + (N_GROUPS - 1) * EXPERTS_PER_GROUP + j:N_GROUPS + (N_GROUPS - 1) * EXPERTS_PER_GROUP + j + 1, :]
        for gi in range(N_GROUPS - 2, -1, -1):
            r = N_GROUPS + gi * EXPERTS_PER_GROUP + j
            v = jnp.where(g_idx == gi, lt[r:r + 1, :], v)
        le.append(v)
    pe = _softmax_rows(le)
    ranks = []
    for j in range(EXPERTS_PER_GROUP):
        r = jnp.zeros(g_val.shape, I32)
        for i in range(EXPERTS_PER_GROUP):
            if i == j:
                continue
            beats = (pe[i] > pe[j]) | (pe[i] == pe[j]) if i < j else (pe[i] > pe[j])
            r = r + beats.astype(I32)
        ranks.append(r)
    vals, idxs = [], []
    for k in range(2):
        v = jnp.zeros(g_val.shape, F32)
        ix = jnp.zeros(g_val.shape, I32)
        for j in range(EXPERTS_PER_GROUP):
            hit = ranks[j] == k
            v = jnp.where(hit, pe[j], v)
            ix = jnp.where(hit, j, ix)
        vals.append(v)
        idxs.append(ix)
    tot = vals[0] + vals[1]
    erow = lax.broadcasted_iota(I32, (LANES, g_val.shape[1]), 0)
    comb_t = jnp.where(erow == ROUTE_GROUP_ROW + g_idx, 1.0, 0.0)
    for k in range(2):
        wk = g_val * (vals[k] / tot)
        w_hi = wk.astype(BF16).astype(F32)
        w_mid = (wk - w_hi).astype(BF16).astype(F32)
        w_lo = (wk - w_hi - w_mid).astype(BF16).astype(F32)
        eid = g_idx * EXPERTS_PER_GROUP + idxs[k]
        for part, term in enumerate((w_hi, w_mid, w_lo)):
            comb_t = comb_t + jnp.where(erow == part * N_EXPERTS + eid, term, 0.0)
    comb_ref[...] = comb_t.T
    grp_ref[...] = comb_t[ROUTE_GROUP_ROW:ROUTE_GROUP_ROW + 8]


def _router(h2, gain, rt, *, tm):
    n, d = h2.shape
    return pl.pallas_call(
        _router_kernel,
        grid=(n // tm,),
        in_specs=[
            pl.BlockSpec((tm, d), lambda i: (i, 0)),
            pl.BlockSpec((1, d), lambda i: (0, 0)),
            pl.BlockSpec(rt.shape, lambda i: (0, 0)),
        ],
        out_specs=[
            pl.BlockSpec((tm, d), lambda i: (i, 0)),
            pl.BlockSpec((tm, LANES), lambda i: (i, 0)),
            pl.BlockSpec((8, tm), lambda i: (0, i)),
        ],
        out_shape=[jax.ShapeDtypeStruct((n, d), BF16), jax.ShapeDtypeStruct((n, LANES), F32),
                   jax.ShapeDtypeStruct((8, n), F32)],
        compiler_params=_params(("arbitrary",)),
        name="moe_router",
    )(h2, gain.reshape(1, d), rt)


def _cast_kernel(x_ref, o_ref):
    o_ref[...] = x_ref[...].astype(o_ref.dtype)


def _cast_bf16(w):
    blk = (1, 1) + w.shape[2:]
    return pl.pallas_call(
        _cast_kernel,
        grid=w.shape[:2],
        in_specs=[pl.BlockSpec(blk, lambda i, j: (i, j, 0, 0))],
        out_specs=pl.BlockSpec(blk, lambda i, j: (i, j, 0, 0)),
        out_shape=jax.ShapeDtypeStruct(w.shape, BF16),
        compiler_params=_params(("arbitrary", "arbitrary")),
        name="cast_weights",
    )(w)


def _moe_kernel(cnt_ref, base_ref, x_ref, comb_ref, grp_ref, h_ref, wgu_ref, wd_ref, o_ref,
                pt_s, z_s, y_s, cs_s, slot_s, *, r_max, unroll):
    w = pl.program_id(0)
    step = pl.program_id(1)
    nw = x_ref.shape[0]
    g = step // (EXPERTS_PER_GROUP // MOE_STEP_EXPERTS)

    @pl.when(step == 0)
    def _():
        lane = lax.broadcasted_iota(I32, (1, LANES), 1)
        in_grp = (lane >= ROUTE_GROUP_ROW) & (lane < ROUTE_GROUP_ROW + N_GROUPS)
        oh = jnp.where(in_grp, comb_ref[...], 0.0)
        base_row = jnp.zeros((1, LANES), F32)
        for gi in range(N_GROUPS):
            base_row = jnp.where(lane == ROUTE_GROUP_ROW + gi, base_ref[w, gi].astype(F32), base_row)
        ri = lax.broadcasted_iota(I32, (nw, nw), 0)
        ci = lax.broadcasted_iota(I32, (nw, nw), 1)
        before = _dot(jnp.where(ci < ri, 1.0, 0.0).astype(BF16), oh.astype(BF16))
        slot_s[...] = jnp.sum(oh * (before + base_row), axis=-1, keepdims=True)
        oh_t = grp_ref[...]
        before_t = _dot(oh_t.astype(BF16), jnp.where(ri < ci, 1.0, 0.0).astype(BF16))
        slot_row = jnp.zeros((1, nw), F32)
        for gi in range(N_GROUPS):
            slot_row = slot_row + oh_t[gi:gi + 1] * (before_t[gi:gi + 1] + base_ref[w, gi].astype(F32))
        rows = lax.broadcasted_iota(I32, (r_max, 1), 0)
        pt_s[...] = jnp.where(rows == slot_row.astype(I32), 1.0, 0.0).astype(BF16)
        z_s[...] = _dot(pt_s[...], x_ref[...]).astype(BF16)
        cs_s[...] = _dot(pt_s[...], comb_ref[...].astype(BF16))
        y_s[...] = jnp.zeros(y_s.shape, F32)

    seg0 = base_ref[w, g]
    seg1 = seg0 + cnt_ref[w, g]
    lane_c = lax.broadcasted_iota(I32, (MOE_CHUNK, LANES), 1)
    in_terms = lane_c < 3 * N_EXPERTS

    def chunk(lo):
        r0 = pl.multiple_of(jnp.minimum(lo, r_max - MOE_CHUNK), MOE_SEG_ALIGN)
        z = z_s[pl.ds(r0, MOE_CHUNK), :]
        cs = cs_s[pl.ds(r0, MOE_CHUNK), :]
        acc = None
        for k in range(MOE_STEP_EXPERTS):
            gu = _dot(z, wgu_ref[0, k])
            a = gu[:, :D_EXPERT]
            hdn = (a * _sigmoid(a) * gu[:, D_EXPERT:]).astype(BF16)
            y = _dot(hdn, wd_ref[0, k])
            is_e = in_terms & (lane_c % N_EXPERTS == step * MOE_STEP_EXPERTS + k)
            wgt = jnp.sum(jnp.where(is_e, cs, 0.0), axis=-1, keepdims=True)
            acc = wgt * y if acc is None else acc + wgt * y
        rows = r0 + lax.broadcasted_iota(I32, (MOE_CHUNK, 1), 0)
        y_s[pl.ds(r0, MOE_CHUNK), :] += jnp.where((rows >= lo) & (rows < seg1), acc, 0.0)

    def chunks(c, carry):
        for u in range(unroll):
            chunk(seg0 + (c * unroll + u) * MOE_CHUNK)
        return carry

    span = unroll * MOE_CHUNK
    lax.fori_loop(0, (cnt_ref[w, g] + span - 1) // span, chunks, 0)

    @pl.when(step == pl.num_programs(1) - 1)
    def _():
        cols = lax.broadcasted_iota(I32, (1, r_max), 1)
        p = jnp.where(cols == slot_s[...].astype(I32), 1.0, 0.0).astype(BF16)
        o_ref[...] = h_ref[...] + _dot(p, y_s[...].astype(BF16))


def _moe(h2, gain, rt, w_gu, w_down, layer, *, tm):
    n, d = h2.shape
    if w_gu.dtype != BF16:
        w_gu, w_down = _cast_bf16(w_gu), _cast_bf16(w_down)
    xn, comb, grp = _router(h2, gain, rt, tm=tm)
    n_win = n // tm
    cnt = jnp.sum(grp[:N_GROUPS].reshape(N_GROUPS, n_win, tm), axis=-1).T.astype(I32)
    seg = (cnt + MOE_SEG_ALIGN - 1) // MOE_SEG_ALIGN * MOE_SEG_ALIGN
    base = jnp.cumsum(seg, axis=1) - seg
    r_max = -(-(tm + N_GROUPS * MOE_SEG_ALIGN + MOE_CHUNK) // LANES) * LANES
    return pl.pallas_call(
        functools.partial(_moe_kernel, r_max=r_max, unroll=max(1, -(-tm // (N_GROUPS * MOE_CHUNK)))),
        grid_spec=pltpu.PrefetchScalarGridSpec(
            num_scalar_prefetch=2,
            grid=(n_win, N_EXPERTS // MOE_STEP_EXPERTS),
            in_specs=[
                pl.BlockSpec((tm, d), lambda i, e, c, b: (i, 0)),
                pl.BlockSpec((tm, LANES), lambda i, e, c, b: (i, 0)),
                pl.BlockSpec((8, tm), lambda i, e, c, b: (0, i)),
                pl.BlockSpec((tm, d), lambda i, e, c, b: (i, 0)),
                pl.BlockSpec((1, MOE_STEP_EXPERTS, d, 2 * D_EXPERT), lambda i, e, c, b: (layer, e, 0, 0)),
                pl.BlockSpec((1, MOE_STEP_EXPERTS, D_EXPERT, d), lambda i, e, c, b: (layer, e, 0, 0)),
            ],
            out_specs=pl.BlockSpec((tm, d), lambda i, e, c, b: (i, 0)),
            scratch_shapes=[
                pltpu.VMEM((r_max, tm), BF16), pltpu.VMEM((r_max, d), BF16), pltpu.VMEM((r_max, d), F32),
                pltpu.VMEM((r_max, LANES), F32), pltpu.VMEM((tm, 1), F32),
            ],
        ),
        out_shape=jax.ShapeDtypeStruct((n, d), F32),
        compiler_params=_params(("arbitrary", "arbitrary"), VMEM_LIMIT),
        name="moe_experts",
    )(cnt, base, xn, comb, grp, h2, w_gu, w_down)


def _head_norm_rope_t(z, gain, c, s):
    outs = []
    for g in range(z.shape[0] // HEAD_DIM):
        zh = z[g * HEAD_DIM:(g + 1) * HEAD_DIM]
        zn = zh * lax.rsqrt(jnp.mean(zh * zh, axis=0, keepdims=True) + EPS) * gain
        x1 = zn[:HALF_DIM]
        x2 = zn[HALF_DIM:]
        outs.append(x1 * c - x2 * s)
        outs.append(x2 * c + x1 * s)
    return jnp.concatenate(outs, axis=0)


def _kvproj_kernel(h_ref, g_ref, wt_ref, gsel_ref, gwin_ref, cos_ref, sin_ref,
                   kvt_ref, wint_ref, craw_ref, ksel_ref, kwin_ref, vselt_ref, vwint_ref, w_s):
    @pl.when((pl.program_id(0) == 0) & (pl.program_id(1) == 0))
    def _():
        w_s[...] = wt_ref[...].astype(BF16)

    xn = _rms(h_ref[0], g_ref[...]).astype(BF16)
    kvt = _nt_dot(w_s[...], xn)
    tt = xn.shape[0]
    c = cos_ref[...]
    s = sin_ref[...]
    gsel = jnp.concatenate([gsel_ref[...]] * (tt // LANES), axis=1)
    gwin = jnp.concatenate([gwin_ref[...]] * (tt // LANES), axis=1)
    ksel = _head_norm_rope_t(kvt[2 * KV_WIDTH:3 * KV_WIDTH], gsel, c, s)
    kwin = _head_norm_rope_t(kvt[4 * KV_WIDTH:5 * KV_WIDTH], gwin, c, s)
    kvt_ref[0] = jnp.concatenate([kvt[:2 * KV_WIDTH], ksel, kvt[3 * KV_WIDTH:4 * KV_WIDTH]], axis=0)
    wint_ref[0] = jnp.concatenate([kwin, kvt[5 * KV_WIDTH:]], axis=0)
    vselt_ref[...] = kvt[3 * KV_WIDTH:4 * KV_WIDTH].astype(BF16)
    vwint_ref[...] = kvt[5 * KV_WIDTH:].astype(BF16)
    raw = kvt[:2 * KV_WIDTH].T
    for k in range(2 * KV_WIDTH // LANES):
        craw_ref[k] = raw[:, k * LANES:(k + 1) * LANES]
    ksel_n = ksel.T
    kwin_n = kwin.T
    for g in range(N_KV_HEADS):
        ksel_ref[g] = ksel_n[:, g * HEAD_DIM:(g + 1) * HEAD_DIM].astype(BF16)
        kwin_ref[g] = kwin_n[:, g * HEAD_DIM:(g + 1) * HEAD_DIM].astype(BF16)


def _kvproj(x3, gain, w_kv, k_norm, cos_tt, sin_tt, *, tt):
    b, t, d = x3.shape
    n = b * t
    nt = t // tt
    wt = w_kv.T
    gsel = jnp.broadcast_to(k_norm[1][:, None], (HEAD_DIM, LANES))
    gwin = jnp.broadcast_to(k_norm[2][:, None], (HEAD_DIM, LANES))
    full = lambda a: pl.BlockSpec(a.shape, lambda i, j: (0,) * a.ndim)
    return pl.pallas_call(
        _kvproj_kernel,
        grid=(b, nt),
        in_specs=[
            pl.BlockSpec((1, tt, d), lambda i, j: (i, j, 0)),
            pl.BlockSpec((1, d), lambda i, j: (0, 0)),
            full(wt), full(gsel), full(gwin),
            pl.BlockSpec((HALF_DIM, tt), lambda i, j: (0, j)),
            pl.BlockSpec((HALF_DIM, tt), lambda i, j: (0, j)),
        ],
        out_specs=[
            pl.BlockSpec((1, 4 * KV_WIDTH, tt), lambda i, j: (i, 0, j)),
            pl.BlockSpec((1, 2 * KV_WIDTH, tt), lambda i, j: (i, 0, j)),
            pl.BlockSpec((2 * KV_WIDTH // LANES, tt, LANES), lambda i, j: (0, i * nt + j, 0)),
            pl.BlockSpec((N_KV_HEADS, tt, HEAD_DIM), lambda i, j: (0, i * nt + j, 0)),
            pl.BlockSpec((N_KV_HEADS, tt, HEAD_DIM), lambda i, j: (0, i * nt + j, 0)),
            pl.BlockSpec((KV_WIDTH, tt), lambda i, j: (0, i * nt + j)),
            pl.BlockSpec((KV_WIDTH, tt), lambda i, j: (0, i * nt + j)),
        ],
        out_shape=[
            jax.ShapeDtypeStruct((b, 4 * KV_WIDTH, t), F32),
            jax.ShapeDtypeStruct((b, 2 * KV_WIDTH, t), F32),
            jax.ShapeDtypeStruct((2 * KV_WIDTH // LANES, n, LANES), F32),
            jax.ShapeDtypeStruct((N_KV_HEADS, n, HEAD_DIM), BF16),
            jax.ShapeDtypeStruct((N_KV_HEADS, n, HEAD_DIM), BF16),
            jax.ShapeDtypeStruct((KV_WIDTH, n), BF16),
            jax.ShapeDtypeStruct((KV_WIDTH, n), BF16),
        ],
        scratch_shapes=[pltpu.VMEM(wt.shape, BF16)],
        compiler_params=_params(("arbitrary", "arbitrary"), VMEM_LIMIT),
        name="kv_proj",
    )(x3, gain.reshape(1, d), wt, gsel, gwin, cos_tt, sin_tt)


def _rope_tables_transposed(pos):
    inv = 1.0 / (ROPE_THETA ** (jnp.arange(HALF_DIM, dtype=F32) * (2.0 / HEAD_DIM)))
    ang = pos.astype(F32)[:, None] * inv[None, :]
    return jnp.cos(ang).T, jnp.sin(ang).T


def _cmp_ab_accumulate(load_rows, wab_ref, n_rows):
    del n_rows
    heads_per_chunk = LANES // HEAD_DIM
    accs = [[None] * N_KV_HEADS for _ in range(2)]
    for s in range(2):
        for c in range(KV_WIDTH // LANES):
            plane = s * (KV_WIDTH // LANES) + c
            lhs = jnp.concatenate([load_rows(r, plane).astype(BF16) for r in range(CMP_STRIDE)], axis=1)
            out = _dot(lhs, wab_ref[s])
            for k in range(heads_per_chunk):
                accs[s][c * heads_per_chunk + k] = out[:, k * 2 * CMP_HIDDEN:(k + 1) * 2 * CMP_HIDDEN]
    return accs


def _cmp_finish_rows(load_ab, n_row, pe_ref, w1_ref, b1_ref, w2_ref, gk_ref, ck_ref, cv_ref):
    for s in range(2):
        bias = _dot(pe_ref[s].astype(BF16), w1_ref[s].astype(BF16)) + b1_ref[s]
        w2 = w2_ref[s].astype(BF16)
        for g in range(N_KV_HEADS):
            ab = load_ab(s, g)
            hid = ab[:, :CMP_HIDDEN] + pltpu.roll(ab[:, CMP_HIDDEN:], n_row - 1, 0) + bias
            cdf = 0.5 * (1.0 + jnp.tanh(0.7978845608028654 * (hid + 0.044715 * (hid * hid * hid))))
            out = _dot((hid * cdf).astype(BF16), w2)
            if s == 0:
                ck_ref[0, g] = _rms(out, gk_ref[...]).astype(BF16)
            else:
                cv_ref[0, g] = out.astype(BF16)


def _cmp_prompt_kernel(craw_ref, wab_ref, pe_ref, w1_ref, b1_ref, w2_ref, gk_ref, ck_ref, cv_ref, *, n_chunk):
    accs = _cmp_ab_accumulate(lambda r, c: craw_ref[c, pl.ds(r, n_chunk, stride=CMP_STRIDE), :], wab_ref, n_chunk)
    _cmp_finish_rows(lambda s, g: accs[s][g], n_chunk, pe_ref, w1_ref, b1_ref, w2_ref, gk_ref, ck_ref, cv_ref)


def _cmp_pages_kernel(pt_ref, *refs, n_pages):
    del pt_ref
    page_refs = refs[:n_pages]
    wab_ref, pe_ref, w1_ref, b1_ref, w2_ref, gk_ref, ck_ref, cv_ref, craw_s, ab_s = refs[n_pages:]
    q = pl.program_id(1)
    pairs = KV_WIDTH // LANES
    for j in range(n_pages):
        for s in range(2):
            for k in range(pairs):
                tile = jnp.concatenate([page_refs[j][0, s, 2 * k], page_refs[j][0, s, 2 * k + 1]], axis=0)
                craw_s[s * pairs + k, j * PAGE_SIZE:(j + 1) * PAGE_SIZE, :] = tile.T
    n_chunk = n_pages * PAGE_SIZE // CMP_STRIDE
    accs = _cmp_ab_accumulate(lambda r, c: craw_s[c, pl.ds(r, n_chunk, stride=CMP_STRIDE), :], wab_ref, n_chunk)
    r0 = pl.multiple_of(q * n_chunk, n_chunk)
    for s in range(2):
        for g in range(N_KV_HEADS):
            ab_s[s, g, pl.ds(r0, n_chunk), :] = accs[s][g]

    @pl.when(q == pl.num_programs(1) - 1)
    def _():
        _cmp_finish_rows(lambda s, g: ab_s[s, g], ab_s.shape[2], pe_ref, w1_ref, b1_ref, w2_ref, gk_ref,
                         ck_ref, cv_ref)


def _cmp_small_operands(cmp_pe, cmp_w1, cmp_b1, cmp_w2, gk):
    return (cmp_pe.reshape(2, 1, CMP_BLOCK * HEAD_DIM), cmp_w1, cmp_b1.reshape(2, 1, CMP_HIDDEN), cmp_w2,
            gk.reshape(1, HEAD_DIM))


def _compress_pages(cache5, page_table, cmp_pe, cmp_w1, cmp_b1, cmp_w2, gk, *, n_pages):
    b, pages_per_seq = page_table.shape
    steps = pages_per_seq // n_pages
    n_row = pages_per_seq * PAGE_SIZE // CMP_STRIDE
    wab = _cmp_weights(cmp_w1)
    small = _cmp_small_operands(cmp_pe, cmp_w1, cmp_b1, cmp_w2, gk)

    def page_spec(j):
        return pl.BlockSpec((1, 2, N_KV_HEADS, HEAD_DIM, PAGE_SIZE),
                            lambda i, q, pt: (pt[i, q * n_pages + j], 0, 0, 0, 0))

    full = lambda a: pl.BlockSpec(a.shape, lambda i, q, pt: (0,) * a.ndim)
    out_spec = pl.BlockSpec((1, N_KV_HEADS, n_row, HEAD_DIM), lambda i, q, pt: (i, 0, 0, 0))
    return pl.pallas_call(
        functools.partial(_cmp_pages_kernel, n_pages=n_pages),
        grid_spec=pltpu.PrefetchScalarGridSpec(
            num_scalar_prefetch=1,
            grid=(b, steps),
            in_specs=[page_spec(j) for j in range(n_pages)] + [full(wab)] + [full(a) for a in small],
            out_specs=[out_spec, out_spec],
            scratch_shapes=[pltpu.VMEM((2 * KV_WIDTH // LANES, n_pages * PAGE_SIZE, LANES), F32),
                            pltpu.VMEM((2, N_KV_HEADS, n_row, 2 * CMP_HIDDEN), F32)],
        ),
        out_shape=[jax.ShapeDtypeStruct((b, N_KV_HEADS, n_row, HEAD_DIM), BF16)] * 2,
        compiler_params=_params(("arbitrary", "arbitrary"), VMEM_LIMIT),
        name="cmp_pages",
    )(page_table, *([cache5] * n_pages), wab, *small)


def _cmp_weights(cmp_w1):
    w = cmp_w1.reshape(2, 2, CMP_STRIDE, HEAD_DIM, CMP_HIDDEN)
    w = w.transpose(0, 2, 3, 1, 4).reshape(2, CMP_STRIDE, HEAD_DIM, 2 * CMP_HIDDEN)
    eye = jnp.eye(LANES // HEAD_DIM, dtype=w.dtype)
    w = w[:, :, None, :, None, :] * eye[None, None, :, None, :, None]
    return w.reshape(2, CMP_STRIDE * LANES, (LANES // HEAD_DIM) * 2 * CMP_HIDDEN).astype(BF16)


def _compress_prompt(craw, cmp_pe, cmp_w1, cmp_b1, cmp_w2, gk, *, b, t):
    n_chunk = t // CMP_STRIDE
    wab = _cmp_weights(cmp_w1)
    small = _cmp_small_operands(cmp_pe, cmp_w1, cmp_b1, cmp_w2, gk)
    full = lambda a: pl.BlockSpec(a.shape, lambda i: (0,) * a.ndim)
    out_spec = pl.BlockSpec((1, N_KV_HEADS, n_chunk, HEAD_DIM), lambda i: (i, 0, 0, 0))
    return pl.pallas_call(
        functools.partial(_cmp_prompt_kernel, n_chunk=n_chunk),
        grid=(b,),
        in_specs=[pl.BlockSpec((craw.shape[0], t, LANES), lambda i: (0, i, 0)), full(wab)]
        + [full(a) for a in small],
        out_specs=[out_spec, out_spec],
        out_shape=[jax.ShapeDtypeStruct((b, N_KV_HEADS, n_chunk, HEAD_DIM), BF16)] * 2,
        compiler_params=_params(("arbitrary",), VMEM_LIMIT),
        name="cmp_prompt",
    )(craw, wab, *small)


def _qproj_kernel(h_ref, g_ref, wt_ref, gq_ref, cos_ref, sin_ref, qn_ref, qr_ref, gt_ref, w_s):
    @pl.when(pl.program_id(0) == 0)
    def _():
        w_s[...] = wt_ref[...].astype(BF16)

    xn = _rms(h_ref[...], g_ref[...]).astype(BF16)
    qg = _nt_dot(w_s[...], xn)
    tt = xn.shape[0]
    gq = jnp.concatenate([gq_ref[...]] * (tt // LANES), axis=1)
    c = cos_ref[...]
    s = sin_ref[...]
    for h in range(N_HEADS):
        qh = qg[h * HEAD_DIM:(h + 1) * HEAD_DIM]
        qn = qh * lax.rsqrt(jnp.mean(qh * qh, axis=0, keepdims=True) + EPS) * gq
        qn_ref[h * HEAD_DIM:(h + 1) * HEAD_DIM, :] = (qn * Q_SCALE).astype(BF16)
        x1 = qn[:HALF_DIM]
        x2 = qn[HALF_DIM:]
        qr_ref[h * HEAD_DIM:h * HEAD_DIM + HALF_DIM, :] = ((x1 * c - x2 * s) * Q_SCALE).astype(BF16)
        qr_ref[h * HEAD_DIM + HALF_DIM:(h + 1) * HEAD_DIM, :] = ((x2 * c + x1 * s) * Q_SCALE).astype(BF16)
    gt_ref[...] = _sigmoid(qg[N_HEADS * HEAD_DIM:])


def _qg_weights(w_qg):
    nq = N_HEADS * HEAD_DIM
    gates = w_qg[:, nq:].reshape(D_MODEL, 3, N_KV_HEADS, HEADS_PER_KV).transpose(2, 1, 3, 0)
    gates = gates.reshape(N_KV_HEADS, 3 * HEADS_PER_KV, D_MODEL)
    gates = jnp.pad(gates, ((0, 0), (0, GATE_ROWS - 3 * HEADS_PER_KV), (0, 0)))
    return jnp.concatenate([w_qg[:, :nq].T, gates.reshape(N_KV_HEADS * GATE_ROWS, D_MODEL)], axis=0)


def _qproj(h2, gain, wt, q_norm, cos_tt, sin_tt, *, tt, pos_blocks):
    n, d = h2.shape
    nq = N_HEADS * HEAD_DIM
    ng = N_KV_HEADS * GATE_ROWS
    gq = jnp.broadcast_to(q_norm[:, None], (HEAD_DIM, LANES))
    full = lambda a: pl.BlockSpec(a.shape, lambda i: (0,) * a.ndim)
    return pl.pallas_call(
        _qproj_kernel,
        grid=(n // tt,),
        in_specs=[
            pl.BlockSpec((tt, d), lambda i: (i, 0)),
            pl.BlockSpec((1, d), lambda i: (0, 0)),
            full(wt), full(gq),
            pl.BlockSpec((HALF_DIM, tt), lambda i: (0, i % pos_blocks)),
            pl.BlockSpec((HALF_DIM, tt), lambda i: (0, i % pos_blocks)),
        ],
        out_specs=[
            pl.BlockSpec((nq, tt), lambda i: (0, i)),
            pl.BlockSpec((nq, tt), lambda i: (0, i)),
            pl.BlockSpec((ng, tt), lambda i: (0, i)),
        ],
        out_shape=[jax.ShapeDtypeStruct((nq, n), BF16), jax.ShapeDtypeStruct((nq, n), BF16),
                   jax.ShapeDtypeStruct((ng, n), F32)],
        scratch_shapes=[pltpu.VMEM(wt.shape, BF16)],
        compiler_params=_params(("arbitrary",), VMEM_LIMIT),
        name="q_proj",
    )(h2, gain.reshape(1, d), wt, gq, cos_tt, sin_tt)


def _attn_kernel(qn_ref, qr_ref, gt_ref, ck_ref, cvt_ref, kaug_ref, kwin_ref, vaug_ref, vwaug_ref,
                 o_ref, pg_s, o_sel_s, o_win_s, *, tq, n_cmp, n_sb, n_qt):
    hp_n = HEADS_PER_KV
    qt = pl.program_id(2)
    t0 = qt * tq
    tpos = t0 + lax.broadcasted_iota(I32, (1, tq), 1)
    tpos4 = jnp.concatenate([tpos] * hp_n, axis=1)
    qn4 = jnp.concatenate([qn_ref[h * HEAD_DIM:(h + 1) * HEAD_DIM, :] for h in range(hp_n)], axis=1)
    qr4 = jnp.concatenate([qr_ref[h * HEAD_DIM:(h + 1) * HEAD_DIM, :] for h in range(hp_n)], axis=1)

    n_row = ck_ref.shape[2]
    s = _dot(ck_ref[0, 0], qn4)
    ci = lax.broadcasted_iota(I32, (n_row, 1), 0)
    vis = (ci * CMP_STRIDE + (CMP_BLOCK - 1) <= tpos4) & (ci < n_cmp)
    s = jnp.where(vis, s, NEG)
    p = jnp.where(vis, jnp.exp2(s - jnp.max(s, axis=0, keepdims=True)), 0.0)
    p = p / jnp.maximum(jnp.sum(p, axis=0, keepdims=True), TINY)
    o_cmp = _dot(cvt_ref[0, 0], p.astype(BF16))
    pg = p[:, 0:tq]
    for h in range(1, hp_n):
        pg = pg + p[:, h * tq:(h + 1) * tq]

    ratio = SEL_BLOCK // CMP_STRIDE
    scores = []
    for c in range(tq // LANES):
        pg_s[c, 0:8, :] = jnp.zeros((8, LANES), F32)
        pg_s[c, 8:8 + n_row, :] = pg[:, c * LANES:(c + 1) * LANES]
        sc_c = pg_s[c, pl.ds(8 + 1 - CMP_BLOCK // CMP_STRIDE, n_sb, stride=ratio), :]
        for o in range(2 - CMP_BLOCK // CMP_STRIDE, ratio):
            sc_c = sc_c + pg_s[c, pl.ds(8 + o, n_sb, stride=ratio), :]
        scores.append(sc_c)
    score = jnp.concatenate(scores, axis=1)
    jrow = lax.broadcasted_iota(I32, (n_sb, 1), 0)
    cur = tpos // SEL_BLOCK
    forced = (jrow == 0) | (jrow == cur) | (jrow == cur - 1)
    sc = jnp.where(jrow * SEL_BLOCK <= tpos, score + jnp.where(forced, BIG, 0.0), -BIG)
    rank = jnp.zeros((n_sb, tq), I32)
    for jp in range(n_sb):
        row = sc[jp:jp + 1, :]
        beats = (row > sc) | ((row == sc) & (jrow > jp))
        rank = rank + beats.astype(I32)
    n_sbp = kaug_ref.shape[3] - HEAD_DIM
    sel_neg = jnp.where(rank < min(N_SEL, n_sb), 0.0, NEG)
    if n_sbp > n_sb:
        sel_neg = jnp.concatenate([sel_neg, jnp.zeros((n_sbp - n_sb, tq), F32)], axis=0)
    sel_neg = sel_neg.astype(BF16)
    q_rot = [qr_ref[h * HEAD_DIM:(h + 1) * HEAD_DIM, :] for h in range(hp_n)]
    q_aug = jnp.concatenate([jnp.concatenate([q, sel_neg], axis=0) for q in q_rot], axis=1)
    q_win = jnp.concatenate(q_rot, axis=1)
    causal = jnp.where(lax.broadcasted_iota(I32, (tq, 1), 0) <= lax.broadcasted_iota(I32, (1, tq), 1), 0.0, NEG)
    causal = jnp.concatenate([causal] * hp_n, axis=1)

    for k in range(n_qt):
        @pl.when(qt == k)
        def _(k=k):
            lo = k * tq
            s_diag = _dot(kaug_ref[0, 0, lo:lo + tq, :], q_aug) + causal
            m = jnp.max(s_diag, axis=0, keepdims=True)
            if k > 0:
                s_top = _dot(kaug_ref[0, 0, 0:lo, :], q_aug)
                m = jnp.maximum(m, jnp.max(s_top, axis=0, keepdims=True))
            acc = _dot(vaug_ref[:, lo:lo + tq], jnp.exp2(s_diag - m).astype(BF16))
            if k > 0:
                acc = acc + _dot(vaug_ref[:, 0:lo], jnp.exp2(s_top - m).astype(BF16))
            o_sel_s[...] = acc[:HEAD_DIM] / jnp.maximum(acc[HEAD_DIM:HEAD_DIM + 1], TINY)

            parts = []
            for kt in range(max(0, k - -(-WINDOW // tq)), k + 1):
                s_t = _dot(kwin_ref[0, 0, kt * tq:(kt + 1) * tq, :], q_win)
                d_lo = (k - kt) * tq - (tq - 1)
                d_hi = (k - kt) * tq + (tq - 1)
                if d_lo < 0 or d_hi >= WINDOW:
                    dq = ((k - kt) * tq + lax.broadcasted_iota(I32, (1, tq), 1)
                          - lax.broadcasted_iota(I32, (tq, 1), 0))
                    wb = jnp.where((dq >= 0) & (dq < WINDOW), 0.0, NEG)
                    s_t = s_t + jnp.concatenate([wb] * hp_n, axis=1)
                parts.append((kt, s_t))
            m_w = functools.reduce(jnp.maximum, [jnp.max(s_t, axis=0, keepdims=True) for _, s_t in parts])
            acc_w = None
            for kt, s_t in parts:
                term = _dot(vwaug_ref[:, kt * tq:(kt + 1) * tq], jnp.exp2(s_t - m_w).astype(BF16))
                acc_w = term if acc_w is None else acc_w + term
            o_win_s[...] = acc_w[:HEAD_DIM] / jnp.maximum(acc_w[HEAD_DIM:HEAD_DIM + 1], TINY)

    o_sel = o_sel_s[...]
    o_win = o_win_s[...]
    gt = gt_ref[...]
    for h in range(hp_n):
        sl = slice(h * tq, (h + 1) * tq)
        o = (gt[h:h + 1] * o_cmp[:, sl] + gt[hp_n + h:hp_n + h + 1] * o_sel[:, sl]
             + gt[2 * hp_n + h:2 * hp_n + h + 1] * o_win[:, sl])
        o_ref[h * HEAD_DIM:(h + 1) * HEAD_DIM, :] = o.astype(BF16)


def _attn_prompt(qn_t, qr_t, g_t, ck, cv_t, ksel, kwin, vsel_t, vwin_t, *, b, t, tq):
    nq = t // tq
    n_sb = t // SEL_BLOCK
    n_row = ck.shape[2]
    n_cmp = t // CMP_STRIDE - CMP_BLOCK // CMP_STRIDE + 1
    n_sbp = -(-n_sb // 32) * 32
    kw = N_KV_HEADS * HEAD_DIM
    onehot = (jnp.arange(t)[:, None] // SEL_BLOCK == jnp.arange(n_sbp)[None, :]).astype(BF16)
    kaug = jnp.concatenate([ksel.reshape(N_KV_HEADS, b, t, HEAD_DIM),
                            jnp.broadcast_to(onehot, (N_KV_HEADS, b, t, n_sbp))], axis=-1)
    ones_rows = jnp.zeros((N_KV_HEADS, V_PAD_ROWS, b * t), BF16).at[:, 0].set(1.0)

    def with_ones(v_t):
        v3 = jnp.concatenate([v_t.reshape(N_KV_HEADS, HEAD_DIM, b * t), ones_rows], axis=1)
        return v3.reshape(N_KV_HEADS * (HEAD_DIM + V_PAD_ROWS), b * t)

    v_rows = HEAD_DIM + V_PAD_ROWS
    return pl.pallas_call(
        functools.partial(_attn_kernel, tq=tq, n_cmp=n_cmp, n_sb=n_sb, n_qt=nq),
        grid=(b, N_KV_HEADS, nq),
        in_specs=[
            pl.BlockSpec((kw, tq), lambda i, g, q: (g, i * nq + q)),
            pl.BlockSpec((kw, tq), lambda i, g, q: (g, i * nq + q)),
            pl.BlockSpec((GATE_ROWS, tq), lambda i, g, q: (g, i * nq + q)),
            pl.BlockSpec((1, 1, n_row, HEAD_DIM), lambda i, g, q: (i, g, 0, 0)),
            pl.BlockSpec((1, 1, HEAD_DIM, n_row), lambda i, g, q: (i, g, 0, 0)),
            pl.BlockSpec((1, 1, t, HEAD_DIM + n_sbp), lambda i, g, q: (g, i, 0, 0)),
            pl.BlockSpec((1, 1, t, HEAD_DIM), lambda i, g, q: (g, i, 0, 0)),
            pl.BlockSpec((v_rows, t), lambda i, g, q: (g, i)),
            pl.BlockSpec((v_rows, t), lambda i, g, q: (g, i)),
        ],
        out_specs=pl.BlockSpec((kw, tq), lambda i, g, q: (g, i * nq + q)),
        out_shape=jax.ShapeDtypeStruct((N_HEADS * HEAD_DIM, b * t), BF16),
        scratch_shapes=[pltpu.VMEM((tq // LANES, n_row + 8, LANES), F32),
                        pltpu.VMEM((HEAD_DIM, HEADS_PER_KV * tq), F32),
                        pltpu.VMEM((HEAD_DIM, HEADS_PER_KV * tq), F32)],
        compiler_params=_params(("arbitrary", "arbitrary", "arbitrary"), VMEM_LIMIT),
        name="attn_prompt",
    )(qn_t, qr_t, g_t, ck, cv_t, kaug, kwin.reshape(N_KV_HEADS, b, t, HEAD_DIM), with_ones(vsel_t), with_ones(vwin_t))


def _oproj_t_kernel(ot_ref, h_ref, w_ref, out_ref, w_s):
    @pl.when(pl.program_id(0) == 0)
    def _():
        w_s[...] = w_ref[...].astype(BF16)

    o = ot_ref[...].astype(F32).T.astype(BF16)
    out_ref[...] = h_ref[...] + _dot(o, w_s[...])


def _oproj_t(o_t, h2, w_o, *, tt):
    n, d = h2.shape
    return pl.pallas_call(
        _oproj_t_kernel,
        grid=(n // tt,),
        in_specs=[
            pl.BlockSpec((o_t.shape[0], tt), lambda i: (0, i)),
            pl.BlockSpec((tt, d), lambda i: (i, 0)),
            pl.BlockSpec(w_o.shape, lambda i: (0, 0)),
        ],
        out_specs=pl.BlockSpec((tt, d), lambda i: (i, 0)),
        out_shape=jax.ShapeDtypeStruct((n, d), F32),
        scratch_shapes=[pltpu.VMEM(w_o.shape, BF16)],
        compiler_params=_params(("arbitrary",), VMEM_LIMIT),
        name="o_proj",
    )(o_t, h2, w_o)


def _oproj_n_kernel(o_ref, h_ref, w_ref, out_ref):
    out_ref[...] = h_ref[...] + _dot(o_ref[...].astype(BF16), w_ref[...].astype(BF16))


def _oproj_n(o, h2, w_o):
    n, d = h2.shape
    return pl.pallas_call(
        _oproj_n_kernel,
        out_shape=jax.ShapeDtypeStruct((n, d), F32),
        compiler_params=_params((), VMEM_LIMIT),
        name="o_proj_sample",
    )(o, h2, w_o)


def _attn_sample_kernel(pt_ref, *refs, n_pages, n_cmp, n_sb, ts, past, n_buf):
    del pt_ref
    page_refs = refs[:n_pages]
    (qn_ref, qr_ref, ckt_ref, cvt_ref, win_ref, knew_ref, vnew_ref, kwnew_ref, vwnew_ref, gate_ref,
     sel_ref, e_ref, o_ref, mask_s, m_s, l_s, acc_s, ocmp_s) = refs[n_pages:]
    q_step = pl.program_id(1)
    rows = HEADS_PER_KV * N_KV_HEADS * ts
    grp_rows = N_KV_HEADS * ts
    row = lax.broadcasted_iota(I32, (rows, 1), 0)
    qpos = past + row % ts
    qr = qr_ref[0]

    def tile_rows(x):
        return jnp.concatenate([x] * HEADS_PER_KV, axis=0)

    def online_update(s, mk, v_dot):
        s = jnp.where(mk, s, NEG)
        m_new = jnp.maximum(m_s[...], jnp.max(s, axis=-1, keepdims=True))
        alpha = jnp.exp2(m_s[...] - m_new)
        p = jnp.where(mk, jnp.exp2(s - m_new), 0.0)
        l_s[...] = l_s[...] * alpha + jnp.sum(p, axis=-1, keepdims=True)
        acc_s[...] = acc_s[...] * alpha + v_dot(p.astype(BF16))
        m_s[...] = m_new

    @pl.when(q_step == 0)
    def _():
        n_row = ckt_ref.shape[2]
        s = _dot(qn_ref[0], ckt_ref[0])
        ci = lax.broadcasted_iota(I32, (1, n_row), 1)
        vis = (ci * CMP_STRIDE + (CMP_BLOCK - 1) <= qpos) & (ci < n_cmp)
        s = jnp.where(vis, s, NEG)
        p = jnp.where(vis, jnp.exp2(s - jnp.max(s, axis=-1, keepdims=True)), 0.0)
        p = p / jnp.maximum(jnp.sum(p, axis=-1, keepdims=True), TINY)
        ocmp_s[...] = _nt_dot(p.astype(BF16), cvt_ref[0])
        pg = p[0:grp_rows]
        for h in range(1, HEADS_PER_KV):
            pg = pg + p[h * grp_rows:(h + 1) * grp_rows]
        score = _dot(pg, sel_ref[...], precision=HIGHEST)
        width = score.shape[1]
        j = lax.broadcasted_iota(I32, (1, width), 1)
        tq = qpos[0:grp_rows]
        cur = tq // SEL_BLOCK
        forced = (j == 0) | (j == cur) | (j == cur - 1)
        sc = jnp.where(j * SEL_BLOCK <= tq, score + jnp.where(forced, BIG, 0.0), -BIG)
        sc = jnp.where(j < n_sb, sc, -2.0 * BIG)
        rank = jnp.zeros((grp_rows, width), I32)
        for jp in range(n_sb):
            col = sc[:, jp:jp + 1]
            beats = (col > sc) | ((col == sc) & (j > jp))
            rank = rank + beats.astype(I32)
        sel = jnp.where((rank < min(N_SEL, n_sb)) & (j < n_sb), 1.0, 0.0).astype(BF16)
        mask_s[...] = _dot(sel, e_ref[...])
        m_s[...] = jnp.full(m_s.shape, NEG, F32)
        l_s[...] = jnp.zeros(l_s.shape, F32)
        acc_s[...] = jnp.zeros(acc_s.shape, F32)

    width = n_pages * PAGE_SIZE
    kt = jnp.concatenate([page_refs[i][0, 0].reshape(KV_WIDTH, PAGE_SIZE) for i in range(n_pages)], axis=1)
    vt = jnp.concatenate([page_refs[i][0, 1].reshape(KV_WIDTH, PAGE_SIZE) for i in range(n_pages)], axis=1)
    k0 = pl.multiple_of(q_step * width, width)
    mk = tile_rows(mask_s[:, pl.ds(k0, width)]) > 0.5
    online_update(_dot(qr, kt.astype(BF16)), mk, lambda p: _nt_dot(p, vt.astype(BF16)))

    @pl.when(q_step == pl.num_programs(1) - 1)
    def _():
        lane = lax.broadcasted_iota(I32, (1, LANES), 1)
        new_ok = (lane < ts) & (past + lane <= qpos)
        mk_new = (tile_rows(mask_s[:, past:past + LANES]) > 0.5) & new_ok
        online_update(_nt_dot(qr, knew_ref[0]), mk_new, lambda p: _dot(p, vnew_ref[0]))
        o_sel = acc_s[...] / jnp.maximum(l_s[...], TINY)
        bi = lax.broadcasted_iota(I32, (1, n_buf), 1)
        dq = qpos - (past - n_buf + bi)
        ok_buf = (dq >= 0) & (dq < WINDOW)
        dq_new = qpos - (past + lane)
        ok_new = (lane < ts) & (dq_new >= 0) & (dq_new < WINDOW)
        s_w = jnp.concatenate([_dot(qr, win_ref[0, 0].astype(BF16)), _nt_dot(qr, kwnew_ref[0])], axis=1)
        ok = jnp.concatenate([jnp.broadcast_to(ok_buf, (rows, n_buf)), jnp.broadcast_to(ok_new, (rows, LANES))], axis=1)
        s_w = jnp.where(ok, s_w, NEG)
        p_w = jnp.where(ok, jnp.exp2(s_w - jnp.max(s_w, axis=-1, keepdims=True)), 0.0)
        p_w = (p_w / jnp.maximum(jnp.sum(p_w, axis=-1, keepdims=True), TINY)).astype(BF16)
        o_win = _nt_dot(p_w[:, :n_buf], win_ref[0, 1].astype(BF16)) + _dot(p_w[:, n_buf:], vwnew_ref[0])
        row_g = (row // ts) % N_KV_HEADS

        def own_group(o):
            out = jnp.zeros((rows, HEAD_DIM), F32)
            for g in range(N_KV_HEADS):
                out = out + jnp.where(row_g == g, o[:, g * HEAD_DIM:(g + 1) * HEAD_DIM], 0.0)
            return out

        o_ref[0] = (gate_ref[0, 0] * own_group(ocmp_s[...]) + gate_ref[0, 1] * own_group(o_sel)
                    + gate_ref[0, 2] * own_group(o_win))


def _attn_sample(cache5, page_table, qn_bd, qr_bd, ck_t, cv_t, win4, knew, vnew, kwnew, vwnew, gates,
                 *, n_pages, ts, past):
    b, pages_per_seq = page_table.shape
    steps = pages_per_seq // n_pages
    rows = qn_bd.shape[1]
    n_chunk = ck_t.shape[2]
    n_cmp = n_chunk - CMP_BLOCK // CMP_STRIDE + 1
    n_keys = past + LANES
    n_sb = -(-(past + ts) // SEL_BLOCK)
    n_sb_pad = -(-n_sb // LANES) * LANES
    n_buf = win4.shape[3]
    ratio = SEL_BLOCK // CMP_STRIDE
    ci = jnp.arange(n_chunk)[:, None]
    jb = jnp.arange(n_sb_pad)[None, :]
    sel_map = ((ci >= ratio * jb + 1 - CMP_BLOCK // CMP_STRIDE) & (ci < ratio * jb + ratio)
               & (ci < n_cmp) & (jb < n_sb)).astype(F32)
    expand = (jnp.arange(n_keys)[None, :] // SEL_BLOCK == jnp.arange(n_sb_pad)[:, None]).astype(BF16)

    def page_spec(j):
        return pl.BlockSpec((1, 2, N_KV_HEADS, HEAD_DIM, PAGE_SIZE),
                            lambda i, q, pt: (pt[i, q * n_pages + j], 1, 0, 0, 0))

    per_seq = lambda a: pl.BlockSpec((1,) + a.shape[1:], lambda i, q, pt: (i,) + (0,) * (a.ndim - 1))
    full = lambda a: pl.BlockSpec(a.shape, lambda i, q, pt: (0,) * a.ndim)
    return pl.pallas_call(
        functools.partial(_attn_sample_kernel, n_pages=n_pages, n_cmp=n_cmp, n_sb=n_sb, ts=ts, past=past,
                          n_buf=n_buf),
        grid_spec=pltpu.PrefetchScalarGridSpec(
            num_scalar_prefetch=1,
            grid=(b, steps),
            in_specs=[page_spec(j) for j in range(n_pages)]
            + [per_seq(a) for a in (qn_bd, qr_bd, ck_t, cv_t, win4, knew, vnew, kwnew, vwnew, gates)]
            + [full(sel_map), full(expand)],
            out_specs=pl.BlockSpec((1, rows, HEAD_DIM), lambda i, q, pt: (i, 0, 0)),
            scratch_shapes=[
                pltpu.VMEM((N_KV_HEADS * ts, n_keys), F32),
                pltpu.VMEM((rows, 1), F32), pltpu.VMEM((rows, 1), F32),
                pltpu.VMEM((rows, KV_WIDTH), F32), pltpu.VMEM((rows, KV_WIDTH), F32),
            ],
        ),
        out_shape=jax.ShapeDtypeStruct((b, rows, HEAD_DIM), F32),
        compiler_params=_params(("arbitrary", "arbitrary"), VMEM_LIMIT),
        name="attn_sample",
    )(page_table, *([cache5] * n_pages), qn_bd, qr_bd, ck_t, cv_t, win4, knew, vnew, kwnew, vwnew, gates,
      sel_map, expand)


def _router_weights(router_group, router_expert):
    rt = jnp.concatenate([router_group, router_expert], axis=1).T
    return jnp.pad(rt, ((0, 24 - rt.shape[0]), (0, 0)))


def _trunk_prompt(x, p, *, tt_pool=256, tm=1024, tt=512, tq=256):
    (norm_mix, norm_ffn, pool_w, pool_scale, kv_norm, w_kv, k_norm, cmp_pe, cmp_w1, cmp_b1, cmp_w2,
     w_qg, q_norm, w_o, router_group, router_expert, w_gate_up, w_down) = p
    b, t, d = x.shape
    n = b * t
    pos = jnp.arange(t)
    h, new_pool = _pool_layer(x, jnp.zeros((b, POOL_HALO, d), F32), norm_mix[0], pool_w[0], pool_scale[0],
                              tt=tt_pool, clip=True)
    h = _moe(h.reshape(n, d), norm_ffn[0], _router_weights(router_group[0], router_expert[0]),
             w_gate_up, w_down, 0, tm=tm)
    cos_t, sin_t = _rope_tables_transposed(pos)
    kv_t, win_t, craw, ksel, kwin, vsel_t, vwin_t = _kvproj(h.reshape(b, t, d), kv_norm, w_kv, k_norm,
                                                            cos_t, sin_t, tt=tt)
    ck, cv = _compress_prompt(craw, cmp_pe, cmp_w1, cmp_b1, cmp_w2, k_norm[0], b=b, t=t)
    qn_t, qr_t, g_t = _qproj(h, norm_mix[1], _qg_weights(w_qg[0]), q_norm[0], cos_t, sin_t,
                             tt=tt, pos_blocks=t // tt)
    o_t = _attn_prompt(qn_t, qr_t, g_t, ck, jnp.swapaxes(cv, 2, 3), ksel, kwin, vsel_t, vwin_t, b=b, t=t, tq=tq)
    h = _oproj_t(o_t, h, w_o[0], tt=tt)
    h = _moe(h, norm_ffn[1], _router_weights(router_group[1], router_expert[1]),
             w_gate_up, w_down, 1, tm=tm)
    n_win = min(WINDOW, t)
    kv_new = kv_t.reshape(b, 4, N_KV_HEADS, HEAD_DIM, t).transpose(0, 4, 1, 2, 3)
    win_new = win_t[:, :, t - n_win:].reshape(b, 2, N_KV_HEADS, HEAD_DIM, n_win).transpose(0, 4, 1, 2, 3)
    return h.reshape(b, t, d), new_pool[None], kv_new, win_new


def _trunk_sample(x, state_pool, cache_kv, page_table, state_win, p, *, n_pages=32):
    (norm_mix, norm_ffn, pool_w, pool_scale, kv_norm, w_kv, k_norm, cmp_pe, cmp_w1, cmp_b1, cmp_w2,
     w_qg, q_norm, w_o, router_group, router_expert, w_gate_up, w_down) = p
    b, ts, d = x.shape
    n = b * ts
    past = page_table.shape[1] * PAGE_SIZE
    n_buf = state_win.shape[1]
    prev16 = jnp.pad(state_pool[0], ((0, 0), (POOL_HALO - POOL_STATE, 0), (0, 0)))
    h, new_pool = _pool_layer(x, prev16, norm_mix[0], pool_w[0], pool_scale[0], tt=ts, clip=False)
    h = _moe(h.reshape(n, d), norm_ffn[0], _router_weights(router_group[0], router_expert[0]),
             w_gate_up, w_down, 0, tm=n)
    cos_t, sin_t = _rope_tables_transposed(past + jnp.arange(ts))
    cos_t = jnp.tile(cos_t, (1, b))
    sin_t = jnp.tile(sin_t, (1, b))
    kv_t, win_t, _, _, _, _, _ = _kvproj(h.reshape(1, n, d), kv_norm, w_kv, k_norm, cos_t, sin_t, tt=n)
    kv_rows = kv_t[0].T
    win_rows = win_t[0].T

    cache5 = cache_kv.transpose(0, 2, 3, 4, 1)
    ck, cv = _compress_pages(cache5, page_table, cmp_pe, cmp_w1, cmp_b1, cmp_w2, k_norm[0], n_pages=n_pages)
    n_chunk = ck.shape[2]
    ck_t = ck.transpose(0, 1, 3, 2).reshape(b, KV_WIDTH, n_chunk)
    cv_t = cv.transpose(0, 1, 3, 2).reshape(b, KV_WIDTH, n_chunk)

    qn_t, qr_t, g_t = _qproj(h, norm_mix[1], _qg_weights(w_qg[0]), q_norm[0], cos_t, sin_t, tt=n, pos_blocks=1)

    def block_diag_queries(q_t):
        q5 = q_t.reshape(N_KV_HEADS, HEADS_PER_KV, HEAD_DIM, b, ts).transpose(3, 1, 0, 4, 2)
        eye = jnp.eye(N_KV_HEADS, dtype=q_t.dtype)
        qbd = q5[:, :, :, :, None, :] * eye[None, None, :, None, :, None]
        return qbd.reshape(b, HEADS_PER_KV * N_KV_HEADS * ts, KV_WIDTH)

    gates = g_t.reshape(N_KV_HEADS, GATE_ROWS, b, ts)[:, :3 * HEADS_PER_KV]
    gates = gates.reshape(N_KV_HEADS, 3, HEADS_PER_KV, b, ts).transpose(3, 1, 2, 0, 4)
    gates = jnp.broadcast_to(gates.reshape(b, 3, HEADS_PER_KV * N_KV_HEADS * ts, 1),
                             (b, 3, HEADS_PER_KV * N_KV_HEADS * ts, HEAD_DIM))

    def new_rows(rows2):
        return jnp.pad(rows2.reshape(b, ts, KV_WIDTH), ((0, 0), (0, LANES - ts), (0, 0))).astype(BF16)

    win4 = state_win.transpose(0, 2, 3, 4, 1).reshape(b, 2, KV_WIDTH, n_buf)
    o = _attn_sample(cache5, page_table, block_diag_queries(qn_t), block_diag_queries(qr_t), ck_t, cv_t, win4,
                     new_rows(kv_rows[:, 2 * KV_WIDTH:3 * KV_WIDTH]), new_rows(kv_rows[:, 3 * KV_WIDTH:]),
                     new_rows(win_rows[:, :KV_WIDTH]), new_rows(win_rows[:, KV_WIDTH:]), gates,
                     n_pages=n_pages, ts=ts, past=past)
    o = o.reshape(b, HEADS_PER_KV, N_KV_HEADS, ts, HEAD_DIM).transpose(0, 3, 2, 1, 4).reshape(n, N_HEADS * HEAD_DIM)
    h = _oproj_n(o, h, w_o[0])
    h = _moe(h, norm_ffn[1], _router_weights(router_group[1], router_expert[1]),
             w_gate_up, w_down, 1, tm=n)
    kv_new = kv_rows.reshape(b, ts, 4, N_KV_HEADS, HEAD_DIM)
    win_new = jnp.concatenate([state_win, win_rows.reshape(b, ts, 2, N_KV_HEADS, HEAD_DIM)], axis=1)[:, -n_buf:]
    return h.reshape(b, ts, d), new_pool[None], kv_new, win_new


def kernel(x_prompt, x_sample, state_pool, cache_kv, page_table, state_win, norm_mix, norm_ffn, pool_w, pool_scale, kv_norm, w_kv, k_norm, cmp_pe, cmp_w1, cmp_b1, cmp_w2, w_qg, q_norm, w_o, router_group, router_expert, w_gate_up, w_down):
    params = (norm_mix, norm_ffn, pool_w, pool_scale, kv_norm, w_kv, k_norm, cmp_pe, cmp_w1, cmp_b1,
              cmp_w2, w_qg, q_norm, w_o, router_group, router_expert, _cast_bf16(w_gate_up), _cast_bf16(w_down))
    y_p, pool_p, kv_p, win_p = _trunk_prompt(x_prompt, params)
    y_s, pool_s, kv_s, win_s = _trunk_sample(x_sample, state_pool, cache_kv, page_table, state_win, params)
    return y_p, y_s, pool_p, pool_s, kv_p, kv_s, win_p, win_s
```

```python
import functools

import jax
import jax.numpy as jnp
from jax import lax
from jax.experimental import pallas as pl
from jax.experimental.pallas import tpu as pltpu

F32 = jnp.float32
BF16 = jnp.bfloat16
I32 = jnp.int32
HIGHEST = lax.Precision.HIGHEST

D_MODEL = 1024
POOL_WINDOWS = (2, 4, 8, 16)
POOL_GROUP_DIM = D_MODEL // len(POOL_WINDOWS)
POOL_STATE = max(POOL_WINDOWS) - 1
POOL_HALO = 16
N_HEADS = 16
HEAD_DIM = 64
HALF_DIM = HEAD_DIM // 2
N_KV_HEADS = 4
HEADS_PER_KV = N_HEADS // N_KV_HEADS
KV_WIDTH = N_KV_HEADS * HEAD_DIM
CMP_BLOCK = 32
CMP_STRIDE = 16
CMP_HIDDEN = 2 * HEAD_DIM
SEL_BLOCK = 64
N_SEL = 16
WINDOW = 512
PAGE_SIZE = 128
ROPE_THETA = 10000.0
SCALE = HEAD_DIM ** -0.5
Q_SCALE = SCALE * 1.4426950408889634
N_GROUPS = 4
EXPERTS_PER_GROUP = 4
N_EXPERTS = N_GROUPS * EXPERTS_PER_GROUP
D_EXPERT = 512
EPS = 1e-6
NEG = -1e30
TINY = 1e-30
BIG = 1e4

LANES = 128
GATE_ROWS = 16
ROUTE_GROUP_ROW = 3 * N_EXPERTS
MOE_SEG_ALIGN = 16
MOE_CHUNK = 144
MOE_STEP_EXPERTS = 2
V_PAD_ROWS = 16
VMEM_LIMIT = 56 * 1024 * 1024


def _params(sem, vmem=None):
    return pltpu.CompilerParams(dimension_semantics=sem, vmem_limit_bytes=vmem)


def _rms(x, g):
    return x * lax.rsqrt(jnp.mean(x * x, axis=-1, keepdims=True) + EPS) * g


def _nt_dot(a, b, precision=None):
    return lax.dot_general(a, b, (((1,), (1,)), ((), ())), precision=precision,
                           preferred_element_type=F32)


def _dot(a, b, precision=None):
    return jnp.dot(a, b, precision=precision, preferred_element_type=F32)


def _sigmoid(x):
    return 1.0 / (1.0 + jnp.exp(-x))


def _pool_kernel(h_ref, prev_ref, g_ref, w_ref, sc_ref, o_ref, np_ref, ext_ref, lvl_ref, *, tt, clip):
    t = pl.program_id(1)
    x = h_ref[0]
    xn = _rms(x, g_ref[...])

    @pl.when(t == 0)
    def _():
        ext_ref[0:POOL_HALO, :] = prev_ref[0]

    @pl.when(t > 0)
    def _():
        ext_ref[0:POOL_HALO, :] = ext_ref[tt:tt + POOL_HALO, :]

    ext_ref[POOL_HALO:POOL_HALO + tt, :] = xn
    if clip:
        tpos = t * tt + lax.broadcasted_iota(I32, (tt, 1), 0)
    outs = []
    for gi, w in enumerate(POOL_WINDOWS):
        lo = gi * POOL_GROUP_DIM
        hi = lo + POOL_GROUP_DIM
        n = tt + POOL_HALO
        lvl_ref[...] = ext_ref[:, lo:hi]
        span = 1
        while span < w:
            lvl_ref[span:n, :] = lvl_ref[span:n, :] + lvl_ref[0:n - span, :]
            span *= 2
        acc = lvl_ref[POOL_HALO:n, :]
        if clip:
            mean = acc / jnp.minimum(tpos + 1, w).astype(F32)
        else:
            mean = acc * (1.0 / w)
        d = (mean - xn[:, lo:hi]).astype(BF16)
        outs.append(_dot(d, w_ref[gi].astype(BF16)))
    o_ref[0] = x + jnp.concatenate(outs, axis=1) * sc_ref[...]

    @pl.when(t == pl.num_programs(1) - 1)
    def _():
        np_ref[0] = ext_ref[tt + POOL_HALO - POOL_STATE:tt + POOL_HALO, :]


def _pool_layer(h, prev16, gain, w_pool, scale, *, tt, clip):
    b, t, d = h.shape
    return pl.pallas_call(
        functools.partial(_pool_kernel, tt=tt, clip=clip),
        grid=(b, t // tt),
        in_specs=[
            pl.BlockSpec((1, tt, d), lambda i, j: (i, j, 0)),
            pl.BlockSpec((1, POOL_HALO, d), lambda i, j: (i, 0, 0)),
            pl.BlockSpec((1, d), lambda i, j: (0, 0)),
            pl.BlockSpec(w_pool.shape, lambda i, j: (0, 0, 0)),
            pl.BlockSpec((1, d), lambda i, j: (0, 0)),
        ],
        out_specs=[
            pl.BlockSpec((1, tt, d), lambda i, j: (i, j, 0)),
            pl.BlockSpec((1, POOL_STATE, d), lambda i, j: (i, 0, 0)),
        ],
        out_shape=[jax.ShapeDtypeStruct((b, t, d), F32),
                   jax.ShapeDtypeStruct((b, POOL_STATE, d), F32)],
        scratch_shapes=[pltpu.VMEM((tt + POOL_HALO, d), F32), pltpu.VMEM((tt + POOL_HALO, POOL_GROUP_DIM), F32)],
        compiler_params=_params(("arbitrary", "arbitrary")),
        name="pool_layer",
    )(h, prev16, gain.reshape(1, d), w_pool, scale.reshape(1, d))


def _softmax_rows(rows):
    m = functools.reduce(jnp.maximum, rows)
    es = [jnp.exp(r - m) for r in rows]
    s = functools.reduce(lambda a, b: a + b, es)
    return [e / s for e in es]


def _router_kernel(h_ref, g_ref, rt_ref, xn_ref, comb_ref, grp_ref):
    xn = _rms(h_ref[...], g_ref[...])
    x_hi = xn.astype(BF16)
    xn_ref[...] = x_hi
    x_lo = (xn - x_hi.astype(F32)).astype(BF16)
    r_hi = rt_ref[...].astype(BF16)
    r_lo = (rt_ref[...] - r_hi.astype(F32)).astype(BF16)
    lt = _nt_dot(r_hi, x_hi) + (_nt_dot(r_hi, x_lo) + _nt_dot(r_lo, x_hi))
    pg = _softmax_rows([lt[i:i + 1, :] for i in range(N_GROUPS)])
    g_val = functools.reduce(jnp.maximum, pg)
    g_idx = jnp.full(g_val.shape, N_GROUPS - 1, I32)
    for i in range(N_GROUPS - 2, -1, -1):
        g_idx = jnp.where(pg[i] == g_val, i, g_idx)
    le = []
    for j in range(EXPERTS_PER_GROUP):
        v = lt[N_GROUPS + (N_GROUPS - 1) * EXPERTS_PER_GROUP + j:N_GROUPS + (N_GROUPS - 1) * EXPERTS_PER_GROUP + j + 1, :]
        for gi in range(N_GROUPS - 2, -1, -1):
            r = N_GROUPS + gi * EXPERTS_PER_GROUP + j
            v = jnp.where(g_idx == gi, lt[r:r + 1, :], v)
        le.append(v)
    pe = _softmax_rows(le)
    ranks = []
    for j in range(EXPERTS_PER_GROUP):
        r = jnp.zeros(g_val.shape, I32)
        for i in range(EXPERTS_PER_GROUP):
            if i == j:
                continue
            beats = (pe[i] > pe[j]) | (pe[i] == pe[j]) if i < j else (pe[i] > pe[j])
            r = r + beats.astype(I32)
        ranks.append(r)
    vals, idxs = [], []
    for k in range(2):
        v = jnp.zeros(g_val.shape, F32)
        ix = jnp.zeros(g_val.shape, I32)
        for j in range(EXPERTS_PER_GROUP):
            hit = ranks[j] == k
            v = jnp.where(hit, pe[j], v)
            ix = jnp.where(hit, j, ix)
        vals.append(v)
        idxs.append(ix)
    tot = vals[0] + vals[1]
    erow = lax.broadcasted_iota(I32, (LANES, g_val.shape[1]), 0)
    comb_t = jnp.where(erow == ROUTE_GROUP_ROW + g_idx, 1.0, 0.0)
    for k in range(2):
        wk = g_val * (vals[k] / tot)
        w_hi = wk.astype(BF16).astype(F32)
        w_mid = (wk - w_hi).astype(BF16).astype(F32)
        w_lo = (wk - w_hi - w_mid).astype(BF16).astype(F32)
        eid = g_idx * EXPERTS_PER_GROUP + idxs[k]
        for part, term in enumerate((w_hi, w_mid, w_lo)):
            comb_t = comb_t + jnp.where(erow == part * N_EXPERTS + eid, term, 0.0)
    comb_ref[...] = comb_t.T
    grp_ref[...] = comb_t[ROUTE_GROUP_ROW:ROUTE_GROUP_ROW + 8]


def _router(h2, gain, rt, *, tm):
    n, d = h2.shape
    return pl.pallas_call(
        _router_kernel,
        grid=(n // tm,),
        in_specs=[
            pl.BlockSpec((tm, d), lambda i: (i, 0)),
            pl.BlockSpec((1, d), lambda i: (0, 0)),
            pl.BlockSpec(rt.shape, lambda i: (0, 0)),
        ],
        out_specs=[
            pl.BlockSpec((tm, d), lambda i: (i, 0)),
            pl.BlockSpec((tm, LANES), lambda i: (i, 0)),
            pl.BlockSpec((8, tm), lambda i: (0, i)),
        ],
        out_shape=[jax.ShapeDtypeStruct((n, d), BF16), jax.ShapeDtypeStruct((n, LANES), F32),
                   jax.ShapeDtypeStruct((8, n), F32)],
        compiler_params=_params(("arbitrary",)),
        name="moe_router",
    )(h2, gain.reshape(1, d), rt)


def _cast_kernel(x_ref, o_ref):
    o_ref[...] = x_ref[...].astype(o_ref.dtype)


def _cast_bf16(w):
    blk = (1, 1) + w.shape[2:]
    return pl.pallas_call(
        _cast_kernel,
        grid=w.shape[:2],
        in_specs=[pl.BlockSpec(blk, lambda i, j: (i, j, 0, 0))],
        out_specs=pl.BlockSpec(blk, lambda i, j: (i, j, 0, 0)),
        out_shape=jax.ShapeDtypeStruct(w.shape, BF16),
        compiler_params=_params(("arbitrary", "arbitrary")),
        name="cast_weights",
    )(w)


def _moe_kernel(cnt_ref, base_ref, x_ref, comb_ref, grp_ref, h_ref, wgu_ref, wd_ref, o_ref,
                pt_s, z_s, y_s, cs_s, slot_s, *, r_max, unroll):
    w = pl.program_id(0)
    step = pl.program_id(1)
    nw = x_ref.shape[0]
    g = step // (EXPERTS_PER_GROUP // MOE_STEP_EXPERTS)

    @pl.when(step == 0)
    def _():
        lane = lax.broadcasted_iota(I32, (1, LANES), 1)
        in_grp = (lane >= ROUTE_GROUP_ROW) & (lane < ROUTE_GROUP_ROW + N_GROUPS)
        oh = jnp.where(in_grp, comb_ref[...], 0.0)
        base_row = jnp.zeros((1, LANES), F32)
        for gi in range(N_GROUPS):
            base_row = jnp.where(lane == ROUTE_GROUP_ROW + gi, base_ref[w, gi].astype(F32), base_row)
        ri = lax.broadcasted_iota(I32, (nw, nw), 0)
        ci = lax.broadcasted_iota(I32, (nw, nw), 1)
        before = _dot(jnp.where(ci < ri, 1.0, 0.0).astype(BF16), oh.astype(BF16))
        slot_s[...] = jnp.sum(oh * (before + base_row), axis=-1, keepdims=True)
        oh_t = grp_ref[...]
        before_t = _dot(oh_t.astype(BF16), jnp.where(ri < ci, 1.0, 0.0).astype(BF16))
        slot_row = jnp.zeros((1, nw), F32)
        for gi in range(N_GROUPS):
            slot_row = slot_row + oh_t[gi:gi + 1] * (before_t[gi:gi + 1] + base_ref[w, gi].astype(F32))
        rows = lax.broadcasted_iota(I32, (r_max, 1), 0)
        pt_s[...] = jnp.where(rows == slot_row.astype(I32), 1.0, 0.0).astype(BF16)
        z_s[...] = _dot(pt_s[...], x_ref[...]).astype(BF16)
        cs_s[...] = _dot(pt_s[...], comb_ref[...].astype(BF16))
        y_s[...] = jnp.zeros(y_s.shape, F32)

    seg0 = base_ref[w, g]
    seg1 = seg0 + cnt_ref[w, g]
    lane_c = lax.broadcasted_iota(I32, (MOE_CHUNK, LANES), 1)
    in_terms = lane_c < 3 * N_EXPERTS

    def chunk(lo):
        r0 = pl.multiple_of(jnp.minimum(lo, r_max - MOE_CHUNK), MOE_SEG_ALIGN)
        z = z_s[pl.ds(r0, MOE_CHUNK), :]
        cs = cs_s[pl.ds(r0, MOE_CHUNK), :]
        acc = None
        for k in range(MOE_STEP_EXPERTS):
            gu = _dot(z, wgu_ref[0, k])
            a = gu[:, :D_EXPERT]
            hdn = (a * _sigmoid(a) * gu[:, D_EXPERT:]).astype(BF16)
            y = _dot(hdn, wd_ref[0, k])
            is_e = in_terms & (lane_c % N_EXPERTS == step * MOE_STEP_EXPERTS + k)
            wgt = jnp.sum(jnp.where(is_e, cs, 0.0), axis=-1, keepdims=True)
            acc = wgt * y if acc is None else acc + wgt * y
        rows = r0 + lax.broadcasted_iota(I32, (MOE_CHUNK, 1), 0)
        y_s[pl.ds(r0, MOE_CHUNK), :] += jnp.where((rows >= lo) & (rows < seg1), acc, 0.0)

    def chunks(c, carry):
        for u in range(unroll):
            chunk(seg0 + (c * unroll + u) * MOE_CHUNK)
        return carry

    n_chunks = (cnt_ref[w, g] + MOE_CHUNK - 1) // MOE_CHUNK
    n_trips = n_chunks // unroll
    lax.fori_loop(0, n_trips, chunks, 0)
    for u in range(unroll - 1):
        @pl.when(n_chunks - n_trips * unroll > u)
        def _(u=u):
            chunk(seg0 + (n_trips * unroll + u) * MOE_CHUNK)

    @pl.when(step == pl.num_programs(1) - 1)
    def _():
        cols = lax.broadcasted_iota(I32, (1, r_max), 1)
        p = jnp.where(cols == slot_s[...].astype(I32), 1.0, 0.0).astype(BF16)
        o_ref[...] = h_ref[...] + _dot(p, y_s[...].astype(BF16))


def _moe(h2, gain, rt, w_gu, w_down, layer, *, tm):
    n, d = h2.shape
    if w_gu.dtype != BF16:
        w_gu, w_down = _cast_bf16(w_gu), _cast_bf16(w_down)
    xn, comb, grp = _router(h2, gain, rt, tm=tm)
    n_win = n // tm
    cnt = jnp.sum(grp[:N_GROUPS].reshape(N_GROUPS, n_win, tm), axis=-1).T.astype(I32)
    seg = (cnt + MOE_SEG_ALIGN - 1) // MOE_SEG_ALIGN * MOE_SEG_ALIGN
    base = jnp.cumsum(seg, axis=1) - seg
    r_max = -(-(tm + N_GROUPS * MOE_SEG_ALIGN + MOE_CHUNK) // LANES) * LANES
    return pl.pallas_call(
        functools.partial(_moe_kernel, r_max=r_max, unroll=max(1, -(-tm // (N_GROUPS * MOE_CHUNK)))),
        grid_spec=pltpu.PrefetchScalarGridSpec(
            num_scalar_prefetch=2,
            grid=(n_win, N_EXPERTS // MOE_STEP_EXPERTS),
            in_specs=[
                pl.BlockSpec((tm, d), lambda i, e, c, b: (i, 0)),
                pl.BlockSpec((tm, LANES), lambda i, e, c, b: (i, 0)),
                pl.BlockSpec((8, tm), lambda i, e, c, b: (0, i)),
                pl.BlockSpec((tm, d), lambda i, e, c, b: (i, 0)),
                pl.BlockSpec((1, MOE_STEP_EXPERTS, d, 2 * D_EXPERT), lambda i, e, c, b: (layer, e, 0, 0)),
                pl.BlockSpec((1, MOE_STEP_EXPERTS, D_EXPERT, d), lambda i, e, c, b: (layer, e, 0, 0)),
            ],
            out_specs=pl.BlockSpec((tm, d), lambda i, e, c, b: (i, 0)),
            scratch_shapes=[
                pltpu.VMEM((r_max, tm), BF16), pltpu.VMEM((r_max, d), BF16), pltpu.VMEM((r_max, d), F32),
                pltpu.VMEM((r_max, LANES), F32), pltpu.VMEM((tm, 1), F32),
            ],
        ),
        out_shape=jax.ShapeDtypeStruct((n, d), F32),
        compiler_params=_params(("arbitrary", "arbitrary"), VMEM_LIMIT),
        name="moe_experts",
    )(cnt, base, xn, comb, grp, h2, w_gu, w_down)


def _head_norm_rope_t(z, gain, c, s):
    outs = []
    for g in range(z.shape[0] // HEAD_DIM):
        zh = z[g * HEAD_DIM:(g + 1) * HEAD_DIM]
        zn = zh * lax.rsqrt(jnp.mean(zh * zh, axis=0, keepdims=True) + EPS) * gain
        x1 = zn[:HALF_DIM]
        x2 = zn[HALF_DIM:]
        outs.append(x1 * c - x2 * s)
        outs.append(x2 * c + x1 * s)
    return jnp.concatenate(outs, axis=0)


def _kvproj_kernel(h_ref, g_ref, wt_ref, gsel_ref, gwin_ref, cos_ref, sin_ref,
                   kvt_ref, wint_ref, craw_ref, ksel_ref, kwin_ref, vselt_ref, vwint_ref, w_s):
    @pl.when((pl.program_id(0) == 0) & (pl.program_id(1) == 0))
    def _():
        w_s[...] = wt_ref[...].astype(BF16)

    xn = _rms(h_ref[0], g_ref[...]).astype(BF16)
    kvt = _nt_dot(w_s[...], xn)
    tt = xn.shape[0]
    c = cos_ref[...]
    s = sin_ref[...]
    gsel = jnp.concatenate([gsel_ref[...]] * (tt // LANES), axis=1)
    gwin = jnp.concatenate([gwin_ref[...]] * (tt // LANES), axis=1)
    ksel = _head_norm_rope_t(kvt[2 * KV_WIDTH:3 * KV_WIDTH], gsel, c, s)
    kwin = _head_norm_rope_t(kvt[4 * KV_WIDTH:5 * KV_WIDTH], gwin, c, s)
    kvt_ref[0] = jnp.concatenate([kvt[:2 * KV_WIDTH], ksel, kvt[3 * KV_WIDTH:4 * KV_WIDTH]], axis=0)
    wint_ref[0] = jnp.concatenate([kwin, kvt[5 * KV_WIDTH:]], axis=0)
    vselt_ref[...] = kvt[3 * KV_WIDTH:4 * KV_WIDTH].astype(BF16)
    vwint_ref[...] = kvt[5 * KV_WIDTH:].astype(BF16)
    raw = kvt[:2 * KV_WIDTH].T
    for k in range(2 * KV_WIDTH // LANES):
        craw_ref[k] = raw[:, k * LANES:(k + 1) * LANES]
    ksel_n = ksel.T
    kwin_n = kwin.T
    for g in range(N_KV_HEADS):
        ksel_ref[g] = ksel_n[:, g * HEAD_DIM:(g + 1) * HEAD_DIM].astype(BF16)
        kwin_ref[g] = kwin_n[:, g * HEAD_DIM:(g + 1) * HEAD_DIM].astype(BF16)


def _kvproj(x3, gain, w_kv, k_norm, cos_tt, sin_tt, *, tt):
    b, t, d = x3.shape
    n = b * t
    nt = t // tt
    wt = w_kv.T
    gsel = jnp.broadcast_to(k_norm[1][:, None], (HEAD_DIM, LANES))
    gwin = jnp.broadcast_to(k_norm[2][:, None], (HEAD_DIM, LANES))
    full = lambda a: pl.BlockSpec(a.shape, lambda i, j: (0,) * a.ndim)
    return pl.pallas_call(
        _kvproj_kernel,
        grid=(b, nt),
        in_specs=[
            pl.BlockSpec((1, tt, d), lambda i, j: (i, j, 0)),
            pl.BlockSpec((1, d), lambda i, j: (0, 0)),
            full(wt), full(gsel), full(gwin),
            pl.BlockSpec((HALF_DIM, tt), lambda i, j: (0, j)),
            pl.BlockSpec((HALF_DIM, tt), lambda i, j: (0, j)),
        ],
        out_specs=[
            pl.BlockSpec((1, 4 * KV_WIDTH, tt), lambda i, j: (i, 0, j)),
            pl.BlockSpec((1, 2 * KV_WIDTH, tt), lambda i, j: (i, 0, j)),
            pl.BlockSpec((2 * KV_WIDTH // LANES, tt, LANES), lambda i, j: (0, i * nt + j, 0)),
            pl.BlockSpec((N_KV_HEADS, tt, HEAD_DIM), lambda i, j: (0, i * nt + j, 0)),
            pl.BlockSpec((N_KV_HEADS, tt, HEAD_DIM), lambda i, j: (0, i * nt + j, 0)),
            pl.BlockSpec((KV_WIDTH, tt), lambda i, j: (0, i * nt + j)),
            pl.BlockSpec((KV_WIDTH, tt), lambda i, j: (0, i * nt + j)),
        ],
        out_shape=[
            jax.ShapeDtypeStruct((b, 4 * KV_WIDTH, t), F32),
            jax.ShapeDtypeStruct((b, 2 * KV_WIDTH, t), F32),
            jax.ShapeDtypeStruct((2 * KV_WIDTH // LANES, n, LANES), F32),
            jax.ShapeDtypeStruct((N_KV_HEADS, n, HEAD_DIM), BF16),
            jax.ShapeDtypeStruct((N_KV_HEADS, n, HEAD_DIM), BF16),
            jax.ShapeDtypeStruct((KV_WIDTH, n), BF16),
            jax.ShapeDtypeStruct((KV_WIDTH, n), BF16),
        ],
        scratch_shapes=[pltpu.VMEM(wt.shape, BF16)],
        compiler_params=_params(("arbitrary", "arbitrary"), VMEM_LIMIT),
        name="kv_proj",
    )(x3, gain.reshape(1, d), wt, gsel, gwin, cos_tt, sin_tt)


def _rope_tables_transposed(pos):
    inv = 1.0 / (ROPE_THETA ** (jnp.arange(HALF_DIM, dtype=F32) * (2.0 / HEAD_DIM)))
    ang = pos.astype(F32)[:, None] * inv[None, :]
    return jnp.cos(ang).T, jnp.sin(ang).T


def _cmp_ab_accumulate(load_rows, wab_ref, n_rows):
    del n_rows
    heads_per_chunk = LANES // HEAD_DIM
    accs = [[None] * N_KV_HEADS for _ in range(2)]
    for s in range(2):
        for c in range(KV_WIDTH // LANES):
            plane = s * (KV_WIDTH // LANES) + c
            lhs = jnp.concatenate([load_rows(r, plane).astype(BF16) for r in range(CMP_STRIDE)], axis=1)
            out = _dot(lhs, wab_ref[s])
            for k in range(heads_per_chunk):
                accs[s][c * heads_per_chunk + k] = out[:, k * 2 * CMP_HIDDEN:(k + 1) * 2 * CMP_HIDDEN]
    return accs


def _cmp_finish_rows(load_ab, n_row, pe_ref, w1_ref, b1_ref, w2_ref, gk_ref, ck_ref, cv_ref):
    for s in range(2):
        bias = _dot(pe_ref[s].astype(BF16), w1_ref[s].astype(BF16)) + b1_ref[s]
        w2 = w2_ref[s].astype(BF16)
        for g in range(N_KV_HEADS):
            ab = load_ab(s, g)
            hid = ab[:, :CMP_HIDDEN] + pltpu.roll(ab[:, CMP_HIDDEN:], n_row - 1, 0) + bias
            cdf = 0.5 * (1.0 + jnp.tanh(0.7978845608028654 * (hid + 0.044715 * (hid * hid * hid))))
            out = _dot((hid * cdf).astype(BF16), w2)
            if s == 0:
                ck_ref[0, g] = _rms(out, gk_ref[...]).astype(BF16)
            else:
                cv_ref[0, g] = out.astype(BF16)


def _cmp_prompt_kernel(craw_ref, wab_ref, pe_ref, w1_ref, b1_ref, w2_ref, gk_ref, ck_ref, cv_ref, *, n_chunk):
    accs = _cmp_ab_accumulate(lambda r, c: craw_ref[c, pl.ds(r, n_chunk, stride=CMP_STRIDE), :], wab_ref, n_chunk)
    _cmp_finish_rows(lambda s, g: accs[s][g], n_chunk, pe_ref, w1_ref, b1_ref, w2_ref, gk_ref, ck_ref, cv_ref)


def _cmp_pages_kernel(pt_ref, *refs, n_pages):
    del pt_ref
    page_refs = refs[:n_pages]
    wab_ref, pe_ref, w1_ref, b1_ref, w2_ref, gk_ref, ck_ref, cv_ref, craw_s, ab_s = refs[n_pages:]
    q = pl.program_id(1)
    pairs = KV_WIDTH // LANES
    for j in range(n_pages):
        for s in range(2):
            for k in range(pairs):
                tile = jnp.concatenate([page_refs[j][0, s, 2 * k], page_refs[j][0, s, 2 * k + 1]], axis=0)
                craw_s[s * pairs + k, j * PAGE_SIZE:(j + 1) * PAGE_SIZE, :] = tile.T
    n_chunk = n_pages * PAGE_SIZE // CMP_STRIDE
    accs = _cmp_ab_accumulate(lambda r, c: craw_s[c, pl.ds(r, n_chunk, stride=CMP_STRIDE), :], wab_ref, n_chunk)
    r0 = pl.multiple_of(q * n_chunk, n_chunk)
    for s in range(2):
        for g in range(N_KV_HEADS):
            ab_s[s, g, pl.ds(r0, n_chunk), :] = accs[s][g]

    @pl.when(q == pl.num_programs(1) - 1)
    def _():
        _cmp_finish_rows(lambda s, g: ab_s[s, g], ab_s.shape[2], pe_ref, w1_ref, b1_ref, w2_ref, gk_ref,
                         ck_ref, cv_ref)


def _cmp_small_operands(cmp_pe, cmp_w1, cmp_b1, cmp_w2, gk):
    return (cmp_pe.reshape(2, 1, CMP_BLOCK * HEAD_DIM), cmp_w1, cmp_b1.reshape(2, 1, CMP_HIDDEN), cmp_w2,
            gk.reshape(1, HEAD_DIM))


def _compress_pages(cache5, page_table, cmp_pe, cmp_w1, cmp_b1, cmp_w2, gk, *, n_pages):
    b, pages_per_seq = page_table.shape
    steps = pages_per_seq // n_pages
    n_row = pages_per_seq * PAGE_SIZE // CMP_STRIDE
    wab = _cmp_weights(cmp_w1)
    small = _cmp_small_operands(cmp_pe, cmp_w1, cmp_b1, cmp_w2, gk)

    def page_spec(j):
        return pl.BlockSpec((1, 2, N_KV_HEADS, HEAD_DIM, PAGE_SIZE),
                            lambda i, q, pt: (pt[i, q * n_pages + j], 0, 0, 0, 0))

    full = lambda a: pl.BlockSpec(a.shape, lambda i, q, pt: (0,) * a.ndim)
    out_spec = pl.BlockSpec((1, N_KV_HEADS, n_row, HEAD_DIM), lambda i, q, pt: (i, 0, 0, 0))
    return pl.pallas_call(
        functools.partial(_cmp_pages_kernel, n_pages=n_pages),
        grid_spec=pltpu.PrefetchScalarGridSpec(
            num_scalar_prefetch=1,
            grid=(b, steps),
            in_specs=[page_spec(j) for j in range(n_pages)] + [full(wab)] + [full(a) for a in small],
            out_specs=[out_spec, out_spec],
            scratch_shapes=[pltpu.VMEM((2 * KV_WIDTH // LANES, n_pages * PAGE_SIZE, LANES), F32),
                            pltpu.VMEM((2, N_KV_HEADS, n_row, 2 * CMP_HIDDEN), F32)],
        ),
        out_shape=[jax.ShapeDtypeStruct((b, N_KV_HEADS, n_row, HEAD_DIM), BF16)] * 2,
        compiler_params=_params(("arbitrary", "arbitrary"), VMEM_LIMIT),
        name="cmp_pages",
    )(page_table, *([cache5] * n_pages), wab, *small)


def _cmp_weights(cmp_w1):
    w = cmp_w1.reshape(2, 2, CMP_STRIDE, HEAD_DIM, CMP_HIDDEN)
    w = w.transpose(0, 2, 3, 1, 4).reshape(2, CMP_STRIDE, HEAD_DIM, 2 * CMP_HIDDEN)
    eye = jnp.eye(LANES // HEAD_DIM, dtype=w.dtype)
    w = w[:, :, None, :, None, :] * eye[None, None, :, None, :, None]
    return w.reshape(2, CMP_STRIDE * LANES, (LANES // HEAD_DIM) * 2 * CMP_HIDDEN).astype(BF16)


def _compress_prompt(craw, cmp_pe, cmp_w1, cmp_b1, cmp_w2, gk, *, b, t):
    n_chunk = t // CMP_STRIDE
    wab = _cmp_weights(cmp_w1)
    small = _cmp_small_operands(cmp_pe, cmp_w1, cmp_b1, cmp_w2, gk)
    full = lambda a: pl.BlockSpec(a.shape, lambda i: (0,) * a.ndim)
    out_spec = pl.BlockSpec((1, N_KV_HEADS, n_chunk, HEAD_DIM), lambda i: (i, 0, 0, 0))
    return pl.pallas_call(
        functools.partial(_cmp_prompt_kernel, n_chunk=n_chunk),
        grid=(b,),
        in_specs=[pl.BlockSpec((craw.shape[0], t, LANES), lambda i: (0, i, 0)), full(wab)]
        + [full(a) for a in small],
        out_specs=[out_spec, out_spec],
        out_shape=[jax.ShapeDtypeStruct((b, N_KV_HEADS, n_chunk, HEAD_DIM), BF16)] * 2,
        compiler_params=_params(("arbitrary",), VMEM_LIMIT),
        name="cmp_prompt",
    )(craw, wab, *small)


def _qproj_kernel(h_ref, g_ref, wt_ref, gq_ref, cos_ref, sin_ref, qn_ref, qr_ref, gt_ref, w_s):
    @pl.when(pl.program_id(0) == 0)
    def _():
        w_s[...] = wt_ref[...].astype(BF16)

    xn = _rms(h_ref[...], g_ref[...]).astype(BF16)
    qg = _nt_dot(w_s[...], xn)
    tt = xn.shape[0]
    gq = jnp.concatenate([gq_ref[...]] * (tt // LANES), axis=1)
    c = cos_ref[...]
    s = sin_ref[...]
    for h in range(N_HEADS):
        qh = qg[h * HEAD_DIM:(h + 1) * HEAD_DIM]
        qn = qh * lax.rsqrt(jnp.mean(qh * qh, axis=0, keepdims=True) + EPS) * gq
        qn_ref[h * HEAD_DIM:(h + 1) * HEAD_DIM, :] = (qn * Q_SCALE).astype(BF16)
        x1 = qn[:HALF_DIM]
        x2 = qn[HALF_DIM:]
        qr_ref[h * HEAD_DIM:h * HEAD_DIM + HALF_DIM, :] = ((x1 * c - x2 * s) * Q_SCALE).astype(BF16)
        qr_ref[h * HEAD_DIM + HALF_DIM:(h + 1) * HEAD_DIM, :] = ((x2 * c + x1 * s) * Q_SCALE).astype(BF16)
    gt_ref[...] = _sigmoid(qg[N_HEADS * HEAD_DIM:])


def _qg_weights(w_qg):
    nq = N_HEADS * HEAD_DIM
    gates = w_qg[:, nq:].reshape(D_MODEL, 3, N_KV_HEADS, HEADS_PER_KV).transpose(2, 1, 3, 0)
    gates = gates.reshape(N_KV_HEADS, 3 * HEADS_PER_KV, D_MODEL)
    gates = jnp.pad(gates, ((0, 0), (0, GATE_ROWS - 3 * HEADS_PER_KV), (0, 0)))
    return jnp.concatenate([w_qg[:, :nq].T, gates.reshape(N_KV_HEADS * GATE_ROWS, D_MODEL)], axis=0)


def _qproj(h2, gain, wt, q_norm, cos_tt, sin_tt, *, tt, pos_blocks):
    n, d = h2.shape
    nq = N_HEADS * HEAD_DIM
    ng = N_KV_HEADS * GATE_ROWS
    gq = jnp.broadcast_to(q_norm[:, None], (HEAD_DIM, LANES))
    full = lambda a: pl.BlockSpec(a.shape, lambda i: (0,) * a.ndim)
    return pl.pallas_call(
        _qproj_kernel,
        grid=(n // tt,),
        in_specs=[
            pl.BlockSpec((tt, d), lambda i: (i, 0)),
            pl.BlockSpec((1, d), lambda i: (0, 0)),
            full(wt), full(gq),
            pl.BlockSpec((HALF_DIM, tt), lambda i: (0, i % pos_blocks)),
            pl.BlockSpec((HALF_DIM, tt), lambda i: (0, i % pos_blocks)),
        ],
        out_specs=[
            pl.BlockSpec((nq, tt), lambda i: (0, i)),
            pl.BlockSpec((nq, tt), lambda i: (0, i)),
            pl.BlockSpec((ng, tt), lambda i: (0, i)),
        ],
        out_shape=[jax.ShapeDtypeStruct((nq, n), BF16), jax.ShapeDtypeStruct((nq, n), BF16),
                   jax.ShapeDtypeStruct((ng, n), F32)],
        scratch_shapes=[pltpu.VMEM(wt.shape, BF16)],
        compiler_params=_params(("arbitrary",), VMEM_LIMIT),
        name="q_proj",
    )(h2, gain.reshape(1, d), wt, gq, cos_tt, sin_tt)


def _attn_kernel(qn_ref, qr_ref, gt_ref, ck_ref, cvt_ref, kaug_ref, kwin_ref, vaug_ref, vwaug_ref,
                 o_ref, pg_s, o_sel_s, o_win_s, *, tq, n_cmp, n_sb, n_qt):
    hp_n = HEADS_PER_KV
    qt = pl.program_id(2)
    t0 = qt * tq
    tpos = t0 + lax.broadcasted_iota(I32, (1, tq), 1)
    tpos4 = jnp.concatenate([tpos] * hp_n, axis=1)
    qn4 = jnp.concatenate([qn_ref[h * HEAD_DIM:(h + 1) * HEAD_DIM, :] for h in range(hp_n)], axis=1)
    qr4 = jnp.concatenate([qr_ref[h * HEAD_DIM:(h + 1) * HEAD_DIM, :] for h in range(hp_n)], axis=1)

    n_row = ck_ref.shape[2]
    s = _dot(ck_ref[0, 0], qn4)
    ci = lax.broadcasted_iota(I32, (n_row, 1), 0)
    vis = (ci * CMP_STRIDE + (CMP_BLOCK - 1) <= tpos4) & (ci < n_cmp)
    s = jnp.where(vis, s, NEG)
    p = jnp.where(vis, jnp.exp2(s - jnp.max(s, axis=0, keepdims=True)), 0.0)
    p = p / jnp.maximum(jnp.sum(p, axis=0, keepdims=True), TINY)
    o_cmp = _dot(cvt_ref[0, 0], p.astype(BF16))
    pg = p[:, 0:tq]
    for h in range(1, hp_n):
        pg = pg + p[:, h * tq:(h + 1) * tq]

    ratio = SEL_BLOCK // CMP_STRIDE
    scores = []
    for c in range(tq // LANES):
        pg_s[c, 0:8, :] = jnp.zeros((8, LANES), F32)
        pg_s[c, 8:8 + n_row, :] = pg[:, c * LANES:(c + 1) * LANES]
        sc_c = pg_s[c, pl.ds(8 + 1 - CMP_BLOCK // CMP_STRIDE, n_sb, stride=ratio), :]
        for o in range(2 - CMP_BLOCK // CMP_STRIDE, ratio):
            sc_c = sc_c + pg_s[c, pl.ds(8 + o, n_sb, stride=ratio), :]
        scores.append(sc_c)
    score = jnp.concatenate(scores, axis=1)
    jrow = lax.broadcasted_iota(I32, (n_sb, 1), 0)
    cur = tpos // SEL_BLOCK
    forced = (jrow == 0) | (jrow == cur) | (jrow == cur - 1)
    sc = jnp.where(jrow * SEL_BLOCK <= tpos, score + jnp.where(forced, BIG, 0.0), -BIG)
    rank = jnp.zeros((n_sb, tq), I32)
    for jp in range(n_sb):
        row = sc[jp:jp + 1, :]
        beats = (row > sc) | ((row == sc) & (jrow > jp))
        rank = rank + beats.astype(I32)
    n_sbp = kaug_ref.shape[3] - HEAD_DIM
    sel_neg = jnp.where(rank < min(N_SEL, n_sb), 0.0, NEG)
    if n_sbp > n_sb:
        sel_neg = jnp.concatenate([sel_neg, jnp.zeros((n_sbp - n_sb, tq), F32)], axis=0)
    sel_neg = sel_neg.astype(BF16)
    q_rot = [qr_ref[h * HEAD_DIM:(h + 1) * HEAD_DIM, :] for h in range(hp_n)]
    q_aug = jnp.concatenate([jnp.concatenate([q, sel_neg], axis=0) for q in q_rot], axis=1)
    q_win = jnp.concatenate(q_rot, axis=1)
    causal = jnp.where(lax.broadcasted_iota(I32, (tq, 1), 0) <= lax.broadcasted_iota(I32, (1, tq), 1), 0.0, NEG)
    causal = jnp.concatenate([causal] * hp_n, axis=1)

    for k in range(n_qt):
        @pl.when(qt == k)
        def _(k=k):
            lo = k * tq
            s_diag = _dot(kaug_ref[0, 0, lo:lo + tq, :], q_aug) + causal
            m = jnp.max(s_diag, axis=0, keepdims=True)
            if k > 0:
                s_top = _dot(kaug_ref[0, 0, 0:lo, :], q_aug)
                m = jnp.maximum(m, jnp.max(s_top, axis=0, keepdims=True))
            acc = _dot(vaug_ref[:, lo:lo + tq], jnp.exp2(s_diag - m).astype(BF16))
            if k > 0:
                acc = acc + _dot(vaug_ref[:, 0:lo], jnp.exp2(s_top - m).astype(BF16))
            o_sel_s[...] = acc[:HEAD_DIM] / jnp.maximum(acc[HEAD_DIM:HEAD_DIM + 1], TINY)

            parts = []
            for kt in range(max(0, k - -(-WINDOW // tq)), k + 1):
                s_t = _dot(kwin_ref[0, 0, kt * tq:(kt + 1) * tq, :], q_win)
                d_lo = (k - kt) * tq - (tq - 1)
                d_hi = (k - kt) * tq + (tq - 1)
                if d_lo < 0 or d_hi >= WINDOW:
                    dq = ((k - kt) * tq + lax.broadcasted_iota(I32, (1, tq), 1)
                          - lax.broadcasted_iota(I32, (tq, 1), 0))
                    wb = jnp.where((dq >= 0) & (dq < WINDOW), 0.0, NEG)
                    s_t = s_t + jnp.concatenate([wb] * hp_n, axis=1)
                parts.append((kt, s_t))
            m_w = functools.reduce(jnp.maximum, [jnp.max(s_t, axis=0, keepdims=True) for _, s_t in parts])
            acc_w = None
            for kt, s_t in parts:
                term = _dot(vwaug_ref[:, kt * tq:(kt + 1) * tq], jnp.exp2(s_t - m_w).astype(BF16))
                acc_w = term if acc_w is None else acc_w + term
            o_win_s[...] = acc_w[:HEAD_DIM] / jnp.maximum(acc_w[HEAD_DIM:HEAD_DIM + 1], TINY)

    o_sel = o_sel_s[...]
    o_win = o_win_s[...]
    gt = gt_ref[...]
    for h in range(hp_n):
        sl = slice(h * tq, (h + 1) * tq)
        o = (gt[h:h + 1] * o_cmp[:, sl] + gt[hp_n + h:hp_n + h + 1] * o_sel[:, sl]
             + gt[2 * hp_n + h:2 * hp_n + h + 1] * o_win[:, sl])
        o_ref[h * HEAD_DIM:(h + 1) * HEAD_DIM, :] = o.astype(BF16)


def _attn_prompt(qn_t, qr_t, g_t, ck, cv_t, ksel, kwin, vsel_t, vwin_t, *, b, t, tq):
    nq = t // tq
    n_sb = t // SEL_BLOCK
    n_row = ck.shape[2]
    n_cmp = t // CMP_STRIDE - CMP_BLOCK // CMP_STRIDE + 1
    n_sbp = -(-n_sb // 32) * 32
    kw = N_KV_HEADS * HEAD_DIM
    onehot = (jnp.arange(t)[:, None] // SEL_BLOCK == jnp.arange(n_sbp)[None, :]).astype(BF16)
    kaug = jnp.concatenate([ksel.reshape(N_KV_HEADS, b, t, HEAD_DIM),
                            jnp.broadcast_to(onehot, (N_KV_HEADS, b, t, n_sbp))], axis=-1)
    ones_rows = jnp.zeros((N_KV_HEADS, V_PAD_ROWS, b * t), BF16).at[:, 0].set(1.0)

    def with_ones(v_t):
        v3 = jnp.concatenate([v_t.reshape(N_KV_HEADS, HEAD_DIM, b * t), ones_rows], axis=1)
        return v3.reshape(N_KV_HEADS * (HEAD_DIM + V_PAD_ROWS), b * t)

    v_rows = HEAD_DIM + V_PAD_ROWS
    return pl.pallas_call(
        functools.partial(_attn_kernel, tq=tq, n_cmp=n_cmp, n_sb=n_sb, n_qt=nq),
        grid=(b, N_KV_HEADS, nq),
        in_specs=[
            pl.BlockSpec((kw, tq), lambda i, g, q: (g, i * nq + q)),
            pl.BlockSpec((kw, tq), lambda i, g, q: (g, i * nq + q)),
            pl.BlockSpec((GATE_ROWS, tq), lambda i, g, q: (g, i * nq + q)),
            pl.BlockSpec((1, 1, n_row, HEAD_DIM), lambda i, g, q: (i, g, 0, 0)),
            pl.BlockSpec((1, 1, HEAD_DIM, n_row), lambda i, g, q: (i, g, 0, 0)),
            pl.BlockSpec((1, 1, t, HEAD_DIM + n_sbp), lambda i, g, q: (g, i, 0, 0)),
            pl.BlockSpec((1, 1, t, HEAD_DIM), lambda i, g, q: (g, i, 0, 0)),
            pl.BlockSpec((v_rows, t), lambda i, g, q: (g, i)),
            pl.BlockSpec((v_rows, t), lambda i, g, q: (g, i)),
        ],
        out_specs=pl.BlockSpec((kw, tq), lambda i, g, q: (g, i * nq + q)),
        out_shape=jax.ShapeDtypeStruct((N_HEADS * HEAD_DIM, b * t), BF16),
        scratch_shapes=[pltpu.VMEM((tq // LANES, n_row + 8, LANES), F32),
                        pltpu.VMEM((HEAD_DIM, HEADS_PER_KV * tq), F32),
                        pltpu.VMEM((HEAD_DIM, HEADS_PER_KV * tq), F32)],
        compiler_params=_params(("arbitrary", "arbitrary", "arbitrary"), VMEM_LIMIT),
        name="attn_prompt",
    )(qn_t, qr_t, g_t, ck, cv_t, kaug, kwin.reshape(N_KV_HEADS, b, t, HEAD_DIM), with_ones(vsel_t), with_ones(vwin_t))


def _oproj_t_kernel(ot_ref, h_ref, w_ref, out_ref, w_s):
    @pl.when(pl.program_id(0) == 0)
    def _():
        w_s[...] = w_ref[...].astype(BF16)

    o = ot_ref[...].astype(F32).T.astype(BF16)
    out_ref[...] = h_ref[...] + _dot(o, w_s[...])


def _oproj_t(o_t, h2, w_o, *, tt):
    n, d = h2.shape
    return pl.pallas_call(
        _oproj_t_kernel,
        grid=(n // tt,),
        in_specs=[
            pl.BlockSpec((o_t.shape[0], tt), lambda i: (0, i)),
            pl.BlockSpec((tt, d), lambda i: (i, 0)),
            pl.BlockSpec(w_o.shape, lambda i: (0, 0)),
        ],
        out_specs=pl.BlockSpec((tt, d), lambda i: (i, 0)),
        out_shape=jax.ShapeDtypeStruct((n, d), F32),
        scratch_shapes=[pltpu.VMEM(w_o.shape, BF16)],
        compiler_params=_params(("arbitrary",), VMEM_LIMIT),
        name="o_proj",
    )(o_t, h2, w_o)


def _oproj_n_kernel(o_ref, h_ref, w_ref, out_ref):
    out_ref[...] = h_ref[...] + _dot(o_ref[...].astype(BF16), w_ref[...].astype(BF16))


def _oproj_n(o, h2, w_o):
    n, d = h2.shape
    return pl.pallas_call(
        _oproj_n_kernel,
        out_shape=jax.ShapeDtypeStruct((n, d), F32),
        compiler_params=_params((), VMEM_LIMIT),
        name="o_proj_sample",
    )(o, h2, w_o)


def _attn_sample_kernel(pt_ref, *refs, n_pages, n_cmp, n_sb, ts, past, n_buf):
    del pt_ref
    page_refs = refs[:n_pages]
    (qn_ref, qr_ref, ckt_ref, cvt_ref, win_ref, knew_ref, vnew_ref, kwnew_ref, vwnew_ref, gate_ref,
     sel_ref, e_ref, o_ref, mask_s, m_s, l_s, acc_s, ocmp_s) = refs[n_pages:]
    q_step = pl.program_id(1)
    rows = HEADS_PER_KV * N_KV_HEADS * ts
    grp_rows = N_KV_HEADS * ts
    row = lax.broadcasted_iota(I32, (rows, 1), 0)
    qpos = past + row % ts
    qr = qr_ref[0]

    def tile_rows(x):
        return jnp.concatenate([x] * HEADS_PER_KV, axis=0)

    def online_update(s, mk, v_dot):
        s = jnp.where(mk, s, NEG)
        m_new = jnp.maximum(m_s[...], jnp.max(s, axis=-1, keepdims=True))
        alpha = jnp.exp2(m_s[...] - m_new)
        p = jnp.where(mk, jnp.exp2(s - m_new), 0.0)
        l_s[...] = l_s[...] * alpha + jnp.sum(p, axis=-1, keepdims=True)
        acc_s[...] = acc_s[...] * alpha + v_dot(p.astype(BF16))
        m_s[...] = m_new

    @pl.when(q_step == 0)
    def _():
        n_row = ckt_ref.shape[2]
        s = _dot(qn_ref[0], ckt_ref[0])
        ci = lax.broadcasted_iota(I32, (1, n_row), 1)
        vis = (ci * CMP_STRIDE + (CMP_BLOCK - 1) <= qpos) & (ci < n_cmp)
        s = jnp.where(vis, s, NEG)
        p = jnp.where(vis, jnp.exp2(s - jnp.max(s, axis=-1, keepdims=True)), 0.0)
        p = p / jnp.maximum(jnp.sum(p, axis=-1, keepdims=True), TINY)
        ocmp_s[...] = _nt_dot(p.astype(BF16), cvt_ref[0])
        pg = p[0:grp_rows]
        for h in range(1, HEADS_PER_KV):
            pg = pg + p[h * grp_rows:(h + 1) * grp_rows]
        score = _dot(pg, sel_ref[...], precision=HIGHEST)
        width = score.shape[1]
        j = lax.broadcasted_iota(I32, (1, width), 1)
        tq = qpos[0:grp_rows]
        cur = tq // SEL_BLOCK
        forced = (j == 0) | (j == cur) | (j == cur - 1)
        sc = jnp.where(j * SEL_BLOCK <= tq, score + jnp.where(forced, BIG, 0.0), -BIG)
        sc = jnp.where(j < n_sb, sc, -2.0 * BIG)
        rank = jnp.zeros((grp_rows, width), I32)
        for jp in range(n_sb):
            col = sc[:, jp:jp + 1]
            beats = (col > sc) | ((col == sc) & (j > jp))
            rank = rank + beats.astype(I32)
        sel = jnp.where((rank < min(N_SEL, n_sb)) & (j < n_sb), 1.0, 0.0).astype(BF16)
        mask_s[...] = _dot(sel, e_ref[...])
        m_s[...] = jnp.full(m_s.shape, NEG, F32)
        l_s[...] = jnp.zeros(l_s.shape, F32)
        acc_s[...] = jnp.zeros(acc_s.shape, F32)

    width = n_pages * PAGE_SIZE
    kt = jnp.concatenate([page_refs[i][0, 0].reshape(KV_WIDTH, PAGE_SIZE) for i in range(n_pages)], axis=1)
    vt = jnp.concatenate([page_refs[i][0, 1].reshape(KV_WIDTH, PAGE_SIZE) for i in range(n_pages)], axis=1)
    k0 = pl.multiple_of(q_step * width, width)
    mk = tile_rows(mask_s[:, pl.ds(k0, width)]) > 0.5
    online_update(_dot(qr, kt.astype(BF16)), mk, lambda p: _nt_dot(p, vt.astype(BF16)))

    @pl.when(q_step == pl.num_programs(1) - 1)
    def _():
        lane = lax.broadcasted_iota(I32, (1, LANES), 1)
        new_ok = (lane < ts) & (past + lane <= qpos)
        mk_new = (tile_rows(mask_s[:, past:past + LANES]) > 0.5) & new_ok
        online_update(_nt_dot(qr, knew_ref[0]), mk_new, lambda p: _dot(p, vnew_ref[0]))
        o_sel = acc_s[...] / jnp.maximum(l_s[...], TINY)
        bi = lax.broadcasted_iota(I32, (1, n_buf), 1)
        dq = qpos - (past - n_buf + bi)
        ok_buf = (dq >= 0) & (dq < WINDOW)
        dq_new = qpos - (past + lane)
        ok_new = (lane < ts) & (dq_new >= 0) & (dq_new < WINDOW)
        s_w = jnp.concatenate([_dot(qr, win_ref[0, 0].astype(BF16)), _nt_dot(qr, kwnew_ref[0])], axis=1)
        ok = jnp.concatenate([jnp.broadcast_to(ok_buf, (rows, n_buf)), jnp.broadcast_to(ok_new, (rows, LANES))], axis=1)
        s_w = jnp.where(ok, s_w, NEG)
        p_w = jnp.where(ok, jnp.exp2(s_w - jnp.max(s_w, axis=-1, keepdims=True)), 0.0)
        p_w = (p_w / jnp.maximum(jnp.sum(p_w, axis=-1, keepdims=True), TINY)).astype(BF16)
        o_win = _nt_dot(p_w[:, :n_buf], win_ref[0, 1].astype(BF16)) + _dot(p_w[:, n_buf:], vwnew_ref[0])
        row_g = (row // ts) % N_KV_HEADS

        def own_group(o):
            out = jnp.zeros((rows, HEAD_DIM), F32)
            for g in range(N_KV_HEADS):
                out = out + jnp.where(row_g == g, o[:, g * HEAD_DIM:(g + 1) * HEAD_DIM], 0.0)
            return out

        o_ref[0] = (gate_ref[0, 0] * own_group(ocmp_s[...]) + gate_ref[0, 1] * own_group(o_sel)
                    + gate_ref[0, 2] * own_group(o_win))


def _attn_sample(cache5, page_table, qn_bd, qr_bd, ck_t, cv_t, win4, knew, vnew, kwnew, vwnew, gates,
                 *, n_pages, ts, past):
    b, pages_per_seq = page_table.shape
    steps = pages_per_seq // n_pages
    rows = qn_bd.shape[1]
    n_chunk = ck_t.shape[2]
    n_cmp = n_chunk - CMP_BLOCK // CMP_STRIDE + 1
    n_keys = past + LANES
    n_sb = -(-(past + ts) // SEL_BLOCK)
    n_sb_pad = -(-n_sb // LANES) * LANES
    n_buf = win4.shape[3]
    ratio = SEL_BLOCK // CMP_STRIDE
    ci = jnp.arange(n_chunk)[:, None]
    jb = jnp.arange(n_sb_pad)[None, :]
    sel_map = ((ci >= ratio * jb + 1 - CMP_BLOCK // CMP_STRIDE) & (ci < ratio * jb + ratio)
               & (ci < n_cmp) & (jb < n_sb)).astype(F32)
    expand = (jnp.arange(n_keys)[None, :] // SEL_BLOCK == jnp.arange(n_sb_pad)[:, None]).astype(BF16)

    def page_spec(j):
        return pl.BlockSpec((1, 2, N_KV_HEADS, HEAD_DIM, PAGE_SIZE),
                            lambda i, q, pt: (pt[i, q * n_pages + j], 1, 0, 0, 0))

    per_seq = lambda a: pl.BlockSpec((1,) + a.shape[1:], lambda i, q, pt: (i,) + (0,) * (a.ndim - 1))
    full = lambda a: pl.BlockSpec(a.shape, lambda i, q, pt: (0,) * a.ndim)
    return pl.pallas_call(
        functools.partial(_attn_sample_kernel, n_pages=n_pages, n_cmp=n_cmp, n_sb=n_sb, ts=ts, past=past,
                          n_buf=n_buf),
        grid_spec=pltpu.PrefetchScalarGridSpec(
            num_scalar_prefetch=1,
            grid=(b, steps),
            in_specs=[page_spec(j) for j in range(n_pages)]
            + [per_seq(a) for a in (qn_bd, qr_bd, ck_t, cv_t, win4, knew, vnew, kwnew, vwnew, gates)]
            + [full(sel_map), full(expand)],
            out_specs=pl.BlockSpec((1, rows, HEAD_DIM), lambda i, q, pt: (i, 0, 0)),
            scratch_shapes=[
                pltpu.VMEM((N_KV_HEADS * ts, n_keys), F32),
                pltpu.VMEM((rows, 1), F32), pltpu.VMEM((rows, 1), F32),
                pltpu.VMEM((rows, KV_WIDTH), F32), pltpu.VMEM((rows, KV_WIDTH), F32),
            ],
        ),
        out_shape=jax.ShapeDtypeStruct((b, rows, HEAD_DIM), F32),
        compiler_params=_params(("arbitrary", "arbitrary"), VMEM_LIMIT),
        name="attn_sample",
    )(page_table, *([cache5] * n_pages), qn_bd, qr_bd, ck_t, cv_t, win4, knew, vnew, kwnew, vwnew, gates,
      sel_map, expand)


def _router_weights(router_group, router_expert):
    rt = jnp.concatenate([router_group, router_expert], axis=1).T
    return jnp.pad(rt, ((0, 24 - rt.shape[0]), (0, 0)))


def _trunk_prompt(x, p, *, tt_pool=256, tm=1024, tt=512, tq=256):
    (norm_mix, norm_ffn, pool_w, pool_scale, kv_norm, w_kv, k_norm, cmp_pe, cmp_w1, cmp_b1, cmp_w2,
     w_qg, q_norm, w_o, router_group, router_expert, w_gate_up, w_down) = p
    b, t, d = x.shape
    n = b * t
    pos = jnp.arange(t)
    h, new_pool = _pool_layer(x, jnp.zeros((b, POOL_HALO, d), F32), norm_mix[0], pool_w[0], pool_scale[0],
                              tt=tt_pool, clip=True)
    h = _moe(h.reshape(n, d), norm_ffn[0], _router_weights(router_group[0], router_expert[0]),
             w_gate_up, w_down, 0, tm=tm)
    cos_t, sin_t = _rope_tables_transposed(pos)
    kv_t, win_t, craw, ksel, kwin, vsel_t, vwin_t = _kvproj(h.reshape(b, t, d), kv_norm, w_kv, k_norm,
                                                            cos_t, sin_t, tt=tt)
    ck, cv = _compress_prompt(craw, cmp_pe, cmp_w1, cmp_b1, cmp_w2, k_norm[0], b=b, t=t)
    qn_t, qr_t, g_t = _qproj(h, norm_mix[1], _qg_weights(w_qg[0]), q_norm[0], cos_t, sin_t,
                             tt=tt, pos_blocks=t // tt)
    o_t = _attn_prompt(qn_t, qr_t, g_t, ck, jnp.swapaxes(cv, 2, 3), ksel, kwin, vsel_t, vwin_t, b=b, t=t, tq=tq)
    h = _oproj_t(o_t, h, w_o[0], tt=tt)
    h = _moe(h, norm_ffn[1], _router_weights(router_group[1], router_expert[1]),
             w_gate_up, w_down, 1, tm=tm)
    n_win = min(WINDOW, t)
    kv_new = kv_t.reshape(b, 4, N_KV_HEADS, HEAD_DIM, t).transpose(0, 4, 1, 2, 3)
    win_new = win_t[:, :, t - n_win:].reshape(b, 2, N_KV_HEADS, HEAD_DIM, n_win).transpose(0, 4, 1, 2, 3)
    return h.reshape(b, t, d), new_pool[None], kv_new, win_new


def _trunk_sample(x, state_pool, cache_kv, page_table, state_win, p, *, n_pages=32):
    (norm_mix, norm_ffn, pool_w, pool_scale, kv_norm, w_kv, k_norm, cmp_pe, cmp_w1, cmp_b1, cmp_w2,
     w_qg, q_norm, w_o, router_group, router_expert, w_gate_up, w_down) = p
    b, ts, d = x.shape
    n = b * ts
    past = page_table.shape[1] * PAGE_SIZE
    n_buf = state_win.shape[1]
    prev16 = jnp.pad(state_pool[0], ((0, 0), (POOL_HALO - POOL_STATE, 0), (0, 0)))
    h, new_pool = _pool_layer(x, prev16, norm_mix[0], pool_w[0], pool_scale[0], tt=ts, clip=False)
    h = _moe(h.reshape(n, d), norm_ffn[0], _router_weights(router_group[0], router_expert[0]),
             w_gate_up, w_down, 0, tm=n)
    cos_t, sin_t = _rope_tables_transposed(past + jnp.arange(ts))
    cos_t = jnp.tile(cos_t, (1, b))
    sin_t = jnp.tile(sin_t, (1, b))
    kv_t, win_t, _, _, _, _, _ = _kvproj(h.reshape(1, n, d), kv_norm, w_kv, k_norm, cos_t, sin_t, tt=n)
    kv_rows = kv_t[0].T
    win_rows = win_t[0].T

    cache5 = cache_kv.transpose(0, 2, 3, 4, 1)
    ck, cv = _compress_pages(cache5, page_table, cmp_pe, cmp_w1, cmp_b1, cmp_w2, k_norm[0], n_pages=n_pages)
    n_chunk = ck.shape[2]
    ck_t = ck.transpose(0, 1, 3, 2).reshape(b, KV_WIDTH, n_chunk)
    cv_t = cv.transpose(0, 1, 3, 2).reshape(b, KV_WIDTH, n_chunk)

    qn_t, qr_t, g_t = _qproj(h, norm_mix[1], _qg_weights(w_qg[0]), q_norm[0], cos_t, sin_t, tt=n, pos_blocks=1)

    def block_diag_queries(q_t):
        q5 = q_t.reshape(N_KV_HEADS, HEADS_PER_KV, HEAD_DIM, b, ts).transpose(3, 1, 0, 4, 2)
        eye = jnp.eye(N_KV_HEADS, dtype=q_t.dtype)
        qbd = q5[:, :, :, :, None, :] * eye[None, None, :, None, :, None]
        return qbd.reshape(b, HEADS_PER_KV * N_KV_HEADS * ts, KV_WIDTH)

    gates = g_t.reshape(N_KV_HEADS, GATE_ROWS, b, ts)[:, :3 * HEADS_PER_KV]
    gates = gates.reshape(N_KV_HEADS, 3, HEADS_PER_KV, b, ts).transpose(3, 1, 2, 0, 4)
    gates = jnp.broadcast_to(gates.reshape(b, 3, HEADS_PER_KV * N_KV_HEADS * ts, 1),
                             (b, 3, HEADS_PER_KV * N_KV_HEADS * ts, HEAD_DIM))

    def new_rows(rows2):
        return jnp.pad(rows2.reshape(b, ts, KV_WIDTH), ((0, 0), (0, LANES - ts), (0, 0))).astype(BF16)

    win4 = state_win.transpose(0, 2, 3, 4, 1).reshape(b, 2, KV_WIDTH, n_buf)
    o = _attn_sample(cache5, page_table, block_diag_queries(qn_t), block_diag_queries(qr_t), ck_t, cv_t, win4,
                     new_rows(kv_rows[:, 2 * KV_WIDTH:3 * KV_WIDTH]), new_rows(kv_rows[:, 3 * KV_WIDTH:]),
                     new_rows(win_rows[:, :KV_WIDTH]), new_rows(win_rows[:, KV_WIDTH:]), gates,
                     n_pages=n_pages, ts=ts, past=past)
    o = o.reshape(b, HEADS_PER_KV, N_KV_HEADS, ts, HEAD_DIM).transpose(0, 3, 2, 1, 4).reshape(n, N_HEADS * HEAD_DIM)
    h = _oproj_n(o, h, w_o[0])
    h = _moe(h, norm_ffn[1], _router_weights(router_group[1], router_expert[1]),
             w_gate_up, w_down, 1, tm=n)
    kv_new = kv_rows.reshape(b, ts, 4, N_KV_HEADS, HEAD_DIM)
    win_new = jnp.concatenate([state_win, win_rows.reshape(b, ts, 2, N_KV_HEADS, HEAD_DIM)], axis=1)[:, -n_buf:]
    return h.reshape(b, ts, d), new_pool[None], kv_new, win_new


def kernel(x_prompt, x_sample, state_pool, cache_kv, page_table, state_win, norm_mix, norm_ffn, pool_w, pool_scale, kv_norm, w_kv, k_norm, cmp_pe, cmp_w1, cmp_b1, cmp_w2, w_qg, q_norm, w_o, router_group, router_expert, w_gate_up, w_down):
    params = (norm_mix, norm_ffn, pool_w, pool_scale, kv_norm, w_kv, k_norm, cmp_pe, cmp_w1, cmp_b1,
              cmp_w2, w_qg, q_norm, w_o, router_group, router_expert, _cast_bf16(w_gate_up), _cast_bf16(w_down))
    y_p, pool_p, kv_p, win_p = _trunk_prompt(x_prompt, params)
    y_s, pool_s, kv_s, win_s = _trunk_sample(x_sample, state_pool, cache_kv, page_table, state_win, params)
    return y_p, y_s, pool_p, pool_s, kv_p, kv_s, win_p, win_s
```
